```python
import jax, jax.numpy as jnp
from jax import lax
import numpy as np

D_MODEL = 2048
BATCH = 4
SEQ = 2048
DEPTH = 4
DEC_BATCH = 128
DEC_SEQ = 1
PAST_LEN = 16384
PAGE_SIZE = 128

H_M = 4
DH_M = 256
W_M = H_M * DH_M
H_R = 4
DH_R = 256
W_R = H_R * DH_R
CONV_W = 4
CHUNK = 128
ROPE_BASE = 10000.0
EPS = 1e-6
NEG = -1e30
N_IN = 3 * W_M + 2 * H_M + 4 * W_R

kernel_name = "hybrid_mlstm_retention_gated_decoder_step"

f32 = jnp.float32


def rmsnorm(x, g):
    xf = x.astype(f32)
    return xf * lax.rsqrt(jnp.mean(xf * xf, axis=-1, keepdims=True) + EPS) * g.astype(f32)


def head_norm(h, g):
    mu = jnp.mean(h, axis=-1, keepdims=True)
    var = jnp.mean(jnp.square(h - mu), axis=-1, keepdims=True)
    hn = (h - mu) * lax.rsqrt(var + EPS)
    B, T = h.shape[:2]
    return hn.reshape(B, T, -1) * g.astype(f32)


def to_chunks(a, L):
    B, T = a.shape[:2]
    return jnp.moveaxis(a.reshape((B, T // L, L) + a.shape[2:]), 1, 0)


def from_chunks(a):
    nc, B, L = a.shape[:3]
    return jnp.moveaxis(a, 0, 1).reshape((B, nc * L) + a.shape[3:])


def causal_conv(u, buf, w, b):
    T = u.shape[1]
    up = jnp.concatenate([buf.astype(f32), u], axis=1)
    out = b.astype(f32)
    for i in range(CONV_W):
        out = out + up[:, i:i + T] * w[i].astype(f32)
    return out, up[:, -(CONV_W - 1):]


def rope(x, pos):
    dh = x.shape[-1]
    half = dh // 2
    inv = ROPE_BASE ** (-jnp.arange(0, half, dtype=f32) / half)
    ang = pos.astype(f32)[:, None] * inv[None, :]
    cos = jnp.cos(ang)[None, :, None, :]
    sin = jnp.sin(ang)[None, :, None, :]
    x1, x2 = x[..., :half], x[..., half:]
    return jnp.concatenate([x1 * cos - x2 * sin, x1 * sin + x2 * cos], axis=-1)


def mlstm_chunkwise(q, k, v, i_pre, logf, C0, n0, m0):
    T = q.shape[1]
    L = min(CHUNK, T)
    causal = jnp.tril(jnp.ones((L, L), dtype=bool))

    def step(carry, xs):
        C, n, m = carry
        qc, kc, vc, ic, fc = xs
        b = jnp.cumsum(fc, axis=1).transpose(0, 2, 1)
        ih = ic.transpose(0, 2, 1)
        dlog = jnp.where(causal, b[..., :, None] - b[..., None, :] + ih[..., None, :], NEG)
        inter = b + m[..., None]
        m_t = jnp.maximum(inter, jnp.max(dlog, axis=-1))
        w = jnp.exp(dlog - m_t[..., None])
        a = jnp.exp(inter - m_t)
        ws = w * jnp.einsum('blhd,bshd->bhls', qc, kc)
        num = (jnp.einsum('bhls,bshe->blhe', ws, vc)
               + a.transpose(0, 2, 1)[..., None] * jnp.einsum('bhed,blhd->blhe', C, qc))
        den = jnp.sum(ws, axis=-1) + a * jnp.einsum('bhd,blhd->bhl', n, qc)
        den = jnp.maximum(jnp.abs(den), jnp.exp(-m_t))
        hc = num / den.transpose(0, 2, 1)[..., None]
        m_new = m_t[..., -1]
        wk = jnp.exp(b[..., -1:] - b + ih - m_new[..., None])
        dec = jnp.exp(b[..., -1] + m - m_new)
        C_new = dec[..., None, None] * C + jnp.einsum('bhs,bshe,bshd->bhed', wk, vc, kc)
        n_new = dec[..., None] * n + jnp.einsum('bhs,bshd->bhd', wk, kc)
        return (C_new, n_new, m_new), hc

    xs = (to_chunks(q, L), to_chunks(k, L), to_chunks(v, L), to_chunks(i_pre, L), to_chunks(logf, L))
    (C, n, m), hs = lax.scan(step, (C0, n0, m0), xs)
    return from_chunks(hs), C, n, m


def retention_chunkwise(q, k, v, log_gamma, S0):
    T = q.shape[1]
    L = min(CHUNK, T)
    j = jnp.arange(L, dtype=f32)
    causal = jnp.tril(jnp.ones((L, L), dtype=bool))
    intra = jnp.where(causal[None], jnp.exp((j[:, None] - j[None, :])[None] * log_gamma[:, None, None]), 0.0)
    q_dec = jnp.exp((j + 1.0)[:, None] * log_gamma[None, :])
    k_dec = jnp.exp((L - 1.0 - j)[:, None] * log_gamma[None, :])
    s_dec = jnp.exp(L * log_gamma)

    def step(S, xs):
        qc, kc, vc = xs
        sc = jnp.einsum('blhd,bshd->bhls', qc, kc) * intra[None]
        o = (jnp.einsum('bhls,bshe->blhe', sc, vc)
             + q_dec[None, :, :, None] * jnp.einsum('blhd,bhde->blhe', qc, S))
        S_new = s_dec[None, :, None, None] * S + jnp.einsum('bshd,bshe,sh->bhde', kc, vc, k_dec)
        return S_new, o

    S, os_ = lax.scan(step, S0, (to_chunks(q, L), to_chunks(k, L), to_chunks(v, L)))
    return from_chunks(os_), S


def layer(x, C, n, m, conv_buf, S, pos0,
          norm_g, w_in, conv_w, conv_b, w_qm, w_km, b_i, b_f,
          hn_m_g, skip_m, hn_r_g, w_pm, w_pr, w_gate, b_gate, w_o):
    B, T, _ = x.shape
    h = rmsnorm(x, norm_g)
    proj = jnp.matmul(h, w_in.astype(f32))
    idx = [W_M, 2 * W_M, 3 * W_M, 3 * W_M + H_M, 3 * W_M + 2 * H_M,
           3 * W_M + 2 * H_M + W_R, 3 * W_M + 2 * H_M + 2 * W_R, 3 * W_M + 2 * H_M + 3 * W_R]
    u_m, v_m, z_m, i_pre, f_pre, q_r, k_r, v_r, z_r = jnp.split(proj, idx, axis=-1)

    c_m, conv_new = causal_conv(u_m, conv_buf, conv_w, conv_b)
    c_m = jax.nn.silu(c_m)
    ch = c_m.reshape(B, T, H_M, DH_M)
    q_m = jnp.einsum('bthd,hde->bthe', ch, w_qm.astype(f32)) * (DH_M ** -0.5)
    k_m = jnp.einsum('bthd,hde->bthe', ch, w_km.astype(f32))
    vh = v_m.reshape(B, T, H_M, DH_M)
    i_g = i_pre + b_i.astype(f32)
    logf = jax.nn.log_sigmoid(f_pre + b_f.astype(f32))
    hm, C_new, n_new, m_new = mlstm_chunkwise(q_m, k_m, vh, i_g, logf,
                                              C.astype(f32), n.astype(f32), m.astype(f32))
    o_m = (head_norm(hm, hn_m_g) + skip_m.astype(f32) * c_m) * jax.nn.silu(z_m)

    pos = pos0 + jnp.arange(T)
    qr = rope(q_r.reshape(B, T, H_R, DH_R), pos)
    kr = rope(k_r.reshape(B, T, H_R, DH_R), pos) * (DH_R ** -0.5)
    vr = v_r.reshape(B, T, H_R, DH_R)
    log_gamma = jnp.log(1.0 - jnp.exp2(-5.0 - jnp.arange(H_R, dtype=f32)))
    hr, S_new = retention_chunkwise(qr, kr, vr, log_gamma, S.astype(f32))
    o_r = head_norm(hr, hn_r_g) * jax.nn.silu(z_r)

    gates = jax.nn.sigmoid(jnp.matmul(h, w_gate.astype(f32)) + b_gate.astype(f32))
    g_m, g_r = jnp.split(gates, 2, axis=-1)
    merged = g_m * jnp.matmul(o_m, w_pm.astype(f32)) + g_r * jnp.matmul(o_r, w_pr.astype(f32))
    y = jnp.matmul(merged, w_o.astype(f32))
    x_new = (x.astype(f32) + y).astype(x.dtype)
    return x_new, C_new, n_new, m_new, conv_new, S_new


def setup_inputs(seed: int = 0) -> dict:
    key = jax.random.key(seed)
    ks = jax.random.split(key, 24)

    def nrm(k, shape, s):
        return jax.random.normal(k, shape, f32) * s

    return {
        "x_prompt": nrm(ks[0], (BATCH, SEQ, D_MODEL), 1.0),
        "x_sample": nrm(ks[1], (DEC_BATCH, DEC_SEQ, D_MODEL), 1.0),
        "state_mlstm_C": nrm(ks[2], (DEPTH, DEC_BATCH, H_M, DH_M, DH_M), 0.05),
        "state_mlstm_n": nrm(ks[3], (DEPTH, DEC_BATCH, H_M, DH_M), 0.5),
        "state_mlstm_m": jax.random.uniform(ks[4], (DEPTH, DEC_BATCH, H_M), f32, 0.0, 2.0),
        "state_mlstm_conv": nrm(ks[5], (DEPTH, DEC_BATCH, CONV_W - 1, W_M), 1.0),
        "state_ret_S": nrm(ks[6], (DEPTH, DEC_BATCH, H_R, DH_R, DH_R), 0.3),
        "norm_g": 1.0 + nrm(ks[7], (DEPTH, D_MODEL), 0.02),
        "w_in": nrm(ks[8], (DEPTH, D_MODEL, N_IN), D_MODEL ** -0.5),
        "conv_w": nrm(ks[9], (DEPTH, CONV_W, W_M), CONV_W ** -0.5),
        "conv_b": nrm(ks[10], (DEPTH, W_M), 0.02),
        "w_qm": nrm(ks[11], (DEPTH, H_M, DH_M, DH_M), DH_M ** -0.5),
        "w_km": nrm(ks[12], (DEPTH, H_M, DH_M, DH_M), DH_M ** -0.5),
        "b_i": nrm(ks[13], (DEPTH, H_M), 0.1),
        "b_f": jnp.linspace(3.0, 6.0, H_M, dtype=f32)[None, :] + nrm(ks[14], (DEPTH, H_M), 0.1),
        "hn_m_g": 1.0 + nrm(ks[15], (DEPTH, W_M), 0.02),
        "skip_m": 1.0 + nrm(ks[16], (DEPTH, W_M), 0.02),
        "hn_r_g": 1.0 + nrm(ks[17], (DEPTH, W_R), 0.02),
        "w_pm": nrm(ks[18], (DEPTH, W_M, D_MODEL), W_M ** -0.5),
        "w_pr": nrm(ks[19], (DEPTH, W_R, D_MODEL), W_R ** -0.5),
        "w_gate": nrm(ks[20], (DEPTH, D_MODEL, 2 * D_MODEL), D_MODEL ** -0.5),
        "b_gate": nrm(ks[21], (DEPTH, 2 * D_MODEL), 0.02),
        "w_o": nrm(ks[22], (DEPTH, D_MODEL, D_MODEL), 0.5 * D_MODEL ** -0.5),
        "final_g": 1.0 + nrm(ks[23], (D_MODEL,), 0.02),
    }


def reference(x_prompt, x_sample, state_mlstm_C, state_mlstm_n, state_mlstm_m, state_mlstm_conv, state_ret_S,
              norm_g, w_in, conv_w, conv_b, w_qm, w_km, b_i, b_f, hn_m_g, skip_m, hn_r_g,
              w_pm, w_pr, w_gate, b_gate, w_o, final_g):
    xp = x_prompt
    xs = x_sample
    Bp = x_prompt.shape[0]
    Cp0 = jnp.zeros((Bp, H_M, DH_M, DH_M), f32)
    np0 = jnp.zeros((Bp, H_M, DH_M), f32)
    mp0 = jnp.full((Bp, H_M), NEG, f32)
    convp0 = jnp.zeros((Bp, CONV_W - 1, W_M), f32)
    Sp0 = jnp.zeros((Bp, H_R, DH_R, DH_R), f32)
    outs_p = [[], [], [], [], []]
    outs_s = [[], [], [], [], []]
    for l in range(DEPTH):
        lw = (norm_g[l], w_in[l], conv_w[l], conv_b[l], w_qm[l], w_km[l], b_i[l], b_f[l],
              hn_m_g[l], skip_m[l], hn_r_g[l], w_pm[l], w_pr[l], w_gate[l], b_gate[l], w_o[l])
        xp, *st_p = layer(xp, Cp0, np0, mp0, convp0, Sp0, 0, *lw)
        xs, *st_s = layer(xs, state_mlstm_C[l], state_mlstm_n[l], state_mlstm_m[l],
                          state_mlstm_conv[l], state_ret_S[l], PAST_LEN, *lw)
        for j in range(5):
            outs_p[j].append(st_p[j])
            outs_s[j].append(st_s[j])
    y_prompt = rmsnorm(xp, final_g).astype(x_prompt.dtype)
    y_sample = rmsnorm(xs, final_g).astype(x_sample.dtype)
    new_C_p = jnp.stack(outs_p[0])
    new_n_p = jnp.stack(outs_p[1])
    new_m_p = jnp.stack(outs_p[2])
    new_conv_p = jnp.stack(outs_p[3])
    new_S_p = jnp.stack(outs_p[4])
    new_C_s = jnp.stack(outs_s[0])
    new_n_s = jnp.stack(outs_s[1])
    new_m_s = jnp.stack(outs_s[2])
    new_conv_s = jnp.stack(outs_s[3])
    new_S_s = jnp.stack(outs_s[4])
    return (y_prompt, y_sample, new_C_p, new_n_p, new_m_p, new_conv_p, new_S_p,
            new_C_s, new_n_s, new_m_s, new_conv_s, new_S_s)
```

```python
import functools
import math

import jax
import jax.numpy as jnp
import numpy as np
from jax import lax
from jax.experimental import pallas as pl
from jax.experimental.pallas import tpu as pltpu

f32 = jnp.float32
bf16 = jnp.bfloat16

D_MODEL = 2048
N_HEADS = 4
D_HEAD = 256
W_MIX = N_HEADS * D_HEAD
CONV_W = 4
CHUNK = 128
ROPE_BASE = 10000.0
EPS = 1e-6
NEG = -1e30
PAST_LEN = 16384
LANES = 128
HALF = D_HEAD // 2
VMEM_LIMIT = 56 * 1024 * 1024

OFF_IF = 3 * W_MIX
OFF_RET = 3 * W_MIX + 2 * N_HEADS


def _pick(n, cands):
    for c in cands:
        if n % c == 0:
            return c
    return n


def _cparams(sem):
    return pltpu.CompilerParams(dimension_semantics=sem, vmem_limit_bytes=VMEM_LIMIT)


def _dot(a, b):
    return jnp.dot(a, b, preferred_element_type=f32)


def _dot_nt(a, b):
    return lax.dot_general(a, b, (((1,), (1,)), ((), ())), preferred_element_type=f32)


def _dot_exact(a, b):
    return jnp.dot(a, b, preferred_element_type=f32, precision=lax.Precision.HIGHEST)


def _silu(x):
    return x * jax.nn.sigmoid(x)


def _log_sigmoid(x):
    return jnp.minimum(x, 0.0) - jnp.log1p(jnp.exp(-jnp.abs(x)))


def _head_norm(hc, g):
    mu = jnp.mean(hc, axis=-1, keepdims=True)
    d = hc - mu
    var = jnp.mean(d * d, axis=-1, keepdims=True)
    return d * lax.rsqrt(var + EPS) * g


def _rope_kernel(inv_ref, invT_ref, cos_ref, sin_ref, cosT_ref, sinT_ref,
                 cosd_ref, sind_ref, cosdT_ref, sindT_ref):
    T = cos_ref.shape[0]
    pos = lax.broadcasted_iota(jnp.int32, (T, HALF), 0).astype(f32)
    ang = pos * inv_ref[...]
    cos_ref[...] = jnp.cos(ang)
    sin_ref[...] = jnp.sin(ang)
    posT = lax.broadcasted_iota(jnp.int32, (HALF, T), 1).astype(f32)
    angT = posT * invT_ref[...]
    cosT_ref[...] = jnp.cos(angT)
    sinT_ref[...] = jnp.sin(angT)
    angd = jnp.full((8, HALF), float(PAST_LEN), f32) * inv_ref[...]
    cosd_ref[...] = jnp.cos(angd)
    sind_ref[...] = jnp.sin(angd)
    angdT = jnp.full((HALF, LANES), float(PAST_LEN), f32) * invT_ref[...]
    cosdT_ref[...] = jnp.cos(angdT)
    sindT_ref[...] = jnp.sin(angdT)


def _rope_tables(T):
    inv = ROPE_BASE ** (-jnp.arange(0, HALF, dtype=f32) / HALF)
    shapes = [(T, HALF), (T, HALF), (HALF, T), (HALF, T),
              (8, HALF), (8, HALF), (HALF, LANES), (HALF, LANES)]
    return pl.pallas_call(
        _rope_kernel,
        out_shape=[jax.ShapeDtypeStruct(s, f32) for s in shapes],
        name="rope_tables",
    )(inv.reshape(1, HALF), inv.reshape(HALF, 1))


def _rms_kernel(x_ref, g_ref, o_ref):
    x = x_ref[...]
    ms = jnp.mean(x * x, axis=-1, keepdims=True)
    o_ref[...] = (x * lax.rsqrt(ms + EPS) * g_ref[...]).astype(o_ref.dtype)


def _rmsnorm(x, g, out_dtype, row0=0, nrows=None):
    R, D = x.shape
    nrows = R if nrows is None else nrows
    tr = _pick(nrows, (832, 512, 256, 128))
    assert row0 % tr == 0
    off = row0 // tr
    return pl.pallas_call(
        _rms_kernel,
        grid=(nrows // tr,),
        in_specs=[pl.BlockSpec((tr, D), lambda i: (i + off, 0)),
                  pl.BlockSpec((1, D), lambda i: (0, 0))],
        out_specs=pl.BlockSpec((tr, D), lambda i: (i, 0)),
        out_shape=jax.ShapeDtypeStruct((nrows, D), out_dtype),
        compiler_params=_cparams(("arbitrary",)),
        name="rmsnorm",
    )(x, g)


_TN_IN = 512
_NJ_A = 3 * W_MIX // _TN_IN
_NJ_B = 3 * W_MIX // _TN_IN
_NJ_K = W_MIX // _TN_IN


def _inproj_kernel(h_ref, win_ref, wret_ref, wkT_ref, wif_ref, p_ref, kt_ref, gi_ref):
    j = pl.program_id(1)
    h = h_ref[...]

    @pl.when(j == 0)
    def _():
        gi_ref[...] = _dot(h, wif_ref[...])

    @pl.when(j < _NJ_A)
    def _():
        p_ref[...] = _dot(h, win_ref[...].astype(bf16)).astype(bf16)

    @pl.when((j >= _NJ_A) & (j < _NJ_A + _NJ_B))
    def _():
        p_ref[...] = _dot(h, wret_ref[...]).astype(bf16)

    @pl.when(j >= _NJ_A + _NJ_B)
    def _():
        kt_ref[...] = _dot_nt(wkT_ref[...], h).astype(bf16)


def _inproj(h, w_in, w_ret, w_krT, w_if, l):
    R = h.shape[0]
    tm = _pick(R, (1664, 1280, 640, 384, 128))
    tn = _TN_IN
    nj = _NJ_A + _NJ_B + _NJ_K
    tpg = W_MIX // tn

    def ret_tile(j):
        jj = jnp.clip(j - _NJ_A, 0, _NJ_B - 1)
        return jnp.where(jj < tpg, jj, jj + tpg)

    return pl.pallas_call(
        _inproj_kernel,
        grid=(R // tm, nj),
        in_specs=[
            pl.BlockSpec((tm, D_MODEL), lambda i, j: (i, 0)),
            pl.BlockSpec((None, D_MODEL, tn), lambda i, j: (l, 0, jnp.minimum(j, _NJ_A - 1))),
            pl.BlockSpec((None, D_MODEL, tn), lambda i, j: (l, 0, ret_tile(j))),
            pl.BlockSpec((None, tn, D_MODEL),
                         lambda i, j: (l, jnp.clip(j - _NJ_A - _NJ_B, 0, _NJ_K - 1), 0)),
            pl.BlockSpec((None, D_MODEL, LANES), lambda i, j: (l, 0, 0)),
        ],
        out_specs=[
            pl.BlockSpec((tm, tn), lambda i, j: (i, jnp.minimum(j, _NJ_A + _NJ_B - 1))),
            pl.BlockSpec((tn, tm), lambda i, j: (jnp.clip(j - _NJ_A - _NJ_B, 0, _NJ_K - 1), i)),
            pl.BlockSpec((tm, LANES), lambda i, j: (i, 0)),
        ],
        out_shape=[
            jax.ShapeDtypeStruct((R, 6 * W_MIX), bf16),
            jax.ShapeDtypeStruct((W_MIX, R), bf16),
            jax.ShapeDtypeStruct((R, LANES), f32),
        ],
        compiler_params=_cparams(("arbitrary", "arbitrary")),
        name="inproj",
    )(h, w_in, w_ret, w_krT, w_if)


def _mlstm_kernel(um_ref, vm_ref, zm_ref, gi_ref, convw_ref, convb_ref, wq_ref, wkT_ref,
                  brow_ref, bcol_ref, hng_ref, skip_ref,
                  om_ref, c_out_ref, n_out_ref, m_out_ref, conv_out_ref,
                  ubuf, cmbuf, ct_s, n_s, m_s):
    g = pl.program_id(1)
    ng = pl.num_programs(1)
    Lg = um_ref.shape[0]
    L = CHUNK
    scale = D_HEAD ** -0.5

    @pl.when(g == 0)
    def _():
        ubuf[0:8, :] = jnp.zeros((8, W_MIX), f32)
        ct_s[...] = jnp.zeros(ct_s.shape, f32)
        n_s[...] = jnp.zeros(n_s.shape, f32)
        m_s[...] = jnp.full(m_s.shape, NEG, f32)

    ubuf[8:8 + Lg, :] = um_ref[...].astype(f32)
    acc = convb_ref[...] + ubuf[5:5 + Lg, :] * convw_ref[0:1, :]
    for i in range(1, CONV_W):
        acc = acc + ubuf[5 + i:5 + i + Lg, :] * convw_ref[i:i + 1, :]
    cmbuf[...] = _silu(acc)
    last3 = ubuf[5 + Lg:8 + Lg, :]
    ubuf[5:8, :] = last3

    row = lax.broadcasted_iota(jnp.int32, (L, L), 0)
    col = lax.broadcasted_iota(jnp.int32, (L, L), 1)
    causal = row >= col
    tril = causal.astype(f32)
    triu = (row <= col).astype(f32)
    brow = brow_ref[...]
    bcol = bcol_ref[...]

    for c in range(Lg // L):
        rows = slice(c * L, (c + 1) * L)
        gb = gi_ref[rows, :] + brow
        gtb = gi_ref[rows, :].T[0:8, :] + bcol
        lf = _log_sigmoid(gb)
        lft = _log_sigmoid(gtb)
        bc_all = _dot_exact(tril, lf)
        br_all = _dot_exact(lft, triu)
        for h in range(N_HEADS):
            cs = slice(h * D_HEAD, (h + 1) * D_HEAD)
            ch = cmbuf[rows, cs]
            chb = ch.astype(bf16)
            q = _dot(chb, wq_ref[h]) * scale
            qb = q.astype(bf16)
            kTb = _dot_nt(wkT_ref[h], chb).astype(bf16)
            v = vm_ref[rows, cs]
            bc = bc_all[:, 4 + h:5 + h]
            ic = gb[:, h:h + 1]
            br = br_all[4 + h:5 + h, :]
            ir = gtb[h:h + 1, :]
            m_prev = m_s[h:h + 1, 0:1]
            dlog = jnp.where(causal, bc - br + ir, NEG)
            inter = bc + m_prev
            m_t = jnp.maximum(inter, jnp.max(dlog, axis=-1, keepdims=True))
            w = jnp.exp(dlog - m_t)
            a = jnp.exp(inter - m_t)
            ws = w * _dot(qb, kTb)
            ct = ct_s[h]
            num = _dot(ws.astype(bf16), v) + a * _dot(qb, ct.astype(bf16))
            nrow = n_s[h]
            den = jnp.sum(ws, axis=-1, keepdims=True) + a * jnp.sum(q * nrow, axis=-1, keepdims=True)
            den = jnp.maximum(jnp.abs(den), jnp.exp(-m_t))
            hc = num / den
            m_new = m_t[L - 1:L, :]
            b_last = bc[L - 1:L, :]
            wk_c = jnp.exp(b_last - bc + ic - m_new)
            wk_r = jnp.exp(b_last - br + ir - m_new)
            dec = jnp.exp(b_last + m_prev - m_new)
            ct_s[h] = dec * ct + _dot(kTb, (wk_c * v.astype(f32)).astype(bf16))
            wk8 = jnp.broadcast_to(wk_r, (8, L)).astype(bf16)
            n_s[h] = dec * nrow + _dot_nt(wk8, kTb)[0:1, :]
            m_s[h:h + 1, :] = jnp.broadcast_to(m_new, (1, LANES))
            hn = _head_norm(hc, hng_ref[:, cs])
            z = zm_ref[rows, cs].astype(f32)
            om_ref[rows, cs] = ((hn + skip_ref[:, cs] * ch) * _silu(z)).astype(bf16)

    @pl.when(g == ng - 1)
    def _():
        for h in range(N_HEADS):
            c_out_ref[h] = ct_s[h].T
            n_out_ref[h] = n_s[h]
        m_out_ref[...] = m_s[...]
        conv_out_ref[...] = last3


def _mlstm_prompt(P, GI, B, T, conv_w, conv_b, wq, wkT, brow, bcol, hng, skip):
    R = P.shape[0]
    Lg = _pick(T, (512, 256, 128))
    ng = T // Lg
    col = lambda k: (lambda b, g: (b * ng + g, k))
    rep2 = lambda b, g: (0, 0)
    rep3 = lambda b, g: (0, 0, 0)
    return pl.pallas_call(
        _mlstm_kernel,
        grid=(B, ng),
        in_specs=[
            pl.BlockSpec((Lg, W_MIX), col(0)),
            pl.BlockSpec((Lg, W_MIX), col(1)),
            pl.BlockSpec((Lg, W_MIX), col(2)),
            pl.BlockSpec((Lg, LANES), col(0)),
            pl.BlockSpec((CONV_W, W_MIX), rep2),
            pl.BlockSpec((1, W_MIX), rep2),
            pl.BlockSpec((N_HEADS, D_HEAD, D_HEAD), rep3),
            pl.BlockSpec((N_HEADS, D_HEAD, D_HEAD), rep3),
            pl.BlockSpec((1, LANES), rep2),
            pl.BlockSpec((8, 1), rep2),
            pl.BlockSpec((1, W_MIX), rep2),
            pl.BlockSpec((1, W_MIX), rep2),
        ],
        out_specs=[
            pl.BlockSpec((Lg, W_MIX), col(0)),
            pl.BlockSpec((None, N_HEADS, D_HEAD, D_HEAD), lambda b, g: (b, 0, 0, 0)),
            pl.BlockSpec((None, N_HEADS, 1, D_HEAD), lambda b, g: (b, 0, 0, 0)),
            pl.BlockSpec((None, 8, LANES), lambda b, g: (b, 0, 0)),
            pl.BlockSpec((None, CONV_W - 1, W_MIX), lambda b, g: (b, 0, 0)),
        ],
        out_shape=[
            jax.ShapeDtypeStruct((R, W_MIX), bf16),
            jax.ShapeDtypeStruct((B, N_HEADS, D_HEAD, D_HEAD), f32),
            jax.ShapeDtypeStruct((B, N_HEADS, 1, D_HEAD), f32),
            jax.ShapeDtypeStruct((B, 8, LANES), f32),
            jax.ShapeDtypeStruct((B, CONV_W - 1, W_MIX), f32),
        ],
        scratch_shapes=[
            pltpu.VMEM((8 + Lg, W_MIX), f32),
            pltpu.VMEM((Lg, W_MIX), f32),
            pltpu.VMEM((N_HEADS, D_HEAD, D_HEAD), f32),
            pltpu.VMEM((N_HEADS, 1, D_HEAD), f32),
            pltpu.VMEM((8, LANES), f32),
        ],
        compiler_params=_cparams(("arbitrary", "arbitrary")),
        name="mlstm_prompt",
    )(P, P, P, GI, conv_w, conv_b, wq, wkT, brow, bcol, hng, skip)


def _log_gammas():
    return [math.log(1.0 - 2.0 ** (-5.0 - h)) for h in range(N_HEADS)]


def _ret_kernel(q_ref, v_ref, z_ref, kt_ref, cos_ref, sin_ref, cosT_ref, sinT_ref, hng_ref,
                o_ref, s_out_ref, s_s):
    g = pl.program_id(1)
    ng = pl.num_programs(1)
    Lg = q_ref.shape[0]
    L = CHUNK

    @pl.when(g == 0)
    def _():
        s_s[...] = jnp.zeros(s_s.shape, f32)

    row = lax.broadcasted_iota(jnp.int32, (L, L), 0)
    col = lax.broadcasted_iota(jnp.int32, (L, L), 1)
    causal = row >= col
    diff = (row - col).astype(f32)
    tcol = lax.broadcasted_iota(jnp.int32, (L, 1), 0).astype(f32)
    lgs = _log_gammas()
    intra = [jnp.where(causal, jnp.exp(diff * lg), 0.0) for lg in lgs]
    q_dec = [jnp.exp((tcol + 1.0) * lg) for lg in lgs]
    k_dec = [jnp.exp((L - 1.0 - tcol) * lg) for lg in lgs]
    s_dec = [math.exp(L * lg) for lg in lgs]
    kscale = D_HEAD ** -0.5

    for c in range(Lg // L):
        rows = slice(c * L, (c + 1) * L)
        cos = cos_ref[rows, :]
        sin = sin_ref[rows, :]
        cosT = cosT_ref[:, rows]
        sinT = sinT_ref[:, rows]
        for h in range(N_HEADS):
            cs = slice(h * D_HEAD, (h + 1) * D_HEAD)
            q1 = q_ref[rows, h * D_HEAD:h * D_HEAD + HALF].astype(f32)
            q2 = q_ref[rows, h * D_HEAD + HALF:(h + 1) * D_HEAD].astype(f32)
            qb = jnp.concatenate([q1 * cos - q2 * sin, q1 * sin + q2 * cos], axis=-1).astype(bf16)
            k1 = kt_ref[h * D_HEAD:h * D_HEAD + HALF, rows].astype(f32)
            k2 = kt_ref[h * D_HEAD + HALF:(h + 1) * D_HEAD, rows].astype(f32)
            kTb = (jnp.concatenate([k1 * cosT - k2 * sinT, k1 * sinT + k2 * cosT], axis=0)
                   * kscale).astype(bf16)
            v = v_ref[rows, cs]
            sc = _dot(qb, kTb) * intra[h]
            s = s_s[h]
            o = _dot(sc.astype(bf16), v) + q_dec[h] * _dot(qb, s.astype(bf16))
            s_s[h] = s_dec[h] * s + _dot(kTb, (v.astype(f32) * k_dec[h]).astype(bf16))
            z = z_ref[rows, cs].astype(f32)
            o_ref[rows, cs] = (_head_norm(o, hng_ref[:, cs]) * _silu(z)).astype(bf16)

    @pl.when(g == ng - 1)
    def _():
        s_out_ref[...] = s_s[...]


def _ret_prompt(P, KT, B, T, tabs, hng):
    R = P.shape[0]
    Lg = _pick(T, (512, 256, 128))
    ng = T // Lg
    col = lambda k: (lambda b, g: (b * ng + g, k))
    cos, sin, cosT, sinT = tabs
    return pl.pallas_call(
        _ret_kernel,
        grid=(B, ng),
        in_specs=[
            pl.BlockSpec((Lg, W_MIX), col(3)),
            pl.BlockSpec((Lg, W_MIX), col(4)),
            pl.BlockSpec((Lg, W_MIX), col(5)),
            pl.BlockSpec((W_MIX, Lg), lambda b, g: (0, b * ng + g)),
            pl.BlockSpec((Lg, HALF), lambda b, g: (g, 0)),
            pl.BlockSpec((Lg, HALF), lambda b, g: (g, 0)),
            pl.BlockSpec((HALF, Lg), lambda b, g: (0, g)),
            pl.BlockSpec((HALF, Lg), lambda b, g: (0, g)),
            pl.BlockSpec((1, W_MIX), lambda b, g: (0, 0)),
        ],
        out_specs=[
            pl.BlockSpec((Lg, W_MIX), col(0)),
            pl.BlockSpec((None, N_HEADS, D_HEAD, D_HEAD), lambda b, g: (b, 0, 0, 0)),
        ],
        out_shape=[
            jax.ShapeDtypeStruct((R, W_MIX), bf16),
            jax.ShapeDtypeStruct((B, N_HEADS, D_HEAD, D_HEAD), f32),
        ],
        scratch_shapes=[pltpu.VMEM((N_HEADS, D_HEAD, D_HEAD), f32)],
        compiler_params=_cparams(("arbitrary", "arbitrary")),
        name="ret_prompt",
    )(P, P, P, KT, cos, sin, cosT, sinT, hng)


_SEQ_PER_STEP = 4


def _decode_kernel(um_ref, vm_ref, zm_ref, qr_ref, vr_ref, zr_ref, kt_ref, gi_ref,
                   conv_ref, n_ref, m_ref, c_ref, s_ref,
                   convw_ref, convb_ref, wq_ref, wk_ref, brow_ref, hngm_ref, skip_ref, hngr_ref,
                   cosd_ref, sind_ref, cosdT_ref, sindT_ref,
                   om_in_ref, or_in_ref, cprev_ref, sprev_ref,
                   om_ref, or_ref, c_out_ref, s_out_ref, n_out_ref, m_out_ref, conv_out_ref,
                   vt_s, qrt_s, krt_s, qm_s, km_s, vr_s, sct_s, ht_s, oacc_s, cm_s):
    del om_in_ref, or_in_ref, cprev_ref, sprev_ref
    s = pl.program_id(0)
    ns = pl.num_programs(0)
    NB = um_ref.shape[0]
    scale = D_HEAD ** -0.5
    lgs = _log_gammas()
    gammas = [math.exp(lg) for lg in lgs]
    lane = lax.broadcasted_iota(jnp.int32, (1, LANES), 1)

    @pl.when(s == 0)
    def _():
        u = um_ref[...].astype(f32)
        acc = convb_ref[...] + u * convw_ref[CONV_W - 1:CONV_W, :]
        for i in range(CONV_W - 1):
            acc = acc + conv_ref[:, i * W_MIX:(i + 1) * W_MIX] * convw_ref[i:i + 1, :]
        cm = _silu(acc)
        cm_s[...] = cm
        conv_out_ref[:, 0:W_MIX] = conv_ref[:, W_MIX:2 * W_MIX]
        conv_out_ref[:, W_MIX:2 * W_MIX] = conv_ref[:, 2 * W_MIX:3 * W_MIX]
        conv_out_ref[:, 2 * W_MIX:3 * W_MIX] = u

        gb = gi_ref[...] + brow_ref[...]
        lf = _log_sigmoid(gb)
        sc = jnp.zeros((NB, LANES), f32)
        mo = jnp.zeros((NB, LANES), f32)
        for h in range(N_HEADS):
            cs = slice(h * D_HEAD, (h + 1) * D_HEAD)
            chb = cm[:, cs].astype(bf16)
            q = _dot(chb, wq_ref[h]) * scale
            k = _dot(chb, wk_ref[h])
            qm_s[:, cs] = q
            km_s[:, cs] = k
            ig = gb[:, h:h + 1]
            lfh = lf[:, 4 + h:5 + h]
            mp = m_ref[:, h:h + 1]
            inter = lfh + mp
            m_t = jnp.maximum(inter, ig)
            w = jnp.exp(ig - m_t)
            a = jnp.exp(inter - m_t)
            nh = n_ref[:, cs]
            ws = w * jnp.sum(q * k, axis=-1, keepdims=True)
            den = ws + a * jnp.sum(nh * q, axis=-1, keepdims=True)
            den = jnp.maximum(jnp.abs(den), jnp.exp(-m_t))
            n_out_ref[:, cs] = a * nh + w * k
            for r, val in ((h, ws), (4 + h, a), (8 + h, den), (12 + h, w)):
                sc = sc + val * (lane == r).astype(f32)
            mo = mo + m_t * (lane == h).astype(f32)
        m_out_ref[...] = mo
        sct_s[...] = sc.T
        vt_s[...] = vm_ref[...].astype(f32).T
        vr_s[...] = vr_ref[...].astype(f32)
        cosd = cosd_ref[0:1, :]
        sind = sind_ref[0:1, :]
        cosdT = cosdT_ref[...]
        sindT = sindT_ref[...]
        for h in range(N_HEADS):
            lo = slice(h * D_HEAD, h * D_HEAD + HALF)
            hi = slice(h * D_HEAD + HALF, (h + 1) * D_HEAD)
            q1 = qr_ref[:, lo].astype(f32)
            q2 = qr_ref[:, hi].astype(f32)
            qrot = jnp.concatenate([q1 * cosd - q2 * sind, q1 * sind + q2 * cosd], axis=-1)
            qrt_s[h * D_HEAD:(h + 1) * D_HEAD, :] = qrot.T
            k1 = kt_ref[lo, :].astype(f32)
            k2 = kt_ref[hi, :].astype(f32)
            krt_s[lo, :] = (k1 * cosdT - k2 * sindT) * scale
            krt_s[hi, :] = (k1 * sindT + k2 * cosdT) * scale
        ht_s[...] = jnp.zeros(ht_s.shape, f32)

    def take_col(mat, mask):
        return jnp.sum(jnp.where(mask, mat, 0.0), axis=-1, keepdims=True)

    for j in range(_SEQ_PER_STEP):
        b = s * _SEQ_PER_STEP + j
        mask = lane == b
        for h in range(N_HEADS):
            cs = slice(h * D_HEAD, (h + 1) * D_HEAD)
            c = c_ref[j, h]
            q_row = qm_s[pl.ds(b, 1), cs]
            k_row = km_s[pl.ds(b, 1), cs]
            v_col = take_col(vt_s[cs, :], mask)
            ws = take_col(sct_s[h:h + 1, :], mask)
            a = take_col(sct_s[4 + h:5 + h, :], mask)
            den = take_col(sct_s[8 + h:9 + h, :], mask)
            w = take_col(sct_s[12 + h:13 + h, :], mask)
            cq = jnp.sum(c * q_row, axis=-1, keepdims=True)
            hcol = (ws * v_col + a * cq) / den
            ht_s[cs, :] = jnp.where(mask, hcol, ht_s[cs, :])
            c_out_ref[j, h] = a * c + (w * v_col) * k_row
            st = s_ref[j, h]
            q_col = take_col(qrt_s[cs, :], mask)
            k_col = take_col(krt_s[cs, :], mask)
            v_row = vr_s[pl.ds(b, 1), cs]
            qk = jnp.sum(q_col * k_col, axis=0, keepdims=True)
            o_row = qk * v_row + gammas[h] * jnp.sum(st * q_col, axis=0, keepdims=True)
            oacc_s[pl.ds(b, 1), cs] = o_row
            s_out_ref[j, h] = gammas[h] * st + k_col * v_row

    @pl.when(s == ns - 1)
    def _():
        hm = ht_s[...].T
        for h in range(N_HEADS):
            cs = slice(h * D_HEAD, (h + 1) * D_HEAD)
            hn = _head_norm(hm[:, cs], hngm_ref[:, cs])
            zm = zm_ref[:, cs].astype(f32)
            om_ref[:, cs] = ((hn + skip_ref[:, cs] * cm_s[:, cs]) * _silu(zm)).astype(bf16)
            zr = zr_ref[:, cs].astype(f32)
            or_ref[:, cs] = (_head_norm(oacc_s[:, cs], hngr_ref[:, cs]) * _silu(zr)).astype(bf16)


def _decode_step(P, KT, GI, OM, OR, Cprev, Sprev, l, BT, conv_st, n_st, m_st, C_st, S_st,
                 conv_w, conv_b, wq, wk, brow, hngm, skip, hngr, dtabs):
    R = P.shape[0]
    NB = R - BT
    assert NB == LANES and BT % NB == 0 and NB % _SEQ_PER_STEP == 0
    rb = BT // NB
    depth = C_st.shape[0]
    bb = _SEQ_PER_STEP
    cosd, sind, cosdT, sindT = dtabs
    pcol = lambda k: (lambda s: (rb, k))
    rep2 = lambda s: (0, 0)
    rep3 = lambda s: (0, 0, 0)
    st_spec = pl.BlockSpec((None, bb, N_HEADS, D_HEAD, D_HEAD), lambda s: (l, s, 0, 0, 0))
    any_spec = pl.BlockSpec(memory_space=pl.ANY)
    in_specs = [
        pl.BlockSpec((NB, W_MIX), pcol(0)), pl.BlockSpec((NB, W_MIX), pcol(1)),
        pl.BlockSpec((NB, W_MIX), pcol(2)), pl.BlockSpec((NB, W_MIX), pcol(3)),
        pl.BlockSpec((NB, W_MIX), pcol(4)), pl.BlockSpec((NB, W_MIX), pcol(5)),
        pl.BlockSpec((W_MIX, NB), lambda s: (0, rb)),
        pl.BlockSpec((NB, LANES), lambda s: (rb, 0)),
        pl.BlockSpec((None, NB, (CONV_W - 1) * W_MIX), lambda s: (l, 0, 0)),
        pl.BlockSpec((None, NB, W_MIX), lambda s: (l, 0, 0)),
        pl.BlockSpec((None, NB, N_HEADS), lambda s: (l, 0, 0)),
        st_spec, st_spec,
        pl.BlockSpec((CONV_W, W_MIX), rep2), pl.BlockSpec((1, W_MIX), rep2),
        pl.BlockSpec((N_HEADS, D_HEAD, D_HEAD), rep3), pl.BlockSpec((N_HEADS, D_HEAD, D_HEAD), rep3),
        pl.BlockSpec((1, LANES), rep2),
        pl.BlockSpec((1, W_MIX), rep2), pl.BlockSpec((1, W_MIX), rep2), pl.BlockSpec((1, W_MIX), rep2),
        pl.BlockSpec((8, HALF), rep2), pl.BlockSpec((8, HALF), rep2),
        pl.BlockSpec((HALF, LANES), rep2), pl.BlockSpec((HALF, LANES), rep2),
        any_spec, any_spec,
    ]
    args = [P, P, P, P, P, P, KT, GI, conv_st, n_st, m_st, C_st, S_st,
            conv_w, conv_b, wq, wk, brow, hngm, skip, hngr, cosd, sind, cosdT, sindT, OM, OR]
    aliases = {len(args) - 2: 0, len(args) - 1: 1}
    if Cprev is not None:
        in_specs += [any_spec, any_spec]
        args += [Cprev, Sprev]
        aliases[len(args) - 2] = 2
        aliases[len(args) - 1] = 3
        kern = _decode_kernel
    else:
        kern = functools.partial(_decode_kernel_first)
    out_row = pl.BlockSpec((NB, W_MIX), lambda s: (rb, 0))
    return pl.pallas_call(
        kern,
        grid=(NB // bb,),
        in_specs=in_specs,
        out_specs=[
            out_row, out_row, st_spec, st_spec,
            pl.BlockSpec((NB, W_MIX), rep2),
            pl.BlockSpec((NB, LANES), rep2),
            pl.BlockSpec((NB, (CONV_W - 1) * W_MIX), rep2),
        ],
        out_shape=[
            jax.ShapeDtypeStruct(OM.shape, bf16),
            jax.ShapeDtypeStruct(OR.shape, bf16),
            jax.ShapeDtypeStruct((depth, NB, N_HEADS, D_HEAD, D_HEAD), f32),
            jax.ShapeDtypeStruct((depth, NB, N_HEADS, D_HEAD, D_HEAD), f32),
            jax.ShapeDtypeStruct((NB, W_MIX), f32),
            jax.ShapeDtypeStruct((NB, LANES), f32),
            jax.ShapeDtypeStruct((NB, (CONV_W - 1) * W_MIX), f32),
        ],
        scratch_shapes=[
            pltpu.VMEM((W_MIX, NB), f32),
            pltpu.VMEM((W_MIX, NB), f32),
            pltpu.VMEM((W_MIX, NB), f32),
            pltpu.VMEM((NB, W_MIX), f32),
            pltpu.VMEM((NB, W_MIX), f32),
            pltpu.VMEM((NB, W_MIX), f32),
            pltpu.VMEM((LANES, NB), f32),
            pltpu.VMEM((W_MIX, NB), f32),
            pltpu.VMEM((NB, W_MIX), f32),
            pltpu.VMEM((NB, W_MIX), f32),
        ],
        input_output_aliases=aliases,
        compiler_params=_cparams(("arbitrary",)),
        name="decode_step",
    )(*args)


def _decode_kernel_first(*refs):
    n_in = 27
    ins, rest = refs[:n_in], refs[n_in:]
    return _decode_kernel(*ins, None, None, *rest)


def _merge_kernel(h_ref, om_ref, or_ref, wgm_ref, wgr_ref, bgm_ref, bgr_ref, wpm_ref, wpr_ref, o_ref):
    h = h_ref[...]
    gm = jax.nn.sigmoid(_dot(h, wgm_ref[...].astype(bf16)) + bgm_ref[...])
    gr = jax.nn.sigmoid(_dot(h, wgr_ref[...].astype(bf16)) + bgr_ref[...])
    pm = _dot(om_ref[...], wpm_ref[...].astype(bf16))
    pr = _dot(or_ref[...], wpr_ref[...].astype(bf16))
    o_ref[...] = (gm * pm + gr * pr).astype(bf16)


def _merge(h, OM, OR, w_gate, b_gate3, w_pm, w_pr, l):
    R = h.shape[0]
    tm = _pick(R, (1664, 1280, 640, 384, 128))
    tn = 256
    nj = D_MODEL // tn
    return pl.pallas_call(
        _merge_kernel,
        grid=(R // tm, nj),
        in_specs=[
            pl.BlockSpec((tm, D_MODEL), lambda i, j: (i, 0)),
            pl.BlockSpec((tm, W_MIX), lambda i, j: (i, 0)),
            pl.BlockSpec((tm, W_MIX), lambda i, j: (i, 0)),
            pl.BlockSpec((None, D_MODEL, tn), lambda i, j: (l, 0, j)),
            pl.BlockSpec((None, D_MODEL, tn), lambda i, j: (l, 0, j + nj)),
            pl.BlockSpec((None, 1, tn), lambda i, j: (l, 0, j)),
            pl.BlockSpec((None, 1, tn), lambda i, j: (l, 0, j + nj)),
            pl.BlockSpec((None, W_MIX, tn), lambda i, j: (l, 0, j)),
            pl.BlockSpec((None, W_MIX, tn), lambda i, j: (l, 0, j)),
        ],
        out_specs=pl.BlockSpec((tm, tn), lambda i, j: (i, j)),
        out_shape=jax.ShapeDtypeStruct((R, D_MODEL), bf16),
        compiler_params=_cparams(("arbitrary", "arbitrary")),
        name="merge",
    )(h, OM, OR, w_gate, w_gate, b_gate3, b_gate3, w_pm, w_pr)


def _outproj_kernel(mg_ref, wo_ref, x_ref, o_ref):
    o_ref[...] = x_ref[...] + _dot(mg_ref[...], wo_ref[...].astype(bf16))


def _outproj(merged, w_o, x, l):
    R = x.shape[0]
    tm = _pick(R, (1664, 1280, 640, 384, 128))
    tn = 512
    return pl.pallas_call(
        _outproj_kernel,
        grid=(R // tm, D_MODEL // tn),
        in_specs=[
            pl.BlockSpec((tm, D_MODEL), lambda i, j: (i, 0)),
            pl.BlockSpec((None, D_MODEL, tn), lambda i, j: (l, 0, j)),
            pl.BlockSpec((tm, tn), lambda i, j: (i, j)),
        ],
        out_specs=pl.BlockSpec((tm, tn), lambda i, j: (i, j)),
        out_shape=jax.ShapeDtypeStruct((R, D_MODEL), f32),
        compiler_params=_cparams(("arbitrary", "arbitrary")),
        name="outproj",
    )(merged, w_o, x)


def kernel(x_prompt, x_sample, state_mlstm_C, state_mlstm_n, state_mlstm_m, state_mlstm_conv, state_ret_S,
           norm_g, w_in, conv_w, conv_b, w_qm, w_km, b_i, b_f, hn_m_g, skip_m, hn_r_g,
           w_pm, w_pr, w_gate, b_gate, w_o, final_g):
    B, T, D = x_prompt.shape
    NB = x_sample.shape[0]
    depth = w_in.shape[0]
    BT = B * T
    assert D == D_MODEL and x_sample.shape[1] == 1 and T % CHUNK == 0

    x = jnp.concatenate([x_prompt.reshape(BT, D), x_sample.reshape(NB, D)], axis=0)

    w_ret = w_in[:, :, OFF_RET:].astype(bf16)
    w_krT = jnp.swapaxes(w_in[:, :, OFF_RET + W_MIX:OFF_RET + 2 * W_MIX], 1, 2).astype(bf16)
    w_if = jnp.pad(w_in[:, :, OFF_IF:OFF_RET], ((0, 0), (0, 0), (0, LANES - 2 * N_HEADS))).astype(bf16)
    wq = w_qm.astype(bf16)
    wk = w_km.astype(bf16)
    wkT = jnp.swapaxes(w_km, 2, 3).astype(bf16)
    bias = jnp.pad(jnp.concatenate([b_i, b_f], axis=1), ((0, 0), (0, LANES - 2 * N_HEADS)))
    b_gate3 = b_gate.reshape(depth, 1, 2 * D)
    conv_st = state_mlstm_conv.reshape(depth, NB, (CONV_W - 1) * W_MIX)
    n_st = state_mlstm_n.reshape(depth, NB, W_MIX)

    cos, sin, cosT, sinT, cosd, sind, cosdT, sindT = _rope_tables(T)

    outs_p = [[], [], [], [], []]
    outs_s = [[], [], []]
    Cs = Ss = None
    for l in range(depth):
        h = _rmsnorm(x, norm_g[l].reshape(1, D), bf16)
        P, KT, GI = _inproj(h, w_in, w_ret, w_krT, w_if, l)
        brow = bias[l].reshape(1, LANES)
        bcol = bias[l, :8].reshape(8, 1)
        cw, cb = conv_w[l], conv_b[l].reshape(1, W_MIX)
        hngm, skp, hngr = (hn_m_g[l].reshape(1, W_MIX), skip_m[l].reshape(1, W_MIX),
                           hn_r_g[l].reshape(1, W_MIX))
        OM, Cp, np_, mp, convp = _mlstm_prompt(P, GI, B, T, cw, cb, wq[l], wkT[l], brow, bcol, hngm, skp)
        OR, Sp = _ret_prompt(P, KT, B, T, (cos, sin, cosT, sinT), hngr)
        OM, OR, Cs, Ss, n_s, m_s, conv_s = _decode_step(
            P, KT, GI, OM, OR, Cs, Ss, l, BT, conv_st, n_st, state_mlstm_m, state_mlstm_C, state_ret_S,
            cw, cb, wq[l], wk[l], brow, hngm, skp, hngr, (cosd, sind, cosdT, sindT))
        merged = _merge(h, OM, OR, w_gate, b_gate3, w_pm, w_pr, l)
        x = _outproj(merged, w_o, x, l)
        for k, val in enumerate((Cp, np_.reshape(B, N_HEADS, D_HEAD), mp[:, :N_HEADS, 0], convp, Sp)):
            outs_p[k].append(val)
        for k, val in enumerate((n_s.reshape(NB, N_HEADS, D_HEAD), m_s[:, :N_HEADS],
                                 conv_s.reshape(NB, CONV_W - 1, W_MIX))):
            outs_s[k].append(val)

    fg = final_g.reshape(1, D)
    y_prompt = _rmsnorm(x, fg, x_prompt.dtype, 0, BT).reshape(B, T, D)
    y_sample = _rmsnorm(x, fg, x_sample.dtype, BT, NB).reshape(NB, 1, D)
    return (y_prompt, y_sample,
            jnp.stack(outs_p[0]), jnp.stack(outs_p[1]), jnp.stack(outs_p[2]),
            jnp.stack(outs_p[3]), jnp.stack(outs_p[4]),
            Cs, jnp.stack(outs_s[0]), jnp.stack(outs_s[1]), jnp.stack(outs_s[2]), Ss)
```

```python
import functools
import math

import jax
import jax.numpy as jnp
import numpy as np
from jax import lax
from jax.experimental import pallas as pl
from jax.experimental.pallas import tpu as pltpu

f32 = jnp.float32
bf16 = jnp.bfloat16

D_MODEL = 2048
N_HEADS = 4
D_HEAD = 256
W_MIX = N_HEADS * D_HEAD
CONV_W = 4
CHUNK = 128
ROPE_BASE = 10000.0
EPS = 1e-6
NEG = -1e30
PAST_LEN = 16384
LANES = 128
HALF = D_HEAD // 2
VMEM_LIMIT = 56 * 1024 * 1024

OFF_IF = 3 * W_MIX
OFF_RET = 3 * W_MIX + 2 * N_HEADS


def _pick(n, cands):
    for c in cands:
        if n % c == 0:
            return c
    return n


def _cparams(sem):
    return pltpu.CompilerParams(dimension_semantics=sem, vmem_limit_bytes=VMEM_LIMIT)


def _dot(a, b):
    return jnp.dot(a, b, preferred_element_type=f32)


def _dot_nt(a, b):
    return lax.dot_general(a, b, (((1,), (1,)), ((), ())), preferred_element_type=f32)


def _split3(x):
    hi = x.astype(bf16).astype(f32)
    r = x - hi
    mid = r.astype(bf16).astype(f32)
    lo = (r - mid).astype(bf16).astype(f32)
    return hi, mid, lo


def _silu(x):
    return x * jax.nn.sigmoid(x)


def _log_sigmoid(x):
    return jnp.minimum(x, 0.0) - jnp.log1p(jnp.exp(-jnp.abs(x)))


def _head_norm(hc, g):
    mu = jnp.mean(hc, axis=-1, keepdims=True)
    d = hc - mu
    var = jnp.mean(d * d, axis=-1, keepdims=True)
    return d * lax.rsqrt(var + EPS) * g


def _rope_kernel(inv_ref, invT_ref, cos_ref, sin_ref, cosT_ref, sinT_ref,
                 cosd_ref, sind_ref, cosdT_ref, sindT_ref):
    T = cos_ref.shape[0]
    pos = lax.broadcasted_iota(jnp.int32, (T, HALF), 0).astype(f32)
    ang = pos * inv_ref[...]
    cos_ref[...] = jnp.cos(ang)
    sin_ref[...] = jnp.sin(ang)
    posT = lax.broadcasted_iota(jnp.int32, (HALF, T), 1).astype(f32)
    angT = posT * invT_ref[...]
    cosT_ref[...] = jnp.cos(angT)
    sinT_ref[...] = jnp.sin(angT)
    angd = jnp.full((8, HALF), float(PAST_LEN), f32) * inv_ref[...]
    cosd_ref[...] = jnp.cos(angd)
    sind_ref[...] = jnp.sin(angd)
    angdT = jnp.full((HALF, LANES), float(PAST_LEN), f32) * invT_ref[...]
    cosdT_ref[...] = jnp.cos(angdT)
    sindT_ref[...] = jnp.sin(angdT)


def _rope_tables(T):
    inv = ROPE_BASE ** (-jnp.arange(0, HALF, dtype=f32) / HALF)
    shapes = [(T, HALF), (T, HALF), (HALF, T), (HALF, T),
              (8, HALF), (8, HALF), (HALF, LANES), (HALF, LANES)]
    return pl.pallas_call(
        _rope_kernel,
        out_shape=[jax.ShapeDtypeStruct(s, f32) for s in shapes],
        name="rope_tables",
    )(inv.reshape(1, HALF), inv.reshape(HALF, 1))


def _rms_kernel(x_ref, g_ref, o_ref):
    x = x_ref[...]
    ms = jnp.mean(x * x, axis=-1, keepdims=True)
    o_ref[...] = (x * lax.rsqrt(ms + EPS) * g_ref[...]).astype(o_ref.dtype)


def _rmsnorm(x, g, out_dtype, row0=0, nrows=None):
    R, D = x.shape
    nrows = R if nrows is None else nrows
    tr = _pick(nrows, (832, 512, 256, 128))
    assert row0 % tr == 0
    off = row0 // tr
    return pl.pallas_call(
        _rms_kernel,
        grid=(nrows // tr,),
        in_specs=[pl.BlockSpec((tr, D), lambda i: (i + off, 0)),
                  pl.BlockSpec((1, D), lambda i: (0, 0))],
        out_specs=pl.BlockSpec((tr, D), lambda i: (i, 0)),
        out_shape=jax.ShapeDtypeStruct((nrows, D), out_dtype),
        compiler_params=_cparams(("arbitrary",)),
        name="rmsnorm",
    )(x, g)


_TN_IN = 512
_TPG = W_MIX // _TN_IN
_NJ_P = 6 * _TPG
_NJ_K = _TPG


def _inproj_row_offset(j):
    jp = jnp.minimum(j, _NJ_P - 1)
    grp = jp // _TPG
    sub = jp - grp * _TPG
    src_grp = jnp.where(grp < 3, grp, jnp.where(grp == 3, 3, grp + 1))
    off_p = src_grp * W_MIX + jnp.where(grp < 3, 0, 2 * N_HEADS) + sub * _TN_IN
    off_k = OFF_RET + W_MIX + jnp.clip(j - _NJ_P, 0, _NJ_K - 1) * _TN_IN
    return jnp.where(j < _NJ_P, off_p, off_k)


def _inproj_kernel(h_ref, wt_ref, wif_ref, p_ref, kt_ref, gi_ref):
    j = pl.program_id(1)
    h = h_ref[...]
    w = wt_ref[0].astype(bf16)

    @pl.when(j == 0)
    def _():
        wif = jnp.concatenate(
            [wif_ref[...], jnp.zeros((LANES - 2 * N_HEADS, D_MODEL), f32)], axis=0).astype(bf16)
        gi_ref[...] = _dot_nt(h, wif)

    @pl.when(j < _NJ_P)
    def _():
        p_ref[...] = _dot_nt(h, w).astype(bf16)

    @pl.when(j >= _NJ_P)
    def _():
        kt_ref[...] = _dot_nt(w, h).astype(bf16)


def _inproj(h, w_inT, l):
    R = h.shape[0]
    tm = _pick(R, (1664, 1280, 640, 384, 128))
    tn = _TN_IN
    nj = _NJ_P + _NJ_K
    return pl.pallas_call(
        _inproj_kernel,
        grid=(R // tm, nj),
        in_specs=[
            pl.BlockSpec((tm, D_MODEL), lambda i, j: (i, 0)),
            pl.BlockSpec((pl.Element(1), pl.Element(tn), pl.Element(D_MODEL)),
                         lambda i, j: (l, pl.multiple_of(_inproj_row_offset(j), 8), 0)),
            pl.BlockSpec((None, 2 * N_HEADS, D_MODEL), lambda i, j: (l, OFF_IF // (2 * N_HEADS), 0)),
        ],
        out_specs=[
            pl.BlockSpec((tm, tn), lambda i, j: (i, jnp.minimum(j, _NJ_P - 1))),
            pl.BlockSpec((tn, tm), lambda i, j: (jnp.clip(j - _NJ_P, 0, _NJ_K - 1), i)),
            pl.BlockSpec((tm, LANES), lambda i, j: (i, 0)),
        ],
        out_shape=[
            jax.ShapeDtypeStruct((R, 6 * W_MIX), bf16),
            jax.ShapeDtypeStruct((W_MIX, R), bf16),
            jax.ShapeDtypeStruct((R, LANES), f32),
        ],
        compiler_params=_cparams(("arbitrary", "arbitrary")),
        name="inproj",
    )(h, w_inT, w_inT)


def _mlstm_kernel(um_ref, vm_ref, zm_ref, gi_ref, convw_ref, convb_ref, wq_ref, wkT_ref,
                  brow_ref, bcol_ref, hng_ref, skip_ref,
                  om_ref, c_out_ref, n_out_ref, m_out_ref, conv_out_ref,
                  ubuf, cmbuf, ct_s, n_s, m_s):
    g = pl.program_id(1)
    ng = pl.num_programs(1)
    Lg = um_ref.shape[0]
    L = CHUNK
    scale = D_HEAD ** -0.5

    @pl.when(g == 0)
    def _():
        ubuf[0:8, :] = jnp.zeros((8, W_MIX), f32)
        ct_s[...] = jnp.zeros(ct_s.shape, f32)
        n_s[...] = jnp.zeros(n_s.shape, f32)
        m_s[...] = jnp.full(m_s.shape, NEG, f32)

    ubuf[8:8 + Lg, :] = um_ref[...].astype(f32)
    acc = convb_ref[...] + ubuf[5:5 + Lg, :] * convw_ref[0:1, :]
    for i in range(1, CONV_W):
        acc = acc + ubuf[5 + i:5 + i + Lg, :] * convw_ref[i:i + 1, :]
    cmbuf[...] = _silu(acc)
    last3 = ubuf[5 + Lg:8 + Lg, :]
    ubuf[5:8, :] = last3

    row = lax.broadcasted_iota(jnp.int32, (L, L), 0)
    col = lax.broadcasted_iota(jnp.int32, (L, L), 1)
    causal = row >= col
    tril = causal.astype(bf16)
    triu = (row <= col).astype(bf16)
    brow = brow_ref[...]
    bcol = bcol_ref[...]

    for c in range(Lg // L):
        rows = slice(c * L, (c + 1) * L)
        gb = gi_ref[rows, :] + brow
        gtb = gi_ref[rows, :].T[0:8, :] + bcol
        lf = _log_sigmoid(gb)
        lft = _log_sigmoid(gtb)
        bc3 = _dot(tril, jnp.concatenate(_split3(lf), axis=1).astype(bf16))
        bc_all = bc3[:, 0:LANES] + bc3[:, LANES:2 * LANES] + bc3[:, 2 * LANES:3 * LANES]
        br3 = _dot(jnp.concatenate(_split3(lft), axis=0).astype(bf16), triu)
        br_all = br3[0:8, :] + br3[8:16, :] + br3[16:24, :]
        for h in range(N_HEADS):
            cs = slice(h * D_HEAD, (h + 1) * D_HEAD)
            ch = cmbuf[rows, cs]
            chb = ch.astype(bf16)
            q = _dot(chb, wq_ref[h]) * scale
            qb = q.astype(bf16)
            kTb = _dot_nt(wkT_ref[h], chb).astype(bf16)
            v = vm_ref[rows, cs]
            bc = bc_all[:, 4 + h:5 + h]
            ic = gb[:, h:h + 1]
            br = br_all[4 + h:5 + h, :]
            ir = gtb[h:h + 1, :]
            m_prev = m_s[h:h + 1, 0:1]
            dlog = jnp.where(causal, bc - br + ir, NEG)
            inter = bc + m_prev
            m_t = jnp.maximum(inter, jnp.max(dlog, axis=-1, keepdims=True))
            w = jnp.exp(dlog - m_t)
            a = jnp.exp(inter - m_t)
            ws = w * _dot(qb, kTb)
            ct = ct_s[h]
            num = _dot(ws.astype(bf16), v) + a * _dot(qb, ct.astype(bf16))
            nrow = n_s[h]
            den = jnp.sum(ws, axis=-1, keepdims=True) + a * jnp.sum(q * nrow, axis=-1, keepdims=True)
            den = jnp.maximum(jnp.abs(den), jnp.exp(-m_t))
            hc = num / den
            m_new = m_t[L - 1:L, :]
            b_last = bc[L - 1:L, :]
            wk_c = jnp.exp(b_last - bc + ic - m_new)
            wk_r = jnp.exp(b_last - br + ir - m_new)
            dec = jnp.exp(b_last + m_prev - m_new)
            ct_s[h] = dec * ct + _dot(kTb, (wk_c * v.astype(f32)).astype(bf16))
            wk8 = jnp.broadcast_to(wk_r, (8, L)).astype(bf16)
            n_s[h] = dec * nrow + _dot_nt(wk8, kTb)[0:1, :]
            m_s[h:h + 1, :] = jnp.broadcast_to(m_new, (1, LANES))
            hn = _head_norm(hc, hng_ref[:, cs])
            z = zm_ref[rows, cs].astype(f32)
            om_ref[rows, cs] = ((hn + skip_ref[:, cs] * ch) * _silu(z)).astype(bf16)

    @pl.when(g == ng - 1)
    def _():
        for h in range(N_HEADS):
            c_out_ref[h] = ct_s[h].T
            n_out_ref[h] = n_s[h]
        m_out_ref[...] = m_s[...]
        conv_out_ref[...] = last3


def _mlstm_prompt(P, GI, B, T, conv_w, conv_b, wq, wkT, brow, bcol, hng, skip):
    R = P.shape[0]
    Lg = _pick(T, (512, 256, 128))
    ng = T // Lg
    col = lambda k: (lambda b, g: (b * ng + g, k))
    rep2 = lambda b, g: (0, 0)
    rep3 = lambda b, g: (0, 0, 0)
    return pl.pallas_call(
        _mlstm_kernel,
        grid=(B, ng),
        in_specs=[
            pl.BlockSpec((Lg, W_MIX), col(0)),
            pl.BlockSpec((Lg, W_MIX), col(1)),
            pl.BlockSpec((Lg, W_MIX), col(2)),
            pl.BlockSpec((Lg, LANES), col(0)),
            pl.BlockSpec((CONV_W, W_MIX), rep2),
            pl.BlockSpec((1, W_MIX), rep2),
            pl.BlockSpec((N_HEADS, D_HEAD, D_HEAD), rep3),
            pl.BlockSpec((N_HEADS, D_HEAD, D_HEAD), rep3),
            pl.BlockSpec((1, LANES), rep2),
            pl.BlockSpec((8, 1), rep2),
            pl.BlockSpec((1, W_MIX), rep2),
            pl.BlockSpec((1, W_MIX), rep2),
        ],
        out_specs=[
            pl.BlockSpec((Lg, W_MIX), col(0)),
            pl.BlockSpec((None, N_HEADS, D_HEAD, D_HEAD), lambda b, g: (b, 0, 0, 0)),
            pl.BlockSpec((None, N_HEADS, 1, D_HEAD), lambda b, g: (b, 0, 0, 0)),
            pl.BlockSpec((None, 8, LANES), lambda b, g: (b, 0, 0)),
            pl.BlockSpec((None, CONV_W - 1, W_MIX), lambda b, g: (b, 0, 0)),
        ],
        out_shape=[
            jax.ShapeDtypeStruct((R, W_MIX), bf16),
            jax.ShapeDtypeStruct((B, N_HEADS, D_HEAD, D_HEAD), f32),
            jax.ShapeDtypeStruct((B, N_HEADS, 1, D_HEAD), f32),
            jax.ShapeDtypeStruct((B, 8, LANES), f32),
            jax.ShapeDtypeStruct((B, CONV_W - 1, W_MIX), f32),
        ],
        scratch_shapes=[
            pltpu.VMEM((8 + Lg, W_MIX), f32),
            pltpu.VMEM((Lg, W_MIX), f32),
            pltpu.VMEM((N_HEADS, D_HEAD, D_HEAD), f32),
            pltpu.VMEM((N_HEADS, 1, D_HEAD), f32),
            pltpu.VMEM((8, LANES), f32),
        ],
        compiler_params=_cparams(("arbitrary", "arbitrary")),
        name="mlstm_prompt",
    )(P, P, P, GI, conv_w, conv_b, wq, wkT, brow, bcol, hng, skip)


def _log_gammas():
    return [math.log(1.0 - 2.0 ** (-5.0 - h)) for h in range(N_HEADS)]


def _ret_kernel(q_ref, v_ref, z_ref, kt_ref, cos_ref, sin_ref, cosT_ref, sinT_ref, hng_ref,
                o_ref, s_out_ref, s_s):
    g = pl.program_id(1)
    ng = pl.num_programs(1)
    Lg = q_ref.shape[0]
    L = CHUNK

    @pl.when(g == 0)
    def _():
        s_s[...] = jnp.zeros(s_s.shape, f32)

    row = lax.broadcasted_iota(jnp.int32, (L, L), 0)
    col = lax.broadcasted_iota(jnp.int32, (L, L), 1)
    causal = row >= col
    diff = (row - col).astype(f32)
    tcol = lax.broadcasted_iota(jnp.int32, (L, 1), 0).astype(f32)
    lgs = _log_gammas()
    intra = [jnp.where(causal, jnp.exp(diff * lg), 0.0) for lg in lgs]
    q_dec = [jnp.exp((tcol + 1.0) * lg) for lg in lgs]
    k_dec = [jnp.exp((L - 1.0 - tcol) * lg) for lg in lgs]
    s_dec = [math.exp(L * lg) for lg in lgs]
    kscale = D_HEAD ** -0.5

    for c in range(Lg // L):
        rows = slice(c * L, (c + 1) * L)
        cos = cos_ref[rows, :]
        sin = sin_ref[rows, :]
        cosT = cosT_ref[:, rows]
        sinT = sinT_ref[:, rows]
        for h in range(N_HEADS):
            cs = slice(h * D_HEAD, (h + 1) * D_HEAD)
            q1 = q_ref[rows, h * D_HEAD:h * D_HEAD + HALF].astype(f32)
            q2 = q_ref[rows, h * D_HEAD + HALF:(h + 1) * D_HEAD].astype(f32)
            qb = jnp.concatenate([q1 * cos - q2 * sin, q1 * sin + q2 * cos], axis=-1).astype(bf16)
            k1 = kt_ref[h * D_HEAD:h * D_HEAD + HALF, rows].astype(f32)
            k2 = kt_ref[h * D_HEAD + HALF:(h + 1) * D_HEAD, rows].astype(f32)
            kTb = (jnp.concatenate([k1 * cosT - k2 * sinT, k1 * sinT + k2 * cosT], axis=0)
                   * kscale).astype(bf16)
            v = v_ref[rows, cs]
            sc = _dot(qb, kTb) * intra[h]
            s = s_s[h]
            o = _dot(sc.astype(bf16), v) + q_dec[h] * _dot(qb, s.astype(bf16))
            s_s[h] = s_dec[h] * s + _dot(kTb, (v.astype(f32) * k_dec[h]).astype(bf16))
            z = z_ref[rows, cs].astype(f32)
            o_ref[rows, cs] = (_head_norm(o, hng_ref[:, cs]) * _silu(z)).astype(bf16)

    @pl.when(g == ng - 1)
    def _():
        s_out_ref[...] = s_s[...]


def _ret_prompt(P, KT, B, T, tabs, hng):
    R = P.shape[0]
    Lg = _pick(T, (512, 256, 128))
    ng = T // Lg
    col = lambda k: (lambda b, g: (b * ng + g, k))
    cos, sin, cosT, sinT = tabs
    return pl.pallas_call(
        _ret_kernel,
        grid=(B, ng),
        in_specs=[
            pl.BlockSpec((Lg, W_MIX), col(3)),
            pl.BlockSpec((Lg, W_MIX), col(4)),
            pl.BlockSpec((Lg, W_MIX), col(5)),
            pl.BlockSpec((W_MIX, Lg), lambda b, g: (0, b * ng + g)),
            pl.BlockSpec((Lg, HALF), lambda b, g: (g, 0)),
            pl.BlockSpec((Lg, HALF), lambda b, g: (g, 0)),
            pl.BlockSpec((HALF, Lg), lambda b, g: (0, g)),
            pl.BlockSpec((HALF, Lg), lambda b, g: (0, g)),
            pl.BlockSpec((1, W_MIX), lambda b, g: (0, 0)),
        ],
        out_specs=[
            pl.BlockSpec((Lg, W_MIX), col(0)),
            pl.BlockSpec((None, N_HEADS, D_HEAD, D_HEAD), lambda b, g: (b, 0, 0, 0)),
        ],
        out_shape=[
            jax.ShapeDtypeStruct((R, W_MIX), bf16),
            jax.ShapeDtypeStruct((B, N_HEADS, D_HEAD, D_HEAD), f32),
        ],
        scratch_shapes=[pltpu.VMEM((N_HEADS, D_HEAD, D_HEAD), f32)],
        compiler_params=_cparams(("arbitrary", "arbitrary")),
        name="ret_prompt",
    )(P, P, P, KT, cos, sin, cosT, sinT, hng)


_SEQ_PER_STEP = 4


def _decode_kernel(um_ref, vm_ref, zm_ref, qr_ref, vr_ref, zr_ref, kt_ref, gi_ref,
                   conv_ref, n_ref, m_ref, c_ref, s_ref,
                   convw_ref, convb_ref, wq_ref, wk_ref, brow_ref, hngm_ref, skip_ref, hngr_ref,
                   cosd_ref, sind_ref, cosdT_ref, sindT_ref,
                   om_in_ref, or_in_ref, cprev_ref, sprev_ref,
                   om_ref, or_ref, c_out_ref, s_out_ref, n_out_ref, m_out_ref, conv_out_ref,
                   vt_s, qrt_s, krt_s, qm_s, km_s, vr_s, sct_s, ht_s, oacc_s, cm_s):
    del om_in_ref, or_in_ref, cprev_ref, sprev_ref
    s = pl.program_id(0)
    ns = pl.num_programs(0)
    NB = um_ref.shape[0]
    scale = D_HEAD ** -0.5
    lgs = _log_gammas()
    gammas = [math.exp(lg) for lg in lgs]
    lane = lax.broadcasted_iota(jnp.int32, (1, LANES), 1)

    @pl.when(s == 0)
    def _():
        u = um_ref[...].astype(f32)
        acc = convb_ref[...] + u * convw_ref[CONV_W - 1:CONV_W, :]
        for i in range(CONV_W - 1):
            acc = acc + conv_ref[:, i * W_MIX:(i + 1) * W_MIX] * convw_ref[i:i + 1, :]
        cm = _silu(acc)
        cm_s[...] = cm
        conv_out_ref[:, 0:W_MIX] = conv_ref[:, W_MIX:2 * W_MIX]
        conv_out_ref[:, W_MIX:2 * W_MIX] = conv_ref[:, 2 * W_MIX:3 * W_MIX]
        conv_out_ref[:, 2 * W_MIX:3 * W_MIX] = u

        gb = gi_ref[...] + brow_ref[...]
        lf = _log_sigmoid(gb)
        sc = jnp.zeros((NB, LANES), f32)
        mo = jnp.zeros((NB, LANES), f32)
        for h in range(N_HEADS):
            cs = slice(h * D_HEAD, (h + 1) * D_HEAD)
            chb = cm[:, cs].astype(bf16)
            q = _dot(chb, wq_ref[h]) * scale
            k = _dot(chb, wk_ref[h])
            qm_s[:, cs] = q
            km_s[:, cs] = k
            ig = gb[:, h:h + 1]
            lfh = lf[:, 4 + h:5 + h]
            mp = m_ref[:, h:h + 1]
            inter = lfh + mp
            m_t = jnp.maximum(inter, ig)
            w = jnp.exp(ig - m_t)
            a = jnp.exp(inter - m_t)
            nh = n_ref[:, cs]
            ws = w * jnp.sum(q * k, axis=-1, keepdims=True)
            den = ws + a * jnp.sum(nh * q, axis=-1, keepdims=True)
            den = jnp.maximum(jnp.abs(den), jnp.exp(-m_t))
            n_out_ref[:, cs] = a * nh + w * k
            for r, val in ((h, ws), (4 + h, a), (8 + h, den), (12 + h, w)):
                sc = sc + val * (lane == r).astype(f32)
            mo = mo + m_t * (lane == h).astype(f32)
        m_out_ref[...] = mo
        sct_s[...] = sc.T
        vt_s[...] = vm_ref[...].astype(f32).T
        vr_s[...] = vr_ref[...].astype(f32)
        cosd = cosd_ref[0:1, :]
        sind = sind_ref[0:1, :]
        cosdT = cosdT_ref[...]
        sindT = sindT_ref[...]
        for h in range(N_HEADS):
            lo = slice(h * D_HEAD, h * D_HEAD + HALF)
            hi = slice(h * D_HEAD + HALF, (h + 1) * D_HEAD)
            q1 = qr_ref[:, lo].astype(f32)
            q2 = qr_ref[:, hi].astype(f32)
            qrot = jnp.concatenate([q1 * cosd - q2 * sind, q1 * sind + q2 * cosd], axis=-1)
            qrt_s[h * D_HEAD:(h + 1) * D_HEAD, :] = qrot.T
            k1 = kt_ref[lo, :].astype(f32)
            k2 = kt_ref[hi, :].astype(f32)
            krt_s[lo, :] = (k1 * cosdT - k2 * sindT) * scale
            krt_s[hi, :] = (k1 * sindT + k2 * cosdT) * scale
        ht_s[...] = jnp.zeros(ht_s.shape, f32)

    def take_col(mat, mask):
        return jnp.sum(jnp.where(mask, mat, 0.0), axis=-1, keepdims=True)

    for j in range(_SEQ_PER_STEP):
        b = s * _SEQ_PER_STEP + j
        mask = lane == b
        for h in range(N_HEADS):
            cs = slice(h * D_HEAD, (h + 1) * D_HEAD)
            c = c_ref[j, h]
            q_row = qm_s[pl.ds(b, 1), cs]
            k_row = km_s[pl.ds(b, 1), cs]
            v_col = take_col(vt_s[cs, :], mask)
            ws = take_col(sct_s[h:h + 1, :], mask)
            a = take_col(sct_s[4 + h:5 + h, :], mask)
            den = take_col(sct_s[8 + h:9 + h, :], mask)
            w = take_col(sct_s[12 + h:13 + h, :], mask)
            cq = jnp.sum(c * q_row, axis=-1, keepdims=True)
            hcol = (ws * v_col + a * cq) / den
            ht_s[cs, :] = jnp.where(mask, hcol, ht_s[cs, :])
            c_out_ref[j, h] = a * c + (w * v_col) * k_row
            st = s_ref[j, h]
            q_col = take_col(qrt_s[cs, :], mask)
            k_col = take_col(krt_s[cs, :], mask)
            v_row = vr_s[pl.ds(b, 1), cs]
            qk = jnp.sum(q_col * k_col, axis=0, keepdims=True)
            o_row = qk * v_row + gammas[h] * jnp.sum(st * q_col, axis=0, keepdims=True)
            oacc_s[pl.ds(b, 1), cs] = o_row
            s_out_ref[j, h] = gammas[h] * st + k_col * v_row

    @pl.when(s == ns - 1)
    def _():
        hm = ht_s[...].T
        for h in range(N_HEADS):
            cs = slice(h * D_HEAD, (h + 1) * D_HEAD)
            hn = _head_norm(hm[:, cs], hngm_ref[:, cs])
            zm = zm_ref[:, cs].astype(f32)
            om_ref[:, cs] = ((hn + skip_ref[:, cs] * cm_s[:, cs]) * _silu(zm)).astype(bf16)
            zr = zr_ref[:, cs].astype(f32)
            or_ref[:, cs] = (_head_norm(oacc_s[:, cs], hngr_ref[:, cs]) * _silu(zr)).astype(bf16)


def _decode_step(P, KT, GI, OM, OR, Cprev, Sprev, l, BT, conv_st, n_st, m_st, C_st, S_st,
                 conv_w, conv_b, wq, wk, brow, hngm, skip, hngr, dtabs):
    R = P.shape[0]
    NB = R - BT
    assert NB == LANES and BT % NB == 0 and NB % _SEQ_PER_STEP == 0
    rb = BT // NB
    depth = C_st.shape[0]
    bb = _SEQ_PER_STEP
    cosd, sind, cosdT, sindT = dtabs
    pcol = lambda k: (lambda s: (rb, k))
    rep2 = lambda s: (0, 0)
    rep3 = lambda s: (0, 0, 0)
    st_spec = pl.BlockSpec((None, bb, N_HEADS, D_HEAD, D_HEAD), lambda s: (l, s, 0, 0, 0))
    any_spec = pl.BlockSpec(memory_space=pl.ANY)
    in_specs = [
        pl.BlockSpec((NB, W_MIX), pcol(0)), pl.BlockSpec((NB, W_MIX), pcol(1)),
        pl.BlockSpec((NB, W_MIX), pcol(2)), pl.BlockSpec((NB, W_MIX), pcol(3)),
        pl.BlockSpec((NB, W_MIX), pcol(4)), pl.BlockSpec((NB, W_MIX), pcol(5)),
        pl.BlockSpec((W_MIX, NB), lambda s: (0, rb)),
        pl.BlockSpec((NB, LANES), lambda s: (rb, 0)),
        pl.BlockSpec((None, NB, (CONV_W - 1) * W_MIX), lambda s: (l, 0, 0)),
        pl.BlockSpec((None, NB, W_MIX), lambda s: (l, 0, 0)),
        pl.BlockSpec((None, NB, N_HEADS), lambda s: (l, 0, 0)),
        st_spec, st_spec,
        pl.BlockSpec((CONV_W, W_MIX), rep2), pl.BlockSpec((1, W_MIX), rep2),
        pl.BlockSpec((N_HEADS, D_HEAD, D_HEAD), rep3), pl.BlockSpec((N_HEADS, D_HEAD, D_HEAD), rep3),
        pl.BlockSpec((1, LANES), rep2),
        pl.BlockSpec((1, W_MIX), rep2), pl.BlockSpec((1, W_MIX), rep2), pl.BlockSpec((1, W_MIX), rep2),
        pl.BlockSpec((8, HALF), rep2), pl.BlockSpec((8, HALF), rep2),
        pl.BlockSpec((HALF, LANES), rep2), pl.BlockSpec((HALF, LANES), rep2),
        any_spec, any_spec,
    ]
    args = [P, P, P, P, P, P, KT, GI, conv_st, n_st, m_st, C_st, S_st,
            conv_w, conv_b, wq, wk, brow, hngm, skip, hngr, cosd, sind, cosdT, sindT, OM, OR]
    aliases = {len(args) - 2: 0, len(args) - 1: 1}
    if Cprev is not None:
        in_specs += [any_spec, any_spec]
        args += [Cprev, Sprev]
        aliases[len(args) - 2] = 2
        aliases[len(args) - 1] = 3
        kern = _decode_kernel
    else:
        kern = functools.partial(_decode_kernel_first)
    out_row = pl.BlockSpec((NB, W_MIX), lambda s: (rb, 0))
    return pl.pallas_call(
        kern,
        grid=(NB // bb,),
        in_specs=in_specs,
        out_specs=[
            out_row, out_row, st_spec, st_spec,
            pl.BlockSpec((NB, W_MIX), rep2),
            pl.BlockSpec((NB, LANES), rep2),
            pl.BlockSpec((NB, (CONV_W - 1) * W_MIX), rep2),
        ],
        out_shape=[
            jax.ShapeDtypeStruct(OM.shape, bf16),
            jax.ShapeDtypeStruct(OR.shape, bf16),
            jax.ShapeDtypeStruct((depth, NB, N_HEADS, D_HEAD, D_HEAD), f32),
            jax.ShapeDtypeStruct((depth, NB, N_HEADS, D_HEAD, D_HEAD), f32),
            jax.ShapeDtypeStruct((NB, W_MIX), f32),
            jax.ShapeDtypeStruct((NB, LANES), f32),
            jax.ShapeDtypeStruct((NB, (CONV_W - 1) * W_MIX), f32),
        ],
        scratch_shapes=[
            pltpu.VMEM((W_MIX, NB), f32),
            pltpu.VMEM((W_MIX, NB), f32),
            pltpu.VMEM((W_MIX, NB), f32),
            pltpu.VMEM((NB, W_MIX), f32),
            pltpu.VMEM((NB, W_MIX), f32),
            pltpu.VMEM((NB, W_MIX), f32),
            pltpu.VMEM((LANES, NB), f32),
            pltpu.VMEM((W_MIX, NB), f32),
            pltpu.VMEM((NB, W_MIX), f32),
            pltpu.VMEM((NB, W_MIX), f32),
        ],
        input_output_aliases=aliases,
        compiler_params=_cparams(("arbitrary",)),
        name="decode_step",
    )(*args)


def _decode_kernel_first(*refs):
    n_in = 27
    ins, rest = refs[:n_in], refs[n_in:]
    return _decode_kernel(*ins, None, None, *rest)


def _merge_kernel(h_ref, om_ref, or_ref, wgm_ref, wgr_ref, bgm_ref, bgr_ref, wpm_ref, wpr_ref, o_ref):
    h = h_ref[...]
    gm = jax.nn.sigmoid(_dot(h, wgm_ref[...].astype(bf16)) + bgm_ref[...])
    gr = jax.nn.sigmoid(_dot(h, wgr_ref[...].astype(bf16)) + bgr_ref[...])
    pm = _dot(om_ref[...], wpm_ref[...].astype(bf16))
    pr = _dot(or_ref[...], wpr_ref[...].astype(bf16))
    o_ref[...] = (gm * pm + gr * pr).astype(bf16)


def _merge(h, OM, OR, w_gate, b_gate3, w_pm, w_pr, l):
    R = h.shape[0]
    tm = _pick(R, (1664, 1280, 640, 384, 128))
    tn = 256
    nj = D_MODEL // tn
    return pl.pallas_call(
        _merge_kernel,
        grid=(R // tm, nj),
        in_specs=[
            pl.BlockSpec((tm, D_MODEL), lambda i, j: (i, 0)),
            pl.BlockSpec((tm, W_MIX), lambda i, j: (i, 0)),
            pl.BlockSpec((tm, W_MIX), lambda i, j: (i, 0)),
            pl.BlockSpec((None, D_MODEL, tn), lambda i, j: (l, 0, j)),
            pl.BlockSpec((None, D_MODEL, tn), lambda i, j: (l, 0, j + nj)),
            pl.BlockSpec((None, 1, tn), lambda i, j: (l, 0, j)),
            pl.BlockSpec((None, 1, tn), lambda i, j: (l, 0, j + nj)),
            pl.BlockSpec((None, W_MIX, tn), lambda i, j: (l, 0, j)),
            pl.BlockSpec((None, W_MIX, tn), lambda i, j: (l, 0, j)),
        ],
        out_specs=pl.BlockSpec((tm, tn), lambda i, j: (i, j)),
        out_shape=jax.ShapeDtypeStruct((R, D_MODEL), bf16),
        compiler_params=_cparams(("arbitrary", "arbitrary")),
        name="merge",
    )(h, OM, OR, w_gate, w_gate, b_gate3, b_gate3, w_pm, w_pr)


def _outproj_kernel(mg_ref, wo_ref, x_ref, o_ref):
    o_ref[...] = x_ref[...] + _dot(mg_ref[...], wo_ref[...].astype(bf16))


def _outproj(merged, w_o, x, l):
    R = x.shape[0]
    tm = _pick(R, (1664, 1280, 640, 384, 128))
    tn = 512
    return pl.pallas_call(
        _outproj_kernel,
        grid=(R // tm, D_MODEL // tn),
        in_specs=[
            pl.BlockSpec((tm, D_MODEL), lambda i, j: (i, 0)),
            pl.BlockSpec((None, D_MODEL, tn), lambda i, j: (l, 0, j)),
            pl.BlockSpec((tm, tn), lambda i, j: (i, j)),
        ],
        out_specs=pl.BlockSpec((tm, tn), lambda i, j: (i, j)),
        out_shape=jax.ShapeDtypeStruct((R, D_MODEL), f32),
        compiler_params=_cparams(("arbitrary", "arbitrary")),
        name="outproj",
    )(merged, w_o, x)


def kernel(x_prompt, x_sample, state_mlstm_C, state_mlstm_n, state_mlstm_m, state_mlstm_conv, state_ret_S,
           norm_g, w_in, conv_w, conv_b, w_qm, w_km, b_i, b_f, hn_m_g, skip_m, hn_r_g,
           w_pm, w_pr, w_gate, b_gate, w_o, final_g):
    B, T, D = x_prompt.shape
    NB = x_sample.shape[0]
    depth = w_in.shape[0]
    BT = B * T
    assert D == D_MODEL and x_sample.shape[1] == 1 and T % CHUNK == 0

    x = jnp.concatenate([x_prompt.reshape(BT, D), x_sample.reshape(NB, D)], axis=0)

    w_inT = jnp.swapaxes(w_in, 1, 2)
    wq = w_qm.astype(bf16)
    wk = w_km.astype(bf16)
    wkT = jnp.swapaxes(w_km, 2, 3).astype(bf16)
    bias = jnp.pad(jnp.concatenate([b_i, b_f], axis=1), ((0, 0), (0, LANES - 2 * N_HEADS)))
    b_gate3 = b_gate.reshape(depth, 1, 2 * D)
    conv_st = state_mlstm_conv.reshape(depth, NB, (CONV_W - 1) * W_MIX)
    n_st = state_mlstm_n.reshape(depth, NB, W_MIX)

    cos, sin, cosT, sinT, cosd, sind, cosdT, sindT = _rope_tables(T)

    outs_p = [[], [], [], [], []]
    outs_s = [[], [], []]
    Cs = Ss = None
    for l in range(depth):
        h = _rmsnorm(x, norm_g[l].reshape(1, D), bf16)
        P, KT, GI = _inproj(h, w_inT, l)
        brow = bias[l].reshape(1, LANES)
        bcol = bias[l, :8].reshape(8, 1)
        cw, cb = conv_w[l], conv_b[l].reshape(1, W_MIX)
        hngm, skp, hngr = (hn_m_g[l].reshape(1, W_MIX), skip_m[l].reshape(1, W_MIX),
                           hn_r_g[l].reshape(1, W_MIX))
        OM, Cp, np_, mp, convp = _mlstm_prompt(P, GI, B, T, cw, cb, wq[l], wkT[l], brow, bcol, hngm, skp)
        OR, Sp = _ret_prompt(P, KT, B, T, (cos, sin, cosT, sinT), hngr)
        OM, OR, Cs, Ss, n_s, m_s, conv_s = _decode_step(
            P, KT, GI, OM, OR, Cs, Ss, l, BT, conv_st, n_st, state_mlstm_m, state_mlstm_C, state_ret_S,
            cw, cb, wq[l], wk[l], brow, hngm, skp, hngr, (cosd, sind, cosdT, sindT))
        merged = _merge(h, OM, OR, w_gate, b_gate3, w_pm, w_pr, l)
        x = _outproj(merged, w_o, x, l)
        for k, val in enumerate((Cp, np_.reshape(B, N_HEADS, D_HEAD), mp[:, :N_HEADS, 0], convp, Sp)):
            outs_p[k].append(val)
        for k, val in enumerate((n_s.reshape(NB, N_HEADS, D_HEAD), m_s[:, :N_HEADS],
                                 conv_s.reshape(NB, CONV_W - 1, W_MIX))):
            outs_s[k].append(val)

    fg = final_g.reshape(1, D)
    y_prompt = _rmsnorm(x, fg, x_prompt.dtype, 0, BT).reshape(B, T, D)
    y_sample = _rmsnorm(x, fg, x_sample.dtype, BT, NB).reshape(NB, 1, D)
    return (y_prompt, y_sample,
            jnp.stack(outs_p[0]), jnp.stack(outs_p[1]), jnp.stack(outs_p[2]),
            jnp.stack(outs_p[3]), jnp.stack(outs_p[4]),
            Cs, jnp.stack(outs_s[0]), jnp.stack(outs_s[1]), jnp.stack(outs_s[2]), Ss)
```

```python
import functools
import math

import jax
import jax.numpy as jnp
import numpy as np
from jax import lax
from jax.experimental import pallas as pl
from jax.experimental.pallas import tpu as pltpu

f32 = jnp.float32
bf16 = jnp.bfloat16

D_MODEL = 2048
N_HEADS = 4
D_HEAD = 256
W_MIX = N_HEADS * D_HEAD
CONV_W = 4
CHUNK = 128
ROPE_BASE = 10000.0
EPS = 1e-6
NEG = -1e30
PAST_LEN = 16384
LANES = 128
HALF = D_HEAD // 2
VMEM_LIMIT = 56 * 1024 * 1024

OFF_IF = 3 * W_MIX
OFF_RET = 3 * W_MIX + 2 * N_HEADS


def _pick(n, cands):
    for c in cands:
        if n % c == 0:
            return c
    return n


def _cparams(sem):
    return pltpu.CompilerParams(dimension_semantics=sem, vmem_limit_bytes=VMEM_LIMIT)


def _dot(a, b):
    return jnp.dot(a, b, preferred_element_type=f32)


def _dot_nt(a, b):
    return lax.dot_general(a, b, (((1,), (1,)), ((), ())), preferred_element_type=f32)


def _split3(x):
    hi = x.astype(bf16).astype(f32)
    r = x - hi
    mid = r.astype(bf16).astype(f32)
    lo = (r - mid).astype(bf16).astype(f32)
    return hi, mid, lo


def _silu(x):
    return x * jax.nn.sigmoid(x)


def _log_sigmoid(x):
    return jnp.minimum(x, 0.0) - jnp.log1p(jnp.exp(-jnp.abs(x)))


def _head_norm(hc, g):
    mu = jnp.mean(hc, axis=-1, keepdims=True)
    d = hc - mu
    var = jnp.mean(d * d, axis=-1, keepdims=True)
    return d * lax.rsqrt(var + EPS) * g


def _rope_kernel(inv_ref, invT_ref, cos_ref, sin_ref, cosT_ref, sinT_ref,
                 cosd_ref, sind_ref, cosdT_ref, sindT_ref):
    T = cos_ref.shape[0]
    pos = lax.broadcasted_iota(jnp.int32, (T, HALF), 0).astype(f32)
    ang = pos * inv_ref[...]
    cos_ref[...] = jnp.cos(ang)
    sin_ref[...] = jnp.sin(ang)
    posT = lax.broadcasted_iota(jnp.int32, (HALF, T), 1).astype(f32)
    angT = posT * invT_ref[...]
    cosT_ref[...] = jnp.cos(angT)
    sinT_ref[...] = jnp.sin(angT)
    angd = jnp.full((8, HALF), float(PAST_LEN), f32) * inv_ref[...]
    cosd_ref[...] = jnp.cos(angd)
    sind_ref[...] = jnp.sin(angd)
    angdT = jnp.full((HALF, LANES), float(PAST_LEN), f32) * invT_ref[...]
    cosdT_ref[...] = jnp.cos(angdT)
    sindT_ref[...] = jnp.sin(angdT)


def _rope_tables(T):
    inv = ROPE_BASE ** (-jnp.arange(0, HALF, dtype=f32) / HALF)
    shapes = [(T, HALF), (T, HALF), (HALF, T), (HALF, T),
              (8, HALF), (8, HALF), (HALF, LANES), (HALF, LANES)]
    return pl.pallas_call(
        _rope_kernel,
        out_shape=[jax.ShapeDtypeStruct(s, f32) for s in shapes],
        name="rope_tables",
    )(inv.reshape(1, HALF), inv.reshape(HALF, 1))


def _rms_kernel(x_ref, g_ref, o_ref):
    x = x_ref[...]
    ms = jnp.mean(x * x, axis=-1, keepdims=True)
    o_ref[...] = (x * lax.rsqrt(ms + EPS) * g_ref[...]).astype(o_ref.dtype)


def _rmsnorm(x, g, out_dtype, row0=0, nrows=None):
    R, D = x.shape
    nrows = R if nrows is None else nrows
    tr = _pick(nrows, (832, 512, 256, 128))
    assert row0 % tr == 0
    off = row0 // tr
    return pl.pallas_call(
        _rms_kernel,
        grid=(nrows // tr,),
        in_specs=[pl.BlockSpec((tr, D), lambda i: (i + off, 0)),
                  pl.BlockSpec((1, D), lambda i: (0, 0))],
        out_specs=pl.BlockSpec((tr, D), lambda i: (i, 0)),
        out_shape=jax.ShapeDtypeStruct((nrows, D), out_dtype),
        compiler_params=_cparams(("arbitrary",)),
        name="rmsnorm",
    )(x, g)


_TN_IN = 512
_TPG = W_MIX // _TN_IN
_NJ_P = 6 * _TPG
_NJ_K = _TPG


def _inproj_row_offset(j):
    jp = jnp.minimum(j, _NJ_P - 1)
    grp = jp // _TPG
    sub = jp - grp * _TPG
    src_grp = jnp.where(grp < 3, grp, jnp.where(grp == 3, 3, grp + 1))
    off_p = src_grp * W_MIX + jnp.where(grp < 3, 0, 2 * N_HEADS) + sub * _TN_IN
    off_k = OFF_RET + W_MIX + jnp.clip(j - _NJ_P, 0, _NJ_K - 1) * _TN_IN
    return jnp.where(j < _NJ_P, off_p, off_k)


def _inproj_kernel(h_ref, wt_ref, wif_ref, p_ref, kt_ref, gi_ref):
    j = pl.program_id(1)
    h = h_ref[...]
    w = wt_ref[0].astype(bf16)

    @pl.when(j == 0)
    def _():
        wif = jnp.concatenate(
            [wif_ref[...], jnp.zeros((LANES - 2 * N_HEADS, D_MODEL), f32)], axis=0).astype(bf16)
        gi_ref[...] = _dot_nt(h, wif)

    is_z = (j // _TPG == 2) | (j // _TPG == 5)

    @pl.when((j < _NJ_P) & jnp.logical_not(is_z))
    def _():
        p_ref[...] = _dot_nt(h, w).astype(bf16)

    @pl.when((j < _NJ_P) & is_z)
    def _():
        p_ref[...] = _silu(_dot_nt(h, w)).astype(bf16)

    @pl.when(j >= _NJ_P)
    def _():
        kt_ref[...] = _dot_nt(w, h).astype(bf16)


def _inproj(h, w_inT, l):
    R = h.shape[0]
    tm = _pick(R, (1664, 1280, 640, 384, 128))
    tn = _TN_IN
    nj = _NJ_P + _NJ_K
    return pl.pallas_call(
        _inproj_kernel,
        grid=(R // tm, nj),
        in_specs=[
            pl.BlockSpec((tm, D_MODEL), lambda i, j: (i, 0)),
            pl.BlockSpec((pl.Element(1), pl.Element(tn), pl.Element(D_MODEL)),
                         lambda i, j: (l, pl.multiple_of(_inproj_row_offset(j), 8), 0)),
            pl.BlockSpec((None, 2 * N_HEADS, D_MODEL), lambda i, j: (l, OFF_IF // (2 * N_HEADS), 0)),
        ],
        out_specs=[
            pl.BlockSpec((tm, tn), lambda i, j: (i, jnp.minimum(j, _NJ_P - 1))),
            pl.BlockSpec((tn, tm), lambda i, j: (jnp.clip(j - _NJ_P, 0, _NJ_K - 1), i)),
            pl.BlockSpec((tm, LANES), lambda i, j: (i, 0)),
        ],
        out_shape=[
            jax.ShapeDtypeStruct((R, 6 * W_MIX), bf16),
            jax.ShapeDtypeStruct((W_MIX, R), bf16),
            jax.ShapeDtypeStruct((R, LANES), f32),
        ],
        compiler_params=_cparams(("arbitrary", "arbitrary")),
        name="inproj",
    )(h, w_inT, w_inT)


def _mlstm_kernel(um_ref, vm_ref, zs_ref, gi_ref, convw_ref, convb_ref, wq_ref, wkT_ref,
                  brow_ref, bcol_ref, hng_ref, skip_ref,
                  om_ref, c_out_ref, n_out_ref, m_out_ref, conv_out_ref,
                  ubuf, cmbuf, cta_s, m_s):
    g = pl.program_id(1)
    ng = pl.num_programs(1)
    Lg = um_ref.shape[0]
    L = CHUNK
    DA = D_HEAD + LANES
    scale = D_HEAD ** -0.5
    heads = range(N_HEADS)

    @pl.when(g == 0)
    def _():
        ubuf[0:8, :] = jnp.zeros((8, W_MIX), f32)
        cta_s[...] = jnp.zeros(cta_s.shape, f32)
        m_s[...] = jnp.full(m_s.shape, NEG, f32)

    ubuf[8:8 + Lg, :] = um_ref[...].astype(f32)
    acc = convb_ref[...] + ubuf[5:5 + Lg, :] * convw_ref[0:1, :]
    for i in range(1, CONV_W):
        acc = acc + ubuf[5 + i:5 + i + Lg, :] * convw_ref[i:i + 1, :]
    cmbuf[...] = _silu(acc)
    last3 = ubuf[5 + Lg:8 + Lg, :]
    ubuf[5:8, :] = last3

    row = lax.broadcasted_iota(jnp.int32, (L, L), 0)
    col = lax.broadcasted_iota(jnp.int32, (L, L), 1)
    causal = row >= col
    tril = causal.astype(bf16)
    triu = (row <= col).astype(bf16)
    brow = brow_ref[...]
    bcol = bcol_ref[...]
    ones_b = jnp.ones((L, LANES), bf16)
    ones_f = jnp.ones((L, LANES), f32)

    for c in range(Lg // L):
        rows = slice(c * L, (c + 1) * L)
        graw = gi_ref[rows, :]
        gb = graw + brow
        gtb = graw.T[0:8, :] + bcol
        lf = _log_sigmoid(gb)
        lft = _log_sigmoid(gtb)
        bc3 = _dot(tril, jnp.concatenate(_split3(lf), axis=1).astype(bf16))
        bc_all = bc3[:, 0:LANES] + bc3[:, LANES:2 * LANES] + bc3[:, 2 * LANES:3 * LANES]
        br3 = _dot(jnp.concatenate(_split3(lft), axis=0).astype(bf16), triu)
        br_all = br3[0:8, :] + br3[8:16, :] + br3[16:24, :]

        cs = [slice(h * D_HEAD, (h + 1) * D_HEAD) for h in heads]
        ch = [cmbuf[rows, cs[h]] for h in heads]
        chb = [x.astype(bf16) for x in ch]
        qb = [(_dot(chb[h], wq_ref[h]) * scale).astype(bf16) for h in heads]
        kTb = [_dot_nt(wkT_ref[h], chb[h]).astype(bf16) for h in heads]
        v = [vm_ref[rows, cs[h]] for h in heads]
        bc = [bc_all[:, 4 + h:5 + h] for h in heads]
        ic = [gb[:, h:h + 1] for h in heads]
        m_prev = [m_s[h][0:1, 0:1] for h in heads]
        dlog = [jnp.where(causal, bc[h] - br_all[4 + h:5 + h, :] + gtb[h:h + 1, :], NEG) for h in heads]
        inter = [bc[h] + m_prev[h] for h in heads]
        m_t = [jnp.maximum(inter[h], jnp.max(dlog[h], axis=-1, keepdims=True)) for h in heads]
        a = [jnp.exp(inter[h] - m_t[h]) for h in heads]
        ws = [(jnp.exp(dlog[h] - m_t[h]) * _dot(qb[h], kTb[h])).astype(bf16) for h in heads]
        cta = [cta_s[h] for h in heads]
        nd = [_dot(ws[h], jnp.concatenate([v[h], ones_b], axis=1))
              + a[h] * _dot(qb[h], cta[h].astype(bf16)) for h in heads]
        m_new = [m_t[h][L - 1:L, :] for h in heads]
        b_last = [bc[h][L - 1:L, :] for h in heads]
        for h in heads:
            wk_c = jnp.exp(b_last[h] - bc[h] + ic[h] - m_new[h])
            dec = jnp.exp(b_last[h] + m_prev[h] - m_new[h])
            vaug = jnp.concatenate([v[h].astype(f32), ones_f], axis=1)
            cta_s[h] = dec * cta[h] + _dot(kTb[h], (wk_c * vaug).astype(bf16))
            m_s[h] = jnp.broadcast_to(m_new[h], (8, LANES))
        for h in heads:
            den = jnp.maximum(jnp.abs(nd[h][:, D_HEAD:DA]), jnp.exp(-m_t[h]))
            rden = 1.0 / den
            hc = jnp.concatenate([nd[h][:, 0:HALF] * rden, nd[h][:, HALF:D_HEAD] * rden], axis=1)
            hn = _head_norm(hc, hng_ref[:, cs[h]])
            zs = zs_ref[rows, cs[h]].astype(f32)
            om_ref[rows, cs[h]] = ((hn + skip_ref[:, cs[h]] * ch[h]) * zs).astype(bf16)

    @pl.when(g == ng - 1)
    def _():
        for h in heads:
            st = cta_s[h]
            c_out_ref[h] = st[:, 0:D_HEAD].T
            n_out_ref[h] = st[:, D_HEAD:DA].T[0:1, :]
        m_out_ref[...] = jnp.concatenate([m_s[h][0:1, :] for h in heads]
                                         + [jnp.zeros((8 - N_HEADS, LANES), f32)], axis=0)
        conv_out_ref[...] = last3


def _mlstm_prompt(P, GI, B, T, conv_w, conv_b, wq, wkT, brow, bcol, hng, skip):
    R = P.shape[0]
    Lg = _pick(T, (512, 256, 128))
    ng = T // Lg
    col = lambda k: (lambda b, g: (b * ng + g, k))
    rep2 = lambda b, g: (0, 0)
    rep3 = lambda b, g: (0, 0, 0)
    return pl.pallas_call(
        _mlstm_kernel,
        grid=(B, ng),
        in_specs=[
            pl.BlockSpec((Lg, W_MIX), col(0)),
            pl.BlockSpec((Lg, W_MIX), col(1)),
            pl.BlockSpec((Lg, W_MIX), col(2)),
            pl.BlockSpec((Lg, LANES), col(0)),
            pl.BlockSpec((CONV_W, W_MIX), rep2),
            pl.BlockSpec((1, W_MIX), rep2),
            pl.BlockSpec((N_HEADS, D_HEAD, D_HEAD), rep3),
            pl.BlockSpec((N_HEADS, D_HEAD, D_HEAD), rep3),
            pl.BlockSpec((1, LANES), rep2),
            pl.BlockSpec((8, 1), rep2),
            pl.BlockSpec((1, W_MIX), rep2),
            pl.BlockSpec((1, W_MIX), rep2),
        ],
        out_specs=[
            pl.BlockSpec((Lg, W_MIX), col(0)),
            pl.BlockSpec((None, N_HEADS, D_HEAD, D_HEAD), lambda b, g: (b, 0, 0, 0)),
            pl.BlockSpec((None, N_HEADS, 1, D_HEAD), lambda b, g: (b, 0, 0, 0)),
            pl.BlockSpec((None, 8, LANES), lambda b, g: (b, 0, 0)),
            pl.BlockSpec((None, CONV_W - 1, W_MIX), lambda b, g: (b, 0, 0)),
        ],
        out_shape=[
            jax.ShapeDtypeStruct((R, W_MIX), bf16),
            jax.ShapeDtypeStruct((B, N_HEADS, D_HEAD, D_HEAD), f32),
            jax.ShapeDtypeStruct((B, N_HEADS, 1, D_HEAD), f32),
            jax.ShapeDtypeStruct((B, 8, LANES), f32),
            jax.ShapeDtypeStruct((B, CONV_W - 1, W_MIX), f32),
        ],
        scratch_shapes=[
            pltpu.VMEM((8 + Lg, W_MIX), f32),
            pltpu.VMEM((Lg, W_MIX), f32),
            pltpu.VMEM((N_HEADS, D_HEAD, D_HEAD + LANES), f32),
            pltpu.VMEM((N_HEADS, 8, LANES), f32),
        ],
        compiler_params=_cparams(("arbitrary", "arbitrary")),
        name="mlstm_prompt",
    )(P, P, P, GI, conv_w, conv_b, wq, wkT, brow, bcol, hng, skip)


def _log_gammas():
    return [math.log(1.0 - 2.0 ** (-5.0 - h)) for h in range(N_HEADS)]


def _ret_kernel(q_ref, v_ref, zs_ref, kt_ref, cos_ref, sin_ref, cosT_ref, sinT_ref, hng_ref,
                o_ref, s_out_ref, s_s):
    g = pl.program_id(1)
    ng = pl.num_programs(1)
    Lg = q_ref.shape[0]
    L = CHUNK

    @pl.when(g == 0)
    def _():
        s_s[...] = jnp.zeros(s_s.shape, f32)

    row = lax.broadcasted_iota(jnp.int32, (L, L), 0)
    col = lax.broadcasted_iota(jnp.int32, (L, L), 1)
    causal = row >= col
    diff = (row - col).astype(f32)
    tcol = lax.broadcasted_iota(jnp.int32, (L, 1), 0).astype(f32)
    lgs = _log_gammas()
    intra = [jnp.where(causal, jnp.exp(diff * lg), 0.0) for lg in lgs]
    q_dec = [jnp.exp((tcol + 1.0) * lg) for lg in lgs]
    k_dec = [jnp.exp((L - 1.0 - tcol) * lg) for lg in lgs]
    s_dec = [math.exp(L * lg) for lg in lgs]
    kscale = D_HEAD ** -0.5

    for c in range(Lg // L):
        rows = slice(c * L, (c + 1) * L)
        cos = cos_ref[rows, :]
        sin = sin_ref[rows, :]
        cosT = cosT_ref[:, rows]
        sinT = sinT_ref[:, rows]
        for h in range(N_HEADS):
            cs = slice(h * D_HEAD, (h + 1) * D_HEAD)
            q1 = q_ref[rows, h * D_HEAD:h * D_HEAD + HALF].astype(f32)
            q2 = q_ref[rows, h * D_HEAD + HALF:(h + 1) * D_HEAD].astype(f32)
            qb = jnp.concatenate([q1 * cos - q2 * sin, q1 * sin + q2 * cos], axis=-1).astype(bf16)
            k1 = kt_ref[h * D_HEAD:h * D_HEAD + HALF, rows].astype(f32)
            k2 = kt_ref[h * D_HEAD + HALF:(h + 1) * D_HEAD, rows].astype(f32)
            kTb = (jnp.concatenate([k1 * cosT - k2 * sinT, k1 * sinT + k2 * cosT], axis=0)
                   * kscale).astype(bf16)
            v = v_ref[rows, cs]
            sc = _dot(qb, kTb) * intra[h]
            s = s_s[h]
            o = _dot(sc.astype(bf16), v) + q_dec[h] * _dot(qb, s.astype(bf16))
            s_s[h] = s_dec[h] * s + _dot(kTb, (v.astype(f32) * k_dec[h]).astype(bf16))
            zs = zs_ref[rows, cs].astype(f32)
            o_ref[rows, cs] = (_head_norm(o, hng_ref[:, cs]) * zs).astype(bf16)

    @pl.when(g == ng - 1)
    def _():
        s_out_ref[...] = s_s[...]


def _ret_prompt(P, KT, B, T, tabs, hng):
    R = P.shape[0]
    Lg = _pick(T, (512, 256, 128))
    ng = T // Lg
    col = lambda k: (lambda b, g: (b * ng + g, k))
    cos, sin, cosT, sinT = tabs
    return pl.pallas_call(
        _ret_kernel,
        grid=(B, ng),
        in_specs=[
            pl.BlockSpec((Lg, W_MIX), col(3)),
            pl.BlockSpec((Lg, W_MIX), col(4)),
            pl.BlockSpec((Lg, W_MIX), col(5)),
            pl.BlockSpec((W_MIX, Lg), lambda b, g: (0, b * ng + g)),
            pl.BlockSpec((Lg, HALF), lambda b, g: (g, 0)),
            pl.BlockSpec((Lg, HALF), lambda b, g: (g, 0)),
            pl.BlockSpec((HALF, Lg), lambda b, g: (0, g)),
            pl.BlockSpec((HALF, Lg), lambda b, g: (0, g)),
            pl.BlockSpec((1, W_MIX), lambda b, g: (0, 0)),
        ],
        out_specs=[
            pl.BlockSpec((Lg, W_MIX), col(0)),
            pl.BlockSpec((None, N_HEADS, D_HEAD, D_HEAD), lambda b, g: (b, 0, 0, 0)),
        ],
        out_shape=[
            jax.ShapeDtypeStruct((R, W_MIX), bf16),
            jax.ShapeDtypeStruct((B, N_HEADS, D_HEAD, D_HEAD), f32),
        ],
        scratch_shapes=[pltpu.VMEM((N_HEADS, D_HEAD, D_HEAD), f32)],
        compiler_params=_cparams(("arbitrary", "arbitrary")),
        name="ret_prompt",
    )(P, P, P, KT, cos, sin, cosT, sinT, hng)


_SEQ_PER_STEP = 4


def _decode_kernel(um_ref, vm_ref, zsm_ref, qr_ref, vr_ref, zsr_ref, kt_ref, gi_ref,
                   conv_ref, n_ref, m_ref, c_ref, s_ref,
                   convw_ref, convb_ref, wq_ref, wk_ref, brow_ref, hngm_ref, skip_ref, hngr_ref,
                   cosd_ref, sind_ref, cosdT_ref, sindT_ref,
                   om_in_ref, or_in_ref, cprev_ref, sprev_ref,
                   om_ref, or_ref, c_out_ref, s_out_ref, n_out_ref, m_out_ref, conv_out_ref,
                   vt_s, qrt_s, krt_s, qm_s, km_s, vr_s, sct_s, ht_s, oacc_s, cm_s):
    del om_in_ref, or_in_ref, cprev_ref, sprev_ref
    s = pl.program_id(0)
    ns = pl.num_programs(0)
    NB = um_ref.shape[0]
    scale = D_HEAD ** -0.5
    lgs = _log_gammas()
    gammas = [math.exp(lg) for lg in lgs]
    lane = lax.broadcasted_iota(jnp.int32, (1, LANES), 1)

    @pl.when(s == 0)
    def _():
        u = um_ref[...].astype(f32)
        acc = convb_ref[...] + u * convw_ref[CONV_W - 1:CONV_W, :]
        for i in range(CONV_W - 1):
            acc = acc + conv_ref[:, i * W_MIX:(i + 1) * W_MIX] * convw_ref[i:i + 1, :]
        cm = _silu(acc)
        cm_s[...] = cm
        conv_out_ref[:, 0:W_MIX] = conv_ref[:, W_MIX:2 * W_MIX]
        conv_out_ref[:, W_MIX:2 * W_MIX] = conv_ref[:, 2 * W_MIX:3 * W_MIX]
        conv_out_ref[:, 2 * W_MIX:3 * W_MIX] = u

        gb = gi_ref[...] + brow_ref[...]
        lf = _log_sigmoid(gb)
        sc = jnp.zeros((NB, LANES), f32)
        mo = jnp.zeros((NB, LANES), f32)
        for h in range(N_HEADS):
            cs = slice(h * D_HEAD, (h + 1) * D_HEAD)
            chb = cm[:, cs].astype(bf16)
            q = _dot(chb, wq_ref[h]) * scale
            k = _dot(chb, wk_ref[h])
            qm_s[:, cs] = q
            km_s[:, cs] = k
            ig = gb[:, h:h + 1]
            lfh = lf[:, 4 + h:5 + h]
            mp = m_ref[:, h:h + 1]
            inter = lfh + mp
            m_t = jnp.maximum(inter, ig)
            w = jnp.exp(ig - m_t)
            a = jnp.exp(inter - m_t)
            nh = n_ref[:, cs]
            ws = w * jnp.sum(q * k, axis=-1, keepdims=True)
            den = ws + a * jnp.sum(nh * q, axis=-1, keepdims=True)
            den = jnp.maximum(jnp.abs(den), jnp.exp(-m_t))
            n_out_ref[:, cs] = a * nh + w * k
            for r, val in ((h, ws), (4 + h, a), (8 + h, den), (12 + h, w)):
                sc = sc + val * (lane == r).astype(f32)
            mo = mo + m_t * (lane == h).astype(f32)
        m_out_ref[...] = mo
        sct_s[...] = sc.T
        vt_s[...] = vm_ref[...].astype(f32).T
        vr_s[...] = vr_ref[...].astype(f32)
        cosd = cosd_ref[0:1, :]
        sind = sind_ref[0:1, :]
        cosdT = cosdT_ref[...]
        sindT = sindT_ref[...]
        for h in range(N_HEADS):
            lo = slice(h * D_HEAD, h * D_HEAD + HALF)
            hi = slice(h * D_HEAD + HALF, (h + 1) * D_HEAD)
            q1 = qr_ref[:, lo].astype(f32)
            q2 = qr_ref[:, hi].astype(f32)
            qrot = jnp.concatenate([q1 * cosd - q2 * sind, q1 * sind + q2 * cosd], axis=-1)
            qrt_s[h * D_HEAD:(h + 1) * D_HEAD, :] = qrot.T
            k1 = kt_ref[lo, :].astype(f32)
            k2 = kt_ref[hi, :].astype(f32)
            krt_s[lo, :] = (k1 * cosdT - k2 * sindT) * scale
            krt_s[hi, :] = (k1 * sindT + k2 * cosdT) * scale
        ht_s[...] = jnp.zeros(ht_s.shape, f32)

    def take_col(mat, mask):
        return jnp.sum(jnp.where(mask, mat, 0.0), axis=-1, keepdims=True)

    for j in range(_SEQ_PER_STEP):
        b = s * _SEQ_PER_STEP + j
        mask = lane == b
        for h in range(N_HEADS):
            cs = slice(h * D_HEAD, (h + 1) * D_HEAD)
            c = c_ref[j, h]
            q_row = qm_s[pl.ds(b, 1), cs]
            k_row = km_s[pl.ds(b, 1), cs]
            v_col = take_col(vt_s[cs, :], mask)
            ws = take_col(sct_s[h:h + 1, :], mask)
            a = take_col(sct_s[4 + h:5 + h, :], mask)
            den = take_col(sct_s[8 + h:9 + h, :], mask)
            w = take_col(sct_s[12 + h:13 + h, :], mask)
            cq = jnp.sum(c * q_row, axis=-1, keepdims=True)
            hcol = (ws * v_col + a * cq) / den
            ht_s[cs, :] = jnp.where(mask, hcol, ht_s[cs, :])
            c_out_ref[j, h] = a * c + (w * v_col) * k_row
            st = s_ref[j, h]
            q_col = take_col(qrt_s[cs, :], mask)
            k_col = take_col(krt_s[cs, :], mask)
            v_row = vr_s[pl.ds(b, 1), cs]
            qk = jnp.sum(q_col * k_col, axis=0, keepdims=True)
            o_row = qk * v_row + gammas[h] * jnp.sum(st * q_col, axis=0, keepdims=True)
            oacc_s[pl.ds(b, 1), cs] = o_row
            s_out_ref[j, h] = gammas[h] * st + k_col * v_row

    @pl.when(s == ns - 1)
    def _():
        hm = ht_s[...].T
        for h in range(N_HEADS):
            cs = slice(h * D_HEAD, (h + 1) * D_HEAD)
            hn = _head_norm(hm[:, cs], hngm_ref[:, cs])
            zsm = zsm_ref[:, cs].astype(f32)
            om_ref[:, cs] = ((hn + skip_ref[:, cs] * cm_s[:, cs]) * zsm).astype(bf16)
            zsr = zsr_ref[:, cs].astype(f32)
            or_ref[:, cs] = (_head_norm(oacc_s[:, cs], hngr_ref[:, cs]) * zsr).astype(bf16)


def _decode_step(P, KT, GI, OM, OR, Cprev, Sprev, l, BT, conv_st, n_st, m_st, C_st, S_st,
                 conv_w, conv_b, wq, wk, brow, hngm, skip, hngr, dtabs):
    R = P.shape[0]
    NB = R - BT
    assert NB == LANES and BT % NB == 0 and NB % _SEQ_PER_STEP == 0
    rb = BT // NB
    depth = C_st.shape[0]
    bb = _SEQ_PER_STEP
    cosd, sind, cosdT, sindT = dtabs
    pcol = lambda k: (lambda s: (rb, k))
    rep2 = lambda s: (0, 0)
    rep3 = lambda s: (0, 0, 0)
    st_spec = pl.BlockSpec((None, bb, N_HEADS, D_HEAD, D_HEAD), lambda s: (l, s, 0, 0, 0))
    any_spec = pl.BlockSpec(memory_space=pl.ANY)
    in_specs = [
        pl.BlockSpec((NB, W_MIX), pcol(0)), pl.BlockSpec((NB, W_MIX), pcol(1)),
        pl.BlockSpec((NB, W_MIX), pcol(2)), pl.BlockSpec((NB, W_MIX), pcol(3)),
        pl.BlockSpec((NB, W_MIX), pcol(4)), pl.BlockSpec((NB, W_MIX), pcol(5)),
        pl.BlockSpec((W_MIX, NB), lambda s: (0, rb)),
        pl.BlockSpec((NB, LANES), lambda s: (rb, 0)),
        pl.BlockSpec((None, NB, (CONV_W - 1) * W_MIX), lambda s: (l, 0, 0)),
        pl.BlockSpec((None, NB, W_MIX), lambda s: (l, 0, 0)),
        pl.BlockSpec((None, NB, N_HEADS), lambda s: (l, 0, 0)),
        st_spec, st_spec,
        pl.BlockSpec((CONV_W, W_MIX), rep2), pl.BlockSpec((1, W_MIX), rep2),
        pl.BlockSpec((N_HEADS, D_HEAD, D_HEAD), rep3), pl.BlockSpec((N_HEADS, D_HEAD, D_HEAD), rep3),
        pl.BlockSpec((1, LANES), rep2),
        pl.BlockSpec((1, W_MIX), rep2), pl.BlockSpec((1, W_MIX), rep2), pl.BlockSpec((1, W_MIX), rep2),
        pl.BlockSpec((8, HALF), rep2), pl.BlockSpec((8, HALF), rep2),
        pl.BlockSpec((HALF, LANES), rep2), pl.BlockSpec((HALF, LANES), rep2),
        any_spec, any_spec,
    ]
    args = [P, P, P, P, P, P, KT, GI, conv_st, n_st, m_st, C_st, S_st,
            conv_w, conv_b, wq, wk, brow, hngm, skip, hngr, cosd, sind, cosdT, sindT, OM, OR]
    aliases = {len(args) - 2: 0, len(args) - 1: 1}
    if Cprev is not None:
        in_specs += [any_spec, any_spec]
        args += [Cprev, Sprev]
        aliases[len(args) - 2] = 2
        aliases[len(args) - 1] = 3
        kern = _decode_kernel
    else:
        kern = functools.partial(_decode_kernel_first)
    out_row = pl.BlockSpec((NB, W_MIX), lambda s: (rb, 0))
    return pl.pallas_call(
        kern,
        grid=(NB // bb,),
        in_specs=in_specs,
        out_specs=[
            out_row, out_row, st_spec, st_spec,
            pl.BlockSpec((NB, W_MIX), rep2),
            pl.BlockSpec((NB, LANES), rep2),
            pl.BlockSpec((NB, (CONV_W - 1) * W_MIX), rep2),
        ],
        out_shape=[
            jax.ShapeDtypeStruct(OM.shape, bf16),
            jax.ShapeDtypeStruct(OR.shape, bf16),
            jax.ShapeDtypeStruct((depth, NB, N_HEADS, D_HEAD, D_HEAD), f32),
            jax.ShapeDtypeStruct((depth, NB, N_HEADS, D_HEAD, D_HEAD), f32),
            jax.ShapeDtypeStruct((NB, W_MIX), f32),
            jax.ShapeDtypeStruct((NB, LANES), f32),
            jax.ShapeDtypeStruct((NB, (CONV_W - 1) * W_MIX), f32),
        ],
        scratch_shapes=[
            pltpu.VMEM((W_MIX, NB), f32),
            pltpu.VMEM((W_MIX, NB), f32),
            pltpu.VMEM((W_MIX, NB), f32),
            pltpu.VMEM((NB, W_MIX), f32),
            pltpu.VMEM((NB, W_MIX), f32),
            pltpu.VMEM((NB, W_MIX), f32),
            pltpu.VMEM((LANES, NB), f32),
            pltpu.VMEM((W_MIX, NB), f32),
            pltpu.VMEM((NB, W_MIX), f32),
            pltpu.VMEM((NB, W_MIX), f32),
        ],
        input_output_aliases=aliases,
        compiler_params=_cparams(("arbitrary",)),
        name="decode_step",
    )(*args)


def _decode_kernel_first(*refs):
    n_in = 27
    ins, rest = refs[:n_in], refs[n_in:]
    return _decode_kernel(*ins, None, None, *rest)


def _merge_kernel(h_ref, om_ref, or_ref, wgm_ref, wgr_ref, bgm_ref, bgr_ref, wpm_ref, wpr_ref, o_ref):
    h = h_ref[...]
    gm = jax.nn.sigmoid(_dot(h, wgm_ref[...].astype(bf16)) + bgm_ref[...])
    gr = jax.nn.sigmoid(_dot(h, wgr_ref[...].astype(bf16)) + bgr_ref[...])
    pm = _dot(om_ref[...], wpm_ref[...].astype(bf16))
    pr = _dot(or_ref[...], wpr_ref[...].astype(bf16))
    o_ref[...] = (gm * pm + gr * pr).astype(bf16)


def _merge(h, OM, OR, w_gate, b_gate3, w_pm, w_pr, l):
    R = h.shape[0]
    tm = _pick(R, (1664, 1280, 640, 384, 128))
    tn = 256
    nj = D_MODEL // tn
    return pl.pallas_call(
        _merge_kernel,
        grid=(R // tm, nj),
        in_specs=[
            pl.BlockSpec((tm, D_MODEL), lambda i, j: (i, 0)),
            pl.BlockSpec((tm, W_MIX), lambda i, j: (i, 0)),
            pl.BlockSpec((tm, W_MIX), lambda i, j: (i, 0)),
            pl.BlockSpec((None, D_MODEL, tn), lambda i, j: (l, 0, j)),
            pl.BlockSpec((None, D_MODEL, tn), lambda i, j: (l, 0, j + nj)),
            pl.BlockSpec((None, 1, tn), lambda i, j: (l, 0, j)),
            pl.BlockSpec((None, 1, tn), lambda i, j: (l, 0, j + nj)),
            pl.BlockSpec((None, W_MIX, tn), lambda i, j: (l, 0, j)),
            pl.BlockSpec((None, W_MIX, tn), lambda i, j: (l, 0, j)),
        ],
        out_specs=pl.BlockSpec((tm, tn), lambda i, j: (i, j)),
        out_shape=jax.ShapeDtypeStruct((R, D_MODEL), bf16),
        compiler_params=_cparams(("arbitrary", "arbitrary")),
        name="merge",
    )(h, OM, OR, w_gate, w_gate, b_gate3, b_gate3, w_pm, w_pr)


def _outproj_kernel(mg_ref, wo_ref, x_ref, o_ref):
    o_ref[...] = x_ref[...] + _dot(mg_ref[...], wo_ref[...])


def _outproj_norm_kernel(mg_ref, wo_ref, x_ref, g_ref, o_ref, h_ref):
    y = x_ref[...] + _dot(mg_ref[...], wo_ref[...])
    o_ref[...] = y
    ms = jnp.mean(y * y, axis=-1, keepdims=True)
    h_ref[...] = (y * lax.rsqrt(ms + EPS) * g_ref[...]).astype(bf16)


def _outproj(merged, w_o_bf, x, l, g_next):
    R = x.shape[0]
    tm = _pick(R, (640, 512, 384, 128))
    row = pl.BlockSpec((tm, D_MODEL), lambda i: (i, 0))
    in_specs = [
        row,
        pl.BlockSpec((None, D_MODEL, D_MODEL), lambda i: (l, 0, 0), pipeline_mode=pl.Buffered(1)),
        row,
    ]
    if g_next is None:
        return pl.pallas_call(
            _outproj_kernel,
            grid=(R // tm,),
            in_specs=in_specs,
            out_specs=row,
            out_shape=jax.ShapeDtypeStruct((R, D_MODEL), f32),
            compiler_params=_cparams(("arbitrary",)),
            name="outproj",
        )(merged, w_o_bf, x), None
    return pl.pallas_call(
        _outproj_norm_kernel,
        grid=(R // tm,),
        in_specs=in_specs + [pl.BlockSpec((1, D_MODEL), lambda i: (0, 0))],
        out_specs=[row, row],
        out_shape=[jax.ShapeDtypeStruct((R, D_MODEL), f32), jax.ShapeDtypeStruct((R, D_MODEL), bf16)],
        compiler_params=_cparams(("arbitrary",)),
        name="outproj_norm",
    )(merged, w_o_bf, x, g_next)


def kernel(x_prompt, x_sample, state_mlstm_C, state_mlstm_n, state_mlstm_m, state_mlstm_conv, state_ret_S,
           norm_g, w_in, conv_w, conv_b, w_qm, w_km, b_i, b_f, hn_m_g, skip_m, hn_r_g,
           w_pm, w_pr, w_gate, b_gate, w_o, final_g):
    B, T, D = x_prompt.shape
    NB = x_sample.shape[0]
    depth = w_in.shape[0]
    BT = B * T
    assert D == D_MODEL and x_sample.shape[1] == 1 and T % CHUNK == 0

    x = jnp.concatenate([x_prompt.reshape(BT, D), x_sample.reshape(NB, D)], axis=0)

    w_inT = jnp.swapaxes(w_in, 1, 2)
    wq = w_qm.astype(bf16)
    wk = w_km.astype(bf16)
    wkT = jnp.swapaxes(w_km, 2, 3).astype(bf16)
    w_o_bf = w_o.astype(bf16)
    bias = jnp.pad(jnp.concatenate([b_i, b_f], axis=1), ((0, 0), (0, LANES - 2 * N_HEADS)))
    b_gate3 = b_gate.reshape(depth, 1, 2 * D)
    conv_st = state_mlstm_conv.reshape(depth, NB, (CONV_W - 1) * W_MIX)
    n_st = state_mlstm_n.reshape(depth, NB, W_MIX)

    cos, sin, cosT, sinT, cosd, sind, cosdT, sindT = _rope_tables(T)

    outs_p = [[], [], [], [], []]
    outs_s = [[], [], []]
    Cs = Ss = None
    h = _rmsnorm(x, norm_g[0].reshape(1, D), bf16)
    for l in range(depth):
        P, KT, GI = _inproj(h, w_inT, l)
        brow = bias[l].reshape(1, LANES)
        bcol = bias[l, :8].reshape(8, 1)
        cw, cb = conv_w[l], conv_b[l].reshape(1, W_MIX)
        hngm, skp, hngr = (hn_m_g[l].reshape(1, W_MIX), skip_m[l].reshape(1, W_MIX),
                           hn_r_g[l].reshape(1, W_MIX))
        OM, Cp, np_, mp, convp = _mlstm_prompt(P, GI, B, T, cw, cb, wq[l], wkT[l], brow, bcol, hngm, skp)
        OR, Sp = _ret_prompt(P, KT, B, T, (cos, sin, cosT, sinT), hngr)
        OM, OR, Cs, Ss, n_s, m_s, conv_s = _decode_step(
            P, KT, GI, OM, OR, Cs, Ss, l, BT, conv_st, n_st, state_mlstm_m, state_mlstm_C, state_ret_S,
            cw, cb, wq[l], wk[l], brow, hngm, skp, hngr, (cosd, sind, cosdT, sindT))
        merged = _merge(h, OM, OR, w_gate, b_gate3, w_pm, w_pr, l)
        g_next = norm_g[l + 1].reshape(1, D) if l + 1 < depth else None
        x, h = _outproj(merged, w_o_bf, x, l, g_next)
        for k, val in enumerate((Cp, np_.reshape(B, N_HEADS, D_HEAD), mp[:, :N_HEADS, 0], convp, Sp)):
            outs_p[k].append(val)
        for k, val in enumerate((n_s.reshape(NB, N_HEADS, D_HEAD), m_s[:, :N_HEADS],
                                 conv_s.reshape(NB, CONV_W - 1, W_MIX))):
            outs_s[k].append(val)

    fg = final_g.reshape(1, D)
    y_prompt = _rmsnorm(x, fg, x_prompt.dtype, 0, BT).reshape(B, T, D)
    y_sample = _rmsnorm(x, fg, x_sample.dtype, BT, NB).reshape(NB, 1, D)
    return (y_prompt, y_sample,
            jnp.stack(outs_p[0]), jnp.stack(outs_p[1]), jnp.stack(outs_p[2]),
            jnp.stack(outs_p[3]), jnp.stack(outs_p[4]),
            Cs, jnp.stack(outs_s[0]), jnp.stack(outs_s[1]), jnp.stack(outs_s[2]), Ss)
```

```python
import functools
import math

import jax
import jax.numpy as jnp
import numpy as np
from jax import lax
from jax.experimental import pallas as pl
from jax.experimental.pallas import tpu as pltpu

f32 = jnp.float32
bf16 = jnp.bfloat16

D_MODEL = 2048
N_HEADS = 4
D_HEAD = 256
W_MIX = N_HEADS * D_HEAD
CONV_W = 4
CHUNK = 128
ROPE_BASE = 10000.0
EPS = 1e-6
NEG = -1e30
PAST_LEN = 16384
LANES = 128
HALF = D_HEAD // 2
VMEM_LIMIT = 56 * 1024 * 1024

OFF_IF = 3 * W_MIX
OFF_RET = 3 * W_MIX + 2 * N_HEADS


def _pick(n, cands):
    for c in cands:
        if n % c == 0:
            return c
    return n


def _cparams(sem):
    return pltpu.CompilerParams(dimension_semantics=sem, vmem_limit_bytes=VMEM_LIMIT)


def _dot(a, b):
    return jnp.dot(a, b, preferred_element_type=f32)


def _dot_nt(a, b):
    return lax.dot_general(a, b, (((1,), (1,)), ((), ())), preferred_element_type=f32)


def _split3(x):
    hi = x.astype(bf16).astype(f32)
    r = x - hi
    mid = r.astype(bf16).astype(f32)
    lo = (r - mid).astype(bf16).astype(f32)
    return hi, mid, lo


def _silu(x):
    return x * jax.nn.sigmoid(x)


def _log_sigmoid(x):
    return jnp.minimum(x, 0.0) - jnp.log1p(jnp.exp(-jnp.abs(x)))


def _head_norm(hc, g):
    mu = jnp.mean(hc, axis=-1, keepdims=True)
    d = hc - mu
    var = jnp.mean(d * d, axis=-1, keepdims=True)
    return d * lax.rsqrt(var + EPS) * g


def _rope_kernel(inv_ref, invT_ref, cos_ref, sin_ref, cosT_ref, sinT_ref,
                 cosd_ref, sind_ref, cosdT_ref, sindT_ref):
    T = cos_ref.shape[0]
    pos = lax.broadcasted_iota(jnp.int32, (T, HALF), 0).astype(f32)
    ang = pos * inv_ref[...]
    cos_ref[...] = jnp.cos(ang)
    sin_ref[...] = jnp.sin(ang)
    posT = lax.broadcasted_iota(jnp.int32, (HALF, T), 1).astype(f32)
    angT = posT * invT_ref[...]
    cosT_ref[...] = jnp.cos(angT)
    sinT_ref[...] = jnp.sin(angT)
    angd = jnp.full((8, HALF), float(PAST_LEN), f32) * inv_ref[...]
    cosd_ref[...] = jnp.cos(angd)
    sind_ref[...] = jnp.sin(angd)
    angdT = jnp.full((HALF, LANES), float(PAST_LEN), f32) * invT_ref[...]
    cosdT_ref[...] = jnp.cos(angdT)
    sindT_ref[...] = jnp.sin(angdT)


def _rope_tables(T):
    inv = ROPE_BASE ** (-jnp.arange(0, HALF, dtype=f32) / HALF)
    shapes = [(T, HALF), (T, HALF), (HALF, T), (HALF, T),
              (8, HALF), (8, HALF), (HALF, LANES), (HALF, LANES)]
    return pl.pallas_call(
        _rope_kernel,
        out_shape=[jax.ShapeDtypeStruct(s, f32) for s in shapes],
        name="rope_tables",
    )(inv.reshape(1, HALF), inv.reshape(HALF, 1))


def _rms_kernel(x_ref, g_ref, o_ref):
    x = x_ref[...]
    ms = jnp.mean(x * x, axis=-1, keepdims=True)
    o_ref[...] = (x * lax.rsqrt(ms + EPS) * g_ref[...]).astype(o_ref.dtype)


def _rmsnorm(x, g, out_dtype, row0=0, nrows=None):
    R, D = x.shape
    nrows = R if nrows is None else nrows
    tr = _pick(nrows, (832, 512, 256, 128))
    assert row0 % tr == 0
    off = row0 // tr
    return pl.pallas_call(
        _rms_kernel,
        grid=(nrows // tr,),
        in_specs=[pl.BlockSpec((tr, D), lambda i: (i + off, 0)),
                  pl.BlockSpec((1, D), lambda i: (0, 0))],
        out_specs=pl.BlockSpec((tr, D), lambda i: (i, 0)),
        out_shape=jax.ShapeDtypeStruct((nrows, D), out_dtype),
        compiler_params=_cparams(("arbitrary",)),
        name="rmsnorm",
    )(x, g)


_TN_IN = 512
_TPG = W_MIX // _TN_IN
_NJ_P = 6 * _TPG
_NJ_K = _TPG


def _inproj_row_offset(j):
    jp = jnp.minimum(j, _NJ_P - 1)
    grp = jp // _TPG
    sub = jp - grp * _TPG
    src_grp = jnp.where(grp < 3, grp, jnp.where(grp == 3, 3, grp + 1))
    off_p = src_grp * W_MIX + jnp.where(grp < 3, 0, 2 * N_HEADS) + sub * _TN_IN
    off_k = OFF_RET + W_MIX + jnp.clip(j - _NJ_P, 0, _NJ_K - 1) * _TN_IN
    return jnp.where(j < _NJ_P, off_p, off_k)


def _inproj_kernel(h_ref, wt_ref, wif_ref, p_ref, kt_ref, gi_ref):
    j = pl.program_id(1)

    def weights():
        return wt_ref[0].astype(bf16)

    @pl.when(j == 0)
    def _():
        wif = jnp.concatenate(
            [wif_ref[...], jnp.zeros((LANES - 2 * N_HEADS, D_MODEL), f32)], axis=0).astype(bf16)
        gi_ref[...] = _dot_nt(h_ref[...], wif)

    is_z = (j // _TPG == 2) | (j // _TPG == 5)

    @pl.when((j < _NJ_P) & jnp.logical_not(is_z))
    def _():
        p_ref[...] = _dot_nt(h_ref[...], weights()).astype(bf16)

    @pl.when((j < _NJ_P) & is_z)
    def _():
        p_ref[...] = _silu(_dot_nt(h_ref[...], weights())).astype(bf16)

    @pl.when(j >= _NJ_P)
    def _():
        kt_ref[...] = _dot_nt(weights(), h_ref[...]).astype(bf16)


def _inproj(h, w_inT, l):
    R = h.shape[0]
    tm = _pick(R, (1664, 1280, 640, 384, 128))
    tn = _TN_IN
    nj = _NJ_P + _NJ_K
    return pl.pallas_call(
        _inproj_kernel,
        grid=(R // tm, nj),
        in_specs=[
            pl.BlockSpec((tm, D_MODEL), lambda i, j: (i, 0)),
            pl.BlockSpec((pl.Element(1), pl.Element(tn), pl.Element(D_MODEL)),
                         lambda i, j: (l, pl.multiple_of(_inproj_row_offset(j), 8), 0)),
            pl.BlockSpec((None, 2 * N_HEADS, D_MODEL), lambda i, j: (l, OFF_IF // (2 * N_HEADS), 0)),
        ],
        out_specs=[
            pl.BlockSpec((tm, tn), lambda i, j: (i, jnp.minimum(j, _NJ_P - 1))),
            pl.BlockSpec((tn, tm), lambda i, j: (jnp.clip(j - _NJ_P, 0, _NJ_K - 1), i)),
            pl.BlockSpec((tm, LANES), lambda i, j: (i, 0)),
        ],
        out_shape=[
            jax.ShapeDtypeStruct((R, 6 * W_MIX), bf16),
            jax.ShapeDtypeStruct((W_MIX, R), bf16),
            jax.ShapeDtypeStruct((R, LANES), f32),
        ],
        compiler_params=_cparams(("arbitrary", "arbitrary")),
        name="inproj",
    )(h, w_inT, w_inT)


def _mlstm_kernel(um_ref, vm_ref, zs_ref, gi_ref, convw_ref, convb_ref, wq_ref, wkT_ref,
                  brow_ref, bcol_ref, hng_ref, skip_ref,
                  om_ref, c_out_ref, n_out_ref, m_out_ref, conv_out_ref,
                  ubuf, cmbuf, cta_s, m_s):
    g = pl.program_id(1)
    ng = pl.num_programs(1)
    Lg = um_ref.shape[0]
    L = CHUNK
    DA = D_HEAD + LANES
    scale = D_HEAD ** -0.5
    heads = range(N_HEADS)

    @pl.when(g == 0)
    def _():
        ubuf[0:8, :] = jnp.zeros((8, W_MIX), f32)
        cta_s[...] = jnp.zeros(cta_s.shape, f32)
        m_s[...] = jnp.full(m_s.shape, NEG, f32)

    ubuf[8:8 + Lg, :] = um_ref[...].astype(f32)
    acc = convb_ref[...] + ubuf[5:5 + Lg, :] * convw_ref[0:1, :]
    for i in range(1, CONV_W):
        acc = acc + ubuf[5 + i:5 + i + Lg, :] * convw_ref[i:i + 1, :]
    cmbuf[...] = _silu(acc)
    last3 = ubuf[5 + Lg:8 + Lg, :]
    ubuf[5:8, :] = last3

    row = lax.broadcasted_iota(jnp.int32, (L, L), 0)
    col = lax.broadcasted_iota(jnp.int32, (L, L), 1)
    causal = row >= col
    tril = causal.astype(bf16)
    triu = (row <= col).astype(bf16)
    brow = brow_ref[...]
    bcol = bcol_ref[...]
    ones_b = jnp.ones((L, LANES), bf16)
    ones_f = jnp.ones((L, LANES), f32)

    for c in range(Lg // L):
        rows = slice(c * L, (c + 1) * L)
        graw = gi_ref[rows, :]
        gb = graw + brow
        gtb = graw.T[0:8, :] + bcol
        lf = _log_sigmoid(gb)
        lft = _log_sigmoid(gtb)
        bc3 = _dot(tril, jnp.concatenate(_split3(lf), axis=1).astype(bf16))
        bc_all = bc3[:, 0:LANES] + bc3[:, LANES:2 * LANES] + bc3[:, 2 * LANES:3 * LANES]
        br3 = _dot(jnp.concatenate(_split3(lft), axis=0).astype(bf16), triu)
        br_all = br3[0:8, :] + br3[8:16, :] + br3[16:24, :]

        cs = [slice(h * D_HEAD, (h + 1) * D_HEAD) for h in heads]
        ch = [cmbuf[rows, cs[h]] for h in heads]
        chb = [x.astype(bf16) for x in ch]
        qb = [(_dot(chb[h], wq_ref[h]) * scale).astype(bf16) for h in heads]
        kTb = [_dot_nt(wkT_ref[h], chb[h]).astype(bf16) for h in heads]
        v = [vm_ref[rows, cs[h]] for h in heads]
        bc = [bc_all[:, 4 + h:5 + h] for h in heads]
        ic = [gb[:, h:h + 1] for h in heads]
        m_prev = [m_s[h][0:1, 0:1] for h in heads]
        dlog = [jnp.where(causal, bc[h] - br_all[4 + h:5 + h, :] + gtb[h:h + 1, :], NEG) for h in heads]
        inter = [bc[h] + m_prev[h] for h in heads]
        m_t = [jnp.maximum(inter[h], jnp.max(dlog[h], axis=-1, keepdims=True)) for h in heads]
        a = [jnp.exp(inter[h] - m_t[h]) for h in heads]
        ws = [(jnp.exp(dlog[h] - m_t[h]) * _dot(qb[h], kTb[h])).astype(bf16) for h in heads]
        cta = [cta_s[h] for h in heads]
        nd = [_dot(ws[h], jnp.concatenate([v[h], ones_b], axis=1))
              + a[h] * _dot(qb[h], cta[h].astype(bf16)) for h in heads]
        m_new = [m_t[h][L - 1:L, :] for h in heads]
        b_last = [bc[h][L - 1:L, :] for h in heads]
        for h in heads:
            wk_c = jnp.exp(b_last[h] - bc[h] + ic[h] - m_new[h])
            dec = jnp.exp(b_last[h] + m_prev[h] - m_new[h])
            vaug = jnp.concatenate([v[h].astype(f32), ones_f], axis=1)
            cta_s[h] = dec * cta[h] + _dot(kTb[h], (wk_c * vaug).astype(bf16))
            m_s[h] = jnp.broadcast_to(m_new[h], (8, LANES))
        for h in heads:
            den = jnp.maximum(jnp.abs(nd[h][:, D_HEAD:DA]), jnp.exp(-m_t[h]))
            rden = 1.0 / den
            hc = jnp.concatenate([nd[h][:, 0:HALF] * rden, nd[h][:, HALF:D_HEAD] * rden], axis=1)
            hn = _head_norm(hc, hng_ref[:, cs[h]])
            zs = zs_ref[rows, cs[h]].astype(f32)
            om_ref[rows, cs[h]] = ((hn + skip_ref[:, cs[h]] * ch[h]) * zs).astype(bf16)

    @pl.when(g == ng - 1)
    def _():
        for h in heads:
            st = cta_s[h]
            c_out_ref[h] = st[:, 0:D_HEAD].T
            n_out_ref[h] = st[:, D_HEAD:DA].T[0:1, :]
        m_out_ref[...] = jnp.concatenate([m_s[h][0:1, :] for h in heads]
                                         + [jnp.zeros((8 - N_HEADS, LANES), f32)], axis=0)
        conv_out_ref[...] = last3


def _mlstm_prompt(P, GI, B, T, conv_w, conv_b, wq, wkT, brow, bcol, hng, skip):
    R = P.shape[0]
    Lg = _pick(T, (512, 256, 128))
    ng = T // Lg
    col = lambda k: (lambda b, g: (b * ng + g, k))
    rep2 = lambda b, g: (0, 0)
    rep3 = lambda b, g: (0, 0, 0)
    return pl.pallas_call(
        _mlstm_kernel,
        grid=(B, ng),
        in_specs=[
            pl.BlockSpec((Lg, W_MIX), col(0)),
            pl.BlockSpec((Lg, W_MIX), col(1)),
            pl.BlockSpec((Lg, W_MIX), col(2)),
            pl.BlockSpec((Lg, LANES), col(0)),
            pl.BlockSpec((CONV_W, W_MIX), rep2),
            pl.BlockSpec((1, W_MIX), rep2),
            pl.BlockSpec((N_HEADS, D_HEAD, D_HEAD), rep3),
            pl.BlockSpec((N_HEADS, D_HEAD, D_HEAD), rep3),
            pl.BlockSpec((1, LANES), rep2),
            pl.BlockSpec((8, 1), rep2),
            pl.BlockSpec((1, W_MIX), rep2),
            pl.BlockSpec((1, W_MIX), rep2),
        ],
        out_specs=[
            pl.BlockSpec((Lg, W_MIX), col(0)),
            pl.BlockSpec((None, N_HEADS, D_HEAD, D_HEAD), lambda b, g: (b, 0, 0, 0)),
            pl.BlockSpec((None, N_HEADS, 1, D_HEAD), lambda b, g: (b, 0, 0, 0)),
            pl.BlockSpec((None, 8, LANES), lambda b, g: (b, 0, 0)),
            pl.BlockSpec((None, CONV_W - 1, W_MIX), lambda b, g: (b, 0, 0)),
        ],
        out_shape=[
            jax.ShapeDtypeStruct((R, W_MIX), bf16),
            jax.ShapeDtypeStruct((B, N_HEADS, D_HEAD, D_HEAD), f32),
            jax.ShapeDtypeStruct((B, N_HEADS, 1, D_HEAD), f32),
            jax.ShapeDtypeStruct((B, 8, LANES), f32),
            jax.ShapeDtypeStruct((B, CONV_W - 1, W_MIX), f32),
        ],
        scratch_shapes=[
            pltpu.VMEM((8 + Lg, W_MIX), f32),
            pltpu.VMEM((Lg, W_MIX), f32),
            pltpu.VMEM((N_HEADS, D_HEAD, D_HEAD + LANES), f32),
            pltpu.VMEM((N_HEADS, 8, LANES), f32),
        ],
        compiler_params=_cparams(("arbitrary", "arbitrary")),
        name="mlstm_prompt",
    )(P, P, P, GI, conv_w, conv_b, wq, wkT, brow, bcol, hng, skip)


def _log_gammas():
    return [math.log(1.0 - 2.0 ** (-5.0 - h)) for h in range(N_HEADS)]


def _ret_kernel(q_ref, v_ref, zs_ref, kt_ref, cos_ref, sin_ref, cosT_ref, sinT_ref, hng_ref,
                o_ref, s_out_ref, s_s):
    g = pl.program_id(1)
    ng = pl.num_programs(1)
    Lg = q_ref.shape[0]
    L = CHUNK

    @pl.when(g == 0)
    def _():
        s_s[...] = jnp.zeros(s_s.shape, f32)

    row = lax.broadcasted_iota(jnp.int32, (L, L), 0)
    col = lax.broadcasted_iota(jnp.int32, (L, L), 1)
    causal = row >= col
    diff = (row - col).astype(f32)
    tcol = lax.broadcasted_iota(jnp.int32, (L, 1), 0).astype(f32)
    lgs = _log_gammas()
    intra = [jnp.where(causal, jnp.exp(diff * lg), 0.0) for lg in lgs]
    q_dec = [jnp.exp((tcol + 1.0) * lg) for lg in lgs]
    k_dec = [jnp.exp((L - 1.0 - tcol) * lg) for lg in lgs]
    s_dec = [math.exp(L * lg) for lg in lgs]
    kscale = D_HEAD ** -0.5

    for c in range(Lg // L):
        rows = slice(c * L, (c + 1) * L)
        cos = cos_ref[rows, :]
        sin = sin_ref[rows, :]
        cosT = cosT_ref[:, rows]
        sinT = sinT_ref[:, rows]
        for h in range(N_HEADS):
            cs = slice(h * D_HEAD, (h + 1) * D_HEAD)
            q1 = q_ref[rows, h * D_HEAD:h * D_HEAD + HALF].astype(f32)
            q2 = q_ref[rows, h * D_HEAD + HALF:(h + 1) * D_HEAD].astype(f32)
            qb = jnp.concatenate([q1 * cos - q2 * sin, q1 * sin + q2 * cos], axis=-1).astype(bf16)
            k1 = kt_ref[h * D_HEAD:h * D_HEAD + HALF, rows].astype(f32)
            k2 = kt_ref[h * D_HEAD + HALF:(h + 1) * D_HEAD, rows].astype(f32)
            kTb = (jnp.concatenate([k1 * cosT - k2 * sinT, k1 * sinT + k2 * cosT], axis=0)
                   * kscale).astype(bf16)
            v = v_ref[rows, cs]
            sc = _dot(qb, kTb) * intra[h]
            s = s_s[h]
            o = _dot(sc.astype(bf16), v) + q_dec[h] * _dot(qb, s.astype(bf16))
            s_s[h] = s_dec[h] * s + _dot(kTb, (v.astype(f32) * k_dec[h]).astype(bf16))
            zs = zs_ref[rows, cs].astype(f32)
            o_ref[rows, cs] = (_head_norm(o, hng_ref[:, cs]) * zs).astype(bf16)

    @pl.when(g == ng - 1)
    def _():
        s_out_ref[...] = s_s[...]


def _ret_prompt(P, KT, B, T, tabs, hng):
    R = P.shape[0]
    Lg = _pick(T, (512, 256, 128))
    ng = T // Lg
    col = lambda k: (lambda b, g: (b * ng + g, k))
    cos, sin, cosT, sinT = tabs
    return pl.pallas_call(
        _ret_kernel,
        grid=(B, ng),
        in_specs=[
            pl.BlockSpec((Lg, W_MIX), col(3)),
            pl.BlockSpec((Lg, W_MIX), col(4)),
            pl.BlockSpec((Lg, W_MIX), col(5)),
            pl.BlockSpec((W_MIX, Lg), lambda b, g: (0, b * ng + g)),
            pl.BlockSpec((Lg, HALF), lambda b, g: (g, 0)),
            pl.BlockSpec((Lg, HALF), lambda b, g: (g, 0)),
            pl.BlockSpec((HALF, Lg), lambda b, g: (0, g)),
            pl.BlockSpec((HALF, Lg), lambda b, g: (0, g)),
            pl.BlockSpec((1, W_MIX), lambda b, g: (0, 0)),
        ],
        out_specs=[
            pl.BlockSpec((Lg, W_MIX), col(0)),
            pl.BlockSpec((None, N_HEADS, D_HEAD, D_HEAD), lambda b, g: (b, 0, 0, 0)),
        ],
        out_shape=[
            jax.ShapeDtypeStruct((R, W_MIX), bf16),
            jax.ShapeDtypeStruct((B, N_HEADS, D_HEAD, D_HEAD), f32),
        ],
        scratch_shapes=[pltpu.VMEM((N_HEADS, D_HEAD, D_HEAD), f32)],
        compiler_params=_cparams(("arbitrary", "arbitrary")),
        name="ret_prompt",
    )(P, P, P, KT, cos, sin, cosT, sinT, hng)


_SEQ_PER_STEP = 4


def _decode_kernel(um_ref, vm_ref, zsm_ref, qr_ref, vr_ref, zsr_ref, kt_ref, gi_ref,
                   conv_ref, n_ref, m_ref, c_ref, s_ref,
                   convw_ref, convb_ref, wq_ref, wk_ref, brow_ref, hngm_ref, skip_ref, hngr_ref,
                   cosd_ref, sind_ref, cosdT_ref, sindT_ref,
                   om_in_ref, or_in_ref, cprev_ref, sprev_ref,
                   om_ref, or_ref, c_out_ref, s_out_ref, n_out_ref, m_out_ref, conv_out_ref,
                   vt_s, qrt_s, krt_s, qm_s, km_s, vr_s, sct_s, ht_s, oacc_s, cm_s):
    del om_in_ref, or_in_ref, cprev_ref, sprev_ref
    s = pl.program_id(0)
    ns = pl.num_programs(0)
    NB = um_ref.shape[0]
    scale = D_HEAD ** -0.5
    lgs = _log_gammas()
    gammas = [math.exp(lg) for lg in lgs]
    lane = lax.broadcasted_iota(jnp.int32, (1, LANES), 1)

    @pl.when(s == 0)
    def _():
        u = um_ref[...].astype(f32)
        acc = convb_ref[...] + u * convw_ref[CONV_W - 1:CONV_W, :]
        for i in range(CONV_W - 1):
            acc = acc + conv_ref[:, i * W_MIX:(i + 1) * W_MIX] * convw_ref[i:i + 1, :]
        cm = _silu(acc)
        cm_s[...] = cm
        conv_out_ref[:, 0:W_MIX] = conv_ref[:, W_MIX:2 * W_MIX]
        conv_out_ref[:, W_MIX:2 * W_MIX] = conv_ref[:, 2 * W_MIX:3 * W_MIX]
        conv_out_ref[:, 2 * W_MIX:3 * W_MIX] = u

        gb = gi_ref[...] + brow_ref[...]
        lf = _log_sigmoid(gb)
        sc = jnp.zeros((NB, LANES), f32)
        mo = jnp.zeros((NB, LANES), f32)
        for h in range(N_HEADS):
            cs = slice(h * D_HEAD, (h + 1) * D_HEAD)
            chb = cm[:, cs].astype(bf16)
            q = _dot(chb, wq_ref[h]) * scale
            k = _dot(chb, wk_ref[h])
            qm_s[:, cs] = q
            km_s[:, cs] = k
            ig = gb[:, h:h + 1]
            lfh = lf[:, 4 + h:5 + h]
            mp = m_ref[:, h:h + 1]
            inter = lfh + mp
            m_t = jnp.maximum(inter, ig)
            w = jnp.exp(ig - m_t)
            a = jnp.exp(inter - m_t)
            nh = n_ref[:, cs]
            ws = w * jnp.sum(q * k, axis=-1, keepdims=True)
            den = ws + a * jnp.sum(nh * q, axis=-1, keepdims=True)
            den = jnp.maximum(jnp.abs(den), jnp.exp(-m_t))
            n_out_ref[:, cs] = a * nh + w * k
            for r, val in ((h, ws), (4 + h, a), (8 + h, den), (12 + h, w)):
                sc = sc + val * (lane == r).astype(f32)
            mo = mo + m_t * (lane == h).astype(f32)
        m_out_ref[...] = mo
        sct_s[...] = sc.T
        vt_s[...] = vm_ref[...].astype(f32).T
        vr_s[...] = vr_ref[...].astype(f32)
        cosd = cosd_ref[0:1, :]
        sind = sind_ref[0:1, :]
        cosdT = cosdT_ref[...]
        sindT = sindT_ref[...]
        for h in range(N_HEADS):
            lo = slice(h * D_HEAD, h * D_HEAD + HALF)
            hi = slice(h * D_HEAD + HALF, (h + 1) * D_HEAD)
            q1 = qr_ref[:, lo].astype(f32)
            q2 = qr_ref[:, hi].astype(f32)
            qrot = jnp.concatenate([q1 * cosd - q2 * sind, q1 * sind + q2 * cosd], axis=-1)
            qrt_s[h * D_HEAD:(h + 1) * D_HEAD, :] = qrot.T
            k1 = kt_ref[lo, :].astype(f32)
            k2 = kt_ref[hi, :].astype(f32)
            krt_s[lo, :] = (k1 * cosdT - k2 * sindT) * scale
            krt_s[hi, :] = (k1 * sindT + k2 * cosdT) * scale
        ht_s[...] = jnp.zeros(ht_s.shape, f32)

    def take_col(mat, mask):
        return jnp.sum(jnp.where(mask, mat, 0.0), axis=-1, keepdims=True)

    for j in range(_SEQ_PER_STEP):
        b = s * _SEQ_PER_STEP + j
        mask = lane == b
        for h in range(N_HEADS):
            cs = slice(h * D_HEAD, (h + 1) * D_HEAD)
            c = c_ref[j, h]
            q_row = qm_s[pl.ds(b, 1), cs]
            k_row = km_s[pl.ds(b, 1), cs]
            v_col = take_col(vt_s[cs, :], mask)
            ws = take_col(sct_s[h:h + 1, :], mask)
            a = take_col(sct_s[4 + h:5 + h, :], mask)
            den = take_col(sct_s[8 + h:9 + h, :], mask)
            w = take_col(sct_s[12 + h:13 + h, :], mask)
            cq = jnp.sum(c * q_row, axis=-1, keepdims=True)
            hcol = (ws * v_col + a * cq) / den
            ht_s[cs, :] = jnp.where(mask, hcol, ht_s[cs, :])
            c_out_ref[j, h] = a * c + (w * v_col) * k_row
            st = s_ref[j, h]
            q_col = take_col(qrt_s[cs, :], mask)
            k_col = take_col(krt_s[cs, :], mask)
            v_row = vr_s[pl.ds(b, 1), cs]
            qk = jnp.sum(q_col * k_col, axis=0, keepdims=True)
            o_row = qk * v_row + gammas[h] * jnp.sum(st * q_col, axis=0, keepdims=True)
            oacc_s[pl.ds(b, 1), cs] = o_row
            s_out_ref[j, h] = gammas[h] * st + k_col * v_row

    @pl.when(s == ns - 1)
    def _():
        hm = ht_s[...].T
        for h in range(N_HEADS):
            cs = slice(h * D_HEAD, (h + 1) * D_HEAD)
            hn = _head_norm(hm[:, cs], hngm_ref[:, cs])
            zsm = zsm_ref[:, cs].astype(f32)
            om_ref[:, cs] = ((hn + skip_ref[:, cs] * cm_s[:, cs]) * zsm).astype(bf16)
            zsr = zsr_ref[:, cs].astype(f32)
            or_ref[:, cs] = (_head_norm(oacc_s[:, cs], hngr_ref[:, cs]) * zsr).astype(bf16)


def _decode_step(P, KT, GI, OM, OR, Cprev, Sprev, l, BT, conv_st, n_st, m_st, C_st, S_st,
                 conv_w, conv_b, wq, wk, brow, hngm, skip, hngr, dtabs):
    R = P.shape[0]
    NB = R - BT
    assert NB == LANES and BT % NB == 0 and NB % _SEQ_PER_STEP == 0
    rb = BT // NB
    depth = C_st.shape[0]
    bb = _SEQ_PER_STEP
    cosd, sind, cosdT, sindT = dtabs
    pcol = lambda k: (lambda s: (rb, k))
    rep2 = lambda s: (0, 0)
    rep3 = lambda s: (0, 0, 0)
    st_spec = pl.BlockSpec((None, bb, N_HEADS, D_HEAD, D_HEAD), lambda s: (l, s, 0, 0, 0))
    any_spec = pl.BlockSpec(memory_space=pl.ANY)
    in_specs = [
        pl.BlockSpec((NB, W_MIX), pcol(0)), pl.BlockSpec((NB, W_MIX), pcol(1)),
        pl.BlockSpec((NB, W_MIX), pcol(2)), pl.BlockSpec((NB, W_MIX), pcol(3)),
        pl.BlockSpec((NB, W_MIX), pcol(4)), pl.BlockSpec((NB, W_MIX), pcol(5)),
        pl.BlockSpec((W_MIX, NB), lambda s: (0, rb)),
        pl.BlockSpec((NB, LANES), lambda s: (rb, 0)),
        pl.BlockSpec((None, NB, (CONV_W - 1) * W_MIX), lambda s: (l, 0, 0)),
        pl.BlockSpec((None, NB, W_MIX), lambda s: (l, 0, 0)),
        pl.BlockSpec((None, NB, N_HEADS), lambda s: (l, 0, 0)),
        st_spec, st_spec,
        pl.BlockSpec((CONV_W, W_MIX), rep2), pl.BlockSpec((1, W_MIX), rep2),
        pl.BlockSpec((N_HEADS, D_HEAD, D_HEAD), rep3), pl.BlockSpec((N_HEADS, D_HEAD, D_HEAD), rep3),
        pl.BlockSpec((1, LANES), rep2),
        pl.BlockSpec((1, W_MIX), rep2), pl.BlockSpec((1, W_MIX), rep2), pl.BlockSpec((1, W_MIX), rep2),
        pl.BlockSpec((8, HALF), rep2), pl.BlockSpec((8, HALF), rep2),
        pl.BlockSpec((HALF, LANES), rep2), pl.BlockSpec((HALF, LANES), rep2),
        any_spec, any_spec,
    ]
    args = [P, P, P, P, P, P, KT, GI, conv_st, n_st, m_st, C_st, S_st,
            conv_w, conv_b, wq, wk, brow, hngm, skip, hngr, cosd, sind, cosdT, sindT, OM, OR]
    aliases = {len(args) - 2: 0, len(args) - 1: 1}
    if Cprev is not None:
        in_specs += [any_spec, any_spec]
        args += [Cprev, Sprev]
        aliases[len(args) - 2] = 2
        aliases[len(args) - 1] = 3
        kern = _decode_kernel
    else:
        kern = functools.partial(_decode_kernel_first)
    out_row = pl.BlockSpec((NB, W_MIX), lambda s: (rb, 0))
    return pl.pallas_call(
        kern,
        grid=(NB // bb,),
        in_specs=in_specs,
        out_specs=[
            out_row, out_row, st_spec, st_spec,
            pl.BlockSpec((NB, W_MIX), rep2),
            pl.BlockSpec((NB, LANES), rep2),
            pl.BlockSpec((NB, (CONV_W - 1) * W_MIX), rep2),
        ],
        out_shape=[
            jax.ShapeDtypeStruct(OM.shape, bf16),
            jax.ShapeDtypeStruct(OR.shape, bf16),
            jax.ShapeDtypeStruct((depth, NB, N_HEADS, D_HEAD, D_HEAD), f32),
            jax.ShapeDtypeStruct((depth, NB, N_HEADS, D_HEAD, D_HEAD), f32),
            jax.ShapeDtypeStruct((NB, W_MIX), f32),
            jax.ShapeDtypeStruct((NB, LANES), f32),
            jax.ShapeDtypeStruct((NB, (CONV_W - 1) * W_MIX), f32),
        ],
        scratch_shapes=[
            pltpu.VMEM((W_MIX, NB), f32),
            pltpu.VMEM((W_MIX, NB), f32),
            pltpu.VMEM((W_MIX, NB), f32),
            pltpu.VMEM((NB, W_MIX), f32),
            pltpu.VMEM((NB, W_MIX), f32),
            pltpu.VMEM((NB, W_MIX), f32),
            pltpu.VMEM((LANES, NB), f32),
            pltpu.VMEM((W_MIX, NB), f32),
            pltpu.VMEM((NB, W_MIX), f32),
            pltpu.VMEM((NB, W_MIX), f32),
        ],
        input_output_aliases=aliases,
        compiler_params=_cparams(("arbitrary",)),
        name="decode_step",
    )(*args)


def _decode_kernel_first(*refs):
    n_in = 27
    ins, rest = refs[:n_in], refs[n_in:]
    return _decode_kernel(*ins, None, None, *rest)


def _merge_kernel(h_ref, om_ref, or_ref, wgm_ref, wgr_ref, bgm_ref, bgr_ref, wpm_ref, wpr_ref, o_ref):
    h = h_ref[...]
    gm = jax.nn.sigmoid(_dot(h, wgm_ref[...].astype(bf16)) + bgm_ref[...])
    gr = jax.nn.sigmoid(_dot(h, wgr_ref[...].astype(bf16)) + bgr_ref[...])
    pm = _dot(om_ref[...], wpm_ref[...].astype(bf16))
    pr = _dot(or_ref[...], wpr_ref[...].astype(bf16))
    o_ref[...] = (gm * pm + gr * pr).astype(bf16)


def _merge(h, OM, OR, w_gate, b_gate3, w_pm, w_pr, l):
    R = h.shape[0]
    tm = _pick(R, (1664, 1280, 640, 384, 128))
    tn = 256
    nj = D_MODEL // tn
    return pl.pallas_call(
        _merge_kernel,
        grid=(R // tm, nj),
        in_specs=[
            pl.BlockSpec((tm, D_MODEL), lambda i, j: (i, 0)),
            pl.BlockSpec((tm, W_MIX), lambda i, j: (i, 0)),
            pl.BlockSpec((tm, W_MIX), lambda i, j: (i, 0)),
            pl.BlockSpec((None, D_MODEL, tn), lambda i, j: (l, 0, j)),
            pl.BlockSpec((None, D_MODEL, tn), lambda i, j: (l, 0, j + nj)),
            pl.BlockSpec((None, 1, tn), lambda i, j: (l, 0, j)),
            pl.BlockSpec((None, 1, tn), lambda i, j: (l, 0, j + nj)),
            pl.BlockSpec((None, W_MIX, tn), lambda i, j: (l, 0, j)),
            pl.BlockSpec((None, W_MIX, tn), lambda i, j: (l, 0, j)),
        ],
        out_specs=pl.BlockSpec((tm, tn), lambda i, j: (i, j)),
        out_shape=jax.ShapeDtypeStruct((R, D_MODEL), bf16),
        compiler_params=_cparams(("arbitrary", "arbitrary")),
        name="merge",
    )(h, OM, OR, w_gate, w_gate, b_gate3, b_gate3, w_pm, w_pr)


def _outproj_norm_kernel(mg_ref, wo_ref, x_ref, g_ref, o_ref, h_ref):
    y = x_ref[...] + _dot(mg_ref[...], wo_ref[...])
    o_ref[...] = y
    ms = jnp.mean(y * y, axis=-1, keepdims=True)
    h_ref[...] = (y * lax.rsqrt(ms + EPS) * g_ref[...]).astype(bf16)


def _outproj_final_kernel(mg_ref, wo_ref, x_ref, g_ref, y_ref):
    y = x_ref[...] + _dot(mg_ref[...], wo_ref[...])
    ms = jnp.mean(y * y, axis=-1, keepdims=True)
    y_ref[...] = y * lax.rsqrt(ms + EPS) * g_ref[...]


def _wo_spec(l):
    return pl.BlockSpec((None, D_MODEL, D_MODEL), lambda i: (l, 0, 0), pipeline_mode=pl.Buffered(1))


def _outproj(merged, w_o_bf, x, l, g_next):
    R = x.shape[0]
    tm = _pick(R, (640, 512, 384, 128))
    row = pl.BlockSpec((tm, D_MODEL), lambda i: (i, 0))
    return pl.pallas_call(
        _outproj_norm_kernel,
        grid=(R // tm,),
        in_specs=[row, _wo_spec(l), row, pl.BlockSpec((1, D_MODEL), lambda i: (0, 0))],
        out_specs=[row, row],
        out_shape=[jax.ShapeDtypeStruct((R, D_MODEL), f32), jax.ShapeDtypeStruct((R, D_MODEL), bf16)],
        compiler_params=_cparams(("arbitrary",)),
        name="outproj_norm",
    )(merged, w_o_bf, x, g_next)


def _outproj_final(merged, w_o_bf, x, l, g, row0, nrows):
    tm = _pick(nrows, (512, 256, 128))
    assert row0 % tm == 0
    off = row0 // tm
    row_in = pl.BlockSpec((tm, D_MODEL), lambda i: (i + off, 0))
    return pl.pallas_call(
        _outproj_final_kernel,
        grid=(nrows // tm,),
        in_specs=[row_in, _wo_spec(l), row_in, pl.BlockSpec((1, D_MODEL), lambda i: (0, 0))],
        out_specs=pl.BlockSpec((tm, D_MODEL), lambda i: (i, 0)),
        out_shape=jax.ShapeDtypeStruct((nrows, D_MODEL), f32),
        compiler_params=_cparams(("arbitrary",)),
        name="outproj_final",
    )(merged, w_o_bf, x, g)


def kernel(x_prompt, x_sample, state_mlstm_C, state_mlstm_n, state_mlstm_m, state_mlstm_conv, state_ret_S,
           norm_g, w_in, conv_w, conv_b, w_qm, w_km, b_i, b_f, hn_m_g, skip_m, hn_r_g,
           w_pm, w_pr, w_gate, b_gate, w_o, final_g):
    B, T, D = x_prompt.shape
    NB = x_sample.shape[0]
    depth = w_in.shape[0]
    BT = B * T
    assert D == D_MODEL and x_sample.shape[1] == 1 and T % CHUNK == 0

    x = jnp.concatenate([x_prompt.reshape(BT, D), x_sample.reshape(NB, D)], axis=0)

    w_inT = jnp.swapaxes(w_in, 1, 2)
    wq = w_qm.astype(bf16)
    wk = w_km.astype(bf16)
    wkT = jnp.swapaxes(w_km, 2, 3).astype(bf16)
    w_o_bf = w_o.astype(bf16)
    bias = jnp.pad(jnp.concatenate([b_i, b_f], axis=1), ((0, 0), (0, LANES - 2 * N_HEADS)))
    b_gate3 = b_gate.reshape(depth, 1, 2 * D)
    conv_st = state_mlstm_conv.reshape(depth, NB, (CONV_W - 1) * W_MIX)
    n_st = state_mlstm_n.reshape(depth, NB, W_MIX)

    cos, sin, cosT, sinT, cosd, sind, cosdT, sindT = _rope_tables(T)

    outs_p = [[], [], [], [], []]
    outs_s = [[], [], []]
    Cs = Ss = None
    h = _rmsnorm(x, norm_g[0].reshape(1, D), bf16)
    for l in range(depth):
        P, KT, GI = _inproj(h, w_inT, l)
        brow = bias[l].reshape(1, LANES)
        bcol = bias[l, :8].reshape(8, 1)
        cw, cb = conv_w[l], conv_b[l].reshape(1, W_MIX)
        hngm, skp, hngr = (hn_m_g[l].reshape(1, W_MIX), skip_m[l].reshape(1, W_MIX),
                           hn_r_g[l].reshape(1, W_MIX))
        OM, Cp, np_, mp, convp = _mlstm_prompt(P, GI, B, T, cw, cb, wq[l], wkT[l], brow, bcol, hngm, skp)
        OR, Sp = _ret_prompt(P, KT, B, T, (cos, sin, cosT, sinT), hngr)
        OM, OR, Cs, Ss, n_s, m_s, conv_s = _decode_step(
            P, KT, GI, OM, OR, Cs, Ss, l, BT, conv_st, n_st, state_mlstm_m, state_mlstm_C, state_ret_S,
            cw, cb, wq[l], wk[l], brow, hngm, skp, hngr, (cosd, sind, cosdT, sindT))
        merged = _merge(h, OM, OR, w_gate, b_gate3, w_pm, w_pr, l)
        if l + 1 < depth:
            x, h = _outproj(merged, w_o_bf, x, l, norm_g[l + 1].reshape(1, D))
        for k, val in enumerate((Cp, np_.reshape(B, N_HEADS, D_HEAD), mp[:, :N_HEADS, 0], convp, Sp)):
            outs_p[k].append(val)
        for k, val in enumerate((n_s.reshape(NB, N_HEADS, D_HEAD), m_s[:, :N_HEADS],
                                 conv_s.reshape(NB, CONV_W - 1, W_MIX))):
            outs_s[k].append(val)

    fg = final_g.reshape(1, D)
    y_prompt = _outproj_final(merged, w_o_bf, x, depth - 1, fg, 0, BT).reshape(B, T, D)
    y_sample = _outproj_final(merged, w_o_bf, x, depth - 1, fg, BT, NB).reshape(NB, 1, D)
    return (y_prompt, y_sample,
            jnp.stack(outs_p[0]), jnp.stack(outs_p[1]), jnp.stack(outs_p[2]),
            jnp.stack(outs_p[3]), jnp.stack(outs_p[4]),
            Cs, jnp.stack(outs_s[0]), jnp.stack(outs_s[1]), jnp.stack(outs_s[2]), Ss)
```

```python
import functools
import math

import jax
import jax.numpy as jnp
import numpy as np
from jax import lax
from jax.experimental import pallas as pl
from jax.experimental.pallas import tpu as pltpu

f32 = jnp.float32
bf16 = jnp.bfloat16

D_MODEL = 2048
N_HEADS = 4
D_HEAD = 256
W_MIX = N_HEADS * D_HEAD
CONV_W = 4
CHUNK = 128
ROPE_BASE = 10000.0
EPS = 1e-6
NEG = -1e30
PAST_LEN = 16384
LANES = 128
HALF = D_HEAD // 2
VMEM_LIMIT = 56 * 1024 * 1024

OFF_IF = 3 * W_MIX
OFF_RET = 3 * W_MIX + 2 * N_HEADS


def _pick(n, cands):
    for c in cands:
        if n % c == 0:
            return c
    return n


def _cparams(sem):
    return pltpu.CompilerParams(dimension_semantics=sem, vmem_limit_bytes=VMEM_LIMIT)


def _dot(a, b):
    return jnp.dot(a, b, preferred_element_type=f32)


def _dot_nt(a, b):
    return lax.dot_general(a, b, (((1,), (1,)), ((), ())), preferred_element_type=f32)


def _split3(x):
    hi = x.astype(bf16).astype(f32)
    r = x - hi
    mid = r.astype(bf16).astype(f32)
    lo = (r - mid).astype(bf16).astype(f32)
    return hi, mid, lo


def _silu(x):
    return x * (0.5 * jnp.tanh(0.5 * x) + 0.5)


def _log_sigmoid(x):
    return jnp.minimum(x, 0.0) - jnp.log1p(jnp.exp(-jnp.abs(x)))


def _head_norm(hc, g):
    mu = jnp.mean(hc, axis=-1, keepdims=True)
    d = hc - mu
    var = jnp.mean(d * d, axis=-1, keepdims=True)
    return d * lax.rsqrt(var + EPS) * g


def _rope_kernel(inv_ref, invT_ref, cos_ref, sin_ref, cosT_ref, sinT_ref,
                 cosd_ref, sind_ref, cosdT_ref, sindT_ref):
    T = cos_ref.shape[0]
    pos = lax.broadcasted_iota(jnp.int32, (T, HALF), 0).astype(f32)
    ang = pos * inv_ref[...]
    cos_ref[...] = jnp.cos(ang)
    sin_ref[...] = jnp.sin(ang)
    posT = lax.broadcasted_iota(jnp.int32, (HALF, T), 1).astype(f32)
    angT = posT * invT_ref[...]
    cosT_ref[...] = jnp.cos(angT)
    sinT_ref[...] = jnp.sin(angT)
    angd = jnp.full((8, HALF), float(PAST_LEN), f32) * inv_ref[...]
    cosd_ref[...] = jnp.cos(angd)
    sind_ref[...] = jnp.sin(angd)
    angdT = jnp.full((HALF, LANES), float(PAST_LEN), f32) * invT_ref[...]
    cosdT_ref[...] = jnp.cos(angdT)
    sindT_ref[...] = jnp.sin(angdT)


def _rope_tables(T):
    inv = ROPE_BASE ** (-jnp.arange(0, HALF, dtype=f32) / HALF)
    shapes = [(T, HALF), (T, HALF), (HALF, T), (HALF, T),
              (8, HALF), (8, HALF), (HALF, LANES), (HALF, LANES)]
    return pl.pallas_call(
        _rope_kernel,
        out_shape=[jax.ShapeDtypeStruct(s, f32) for s in shapes],
        name="rope_tables",
    )(inv.reshape(1, HALF), inv.reshape(HALF, 1))


def _rms_kernel(x_ref, g_ref, o_ref):
    x = x_ref[...]
    ms = jnp.mean(x * x, axis=-1, keepdims=True)
    o_ref[...] = (x * lax.rsqrt(ms + EPS) * g_ref[...]).astype(o_ref.dtype)


def _rmsnorm(x, g, out_dtype, row0=0, nrows=None):
    R, D = x.shape
    nrows = R if nrows is None else nrows
    tr = _pick(nrows, (832, 512, 256, 128))
    assert row0 % tr == 0
    off = row0 // tr
    return pl.pallas_call(
        _rms_kernel,
        grid=(nrows // tr,),
        in_specs=[pl.BlockSpec((tr, D), lambda i: (i + off, 0)),
                  pl.BlockSpec((1, D), lambda i: (0, 0))],
        out_specs=pl.BlockSpec((tr, D), lambda i: (i, 0)),
        out_shape=jax.ShapeDtypeStruct((nrows, D), out_dtype),
        compiler_params=_cparams(("arbitrary",)),
        name="rmsnorm",
    )(x, g)


_TN_IN = 512
_TPG = W_MIX // _TN_IN
_NJ_P = 6 * _TPG
_NJ_K = _TPG


def _inproj_row_offset(j):
    jp = jnp.minimum(j, _NJ_P - 1)
    grp = jp // _TPG
    sub = jp - grp * _TPG
    src_grp = jnp.where(grp < 3, grp, jnp.where(grp == 3, 3, grp + 1))
    off_p = src_grp * W_MIX + jnp.where(grp < 3, 0, 2 * N_HEADS) + sub * _TN_IN
    off_k = OFF_RET + W_MIX + jnp.clip(j - _NJ_P, 0, _NJ_K - 1) * _TN_IN
    return jnp.where(j < _NJ_P, off_p, off_k)


def _inproj_kernel(h_ref, wt_ref, wif_ref, p_ref, kt_ref, gi_ref):
    j = pl.program_id(1)

    def weights():
        return wt_ref[0].astype(bf16)

    @pl.when(j == 0)
    def _():
        wif = jnp.concatenate(
            [wif_ref[...], jnp.zeros((LANES - 2 * N_HEADS, D_MODEL), f32)], axis=0).astype(bf16)
        gi_ref[...] = _dot_nt(h_ref[...], wif)

    is_z = (j // _TPG == 2) | (j // _TPG == 5)

    @pl.when((j < _NJ_P) & jnp.logical_not(is_z))
    def _():
        p_ref[...] = _dot_nt(h_ref[...], weights()).astype(bf16)

    @pl.when((j < _NJ_P) & is_z)
    def _():
        p_ref[...] = _silu(_dot_nt(h_ref[...], weights())).astype(bf16)

    @pl.when(j >= _NJ_P)
    def _():
        kt_ref[...] = _dot_nt(weights(), h_ref[...]).astype(bf16)


def _inproj(h, w_inT, l):
    R = h.shape[0]
    tm = _pick(R, (1664, 1280, 640, 384, 128))
    tn = _TN_IN
    nj = _NJ_P + _NJ_K
    return pl.pallas_call(
        _inproj_kernel,
        grid=(R // tm, nj),
        in_specs=[
            pl.BlockSpec((tm, D_MODEL), lambda i, j: (i, 0)),
            pl.BlockSpec((pl.Element(1), pl.Element(tn), pl.Element(D_MODEL)),
                         lambda i, j: (l, pl.multiple_of(_inproj_row_offset(j), 8), 0)),
            pl.BlockSpec((None, 2 * N_HEADS, D_MODEL), lambda i, j: (l, OFF_IF // (2 * N_HEADS), 0)),
        ],
        out_specs=[
            pl.BlockSpec((tm, tn), lambda i, j: (i, jnp.minimum(j, _NJ_P - 1))),
            pl.BlockSpec((tn, tm), lambda i, j: (jnp.clip(j - _NJ_P, 0, _NJ_K - 1), i)),
            pl.BlockSpec((tm, LANES), lambda i, j: (i, 0)),
        ],
        out_shape=[
            jax.ShapeDtypeStruct((R, 6 * W_MIX), bf16),
            jax.ShapeDtypeStruct((W_MIX, R), bf16),
            jax.ShapeDtypeStruct((R, LANES), f32),
        ],
        compiler_params=_cparams(("arbitrary", "arbitrary")),
        name="inproj",
    )(h, w_inT, w_inT)


def _mlstm_kernel(um_ref, vm_ref, zs_ref, gi_ref, convw_ref, convb_ref, wq_ref, wkT_ref,
                  brow_ref, bcol_ref, hng_ref, skip_ref,
                  om_ref, c_out_ref, n_out_ref, m_out_ref, conv_out_ref,
                  ubuf, cmbuf, cta_s, m_s):
    g = pl.program_id(1)
    ng = pl.num_programs(1)
    Lg = um_ref.shape[0]
    L = CHUNK
    DA = D_HEAD + LANES
    scale = D_HEAD ** -0.5
    heads = range(N_HEADS)

    @pl.when(g == 0)
    def _():
        ubuf[0:8, :] = jnp.zeros((8, W_MIX), f32)
        cta_s[...] = jnp.zeros(cta_s.shape, f32)
        m_s[...] = jnp.full(m_s.shape, NEG, f32)

    ubuf[8:8 + Lg, :] = um_ref[...].astype(f32)
    acc = convb_ref[...] + ubuf[5:5 + Lg, :] * convw_ref[0:1, :]
    for i in range(1, CONV_W):
        acc = acc + ubuf[5 + i:5 + i + Lg, :] * convw_ref[i:i + 1, :]
    cmbuf[...] = _silu(acc)
    last3 = ubuf[5 + Lg:8 + Lg, :]
    ubuf[5:8, :] = last3

    row = lax.broadcasted_iota(jnp.int32, (L, L), 0)
    col = lax.broadcasted_iota(jnp.int32, (L, L), 1)
    causal = row >= col
    tril = causal.astype(bf16)
    triu = (row <= col).astype(bf16)
    brow = brow_ref[...]
    bcol = bcol_ref[...]
    ones_b = jnp.ones((L, LANES), bf16)
    ones_f = jnp.ones((L, LANES), f32)

    for c in range(Lg // L):
        rows = slice(c * L, (c + 1) * L)
        graw = gi_ref[rows, :]
        gb = graw + brow
        gtb = graw.T[0:8, :] + bcol
        lf = _log_sigmoid(gb)
        lft = _log_sigmoid(gtb)
        bc3 = _dot(tril, jnp.concatenate(_split3(lf), axis=1).astype(bf16))
        bc_all = bc3[:, 0:LANES] + bc3[:, LANES:2 * LANES] + bc3[:, 2 * LANES:3 * LANES]
        br3 = _dot(jnp.concatenate(_split3(lft), axis=0).astype(bf16), triu)
        br_all = br3[0:8, :] + br3[8:16, :] + br3[16:24, :]

        cs = [slice(h * D_HEAD, (h + 1) * D_HEAD) for h in heads]
        ch = [cmbuf[rows, cs[h]] for h in heads]
        chb = [x.astype(bf16) for x in ch]
        qb = [(_dot(chb[h], wq_ref[h]) * scale).astype(bf16) for h in heads]
        kTb = [_dot_nt(wkT_ref[h], chb[h]).astype(bf16) for h in heads]
        v = [vm_ref[rows, cs[h]] for h in heads]
        bc = [bc_all[:, 4 + h:5 + h] for h in heads]
        ic = [gb[:, h:h + 1] for h in heads]
        m_prev = [m_s[h][0:1, 0:1] for h in heads]
        dlog = [jnp.where(causal, bc[h] - br_all[4 + h:5 + h, :] + gtb[h:h + 1, :], NEG) for h in heads]
        inter = [bc[h] + m_prev[h] for h in heads]
        m_t = [jnp.maximum(inter[h], jnp.max(dlog[h], axis=-1, keepdims=True)) for h in heads]
        a = [jnp.exp(inter[h] - m_t[h]) for h in heads]
        ws = [(jnp.exp(dlog[h] - m_t[h]) * _dot(qb[h], kTb[h])).astype(bf16) for h in heads]
        cta = [cta_s[h] for h in heads]
        nd = [_dot(ws[h], jnp.concatenate([v[h], ones_b], axis=1))
              + a[h] * _dot(qb[h], cta[h].astype(bf16)) for h in heads]
        m_new = [m_t[h][L - 1:L, :] for h in heads]
        b_last = [bc[h][L - 1:L, :] for h in heads]
        for h in heads:
            wk_c = jnp.exp(b_last[h] - bc[h] + ic[h] - m_new[h])
            dec = jnp.exp(b_last[h] + m_prev[h] - m_new[h])
            vaug = jnp.concatenate([v[h].astype(f32), ones_f], axis=1)
            cta_s[h] = dec * cta[h] + _dot(kTb[h], (wk_c * vaug).astype(bf16))
            m_s[h] = jnp.broadcast_to(m_new[h], (8, LANES))
        for h in heads:
            den = jnp.maximum(jnp.abs(nd[h][:, D_HEAD:DA]), jnp.exp(-m_t[h]))
            rden = 1.0 / den
            hc = jnp.concatenate([nd[h][:, 0:HALF] * rden, nd[h][:, HALF:D_HEAD] * rden], axis=1)
            hn = _head_norm(hc, hng_ref[:, cs[h]])
            zs = zs_ref[rows, cs[h]].astype(f32)
            om_ref[rows, cs[h]] = ((hn + skip_ref[:, cs[h]] * ch[h]) * zs).astype(bf16)

    @pl.when(g == ng - 1)
    def _():
        for h in heads:
            st = cta_s[h]
            c_out_ref[h] = st[:, 0:D_HEAD].T
            n_out_ref[h] = st[:, D_HEAD:DA].T[0:1, :]
        m_out_ref[...] = jnp.concatenate([m_s[h][0:1, :] for h in heads]
                                         + [jnp.zeros((8 - N_HEADS, LANES), f32)], axis=0)
        conv_out_ref[...] = last3


def _drop_refs(kernel_fn, n_in, n_drop, *refs):
    return kernel_fn(*refs[:n_in], *refs[n_in + n_drop:])


def _chain(kernel_fn, n_in, prev, first_out):
    if prev is None:
        return kernel_fn, [], [], {}
    specs = [pl.BlockSpec(memory_space=pl.ANY)] * len(prev)
    aliases = {n_in + k: first_out + k for k in range(len(prev))}
    return functools.partial(_drop_refs, kernel_fn, n_in, len(prev)), specs, list(prev), aliases


def _layer_spec(shape, l):
    nd = len(shape)
    return pl.BlockSpec((None,) + tuple(shape), lambda *_: (l,) + (0,) * nd)


def _mlstm_prompt(P, GI, B, T, l, depth, prev, conv_w, conv_b, wq, wkT, brow, bcol, hng, skip):
    R = P.shape[0]
    Lg = _pick(T, (512, 256, 128))
    ng = T // Lg
    col = lambda k: (lambda b, g: (b * ng + g, k))
    in_specs = [
        pl.BlockSpec((Lg, W_MIX), col(0)),
        pl.BlockSpec((Lg, W_MIX), col(1)),
        pl.BlockSpec((Lg, W_MIX), col(2)),
        pl.BlockSpec((Lg, LANES), col(0)),
        _layer_spec((CONV_W, W_MIX), l),
        _layer_spec((1, W_MIX), l),
        _layer_spec((N_HEADS, D_HEAD, D_HEAD), l),
        _layer_spec((N_HEADS, D_HEAD, D_HEAD), l),
        _layer_spec((1, LANES), l),
        _layer_spec((8, 1), l),
        _layer_spec((1, W_MIX), l),
        _layer_spec((1, W_MIX), l),
    ]
    args = [P, P, P, GI, conv_w, conv_b, wq, wkT, brow, bcol, hng, skip]
    kern, xspecs, xargs, aliases = _chain(_mlstm_kernel, len(args), prev, 1)
    return pl.pallas_call(
        kern,
        grid=(B, ng),
        in_specs=in_specs + xspecs,
        out_specs=[
            pl.BlockSpec((Lg, W_MIX), col(0)),
            pl.BlockSpec((None, None, N_HEADS, D_HEAD, D_HEAD), lambda b, g: (l, b, 0, 0, 0)),
            pl.BlockSpec((None, None, N_HEADS, 1, D_HEAD), lambda b, g: (l, b, 0, 0, 0)),
            pl.BlockSpec((None, None, 8, LANES), lambda b, g: (l, b, 0, 0)),
            pl.BlockSpec((None, None, CONV_W - 1, W_MIX), lambda b, g: (l, b, 0, 0)),
        ],
        out_shape=[
            jax.ShapeDtypeStruct((R, W_MIX), bf16),
            jax.ShapeDtypeStruct((depth, B, N_HEADS, D_HEAD, D_HEAD), f32),
            jax.ShapeDtypeStruct((depth, B, N_HEADS, 1, D_HEAD), f32),
            jax.ShapeDtypeStruct((depth, B, 8, LANES), f32),
            jax.ShapeDtypeStruct((depth, B, CONV_W - 1, W_MIX), f32),
        ],
        scratch_shapes=[
            pltpu.VMEM((8 + Lg, W_MIX), f32),
            pltpu.VMEM((Lg, W_MIX), f32),
            pltpu.VMEM((N_HEADS, D_HEAD, D_HEAD + LANES), f32),
            pltpu.VMEM((N_HEADS, 8, LANES), f32),
        ],
        input_output_aliases=aliases,
        compiler_params=_cparams(("arbitrary", "arbitrary")),
        name="mlstm_prompt",
    )(*args, *xargs)


def _log_gammas():
    return [math.log(1.0 - 2.0 ** (-5.0 - h)) for h in range(N_HEADS)]


def _ret_kernel(q_ref, v_ref, zs_ref, kt_ref, cos_ref, sin_ref, cosT_ref, sinT_ref, hng_ref,
                o_ref, s_out_ref, s_s):
    g = pl.program_id(1)
    ng = pl.num_programs(1)
    Lg = q_ref.shape[0]
    L = CHUNK

    @pl.when(g == 0)
    def _():
        s_s[...] = jnp.zeros(s_s.shape, f32)

    row = lax.broadcasted_iota(jnp.int32, (L, L), 0)
    col = lax.broadcasted_iota(jnp.int32, (L, L), 1)
    causal = row >= col
    diff = (row - col).astype(f32)
    tcol = lax.broadcasted_iota(jnp.int32, (L, 1), 0).astype(f32)
    lgs = _log_gammas()
    intra = [jnp.where(causal, jnp.exp(diff * lg), 0.0) for lg in lgs]
    q_dec = [jnp.exp((tcol + 1.0) * lg) for lg in lgs]
    k_dec = [jnp.exp((L - 1.0 - tcol) * lg) for lg in lgs]
    s_dec = [math.exp(L * lg) for lg in lgs]
    kscale = D_HEAD ** -0.5

    for c in range(Lg // L):
        rows = slice(c * L, (c + 1) * L)
        cos = cos_ref[rows, :]
        sin = sin_ref[rows, :]
        cosT = cosT_ref[:, rows]
        sinT = sinT_ref[:, rows]
        for h in range(N_HEADS):
            cs = slice(h * D_HEAD, (h + 1) * D_HEAD)
            q1 = q_ref[rows, h * D_HEAD:h * D_HEAD + HALF].astype(f32)
            q2 = q_ref[rows, h * D_HEAD + HALF:(h + 1) * D_HEAD].astype(f32)
            qb = jnp.concatenate([q1 * cos - q2 * sin, q1 * sin + q2 * cos], axis=-1).astype(bf16)
            k1 = kt_ref[h * D_HEAD:h * D_HEAD + HALF, rows].astype(f32)
            k2 = kt_ref[h * D_HEAD + HALF:(h + 1) * D_HEAD, rows].astype(f32)
            kTb = (jnp.concatenate([k1 * cosT - k2 * sinT, k1 * sinT + k2 * cosT], axis=0)
                   * kscale).astype(bf16)
            v = v_ref[rows, cs]
            sc = _dot(qb, kTb) * intra[h]
            s = s_s[h]
            o = _dot(sc.astype(bf16), v) + q_dec[h] * _dot(qb, s.astype(bf16))
            s_s[h] = s_dec[h] * s + _dot(kTb, (v.astype(f32) * k_dec[h]).astype(bf16))
            zs = zs_ref[rows, cs].astype(f32)
            o_ref[rows, cs] = (_head_norm(o, hng_ref[:, cs]) * zs).astype(bf16)

    @pl.when(g == ng - 1)
    def _():
        s_out_ref[...] = s_s[...]


def _ret_prompt(P, KT, B, T, l, depth, prev, tabs, hng):
    R = P.shape[0]
    Lg = _pick(T, (512, 256, 128))
    ng = T // Lg
    col = lambda k: (lambda b, g: (b * ng + g, k))
    cos, sin, cosT, sinT = tabs
    in_specs = [
        pl.BlockSpec((Lg, W_MIX), col(3)),
        pl.BlockSpec((Lg, W_MIX), col(4)),
        pl.BlockSpec((Lg, W_MIX), col(5)),
        pl.BlockSpec((W_MIX, Lg), lambda b, g: (0, b * ng + g)),
        pl.BlockSpec((Lg, HALF), lambda b, g: (g, 0)),
        pl.BlockSpec((Lg, HALF), lambda b, g: (g, 0)),
        pl.BlockSpec((HALF, Lg), lambda b, g: (0, g)),
        pl.BlockSpec((HALF, Lg), lambda b, g: (0, g)),
        _layer_spec((1, W_MIX), l),
    ]
    args = [P, P, P, KT, cos, sin, cosT, sinT, hng]
    kern, xspecs, xargs, aliases = _chain(_ret_kernel, len(args), prev, 1)
    return pl.pallas_call(
        kern,
        grid=(B, ng),
        in_specs=in_specs + xspecs,
        out_specs=[
            pl.BlockSpec((Lg, W_MIX), col(0)),
            pl.BlockSpec((None, None, N_HEADS, D_HEAD, D_HEAD), lambda b, g: (l, b, 0, 0, 0)),
        ],
        out_shape=[
            jax.ShapeDtypeStruct((R, W_MIX), bf16),
            jax.ShapeDtypeStruct((depth, B, N_HEADS, D_HEAD, D_HEAD), f32),
        ],
        scratch_shapes=[pltpu.VMEM((N_HEADS, D_HEAD, D_HEAD), f32)],
        input_output_aliases=aliases,
        compiler_params=_cparams(("arbitrary", "arbitrary")),
        name="ret_prompt",
    )(*args, *xargs)


_SEQ_PER_STEP = 4


def _decode_kernel(um_ref, vm_ref, zsm_ref, qr_ref, vr_ref, zsr_ref, kt_ref, gi_ref,
                   conv_ref, n_ref, m_ref, c_ref, s_ref,
                   convw_ref, convb_ref, wq_ref, wk_ref, brow_ref, hngm_ref, skip_ref, hngr_ref,
                   cosd_ref, sind_ref, cosdT_ref, sindT_ref,
                   om_ref, or_ref, c_out_ref, s_out_ref, n_out_ref, m_out_ref, conv_out_ref,
                   vt_s, qrt_s, krt_s, qm_s, km_s, vr_s, sct_s, cqt_s, oacc_s, cm_s):
    s = pl.program_id(0)
    ns = pl.num_programs(0)
    NB = um_ref.shape[0]
    scale = D_HEAD ** -0.5
    lgs = _log_gammas()
    gammas = [math.exp(lg) for lg in lgs]
    lane = lax.broadcasted_iota(jnp.int32, (1, LANES), 1)

    @pl.when(s == 0)
    def _():
        u = um_ref[...].astype(f32)
        acc = convb_ref[...] + u * convw_ref[CONV_W - 1:CONV_W, :]
        for i in range(CONV_W - 1):
            acc = acc + conv_ref[i] * convw_ref[i:i + 1, :]
        cm = _silu(acc)
        cm_s[...] = cm
        for i in range(CONV_W - 2):
            conv_out_ref[i] = conv_ref[i + 1]
        conv_out_ref[CONV_W - 2] = u

        gb = gi_ref[...] + brow_ref[...]
        lf = _log_sigmoid(gb)
        m_cols = jnp.concatenate([m_ref[...], jnp.zeros((8 - N_HEADS, NB), f32)], axis=0).T
        sc = jnp.zeros((NB, LANES), f32)
        mo = jnp.zeros((NB, LANES), f32)
        for h in range(N_HEADS):
            cs = slice(h * D_HEAD, (h + 1) * D_HEAD)
            chb = cm[:, cs].astype(bf16)
            q = _dot(chb, wq_ref[h]) * scale
            k = _dot(chb, wk_ref[h])
            qm_s[:, cs] = q
            km_s[:, cs] = k
            ig = gb[:, h:h + 1]
            lfh = lf[:, 4 + h:5 + h]
            mp = m_cols[:, h:h + 1]
            inter = lfh + mp
            m_t = jnp.maximum(inter, ig)
            w = jnp.exp(ig - m_t)
            a = jnp.exp(inter - m_t)
            nh = n_ref[:, cs]
            ws = w * jnp.sum(q * k, axis=-1, keepdims=True)
            den = ws + a * jnp.sum(nh * q, axis=-1, keepdims=True)
            den = jnp.maximum(jnp.abs(den), jnp.exp(-m_t))
            n_out_ref[:, cs] = a * nh + w * k
            for r, val in ((h, ws), (4 + h, a), (8 + h, den), (12 + h, w)):
                sc = sc + val * (lane == r).astype(f32)
            mo = mo + m_t * (lane == h).astype(f32)
        m_out_ref[...] = mo.T[0:N_HEADS, :]
        sct_s[...] = sc.T
        vt_s[...] = vm_ref[...].astype(f32).T
        vr_s[...] = vr_ref[...].astype(f32)
        cosd = cosd_ref[0:1, :]
        sind = sind_ref[0:1, :]
        cosdT = cosdT_ref[...]
        sindT = sindT_ref[...]
        for h in range(N_HEADS):
            lo = slice(h * D_HEAD, h * D_HEAD + HALF)
            hi = slice(h * D_HEAD + HALF, (h + 1) * D_HEAD)
            q1 = qr_ref[:, lo].astype(f32)
            q2 = qr_ref[:, hi].astype(f32)
            qrot = jnp.concatenate([q1 * cosd - q2 * sind, q1 * sind + q2 * cosd], axis=-1)
            qrt_s[h * D_HEAD:(h + 1) * D_HEAD, :] = qrot.T
            k1 = kt_ref[lo, :].astype(f32)
            k2 = kt_ref[hi, :].astype(f32)
            krt_s[lo, :] = (k1 * cosdT - k2 * sindT) * scale
            krt_s[hi, :] = (k1 * sindT + k2 * cosdT) * scale
        cqt_s[...] = jnp.zeros(cqt_s.shape, f32)

    def take_col(mat, mask):
        return jnp.sum(jnp.where(mask, mat, 0.0), axis=-1, keepdims=True)

    for j in range(_SEQ_PER_STEP):
        b = s * _SEQ_PER_STEP + j
        mask = lane == b
        for h in range(N_HEADS):
            cs = slice(h * D_HEAD, (h + 1) * D_HEAD)
            c = c_ref[j, h]
            q_row = qm_s[pl.ds(b, 1), cs]
            k_row = km_s[pl.ds(b, 1), cs]
            v_col = take_col(vt_s[cs, :], mask)
            a = take_col(sct_s[4 + h:5 + h, :], mask)
            w = take_col(sct_s[12 + h:13 + h, :], mask)
            cq = jnp.sum(c * q_row, axis=-1, keepdims=True)
            cqt_s[cs, :] = jnp.where(mask, cq, cqt_s[cs, :])
            c_out_ref[j, h] = a * c + (w * v_col) * k_row
            st = s_ref[j, h]
            q_col = take_col(qrt_s[cs, :], mask)
            k_col = take_col(krt_s[cs, :], mask)
            v_row = vr_s[pl.ds(b, 1), cs]
            qk = jnp.sum(q_col * k_col, axis=0, keepdims=True)
            o_row = qk * v_row + gammas[h] * jnp.sum(st * q_col, axis=0, keepdims=True)
            oacc_s[pl.ds(b, 1), cs] = o_row
            s_out_ref[j, h] = gammas[h] * st + k_col * v_row

    @pl.when(s == ns - 1)
    def _():
        for h in range(N_HEADS):
            cs = slice(h * D_HEAD, (h + 1) * D_HEAD)
            ht = ((sct_s[h:h + 1, :] * vt_s[cs, :] + sct_s[4 + h:5 + h, :] * cqt_s[cs, :])
                  / sct_s[8 + h:9 + h, :])
            hn = _head_norm(ht.T, hngm_ref[:, cs])
            zsm = zsm_ref[:, cs].astype(f32)
            om_ref[:, cs] = ((hn + skip_ref[:, cs] * cm_s[:, cs]) * zsm).astype(bf16)
            zsr = zsr_ref[:, cs].astype(f32)
            or_ref[:, cs] = (_head_norm(oacc_s[:, cs], hngr_ref[:, cs]) * zsr).astype(bf16)


def _decode_step(P, KT, GI, OM, OR, prev, l, BT, conv_st, n_st, m_st, C_st, S_st,
                 conv_w, conv_b, wq, wk, brow, hngm, skip, hngr, dtabs):
    R = P.shape[0]
    NB = R - BT
    assert NB == LANES and BT % NB == 0 and NB % _SEQ_PER_STEP == 0
    rb = BT // NB
    depth = C_st.shape[0]
    bb = _SEQ_PER_STEP
    cosd, sind, cosdT, sindT = dtabs
    pcol = lambda k: (lambda s: (rb, k))
    rep2 = lambda s: (0, 0)
    st_spec = pl.BlockSpec((None, bb, N_HEADS, D_HEAD, D_HEAD), lambda s: (l, s, 0, 0, 0))
    conv_spec = _layer_spec((CONV_W - 1, NB, W_MIX), l)
    n_spec = _layer_spec((NB, W_MIX), l)
    m_spec = _layer_spec((N_HEADS, NB), l)
    in_specs = [
        pl.BlockSpec((NB, W_MIX), pcol(0)), pl.BlockSpec((NB, W_MIX), pcol(1)),
        pl.BlockSpec((NB, W_MIX), pcol(2)), pl.BlockSpec((NB, W_MIX), pcol(3)),
        pl.BlockSpec((NB, W_MIX), pcol(4)), pl.BlockSpec((NB, W_MIX), pcol(5)),
        pl.BlockSpec((W_MIX, NB), lambda s: (0, rb)),
        pl.BlockSpec((NB, LANES), lambda s: (rb, 0)),
        conv_spec, n_spec, m_spec, st_spec, st_spec,
        _layer_spec((CONV_W, W_MIX), l), _layer_spec((1, W_MIX), l),
        _layer_spec((N_HEADS, D_HEAD, D_HEAD), l), _layer_spec((N_HEADS, D_HEAD, D_HEAD), l),
        _layer_spec((1, LANES), l),
        _layer_spec((1, W_MIX), l), _layer_spec((1, W_MIX), l), _layer_spec((1, W_MIX), l),
        pl.BlockSpec((8, HALF), rep2), pl.BlockSpec((8, HALF), rep2),
        pl.BlockSpec((HALF, LANES), rep2), pl.BlockSpec((HALF, LANES), rep2),
    ]
    args = [P, P, P, P, P, P, KT, GI, conv_st, n_st, m_st, C_st, S_st,
            conv_w, conv_b, wq, wk, brow, hngm, skip, hngr, cosd, sind, cosdT, sindT]
    carried = [OM, OR] + (list(prev) if prev is not None else [])
    kern, xspecs, xargs, aliases = _chain(_decode_kernel, len(args), carried, 0)
    out_row = pl.BlockSpec((NB, W_MIX), lambda s: (rb, 0))
    return pl.pallas_call(
        kern,
        grid=(NB // bb,),
        in_specs=in_specs + xspecs,
        out_specs=[out_row, out_row, st_spec, st_spec, n_spec, m_spec, conv_spec],
        out_shape=[
            jax.ShapeDtypeStruct(OM.shape, bf16),
            jax.ShapeDtypeStruct(OR.shape, bf16),
            jax.ShapeDtypeStruct((depth, NB, N_HEADS, D_HEAD, D_HEAD), f32),
            jax.ShapeDtypeStruct((depth, NB, N_HEADS, D_HEAD, D_HEAD), f32),
            jax.ShapeDtypeStruct((depth, NB, W_MIX), f32),
            jax.ShapeDtypeStruct((depth, N_HEADS, NB), f32),
            jax.ShapeDtypeStruct((depth, CONV_W - 1, NB, W_MIX), f32),
        ],
        scratch_shapes=[
            pltpu.VMEM((W_MIX, NB), f32),
            pltpu.VMEM((W_MIX, NB), f32),
            pltpu.VMEM((W_MIX, NB), f32),
            pltpu.VMEM((NB, W_MIX), f32),
            pltpu.VMEM((NB, W_MIX), f32),
            pltpu.VMEM((NB, W_MIX), f32),
            pltpu.VMEM((LANES, NB), f32),
            pltpu.VMEM((W_MIX, NB), f32),
            pltpu.VMEM((NB, W_MIX), f32),
            pltpu.VMEM((NB, W_MIX), f32),
        ],
        input_output_aliases=aliases,
        compiler_params=_cparams(("arbitrary",)),
        name="decode_step",
    )(*args, *xargs)


def _merge_kernel(h_ref, om_ref, or_ref, wgm_ref, wgr_ref, bgm_ref, bgr_ref, wpm_ref, wpr_ref, o_ref):
    h = h_ref[...]
    gm = jax.nn.sigmoid(_dot(h, wgm_ref[...].astype(bf16)) + bgm_ref[...])
    gr = jax.nn.sigmoid(_dot(h, wgr_ref[...].astype(bf16)) + bgr_ref[...])
    pm = _dot(om_ref[...], wpm_ref[...].astype(bf16))
    pr = _dot(or_ref[...], wpr_ref[...].astype(bf16))
    o_ref[...] = (gm * pm + gr * pr).astype(bf16)


def _merge(h, OM, OR, w_gate, b_gate3, w_pm, w_pr, l):
    R = h.shape[0]
    tm = _pick(R, (1664, 1280, 640, 384, 128))
    tn = 256
    nj = D_MODEL // tn
    return pl.pallas_call(
        _merge_kernel,
        grid=(R // tm, nj),
        in_specs=[
            pl.BlockSpec((tm, D_MODEL), lambda i, j: (i, 0)),
            pl.BlockSpec((tm, W_MIX), lambda i, j: (i, 0)),
            pl.BlockSpec((tm, W_MIX), lambda i, j: (i, 0)),
            pl.BlockSpec((None, D_MODEL, tn), lambda i, j: (l, 0, j)),
            pl.BlockSpec((None, D_MODEL, tn), lambda i, j: (l, 0, j + nj)),
            pl.BlockSpec((None, 1, tn), lambda i, j: (l, 0, j)),
            pl.BlockSpec((None, 1, tn), lambda i, j: (l, 0, j + nj)),
            pl.BlockSpec((None, W_MIX, tn), lambda i, j: (l, 0, j)),
            pl.BlockSpec((None, W_MIX, tn), lambda i, j: (l, 0, j)),
        ],
        out_specs=pl.BlockSpec((tm, tn), lambda i, j: (i, j)),
        out_shape=jax.ShapeDtypeStruct((R, D_MODEL), bf16),
        compiler_params=_cparams(("arbitrary", "arbitrary")),
        name="merge",
    )(h, OM, OR, w_gate, w_gate, b_gate3, b_gate3, w_pm, w_pr)


def _outproj_norm_kernel(mg_ref, wo_ref, x_ref, g_ref, o_ref, h_ref):
    y = x_ref[...] + _dot(mg_ref[...], wo_ref[...])
    o_ref[...] = y
    ms = jnp.mean(y * y, axis=-1, keepdims=True)
    h_ref[...] = (y * lax.rsqrt(ms + EPS) * g_ref[...]).astype(bf16)


def _outproj_final_kernel(mg_ref, wo_ref, x_ref, g_ref, y_ref):
    y = x_ref[...] + _dot(mg_ref[...], wo_ref[...])
    ms = jnp.mean(y * y, axis=-1, keepdims=True)
    y_ref[...] = y * lax.rsqrt(ms + EPS) * g_ref[...]


def _wo_spec(l):
    return pl.BlockSpec((None, D_MODEL, D_MODEL), lambda i: (l, 0, 0), pipeline_mode=pl.Buffered(1))


def _outproj(merged, w_o_bf, x, l, norm_g3):
    R = x.shape[0]
    tm = _pick(R, (640, 512, 384, 128))
    row = pl.BlockSpec((tm, D_MODEL), lambda i: (i, 0))
    return pl.pallas_call(
        _outproj_norm_kernel,
        grid=(R // tm,),
        in_specs=[row, _wo_spec(l), row, _layer_spec((1, D_MODEL), l + 1)],
        out_specs=[row, row],
        out_shape=[jax.ShapeDtypeStruct((R, D_MODEL), f32), jax.ShapeDtypeStruct((R, D_MODEL), bf16)],
        compiler_params=_cparams(("arbitrary",)),
        name="outproj_norm",
    )(merged, w_o_bf, x, norm_g3)


def _outproj_final(merged, w_o_bf, x, l, g, row0, nrows):
    tm = _pick(nrows, (512, 256, 128))
    assert row0 % tm == 0
    off = row0 // tm
    row_in = pl.BlockSpec((tm, D_MODEL), lambda i: (i + off, 0))
    return pl.pallas_call(
        _outproj_final_kernel,
        grid=(nrows // tm,),
        in_specs=[row_in, _wo_spec(l), row_in, pl.BlockSpec((1, D_MODEL), lambda i: (0, 0))],
        out_specs=pl.BlockSpec((tm, D_MODEL), lambda i: (i, 0)),
        out_shape=jax.ShapeDtypeStruct((nrows, D_MODEL), f32),
        compiler_params=_cparams(("arbitrary",)),
        name="outproj_final",
    )(merged, w_o_bf, x, g)


def kernel(x_prompt, x_sample, state_mlstm_C, state_mlstm_n, state_mlstm_m, state_mlstm_conv, state_ret_S,
           norm_g, w_in, conv_w, conv_b, w_qm, w_km, b_i, b_f, hn_m_g, skip_m, hn_r_g,
           w_pm, w_pr, w_gate, b_gate, w_o, final_g):
    B, T, D = x_prompt.shape
    NB = x_sample.shape[0]
    depth = w_in.shape[0]
    BT = B * T
    assert D == D_MODEL and x_sample.shape[1] == 1 and T % CHUNK == 0

    x = jnp.concatenate([x_prompt.reshape(BT, D), x_sample.reshape(NB, D)], axis=0)

    w_inT = jnp.swapaxes(w_in, 1, 2)
    wq = w_qm.astype(bf16)
    wk = w_km.astype(bf16)
    wkT = jnp.swapaxes(w_km, 2, 3).astype(bf16)
    w_o_bf = w_o.astype(bf16)
    bias = jnp.pad(jnp.concatenate([b_i, b_f], axis=1), ((0, 0), (0, LANES - 2 * N_HEADS)))
    brow = bias.reshape(depth, 1, LANES)
    bcol = bias[:, :8].reshape(depth, 8, 1)
    b_gate3 = b_gate.reshape(depth, 1, 2 * D)
    norm_g3 = norm_g.reshape(depth, 1, D)
    conv_b3 = conv_b.reshape(depth, 1, W_MIX)
    hngm = hn_m_g.reshape(depth, 1, W_MIX)
    skp = skip_m.reshape(depth, 1, W_MIX)
    hngr = hn_r_g.reshape(depth, 1, W_MIX)
    conv_st = jnp.transpose(state_mlstm_conv, (0, 2, 1, 3))
    m_st = jnp.swapaxes(state_mlstm_m, 1, 2)
    n_st = state_mlstm_n.reshape(depth, NB, W_MIX)

    cos, sin, cosT, sinT, cosd, sind, cosdT, sindT = _rope_tables(T)

    prev_m = prev_r = prev_d = None
    h = _rmsnorm(x, norm_g3[0], bf16)
    for l in range(depth):
        P, KT, GI = _inproj(h, w_inT, l)
        OM, *prev_m = _mlstm_prompt(P, GI, B, T, l, depth, prev_m, conv_w, conv_b3, wq, wkT,
                                    brow, bcol, hngm, skp)
        OR, *prev_r = _ret_prompt(P, KT, B, T, l, depth, prev_r, (cos, sin, cosT, sinT), hngr)
        OM, OR, *prev_d = _decode_step(
            P, KT, GI, OM, OR, prev_d, l, BT, conv_st, n_st, m_st, state_mlstm_C, state_ret_S,
            conv_w, conv_b3, wq, wk, brow, hngm, skp, hngr, (cosd, sind, cosdT, sindT))
        merged = _merge(h, OM, OR, w_gate, b_gate3, w_pm, w_pr, l)
        if l + 1 < depth:
            x, h = _outproj(merged, w_o_bf, x, l, norm_g3)

    fg = final_g.reshape(1, D)
    y_prompt = _outproj_final(merged, w_o_bf, x, depth - 1, fg, 0, BT).reshape(B, T, D)
    y_sample = _outproj_final(merged, w_o_bf, x, depth - 1, fg, BT, NB).reshape(NB, 1, D)
    Cp, np_, mp, convp = prev_m
    (Sp,) = prev_r
    Cs, Ss, n_s, m_s, conv_s = prev_d
    return (y_prompt, y_sample,
            Cp, np_.reshape(depth, B, N_HEADS, D_HEAD), mp[:, :, :N_HEADS, 0], convp, Sp,
            Cs, n_s.reshape(depth, NB, N_HEADS, D_HEAD), jnp.swapaxes(m_s, 1, 2),
            jnp.transpose(conv_s, (0, 2, 1, 3)), Ss)
```

```python
import functools
import math

import jax
import jax.numpy as jnp
import numpy as np
from jax import lax
from jax.experimental import pallas as pl
from jax.experimental.pallas import tpu as pltpu

f32 = jnp.float32
bf16 = jnp.bfloat16

D_MODEL = 2048
N_HEADS = 4
D_HEAD = 256
W_MIX = N_HEADS * D_HEAD
CONV_W = 4
CHUNK = 128
ROPE_BASE = 10000.0
EPS = 1e-6
NEG = -1e30
PAST_LEN = 16384
LANES = 128
HALF = D_HEAD // 2
VMEM_LIMIT = 56 * 1024 * 1024

OFF_IF = 3 * W_MIX
OFF_RET = 3 * W_MIX + 2 * N_HEADS


def _pick(n, cands):
    for c in cands:
        if n % c == 0:
            return c
    return n


def _cparams(sem):
    return pltpu.CompilerParams(dimension_semantics=sem, vmem_limit_bytes=VMEM_LIMIT)


def _dot(a, b):
    return jnp.dot(a, b, preferred_element_type=f32)


def _dot_nt(a, b):
    return lax.dot_general(a, b, (((1,), (1,)), ((), ())), preferred_element_type=f32)


def _split3(x):
    hi = x.astype(bf16).astype(f32)
    r = x - hi
    mid = r.astype(bf16).astype(f32)
    lo = (r - mid).astype(bf16).astype(f32)
    return hi, mid, lo


def _silu(x):
    return x * (0.5 * jnp.tanh(0.5 * x) + 0.5)


def _log_sigmoid(x):
    return jnp.minimum(x, 0.0) - jnp.log1p(jnp.exp(-jnp.abs(x)))


def _head_norm(hc, g):
    mu = jnp.mean(hc, axis=-1, keepdims=True)
    d = hc - mu
    var = jnp.mean(d * d, axis=-1, keepdims=True)
    return d * lax.rsqrt(var + EPS) * g


def _rope_kernel(inv_ref, invT_ref, cos_ref, sin_ref, cosT_ref, sinT_ref,
                 cosd_ref, sind_ref, cosdT_ref, sindT_ref):
    T = cos_ref.shape[0]
    pos = lax.broadcasted_iota(jnp.int32, (T, HALF), 0).astype(f32)
    ang = pos * inv_ref[...]
    cos_ref[...] = jnp.cos(ang)
    sin_ref[...] = jnp.sin(ang)
    posT = lax.broadcasted_iota(jnp.int32, (HALF, T), 1).astype(f32)
    angT = posT * invT_ref[...]
    cosT_ref[...] = jnp.cos(angT)
    sinT_ref[...] = jnp.sin(angT)
    angd = jnp.full((8, HALF), float(PAST_LEN), f32) * inv_ref[...]
    cosd_ref[...] = jnp.cos(angd)
    sind_ref[...] = jnp.sin(angd)
    angdT = jnp.full((HALF, LANES), float(PAST_LEN), f32) * invT_ref[...]
    cosdT_ref[...] = jnp.cos(angdT)
    sindT_ref[...] = jnp.sin(angdT)


def _rope_tables(T):
    inv = ROPE_BASE ** (-jnp.arange(0, HALF, dtype=f32) / HALF)
    shapes = [(T, HALF), (T, HALF), (HALF, T), (HALF, T),
              (8, HALF), (8, HALF), (HALF, LANES), (HALF, LANES)]
    return pl.pallas_call(
        _rope_kernel,
        out_shape=[jax.ShapeDtypeStruct(s, f32) for s in shapes],
        name="rope_tables",
    )(inv.reshape(1, HALF), inv.reshape(HALF, 1))


def _rms_kernel(x_ref, g_ref, o_ref):
    x = x_ref[...]
    ms = jnp.mean(x * x, axis=-1, keepdims=True)
    o_ref[...] = (x * lax.rsqrt(ms + EPS) * g_ref[...]).astype(o_ref.dtype)


def _rmsnorm_into(x, norm_g3, l, R, row0, prev):
    n, D = x.shape
    tr = _pick(n, (512, 256, 128))
    assert row0 % tr == 0
    off = row0 // tr
    kern, xspecs, xargs, aliases = _chain(_rms_kernel, 2, None if prev is None else [prev], 0)
    return pl.pallas_call(
        kern,
        grid=(n // tr,),
        in_specs=[pl.BlockSpec((tr, D), lambda i: (i, 0)), _layer_spec((1, D), l)] + xspecs,
        out_specs=pl.BlockSpec((tr, D), lambda i: (i + off, 0)),
        out_shape=jax.ShapeDtypeStruct((R, D), bf16),
        input_output_aliases=aliases,
        compiler_params=_cparams(("arbitrary",)),
        name="rmsnorm",
    )(x, norm_g3, *xargs)


_TN_IN = 512
_TPG = W_MIX // _TN_IN
_NJ_P = 6 * _TPG
_NJ_K = _TPG


def _inproj_row_offset(j):
    jp = jnp.minimum(j, _NJ_P - 1)
    grp = jp // _TPG
    sub = jp - grp * _TPG
    src_grp = jnp.where(grp < 3, grp, jnp.where(grp == 3, 3, grp + 1))
    off_p = src_grp * W_MIX + jnp.where(grp < 3, 0, 2 * N_HEADS) + sub * _TN_IN
    off_k = OFF_RET + W_MIX + jnp.clip(j - _NJ_P, 0, _NJ_K - 1) * _TN_IN
    return jnp.where(j < _NJ_P, off_p, off_k)


def _inproj_kernel(h_ref, wt_ref, wif_ref, wo_ref, p_ref, kt_ref, gi_ref, wob_ref):
    j = pl.program_id(1)
    wob_ref[...] = wo_ref[...].astype(bf16)

    def weights():
        return wt_ref[0].astype(bf16)

    @pl.when(j == 0)
    def _():
        wif = jnp.concatenate(
            [wif_ref[...], jnp.zeros((LANES - 2 * N_HEADS, D_MODEL), f32)], axis=0).astype(bf16)
        gi_ref[...] = _dot_nt(h_ref[...], wif)

    is_z = (j // _TPG == 2) | (j // _TPG == 5)

    @pl.when((j < _NJ_P) & jnp.logical_not(is_z))
    def _():
        p_ref[...] = _dot_nt(h_ref[...], weights()).astype(bf16)

    @pl.when((j < _NJ_P) & is_z)
    def _():
        p_ref[...] = _silu(_dot_nt(h_ref[...], weights())).astype(bf16)

    @pl.when(j >= _NJ_P)
    def _():
        kt_ref[...] = _dot_nt(weights(), h_ref[...]).astype(bf16)


def _inproj(h, w_inT, w_o, l):
    R = h.shape[0]
    tm = _pick(R, (1664, 1280, 640, 384, 128))
    tn = _TN_IN
    nj = _NJ_P + _NJ_K
    nsteps = (R // tm) * nj
    rps = next(r for r in (32, 64, 128, 256, 512, 1024, 2048) if D_MODEL // r <= nsteps)
    wo_blk = lambda i, j: jnp.minimum(i * nj + j, D_MODEL // rps - 1)
    return pl.pallas_call(
        _inproj_kernel,
        grid=(R // tm, nj),
        in_specs=[
            pl.BlockSpec((tm, D_MODEL), lambda i, j: (i, 0)),
            pl.BlockSpec((pl.Element(1), pl.Element(tn), pl.Element(D_MODEL)),
                         lambda i, j: (l, pl.multiple_of(_inproj_row_offset(j), 8), 0)),
            pl.BlockSpec((None, 2 * N_HEADS, D_MODEL), lambda i, j: (l, OFF_IF // (2 * N_HEADS), 0)),
            pl.BlockSpec((None, rps, D_MODEL), lambda i, j: (l, wo_blk(i, j), 0)),
        ],
        out_specs=[
            pl.BlockSpec((tm, tn), lambda i, j: (i, jnp.minimum(j, _NJ_P - 1))),
            pl.BlockSpec((tn, tm), lambda i, j: (jnp.clip(j - _NJ_P, 0, _NJ_K - 1), i)),
            pl.BlockSpec((tm, LANES), lambda i, j: (i, 0)),
            pl.BlockSpec((rps, D_MODEL), lambda i, j: (wo_blk(i, j), 0)),
        ],
        out_shape=[
            jax.ShapeDtypeStruct((R, 6 * W_MIX), bf16),
            jax.ShapeDtypeStruct((W_MIX, R), bf16),
            jax.ShapeDtypeStruct((R, LANES), f32),
            jax.ShapeDtypeStruct((D_MODEL, D_MODEL), bf16),
        ],
        compiler_params=_cparams(("arbitrary", "arbitrary")),
        name="inproj",
    )(h, w_inT, w_inT, w_o)


def _mlstm_kernel(um_ref, vm_ref, zs_ref, gi_ref, convw_ref, convb_ref, wq_ref, wkT_ref,
                  brow_ref, bcol_ref, hng_ref, skip_ref,
                  om_ref, c_out_ref, n_out_ref, m_out_ref, conv_out_ref,
                  ubuf, cmbuf, cta_s, m_s):
    g = pl.program_id(1)
    ng = pl.num_programs(1)
    Lg = um_ref.shape[0]
    L = CHUNK
    DA = D_HEAD + LANES
    scale = D_HEAD ** -0.5
    heads = range(N_HEADS)

    @pl.when(g == 0)
    def _():
        ubuf[0:8, :] = jnp.zeros((8, W_MIX), f32)
        cta_s[...] = jnp.zeros(cta_s.shape, f32)
        m_s[...] = jnp.full(m_s.shape, NEG, f32)

    ubuf[8:8 + Lg, :] = um_ref[...].astype(f32)
    acc = convb_ref[...] + ubuf[5:5 + Lg, :] * convw_ref[0:1, :]
    for i in range(1, CONV_W):
        acc = acc + ubuf[5 + i:5 + i + Lg, :] * convw_ref[i:i + 1, :]
    cmbuf[...] = _silu(acc)
    last3 = ubuf[5 + Lg:8 + Lg, :]
    ubuf[5:8, :] = last3

    row = lax.broadcasted_iota(jnp.int32, (L, L), 0)
    col = lax.broadcasted_iota(jnp.int32, (L, L), 1)
    causal = row >= col
    tril = causal.astype(bf16)
    triu = (row <= col).astype(bf16)
    brow = brow_ref[...]
    bcol = bcol_ref[...]
    ones_b = jnp.ones((L, LANES), bf16)
    ones_f = jnp.ones((L, LANES), f32)

    for c in range(Lg // L):
        rows = slice(c * L, (c + 1) * L)
        graw = gi_ref[rows, :]
        gb = graw + brow
        gtb = graw.T[0:8, :] + bcol
        lf = _log_sigmoid(gb)
        lft = _log_sigmoid(gtb)
        bc3 = _dot(tril, jnp.concatenate(_split3(lf), axis=1).astype(bf16))
        bc_all = bc3[:, 0:LANES] + bc3[:, LANES:2 * LANES] + bc3[:, 2 * LANES:3 * LANES]
        br3 = _dot(jnp.concatenate(_split3(lft), axis=0).astype(bf16), triu)
        br_all = br3[0:8, :] + br3[8:16, :] + br3[16:24, :]

        cs = [slice(h * D_HEAD, (h + 1) * D_HEAD) for h in heads]
        ch = [cmbuf[rows, cs[h]] for h in heads]
        chb = [x.astype(bf16) for x in ch]
        qb = [(_dot(chb[h], wq_ref[h]) * scale).astype(bf16) for h in heads]
        kTb = [_dot_nt(wkT_ref[h], chb[h]).astype(bf16) for h in heads]
        v = [vm_ref[rows, cs[h]] for h in heads]
        bc = [bc_all[:, 4 + h:5 + h] for h in heads]
        ic = [gb[:, h:h + 1] for h in heads]
        m_prev = [m_s[h][0:1, 0:1] for h in heads]
        dlog = [jnp.where(causal, bc[h] - br_all[4 + h:5 + h, :] + gtb[h:h + 1, :], NEG) for h in heads]
        inter = [bc[h] + m_prev[h] for h in heads]
        m_t = [jnp.maximum(inter[h], jnp.max(dlog[h], axis=-1, keepdims=True)) for h in heads]
        a = [jnp.exp(inter[h] - m_t[h]) for h in heads]
        ws = [(jnp.exp(dlog[h] - m_t[h]) * _dot(qb[h], kTb[h])).astype(bf16) for h in heads]
        cta = [cta_s[h] for h in heads]
        nd = [_dot(ws[h], jnp.concatenate([v[h], ones_b], axis=1))
              + a[h] * _dot(qb[h], cta[h].astype(bf16)) for h in heads]
        m_new = [m_t[h][L - 1:L, :] for h in heads]
        b_last = [bc[h][L - 1:L, :] for h in heads]
        for h in heads:
            wk_c = jnp.exp(b_last[h] - bc[h] + ic[h] - m_new[h])
            dec = jnp.exp(b_last[h] + m_prev[h] - m_new[h])
            vaug = jnp.concatenate([v[h].astype(f32), ones_f], axis=1)
            cta_s[h] = dec * cta[h] + _dot(kTb[h], (wk_c * vaug).astype(bf16))
            m_s[h] = jnp.broadcast_to(m_new[h], (8, LANES))
        for h in heads:
            den = jnp.maximum(jnp.abs(nd[h][:, D_HEAD:DA]), jnp.exp(-m_t[h]))
            rden = 1.0 / den
            hc = jnp.concatenate([nd[h][:, 0:HALF] * rden, nd[h][:, HALF:D_HEAD] * rden], axis=1)
            hn = _head_norm(hc, hng_ref[:, cs[h]])
            zs = zs_ref[rows, cs[h]].astype(f32)
            om_ref[rows, cs[h]] = ((hn + skip_ref[:, cs[h]] * ch[h]) * zs).astype(bf16)

    @pl.when(g == ng - 1)
    def _():
        for h in heads:
            st = cta_s[h]
            c_out_ref[h] = st[:, 0:D_HEAD].T
            n_out_ref[h] = st[:, D_HEAD:DA].T[0:1, :]
        m_out_ref[...] = jnp.concatenate([m_s[h][0:1, :] for h in heads]
                                         + [jnp.zeros((8 - N_HEADS, LANES), f32)], axis=0)
        conv_out_ref[...] = last3


def _drop_refs(kernel_fn, n_in, n_drop, *refs):
    return kernel_fn(*refs[:n_in], *refs[n_in + n_drop:])


def _chain(kernel_fn, n_in, prev, first_out):
    if prev is None:
        return kernel_fn, [], [], {}
    specs = [pl.BlockSpec(memory_space=pl.ANY)] * len(prev)
    aliases = {n_in + k: first_out + k for k in range(len(prev))}
    return functools.partial(_drop_refs, kernel_fn, n_in, len(prev)), specs, list(prev), aliases


def _layer_spec(shape, l):
    nd = len(shape)
    return pl.BlockSpec((None,) + tuple(shape), lambda *_: (l,) + (0,) * nd)


def _mlstm_prompt(P, GI, B, T, l, depth, prev, conv_w, conv_b, wq, wkT, brow, bcol, hng, skip):
    R = P.shape[0]
    Lg = _pick(T, (512, 256, 128))
    ng = T // Lg
    col = lambda k: (lambda b, g: (b * ng + g, k))
    in_specs = [
        pl.BlockSpec((Lg, W_MIX), col(0)),
        pl.BlockSpec((Lg, W_MIX), col(1)),
        pl.BlockSpec((Lg, W_MIX), col(2)),
        pl.BlockSpec((Lg, LANES), col(0)),
        _layer_spec((CONV_W, W_MIX), l),
        _layer_spec((1, W_MIX), l),
        _layer_spec((N_HEADS, D_HEAD, D_HEAD), l),
        _layer_spec((N_HEADS, D_HEAD, D_HEAD), l),
        _layer_spec((1, LANES), l),
        _layer_spec((8, 1), l),
        _layer_spec((1, W_MIX), l),
        _layer_spec((1, W_MIX), l),
    ]
    args = [P, P, P, GI, conv_w, conv_b, wq, wkT, brow, bcol, hng, skip]
    kern, xspecs, xargs, aliases = _chain(_mlstm_kernel, len(args), prev, 1)
    return pl.pallas_call(
        kern,
        grid=(B, ng),
        in_specs=in_specs + xspecs,
        out_specs=[
            pl.BlockSpec((Lg, W_MIX), col(0)),
            pl.BlockSpec((None, None, N_HEADS, D_HEAD, D_HEAD), lambda b, g: (l, b, 0, 0, 0)),
            pl.BlockSpec((None, None, N_HEADS, 1, D_HEAD), lambda b, g: (l, b, 0, 0, 0)),
            pl.BlockSpec((None, None, 8, LANES), lambda b, g: (l, b, 0, 0)),
            pl.BlockSpec((None, None, CONV_W - 1, W_MIX), lambda b, g: (l, b, 0, 0)),
        ],
        out_shape=[
            jax.ShapeDtypeStruct((R, W_MIX), bf16),
            jax.ShapeDtypeStruct((depth, B, N_HEADS, D_HEAD, D_HEAD), f32),
            jax.ShapeDtypeStruct((depth, B, N_HEADS, 1, D_HEAD), f32),
            jax.ShapeDtypeStruct((depth, B, 8, LANES), f32),
            jax.ShapeDtypeStruct((depth, B, CONV_W - 1, W_MIX), f32),
        ],
        scratch_shapes=[
            pltpu.VMEM((8 + Lg, W_MIX), f32),
            pltpu.VMEM((Lg, W_MIX), f32),
            pltpu.VMEM((N_HEADS, D_HEAD, D_HEAD + LANES), f32),
            pltpu.VMEM((N_HEADS, 8, LANES), f32),
        ],
        input_output_aliases=aliases,
        compiler_params=_cparams(("arbitrary", "arbitrary")),
        name="mlstm_prompt",
    )(*args, *xargs)


def _log_gammas():
    return [math.log(1.0 - 2.0 ** (-5.0 - h)) for h in range(N_HEADS)]


def _ret_kernel(q_ref, v_ref, zs_ref, kt_ref, cos_ref, sin_ref, cosT_ref, sinT_ref, hng_ref,
                o_ref, s_out_ref, s_s):
    g = pl.program_id(1)
    ng = pl.num_programs(1)
    Lg = q_ref.shape[0]
    L = CHUNK

    @pl.when(g == 0)
    def _():
        s_s[...] = jnp.zeros(s_s.shape, f32)

    row = lax.broadcasted_iota(jnp.int32, (L, L), 0)
    col = lax.broadcasted_iota(jnp.int32, (L, L), 1)
    causal = row >= col
    diff = (row - col).astype(f32)
    tcol = lax.broadcasted_iota(jnp.int32, (L, 1), 0).astype(f32)
    lgs = _log_gammas()
    intra = [jnp.where(causal, jnp.exp(diff * lg), 0.0) for lg in lgs]
    q_dec = [jnp.exp((tcol + 1.0) * lg) for lg in lgs]
    k_dec = [jnp.exp((L - 1.0 - tcol) * lg) for lg in lgs]
    s_dec = [math.exp(L * lg) for lg in lgs]
    kscale = D_HEAD ** -0.5

    for c in range(Lg // L):
        rows = slice(c * L, (c + 1) * L)
        cos = cos_ref[rows, :]
        sin = sin_ref[rows, :]
        cosT = cosT_ref[:, rows]
        sinT = sinT_ref[:, rows]
        for h in range(N_HEADS):
            cs = slice(h * D_HEAD, (h + 1) * D_HEAD)
            q1 = q_ref[rows, h * D_HEAD:h * D_HEAD + HALF].astype(f32)
            q2 = q_ref[rows, h * D_HEAD + HALF:(h + 1) * D_HEAD].astype(f32)
            qb = jnp.concatenate([q1 * cos - q2 * sin, q1 * sin + q2 * cos], axis=-1).astype(bf16)
            k1 = kt_ref[h * D_HEAD:h * D_HEAD + HALF, rows].astype(f32)
            k2 = kt_ref[h * D_HEAD + HALF:(h + 1) * D_HEAD, rows].astype(f32)
            kTb = (jnp.concatenate([k1 * cosT - k2 * sinT, k1 * sinT + k2 * cosT], axis=0)
                   * kscale).astype(bf16)
            v = v_ref[rows, cs]
            sc = _dot(qb, kTb) * intra[h]
            s = s_s[h]
            o = _dot(sc.astype(bf16), v) + q_dec[h] * _dot(qb, s.astype(bf16))
            s_s[h] = s_dec[h] * s + _dot(kTb, (v.astype(f32) * k_dec[h]).astype(bf16))
            zs = zs_ref[rows, cs].astype(f32)
            o_ref[rows, cs] = (_head_norm(o, hng_ref[:, cs]) * zs).astype(bf16)

    @pl.when(g == ng - 1)
    def _():
        s_out_ref[...] = s_s[...]


def _ret_prompt(P, KT, B, T, l, depth, prev, tabs, hng):
    R = P.shape[0]
    Lg = _pick(T, (512, 256, 128))
    ng = T // Lg
    col = lambda k: (lambda b, g: (b * ng + g, k))
    cos, sin, cosT, sinT = tabs
    in_specs = [
        pl.BlockSpec((Lg, W_MIX), col(3)),
        pl.BlockSpec((Lg, W_MIX), col(4)),
        pl.BlockSpec((Lg, W_MIX), col(5)),
        pl.BlockSpec((W_MIX, Lg), lambda b, g: (0, b * ng + g)),
        pl.BlockSpec((Lg, HALF), lambda b, g: (g, 0)),
        pl.BlockSpec((Lg, HALF), lambda b, g: (g, 0)),
        pl.BlockSpec((HALF, Lg), lambda b, g: (0, g)),
        pl.BlockSpec((HALF, Lg), lambda b, g: (0, g)),
        _layer_spec((1, W_MIX), l),
    ]
    args = [P, P, P, KT, cos, sin, cosT, sinT, hng]
    kern, xspecs, xargs, aliases = _chain(_ret_kernel, len(args), prev, 1)
    return pl.pallas_call(
        kern,
        grid=(B, ng),
        in_specs=in_specs + xspecs,
        out_specs=[
            pl.BlockSpec((Lg, W_MIX), col(0)),
            pl.BlockSpec((None, None, N_HEADS, D_HEAD, D_HEAD), lambda b, g: (l, b, 0, 0, 0)),
        ],
        out_shape=[
            jax.ShapeDtypeStruct((R, W_MIX), bf16),
            jax.ShapeDtypeStruct((depth, B, N_HEADS, D_HEAD, D_HEAD), f32),
        ],
        scratch_shapes=[pltpu.VMEM((N_HEADS, D_HEAD, D_HEAD), f32)],
        input_output_aliases=aliases,
        compiler_params=_cparams(("arbitrary", "arbitrary")),
        name="ret_prompt",
    )(*args, *xargs)


_SEQ_PER_STEP = 4


def _decode_kernel(um_ref, vm_ref, zsm_ref, qr_ref, vr_ref, zsr_ref, kt_ref, gi_ref,
                   conv_ref, n_ref, m_ref, c_ref, s_ref,
                   convw_ref, convb_ref, wq_ref, wk_ref, brow_ref, hngm_ref, skip_ref, hngr_ref,
                   cosd_ref, sind_ref, cosdT_ref, sindT_ref,
                   om_ref, or_ref, c_out_ref, s_out_ref, n_out_ref, m_out_ref, conv_out_ref,
                   vt_s, qrt_s, krt_s, qm_s, km_s, vr_s, sct_s, cqt_s, oacc_s, cm_s):
    s = pl.program_id(0)
    ns = pl.num_programs(0)
    NB = um_ref.shape[0]
    scale = D_HEAD ** -0.5
    lgs = _log_gammas()
    gammas = [math.exp(lg) for lg in lgs]
    lane = lax.broadcasted_iota(jnp.int32, (1, LANES), 1)

    @pl.when(s == 0)
    def _():
        u = um_ref[...].astype(f32)
        acc = convb_ref[...] + u * convw_ref[CONV_W - 1:CONV_W, :]
        for i in range(CONV_W - 1):
            acc = acc + conv_ref[i] * convw_ref[i:i + 1, :]
        cm = _silu(acc)
        cm_s[...] = cm
        for i in range(CONV_W - 2):
            conv_out_ref[i] = conv_ref[i + 1]
        conv_out_ref[CONV_W - 2] = u

        gb = gi_ref[...] + brow_ref[...]
        lf = _log_sigmoid(gb)
        m_cols = jnp.concatenate([m_ref[...], jnp.zeros((8 - N_HEADS, NB), f32)], axis=0).T
        sc = jnp.zeros((NB, LANES), f32)
        mo = jnp.zeros((NB, LANES), f32)
        for h in range(N_HEADS):
            cs = slice(h * D_HEAD, (h + 1) * D_HEAD)
            chb = cm[:, cs].astype(bf16)
            q = _dot(chb, wq_ref[h]) * scale
            k = _dot(chb, wk_ref[h])
            qm_s[:, cs] = q
            km_s[:, cs] = k
            ig = gb[:, h:h + 1]
            lfh = lf[:, 4 + h:5 + h]
            mp = m_cols[:, h:h + 1]
            inter = lfh + mp
            m_t = jnp.maximum(inter, ig)
            w = jnp.exp(ig - m_t)
            a = jnp.exp(inter - m_t)
            nh = n_ref[:, cs]
            ws = w * jnp.sum(q * k, axis=-1, keepdims=True)
            den = ws + a * jnp.sum(nh * q, axis=-1, keepdims=True)
            den = jnp.maximum(jnp.abs(den), jnp.exp(-m_t))
            n_out_ref[:, cs] = a * nh + w * k
            for r, val in ((h, ws), (4 + h, a), (8 + h, den), (12 + h, w)):
                sc = sc + val * (lane == r).astype(f32)
            mo = mo + m_t * (lane == h).astype(f32)
        m_out_ref[...] = mo.T[0:N_HEADS, :]
        sct_s[...] = sc.T
        vt_s[...] = vm_ref[...].astype(f32).T
        vr_s[...] = vr_ref[...].astype(f32)
        cosd = cosd_ref[0:1, :]
        sind = sind_ref[0:1, :]
        cosdT = cosdT_ref[...]
        sindT = sindT_ref[...]
        for h in range(N_HEADS):
            lo = slice(h * D_HEAD, h * D_HEAD + HALF)
            hi = slice(h * D_HEAD + HALF, (h + 1) * D_HEAD)
            q1 = qr_ref[:, lo].astype(f32)
            q2 = qr_ref[:, hi].astype(f32)
            qrot = jnp.concatenate([q1 * cosd - q2 * sind, q1 * sind + q2 * cosd], axis=-1)
            qrt_s[h * D_HEAD:(h + 1) * D_HEAD, :] = qrot.T
            k1 = kt_ref[lo, :].astype(f32)
            k2 = kt_ref[hi, :].astype(f32)
            krt_s[lo, :] = (k1 * cosdT - k2 * sindT) * scale
            krt_s[hi, :] = (k1 * sindT + k2 * cosdT) * scale
        cqt_s[...] = jnp.zeros(cqt_s.shape, f32)

    def take_col(mat, mask):
        return jnp.sum(jnp.where(mask, mat, 0.0), axis=-1, keepdims=True)

    for j in range(_SEQ_PER_STEP):
        b = s * _SEQ_PER_STEP + j
        mask = lane == b
        for h in range(N_HEADS):
            cs = slice(h * D_HEAD, (h + 1) * D_HEAD)
            c = c_ref[j, h]
            q_row = qm_s[pl.ds(b, 1), cs]
            k_row = km_s[pl.ds(b, 1), cs]
            v_col = take_col(vt_s[cs, :], mask)
            a = take_col(sct_s[4 + h:5 + h, :], mask)
            w = take_col(sct_s[12 + h:13 + h, :], mask)
            cq = jnp.sum(c * q_row, axis=-1, keepdims=True)
            cqt_s[cs, :] = jnp.where(mask, cq, cqt_s[cs, :])
            c_out_ref[j, h] = a * c + (w * v_col) * k_row
            st = s_ref[j, h]
            q_col = take_col(qrt_s[cs, :], mask)
            k_col = take_col(krt_s[cs, :], mask)
            v_row = vr_s[pl.ds(b, 1), cs]
            qk = jnp.sum(q_col * k_col, axis=0, keepdims=True)
            o_row = qk * v_row + gammas[h] * jnp.sum(st * q_col, axis=0, keepdims=True)
            oacc_s[pl.ds(b, 1), cs] = o_row
            s_out_ref[j, h] = gammas[h] * st + k_col * v_row

    @pl.when(s == ns - 1)
    def _():
        for h in range(N_HEADS):
            cs = slice(h * D_HEAD, (h + 1) * D_HEAD)
            ht = ((sct_s[h:h + 1, :] * vt_s[cs, :] + sct_s[4 + h:5 + h, :] * cqt_s[cs, :])
                  / sct_s[8 + h:9 + h, :])
            hn = _head_norm(ht.T, hngm_ref[:, cs])
            zsm = zsm_ref[:, cs].astype(f32)
            om_ref[:, cs] = ((hn + skip_ref[:, cs] * cm_s[:, cs]) * zsm).astype(bf16)
            zsr = zsr_ref[:, cs].astype(f32)
            or_ref[:, cs] = (_head_norm(oacc_s[:, cs], hngr_ref[:, cs]) * zsr).astype(bf16)


def _decode_step(P, KT, GI, OM, OR, prev, l, BT, conv_st, n_st, m_st, C_st, S_st,
                 conv_w, conv_b, wq, wk, brow, hngm, skip, hngr, dtabs):
    R = P.shape[0]
    NB = R - BT
    assert NB == LANES and BT % NB == 0 and NB % _SEQ_PER_STEP == 0
    rb = BT // NB
    depth = C_st.shape[0]
    bb = _SEQ_PER_STEP
    cosd, sind, cosdT, sindT = dtabs
    pcol = lambda k: (lambda s: (rb, k))
    rep2 = lambda s: (0, 0)
    st_spec = pl.BlockSpec((None, bb, N_HEADS, D_HEAD, D_HEAD), lambda s: (l, s, 0, 0, 0))
    conv_spec = _layer_spec((CONV_W - 1, NB, W_MIX), l)
    n_spec = _layer_spec((NB, W_MIX), l)
    m_spec = _layer_spec((N_HEADS, NB), l)
    in_specs = [
        pl.BlockSpec((NB, W_MIX), pcol(0)), pl.BlockSpec((NB, W_MIX), pcol(1)),
        pl.BlockSpec((NB, W_MIX), pcol(2)), pl.BlockSpec((NB, W_MIX), pcol(3)),
        pl.BlockSpec((NB, W_MIX), pcol(4)), pl.BlockSpec((NB, W_MIX), pcol(5)),
        pl.BlockSpec((W_MIX, NB), lambda s: (0, rb)),
        pl.BlockSpec((NB, LANES), lambda s: (rb, 0)),
        conv_spec, n_spec, m_spec, st_spec, st_spec,
        _layer_spec((CONV_W, W_MIX), l), _layer_spec((1, W_MIX), l),
        _layer_spec((N_HEADS, D_HEAD, D_HEAD), l), _layer_spec((N_HEADS, D_HEAD, D_HEAD), l),
        _layer_spec((1, LANES), l),
        _layer_spec((1, W_MIX), l), _layer_spec((1, W_MIX), l), _layer_spec((1, W_MIX), l),
        pl.BlockSpec((8, HALF), rep2), pl.BlockSpec((8, HALF), rep2),
        pl.BlockSpec((HALF, LANES), rep2), pl.BlockSpec((HALF, LANES), rep2),
    ]
    args = [P, P, P, P, P, P, KT, GI, conv_st, n_st, m_st, C_st, S_st,
            conv_w, conv_b, wq, wk, brow, hngm, skip, hngr, cosd, sind, cosdT, sindT]
    carried = [OM, OR] + (list(prev) if prev is not None else [])
    kern, xspecs, xargs, aliases = _chain(_decode_kernel, len(args), carried, 0)
    out_row = pl.BlockSpec((NB, W_MIX), lambda s: (rb, 0))
    return pl.pallas_call(
        kern,
        grid=(NB // bb,),
        in_specs=in_specs + xspecs,
        out_specs=[out_row, out_row, st_spec, st_spec, n_spec, m_spec, conv_spec],
        out_shape=[
            jax.ShapeDtypeStruct(OM.shape, bf16),
            jax.ShapeDtypeStruct(OR.shape, bf16),
            jax.ShapeDtypeStruct((depth, NB, N_HEADS, D_HEAD, D_HEAD), f32),
            jax.ShapeDtypeStruct((depth, NB, N_HEADS, D_HEAD, D_HEAD), f32),
            jax.ShapeDtypeStruct((depth, NB, W_MIX), f32),
            jax.ShapeDtypeStruct((depth, N_HEADS, NB), f32),
            jax.ShapeDtypeStruct((depth, CONV_W - 1, NB, W_MIX), f32),
        ],
        scratch_shapes=[
            pltpu.VMEM((W_MIX, NB), f32),
            pltpu.VMEM((W_MIX, NB), f32),
            pltpu.VMEM((W_MIX, NB), f32),
            pltpu.VMEM((NB, W_MIX), f32),
            pltpu.VMEM((NB, W_MIX), f32),
            pltpu.VMEM((NB, W_MIX), f32),
            pltpu.VMEM((LANES, NB), f32),
            pltpu.VMEM((W_MIX, NB), f32),
            pltpu.VMEM((NB, W_MIX), f32),
            pltpu.VMEM((NB, W_MIX), f32),
        ],
        input_output_aliases=aliases,
        compiler_params=_cparams(("arbitrary",)),
        name="decode_step",
    )(*args, *xargs)


def _merge_kernel(h_ref, om_ref, or_ref, wgm_ref, wgr_ref, bgm_ref, bgr_ref, wpm_ref, wpr_ref, o_ref):
    h = h_ref[...]
    gm = jax.nn.sigmoid(_dot(h, wgm_ref[...].astype(bf16)) + bgm_ref[...])
    gr = jax.nn.sigmoid(_dot(h, wgr_ref[...].astype(bf16)) + bgr_ref[...])
    pm = _dot(om_ref[...], wpm_ref[...].astype(bf16))
    pr = _dot(or_ref[...], wpr_ref[...].astype(bf16))
    o_ref[...] = (gm * pm + gr * pr).astype(bf16)


def _merge(h, OM, OR, w_gate, b_gate3, w_pm, w_pr, l):
    R = h.shape[0]
    tm = _pick(R, (1664, 1280, 640, 384, 128))
    tn = 256
    nj = D_MODEL // tn
    return pl.pallas_call(
        _merge_kernel,
        grid=(R // tm, nj),
        in_specs=[
            pl.BlockSpec((tm, D_MODEL), lambda i, j: (i, 0)),
            pl.BlockSpec((tm, W_MIX), lambda i, j: (i, 0)),
            pl.BlockSpec((tm, W_MIX), lambda i, j: (i, 0)),
            pl.BlockSpec((None, D_MODEL, tn), lambda i, j: (l, 0, j)),
            pl.BlockSpec((None, D_MODEL, tn), lambda i, j: (l, 0, j + nj)),
            pl.BlockSpec((None, 1, tn), lambda i, j: (l, 0, j)),
            pl.BlockSpec((None, 1, tn), lambda i, j: (l, 0, j + nj)),
            pl.BlockSpec((None, W_MIX, tn), lambda i, j: (l, 0, j)),
            pl.BlockSpec((None, W_MIX, tn), lambda i, j: (l, 0, j)),
        ],
        out_specs=pl.BlockSpec((tm, tn), lambda i, j: (i, j)),
        out_shape=jax.ShapeDtypeStruct((R, D_MODEL), bf16),
        compiler_params=_cparams(("arbitrary", "arbitrary")),
        name="merge",
    )(h, OM, OR, w_gate, w_gate, b_gate3, b_gate3, w_pm, w_pr)


def _row_halves(tm):
    return (slice(0, tm // 2), slice(tm // 2, tm))


def _outproj_norm_kernel(mg_ref, wo_ref, x_ref, g_ref, o_ref, h_ref):
    for r in _row_halves(x_ref.shape[0]):
        y = x_ref[r, :] + _dot(mg_ref[r, :], wo_ref[...])
        o_ref[r, :] = y
        ms = jnp.mean(y * y, axis=-1, keepdims=True)
        h_ref[r, :] = (y * lax.rsqrt(ms + EPS) * g_ref[...]).astype(bf16)


def _outproj_final_kernel(mg_ref, wo_ref, x_ref, g_ref, y_ref):
    for r in _row_halves(x_ref.shape[0]):
        y = x_ref[r, :] + _dot(mg_ref[r, :], wo_ref[...])
        ms = jnp.mean(y * y, axis=-1, keepdims=True)
        y_ref[r, :] = y * lax.rsqrt(ms + EPS) * g_ref[...]


def _wo_spec():
    return pl.BlockSpec((D_MODEL, D_MODEL), lambda i: (0, 0), pipeline_mode=pl.Buffered(1))


def _outproj(merged, w_o_bf, x, norm_g3, l_next, row0, nrows, x_local, prev):
    R = merged.shape[0]
    tm = _pick(nrows, (640, 512, 384, 128))
    assert row0 % tm == 0
    off = row0 // tm
    row_g = pl.BlockSpec((tm, D_MODEL), lambda i: (i + off, 0))
    row_x = pl.BlockSpec((tm, D_MODEL), lambda i: (i, 0)) if x_local else row_g
    kern, xspecs, xargs, aliases = _chain(_outproj_norm_kernel, 4, prev, 0)
    return pl.pallas_call(
        kern,
        grid=(nrows // tm,),
        in_specs=[row_g, _wo_spec(), row_x, _layer_spec((1, D_MODEL), l_next)] + xspecs,
        out_specs=[row_g, row_g],
        out_shape=[jax.ShapeDtypeStruct((R, D_MODEL), f32), jax.ShapeDtypeStruct((R, D_MODEL), bf16)],
        input_output_aliases=aliases,
        compiler_params=_cparams(("arbitrary",)),
        name="outproj_norm",
    )(merged, w_o_bf, x, norm_g3, *xargs)


def _outproj_final(merged, w_o_bf, x, g, row0, nrows):
    tm = _pick(nrows, (512, 256, 128))
    assert row0 % tm == 0
    off = row0 // tm
    row_in = pl.BlockSpec((tm, D_MODEL), lambda i: (i + off, 0))
    return pl.pallas_call(
        _outproj_final_kernel,
        grid=(nrows // tm,),
        in_specs=[row_in, _wo_spec(), row_in, pl.BlockSpec((1, D_MODEL), lambda i: (0, 0))],
        out_specs=pl.BlockSpec((tm, D_MODEL), lambda i: (i, 0)),
        out_shape=jax.ShapeDtypeStruct((nrows, D_MODEL), f32),
        compiler_params=_cparams(("arbitrary",)),
        name="outproj_final",
    )(merged, w_o_bf, x, g)


def kernel(x_prompt, x_sample, state_mlstm_C, state_mlstm_n, state_mlstm_m, state_mlstm_conv, state_ret_S,
           norm_g, w_in, conv_w, conv_b, w_qm, w_km, b_i, b_f, hn_m_g, skip_m, hn_r_g,
           w_pm, w_pr, w_gate, b_gate, w_o, final_g):
    B, T, D = x_prompt.shape
    NB = x_sample.shape[0]
    depth = w_in.shape[0]
    BT = B * T
    assert D == D_MODEL and x_sample.shape[1] == 1 and T % CHUNK == 0 and depth >= 2

    R = BT + NB
    xp = x_prompt.reshape(BT, D)
    xs = x_sample.reshape(NB, D)

    w_inT = jnp.swapaxes(w_in, 1, 2)
    wq = w_qm.astype(bf16)
    wk = w_km.astype(bf16)
    wkT = jnp.swapaxes(w_km, 2, 3).astype(bf16)
    bias = jnp.pad(jnp.concatenate([b_i, b_f], axis=1), ((0, 0), (0, LANES - 2 * N_HEADS)))
    brow = bias.reshape(depth, 1, LANES)
    bcol = bias[:, :8].reshape(depth, 8, 1)
    b_gate3 = b_gate.reshape(depth, 1, 2 * D)
    norm_g3 = norm_g.reshape(depth, 1, D)
    conv_b3 = conv_b.reshape(depth, 1, W_MIX)
    hngm = hn_m_g.reshape(depth, 1, W_MIX)
    skp = skip_m.reshape(depth, 1, W_MIX)
    hngr = hn_r_g.reshape(depth, 1, W_MIX)
    conv_st = jnp.transpose(state_mlstm_conv, (0, 2, 1, 3))
    m_st = jnp.swapaxes(state_mlstm_m, 1, 2)
    n_st = state_mlstm_n.reshape(depth, NB, W_MIX)

    cos, sin, cosT, sinT, cosd, sind, cosdT, sindT = _rope_tables(T)

    prev_m = prev_r = prev_d = None
    h = _rmsnorm_into(xp, norm_g3, 0, R, 0, None)
    h = _rmsnorm_into(xs, norm_g3, 0, R, BT, h)
    x = None
    for l in range(depth):
        P, KT, GI, w_o_bf = _inproj(h, w_inT, w_o, l)
        OM, *prev_m = _mlstm_prompt(P, GI, B, T, l, depth, prev_m, conv_w, conv_b3, wq, wkT,
                                    brow, bcol, hngm, skp)
        OR, *prev_r = _ret_prompt(P, KT, B, T, l, depth, prev_r, (cos, sin, cosT, sinT), hngr)
        OM, OR, *prev_d = _decode_step(
            P, KT, GI, OM, OR, prev_d, l, BT, conv_st, n_st, m_st, state_mlstm_C, state_ret_S,
            conv_w, conv_b3, wq, wk, brow, hngm, skp, hngr, (cosd, sind, cosdT, sindT))
        merged = _merge(h, OM, OR, w_gate, b_gate3, w_pm, w_pr, l)
        if l + 1 == depth:
            break
        if x is None:
            x, h = _outproj(merged, w_o_bf, xp, norm_g3, l + 1, 0, BT, True, None)
            x, h = _outproj(merged, w_o_bf, xs, norm_g3, l + 1, BT, NB, True, (x, h))
        else:
            x, h = _outproj(merged, w_o_bf, x, norm_g3, l + 1, 0, R, False, None)

    fg = final_g.reshape(1, D)
    y_prompt = _outproj_final(merged, w_o_bf, x, fg, 0, BT).reshape(B, T, D)
    y_sample = _outproj_final(merged, w_o_bf, x, fg, BT, NB).reshape(NB, 1, D)
    Cp, np_, mp, convp = prev_m
    (Sp,) = prev_r
    Cs, Ss, n_s, m_s, conv_s = prev_d
    return (y_prompt, y_sample,
            Cp, np_.reshape(depth, B, N_HEADS, D_HEAD), mp[:, :, :N_HEADS, 0], convp, Sp,
            Cs, n_s.reshape(depth, NB, N_HEADS, D_HEAD), jnp.swapaxes(m_s, 1, 2),
            jnp.transpose(conv_s, (0, 2, 1, 3)), Ss)
```

```python
import functools
import math

import jax
import jax.numpy as jnp
import numpy as np
from jax import lax
from jax.experimental import pallas as pl
from jax.experimental.pallas import tpu as pltpu

f32 = jnp.float32
bf16 = jnp.bfloat16

D_MODEL = 2048
N_HEADS = 4
D_HEAD = 256
W_MIX = N_HEADS * D_HEAD
CONV_W = 4
CHUNK = 128
ROPE_BASE = 10000.0
EPS = 1e-6
NEG = -1e30
PAST_LEN = 16384
LANES = 128
HALF = D_HEAD // 2
VMEM_LIMIT = 56 * 1024 * 1024

OFF_IF = 3 * W_MIX
OFF_RET = 3 * W_MIX + 2 * N_HEADS


def _pick(n, cands):
    for c in cands:
        if n % c == 0:
            return c
    return n


def _cparams(sem):
    return pltpu.CompilerParams(dimension_semantics=sem, vmem_limit_bytes=VMEM_LIMIT)


def _dot(a, b):
    return jnp.dot(a, b, preferred_element_type=f32)


def _dot_nt(a, b):
    return lax.dot_general(a, b, (((1,), (1,)), ((), ())), preferred_element_type=f32)


def _split3(x):
    hi = x.astype(bf16).astype(f32)
    r = x - hi
    mid = r.astype(bf16).astype(f32)
    lo = (r - mid).astype(bf16).astype(f32)
    return hi, mid, lo


def _silu(x):
    return x * (0.5 * jnp.tanh(0.5 * x) + 0.5)


def _log_sigmoid(x):
    return jnp.minimum(x, 0.0) - jnp.log1p(jnp.exp(-jnp.abs(x)))


def _head_norm(hc, g):
    mu = jnp.mean(hc, axis=-1, keepdims=True)
    d = hc - mu
    var = jnp.mean(d * d, axis=-1, keepdims=True)
    return d * lax.rsqrt(var + EPS) * g


def _rope_kernel(inv_ref, invT_ref, cos_ref, sin_ref, cosT_ref, sinT_ref,
                 cosd_ref, sind_ref, cosdT_ref, sindT_ref):
    T = cos_ref.shape[0]
    pos = lax.broadcasted_iota(jnp.int32, (T, HALF), 0).astype(f32)
    ang = pos * inv_ref[...]
    cos = jnp.cos(ang)
    sin = jnp.sin(ang)
    cos_ref[...] = cos
    sin_ref[...] = sin
    cosT_ref[...] = cos.T
    sinT_ref[...] = sin.T
    angd = jnp.full((8, HALF), float(PAST_LEN), f32) * inv_ref[...]
    cosd_ref[...] = jnp.cos(angd)
    sind_ref[...] = jnp.sin(angd)
    angdT = jnp.full((HALF, LANES), float(PAST_LEN), f32) * invT_ref[...]
    cosdT_ref[...] = jnp.cos(angdT)
    sindT_ref[...] = jnp.sin(angdT)


def _rope_tables(T):
    inv = ROPE_BASE ** (-jnp.arange(0, HALF, dtype=f32) / HALF)
    shapes = [(T, HALF), (T, HALF), (HALF, T), (HALF, T),
              (8, HALF), (8, HALF), (HALF, LANES), (HALF, LANES)]
    return pl.pallas_call(
        _rope_kernel,
        out_shape=[jax.ShapeDtypeStruct(s, f32) for s in shapes],
        name="rope_tables",
    )(inv.reshape(1, HALF), inv.reshape(HALF, 1))


def _rms_kernel(x_ref, g_ref, o_ref):
    x = x_ref[...]
    ms = jnp.mean(x * x, axis=-1, keepdims=True)
    o_ref[...] = (x * lax.rsqrt(ms + EPS) * g_ref[...]).astype(o_ref.dtype)


def _rmsnorm_into(x, norm_g3, l, R, row0, prev):
    n, D = x.shape
    tr = _pick(n, (512, 256, 128))
    assert row0 % tr == 0
    off = row0 // tr
    kern, xspecs, xargs, aliases = _chain(_rms_kernel, 2, None if prev is None else [prev], 0)
    return pl.pallas_call(
        kern,
        grid=(n // tr,),
        in_specs=[pl.BlockSpec((tr, D), lambda i: (i, 0)), _layer_spec((1, D), l)] + xspecs,
        out_specs=pl.BlockSpec((tr, D), lambda i: (i + off, 0)),
        out_shape=jax.ShapeDtypeStruct((R, D), bf16),
        input_output_aliases=aliases,
        compiler_params=_cparams(("arbitrary",)),
        name="rmsnorm",
    )(x, norm_g3, *xargs)


_TN_IN = 512
_TPG = W_MIX // _TN_IN
_NJ_P = 6 * _TPG
_NJ_K = _TPG


def _inproj_row_offset(j):
    jp = jnp.minimum(j, _NJ_P - 1)
    grp = jp // _TPG
    sub = jp - grp * _TPG
    src_grp = jnp.where(grp < 3, grp, jnp.where(grp == 3, 3, grp + 1))
    off_p = src_grp * W_MIX + jnp.where(grp < 3, 0, 2 * N_HEADS) + sub * _TN_IN
    off_k = OFF_RET + W_MIX + jnp.clip(j - _NJ_P, 0, _NJ_K - 1) * _TN_IN
    return jnp.where(j < _NJ_P, off_p, off_k)


def _inproj_kernel(h_ref, wt_ref, wif_ref, wo_ref, p_ref, kt_ref, gi_ref, wob_ref):
    j = pl.program_id(1)
    wob_ref[...] = wo_ref[...].astype(bf16)

    def weights():
        return wt_ref[0].astype(bf16)

    @pl.when(j == 0)
    def _():
        wif = jnp.concatenate(
            [wif_ref[...], jnp.zeros((LANES - 2 * N_HEADS, D_MODEL), f32)], axis=0).astype(bf16)
        gi_ref[...] = _dot_nt(h_ref[...], wif)

    is_z = (j // _TPG == 2) | (j // _TPG == 5)

    @pl.when((j < _NJ_P) & jnp.logical_not(is_z))
    def _():
        p_ref[...] = _dot_nt(h_ref[...], weights()).astype(bf16)

    @pl.when((j < _NJ_P) & is_z)
    def _():
        p_ref[...] = _silu(_dot_nt(h_ref[...], weights())).astype(bf16)

    @pl.when(j >= _NJ_P)
    def _():
        kt_ref[...] = _dot_nt(weights(), h_ref[...]).astype(bf16)


def _inproj(h, w_inT, w_o, l):
    R = h.shape[0]
    tm = _pick(R, (1664, 1280, 640, 384, 128))
    tn = _TN_IN
    nj = _NJ_P + _NJ_K
    nsteps = (R // tm) * nj
    rps = next(r for r in (32, 64, 128, 256, 512, 1024, 2048) if D_MODEL // r <= nsteps)
    wo_blk = lambda i, j: jnp.minimum(i * nj + j, D_MODEL // rps - 1)
    return pl.pallas_call(
        _inproj_kernel,
        grid=(R // tm, nj),
        in_specs=[
            pl.BlockSpec((tm, D_MODEL), lambda i, j: (i, 0)),
            pl.BlockSpec((pl.Element(1), pl.Element(tn), pl.Element(D_MODEL)),
                         lambda i, j: (l, pl.multiple_of(_inproj_row_offset(j), 8), 0)),
            pl.BlockSpec((None, 2 * N_HEADS, D_MODEL), lambda i, j: (l, OFF_IF // (2 * N_HEADS), 0)),
            pl.BlockSpec((None, rps, D_MODEL), lambda i, j: (l, wo_blk(i, j), 0)),
        ],
        out_specs=[
            pl.BlockSpec((tm, tn), lambda i, j: (i, jnp.minimum(j, _NJ_P - 1))),
            pl.BlockSpec((tn, tm), lambda i, j: (jnp.clip(j - _NJ_P, 0, _NJ_K - 1), i)),
            pl.BlockSpec((tm, LANES), lambda i, j: (i, 0)),
            pl.BlockSpec((rps, D_MODEL), lambda i, j: (wo_blk(i, j), 0)),
        ],
        out_shape=[
            jax.ShapeDtypeStruct((R, 6 * W_MIX), bf16),
            jax.ShapeDtypeStruct((W_MIX, R), bf16),
            jax.ShapeDtypeStruct((R, LANES), f32),
            jax.ShapeDtypeStruct((D_MODEL, D_MODEL), bf16),
        ],
        compiler_params=_cparams(("arbitrary", "arbitrary")),
        name="inproj",
    )(h, w_inT, w_inT, w_o)


def _mlstm_kernel(um_ref, vm_ref, zs_ref, gi_ref, convw_ref, convb_ref, wq_ref, wkT_ref,
                  brow_ref, bcol_ref, hng_ref, skip_ref,
                  om_ref, c_out_ref, n_out_ref, m_out_ref, conv_out_ref,
                  tail_s, cta_s, m_s):
    g = pl.program_id(1)
    ng = pl.num_programs(1)
    Lg = um_ref.shape[0]
    L = CHUNK
    DA = D_HEAD + LANES
    scale = D_HEAD ** -0.5
    heads = range(N_HEADS)

    @pl.when(g == 0)
    def _():
        tail_s[...] = jnp.zeros(tail_s.shape, bf16)
        cta_s[...] = jnp.zeros(cta_s.shape, f32)
        m_s[...] = jnp.full(m_s.shape, NEG, f32)

    srow = lax.broadcasted_iota(jnp.int32, (L, 2 * L), 0)
    scol = lax.broadcasted_iota(jnp.int32, (L, 2 * L), 1)
    shift_mat = jnp.concatenate([(scol == srow + L - s).astype(bf16) for s in range(1, CONV_W)], axis=0)

    row = lax.broadcasted_iota(jnp.int32, (L, L), 0)
    col = lax.broadcasted_iota(jnp.int32, (L, L), 1)
    causal = row >= col
    tril = causal.astype(bf16)
    triu = (row <= col).astype(bf16)
    brow = brow_ref[...]
    bcol = bcol_ref[...]
    ones_b = jnp.ones((L, LANES), bf16)
    ones_f = jnp.ones((L, LANES), f32)

    for c in range(Lg // L):
        rows = slice(c * L, (c + 1) * L)
        graw = gi_ref[rows, :]
        gb = graw + brow
        gtb = graw.T[0:8, :] + bcol
        lf = _log_sigmoid(gb)
        lft = _log_sigmoid(gtb)
        bc3 = _dot(tril, jnp.concatenate(_split3(lf), axis=1).astype(bf16))
        bc_all = bc3[:, 0:LANES] + bc3[:, LANES:2 * LANES] + bc3[:, 2 * LANES:3 * LANES]
        br3 = _dot(jnp.concatenate(_split3(lft), axis=0).astype(bf16), triu)
        br_all = br3[0:8, :] + br3[8:16, :] + br3[16:24, :]

        u_cur = um_ref[rows, :]
        u_prev = tail_s[...] if c == 0 else um_ref[(c - 1) * L:c * L, :]
        delayed = _dot(shift_mat, jnp.concatenate([u_prev, u_cur], axis=0))
        acc = convb_ref[...] + u_cur.astype(f32) * convw_ref[CONV_W - 1:CONV_W, :]
        for s in range(1, CONV_W):
            acc = acc + delayed[(s - 1) * L:s * L, :] * convw_ref[CONV_W - 1 - s:CONV_W - s, :]
        cm = _silu(acc)

        cs = [slice(h * D_HEAD, (h + 1) * D_HEAD) for h in heads]
        ch = [cm[:, cs[h]] for h in heads]
        chb = [x.astype(bf16) for x in ch]
        qb = [(_dot(chb[h], wq_ref[h]) * scale).astype(bf16) for h in heads]
        kTb = [_dot_nt(wkT_ref[h], chb[h]).astype(bf16) for h in heads]
        v = [vm_ref[rows, cs[h]] for h in heads]
        bc = [bc_all[:, 4 + h:5 + h] for h in heads]
        ic = [gb[:, h:h + 1] for h in heads]
        m_prev = [m_s[h][0:1, 0:1] for h in heads]
        dlog = [jnp.where(causal, bc[h] - br_all[4 + h:5 + h, :] + gtb[h:h + 1, :], NEG) for h in heads]
        inter = [bc[h] + m_prev[h] for h in heads]
        m_t = [jnp.maximum(inter[h], jnp.max(dlog[h], axis=-1, keepdims=True)) for h in heads]
        a = [jnp.exp(inter[h] - m_t[h]) for h in heads]
        ws = [(jnp.exp(dlog[h] - m_t[h]) * _dot(qb[h], kTb[h])).astype(bf16) for h in heads]
        cta = [cta_s[h] for h in heads]
        nd = [_dot(ws[h], jnp.concatenate([v[h], ones_b], axis=1))
              + a[h] * _dot(qb[h], cta[h].astype(bf16)) for h in heads]
        m_new = [m_t[h][L - 1:L, :] for h in heads]
        b_last = [bc[h][L - 1:L, :] for h in heads]
        for h in heads:
            wk_c = jnp.exp(b_last[h] - bc[h] + ic[h] - m_new[h])
            dec = jnp.exp(b_last[h] + m_prev[h] - m_new[h])
            vaug = jnp.concatenate([v[h].astype(f32), ones_f], axis=1)
            cta_s[h] = dec * cta[h] + _dot(kTb[h], (wk_c * vaug).astype(bf16))
            m_s[h] = jnp.broadcast_to(m_new[h], (8, LANES))
        for h in heads:
            den = jnp.maximum(jnp.abs(nd[h][:, D_HEAD:DA]), jnp.exp(-m_t[h]))
            rden = 1.0 / den
            hc = jnp.concatenate([nd[h][:, 0:HALF] * rden, nd[h][:, HALF:D_HEAD] * rden], axis=1)
            hn = _head_norm(hc, hng_ref[:, cs[h]])
            zs = zs_ref[rows, cs[h]].astype(f32)
            om_ref[rows, cs[h]] = ((hn + skip_ref[:, cs[h]] * ch[h]) * zs).astype(bf16)

    tail_s[...] = um_ref[Lg - L:Lg, :]

    @pl.when(g == ng - 1)
    def _():
        for h in heads:
            st = cta_s[h]
            c_out_ref[h] = st[:, 0:D_HEAD].T
            n_out_ref[h] = st[:, D_HEAD:DA].T[0:1, :]
        m_out_ref[...] = jnp.concatenate([m_s[h][0:1, :] for h in heads]
                                         + [jnp.zeros((8 - N_HEADS, LANES), f32)], axis=0)
        conv_out_ref[...] = um_ref[Lg - 8:Lg, :].astype(f32)[8 - (CONV_W - 1):8, :]


def _drop_refs(kernel_fn, n_in, n_drop, *refs):
    return kernel_fn(*refs[:n_in], *refs[n_in + n_drop:])


def _chain(kernel_fn, n_in, prev, first_out):
    if prev is None:
        return kernel_fn, [], [], {}
    specs = [pl.BlockSpec(memory_space=pl.ANY)] * len(prev)
    aliases = {n_in + k: first_out + k for k in range(len(prev))}
    return functools.partial(_drop_refs, kernel_fn, n_in, len(prev)), specs, list(prev), aliases


def _layer_spec(shape, l):
    nd = len(shape)
    return pl.BlockSpec((None,) + tuple(shape), lambda *_: (l,) + (0,) * nd)


def _mlstm_prompt(P, GI, B, T, l, depth, prev, conv_w, conv_b, wq, wkT, brow, bcol, hng, skip):
    R = P.shape[0]
    Lg = _pick(T, (512, 256, 128))
    ng = T // Lg
    col = lambda k: (lambda b, g: (b * ng + g, k))
    in_specs = [
        pl.BlockSpec((Lg, W_MIX), col(0)),
        pl.BlockSpec((Lg, W_MIX), col(1)),
        pl.BlockSpec((Lg, W_MIX), col(2)),
        pl.BlockSpec((Lg, LANES), col(0)),
        _layer_spec((CONV_W, W_MIX), l),
        _layer_spec((1, W_MIX), l),
        _layer_spec((N_HEADS, D_HEAD, D_HEAD), l),
        _layer_spec((N_HEADS, D_HEAD, D_HEAD), l),
        _layer_spec((1, LANES), l),
        _layer_spec((8, 1), l),
        _layer_spec((1, W_MIX), l),
        _layer_spec((1, W_MIX), l),
    ]
    args = [P, P, P, GI, conv_w, conv_b, wq, wkT, brow, bcol, hng, skip]
    kern, xspecs, xargs, aliases = _chain(_mlstm_kernel, len(args), prev, 1)
    return pl.pallas_call(
        kern,
        grid=(B, ng),
        in_specs=in_specs + xspecs,
        out_specs=[
            pl.BlockSpec((Lg, W_MIX), col(0)),
            pl.BlockSpec((None, None, N_HEADS, D_HEAD, D_HEAD), lambda b, g: (l, b, 0, 0, 0)),
            pl.BlockSpec((None, None, N_HEADS, 1, D_HEAD), lambda b, g: (l, b, 0, 0, 0)),
            pl.BlockSpec((None, None, 8, LANES), lambda b, g: (l, b, 0, 0)),
            pl.BlockSpec((None, None, CONV_W - 1, W_MIX), lambda b, g: (l, b, 0, 0)),
        ],
        out_shape=[
            jax.ShapeDtypeStruct((R, W_MIX), bf16),
            jax.ShapeDtypeStruct((depth, B, N_HEADS, D_HEAD, D_HEAD), f32),
            jax.ShapeDtypeStruct((depth, B, N_HEADS, 1, D_HEAD), f32),
            jax.ShapeDtypeStruct((depth, B, 8, LANES), f32),
            jax.ShapeDtypeStruct((depth, B, CONV_W - 1, W_MIX), f32),
        ],
        scratch_shapes=[
            pltpu.VMEM((CHUNK, W_MIX), bf16),
            pltpu.VMEM((N_HEADS, D_HEAD, D_HEAD + LANES), f32),
            pltpu.VMEM((N_HEADS, 8, LANES), f32),
        ],
        input_output_aliases=aliases,
        compiler_params=_cparams(("arbitrary", "arbitrary")),
        name="mlstm_prompt",
    )(*args, *xargs)


def _log_gammas():
    return [math.log(1.0 - 2.0 ** (-5.0 - h)) for h in range(N_HEADS)]


def _ret_kernel(q_ref, v_ref, zs_ref, kt_ref, cos_ref, sin_ref, cosT_ref, sinT_ref, hng_ref,
                o_ref, s_out_ref, s_s):
    g = pl.program_id(1)
    ng = pl.num_programs(1)
    Lg = q_ref.shape[0]
    L = CHUNK

    @pl.when(g == 0)
    def _():
        s_s[...] = jnp.zeros(s_s.shape, f32)

    row = lax.broadcasted_iota(jnp.int32, (L, L), 0)
    col = lax.broadcasted_iota(jnp.int32, (L, L), 1)
    causal = row >= col
    diff = (row - col).astype(f32)
    tcol = lax.broadcasted_iota(jnp.int32, (L, 1), 0).astype(f32)
    lgs = _log_gammas()
    intra = [jnp.where(causal, jnp.exp(diff * lg), 0.0) for lg in lgs]
    q_dec = [jnp.exp((tcol + 1.0) * lg) for lg in lgs]
    k_dec = [jnp.exp((L - 1.0 - tcol) * lg) for lg in lgs]
    s_dec = [math.exp(L * lg) for lg in lgs]
    kscale = D_HEAD ** -0.5

    for c in range(Lg // L):
        rows = slice(c * L, (c + 1) * L)
        cos = cos_ref[rows, :]
        sin = sin_ref[rows, :]
        cosT = cosT_ref[:, rows] * kscale
        sinT = sinT_ref[:, rows] * kscale
        heads = range(N_HEADS)
        cs = [slice(h * D_HEAD, (h + 1) * D_HEAD) for h in heads]
        lo = [slice(h * D_HEAD, h * D_HEAD + HALF) for h in heads]
        hi = [slice(h * D_HEAD + HALF, (h + 1) * D_HEAD) for h in heads]
        q1 = [q_ref[rows, lo[h]].astype(f32) for h in heads]
        q2 = [q_ref[rows, hi[h]].astype(f32) for h in heads]
        qb = [jnp.concatenate([q1[h] * cos - q2[h] * sin, q1[h] * sin + q2[h] * cos], axis=-1).astype(bf16)
              for h in heads]
        k1 = [kt_ref[lo[h], rows].astype(f32) for h in heads]
        k2 = [kt_ref[hi[h], rows].astype(f32) for h in heads]
        kTb = [jnp.concatenate([k1[h] * cosT - k2[h] * sinT, k1[h] * sinT + k2[h] * cosT],
                               axis=0).astype(bf16) for h in heads]
        v = [v_ref[rows, cs[h]] for h in heads]
        sc = [(_dot(qb[h], kTb[h]) * intra[h]).astype(bf16) for h in heads]
        s = [s_s[h] for h in heads]
        o = [_dot(sc[h], v[h]) + q_dec[h] * _dot(qb[h], s[h].astype(bf16)) for h in heads]
        for h in heads:
            s_s[h] = s_dec[h] * s[h] + _dot(kTb[h], (v[h].astype(f32) * k_dec[h]).astype(bf16))
        for h in heads:
            zs = zs_ref[rows, cs[h]].astype(f32)
            o_ref[rows, cs[h]] = (_head_norm(o[h], hng_ref[:, cs[h]]) * zs).astype(bf16)

    @pl.when(g == ng - 1)
    def _():
        s_out_ref[...] = s_s[...]


def _ret_prompt(P, KT, B, T, l, depth, prev, tabs, hng):
    R = P.shape[0]
    Lg = _pick(T, (512, 256, 128))
    ng = T // Lg
    col = lambda k: (lambda b, g: (b * ng + g, k))
    cos, sin, cosT, sinT = tabs
    in_specs = [
        pl.BlockSpec((Lg, W_MIX), col(3)),
        pl.BlockSpec((Lg, W_MIX), col(4)),
        pl.BlockSpec((Lg, W_MIX), col(5)),
        pl.BlockSpec((W_MIX, Lg), lambda b, g: (0, b * ng + g)),
        pl.BlockSpec((Lg, HALF), lambda b, g: (g, 0)),
        pl.BlockSpec((Lg, HALF), lambda b, g: (g, 0)),
        pl.BlockSpec((HALF, Lg), lambda b, g: (0, g)),
        pl.BlockSpec((HALF, Lg), lambda b, g: (0, g)),
        _layer_spec((1, W_MIX), l),
    ]
    args = [P, P, P, KT, cos, sin, cosT, sinT, hng]
    kern, xspecs, xargs, aliases = _chain(_ret_kernel, len(args), prev, 1)
    return pl.pallas_call(
        kern,
        grid=(B, ng),
        in_specs=in_specs + xspecs,
        out_specs=[
            pl.BlockSpec((Lg, W_MIX), col(0)),
            pl.BlockSpec((None, None, N_HEADS, D_HEAD, D_HEAD), lambda b, g: (l, b, 0, 0, 0)),
        ],
        out_shape=[
            jax.ShapeDtypeStruct((R, W_MIX), bf16),
            jax.ShapeDtypeStruct((depth, B, N_HEADS, D_HEAD, D_HEAD), f32),
        ],
        scratch_shapes=[pltpu.VMEM((N_HEADS, D_HEAD, D_HEAD), f32)],
        input_output_aliases=aliases,
        compiler_params=_cparams(("arbitrary", "arbitrary")),
        name="ret_prompt",
    )(*args, *xargs)


_SEQ_PER_STEP = 4


def _decode_kernel(um_ref, vm_ref, zsm_ref, qr_ref, vr_ref, zsr_ref, kt_ref, gi_ref,
                   conv_ref, n_ref, m_ref, c_ref, s_ref,
                   convw_ref, convb_ref, wq_ref, wk_ref, brow_ref, hngm_ref, skip_ref, hngr_ref,
                   cosd_ref, sind_ref, cosdT_ref, sindT_ref,
                   om_ref, or_ref, c_out_ref, s_out_ref, n_out_ref, m_out_ref, conv_out_ref,
                   vt_s, qrt_s, krt_s, qm_s, km_s, vr_s, sct_s, cqt_s, oacc_s, cm_s):
    s = pl.program_id(0)
    ns = pl.num_programs(0)
    NB = um_ref.shape[0]
    scale = D_HEAD ** -0.5
    lgs = _log_gammas()
    gammas = [math.exp(lg) for lg in lgs]
    lane = lax.broadcasted_iota(jnp.int32, (1, LANES), 1)

    @pl.when(s == 0)
    def _():
        u = um_ref[...].astype(f32)
        acc = convb_ref[...] + u * convw_ref[CONV_W - 1:CONV_W, :]
        for i in range(CONV_W - 1):
            acc = acc + conv_ref[i] * convw_ref[i:i + 1, :]
        cm = _silu(acc)
        cm_s[...] = cm
        for i in range(CONV_W - 2):
            conv_out_ref[i] = conv_ref[i + 1]
        conv_out_ref[CONV_W - 2] = u

        gb = gi_ref[...] + brow_ref[...]
        lf = _log_sigmoid(gb)
        m_cols = jnp.concatenate([m_ref[...], jnp.zeros((8 - N_HEADS, NB), f32)], axis=0).T
        sc = jnp.zeros((NB, LANES), f32)
        mo = jnp.zeros((NB, LANES), f32)
        for h in range(N_HEADS):
            cs = slice(h * D_HEAD, (h + 1) * D_HEAD)
            chb = cm[:, cs].astype(bf16)
            q = _dot(chb, wq_ref[h]) * scale
            k = _dot(chb, wk_ref[h])
            qm_s[:, cs] = q
            km_s[:, cs] = k
            ig = gb[:, h:h + 1]
            lfh = lf[:, 4 + h:5 + h]
            mp = m_cols[:, h:h + 1]
            inter = lfh + mp
            m_t = jnp.maximum(inter, ig)
            w = jnp.exp(ig - m_t)
            a = jnp.exp(inter - m_t)
            nh = n_ref[:, cs]
            ws = w * jnp.sum(q * k, axis=-1, keepdims=True)
            den = ws + a * jnp.sum(nh * q, axis=-1, keepdims=True)
            den = jnp.maximum(jnp.abs(den), jnp.exp(-m_t))
            n_out_ref[:, cs] = a * nh + w * k
            for r, val in ((h, ws), (4 + h, a), (8 + h, den), (12 + h, w)):
                sc = sc + val * (lane == r).astype(f32)
            mo = mo + m_t * (lane == h).astype(f32)
        m_out_ref[...] = mo.T[0:N_HEADS, :]
        sct_s[...] = sc.T
        vt_s[...] = vm_ref[...].astype(f32).T
        vr_s[...] = vr_ref[...].astype(f32)
        cosd = cosd_ref[0:1, :]
        sind = sind_ref[0:1, :]
        cosdT = cosdT_ref[...]
        sindT = sindT_ref[...]
        for h in range(N_HEADS):
            lo = slice(h * D_HEAD, h * D_HEAD + HALF)
            hi = slice(h * D_HEAD + HALF, (h + 1) * D_HEAD)
            q1 = qr_ref[:, lo].astype(f32)
            q2 = qr_ref[:, hi].astype(f32)
            qrot = jnp.concatenate([q1 * cosd - q2 * sind, q1 * sind + q2 * cosd], axis=-1)
            qrt_s[h * D_HEAD:(h + 1) * D_HEAD, :] = qrot.T
            k1 = kt_ref[lo, :].astype(f32)
            k2 = kt_ref[hi, :].astype(f32)
            krt_s[lo, :] = (k1 * cosdT - k2 * sindT) * scale
            krt_s[hi, :] = (k1 * sindT + k2 * cosdT) * scale
        cqt_s[...] = jnp.zeros(cqt_s.shape, f32)

    def take_col(mat, mask):
        return jnp.sum(jnp.where(mask, mat, 0.0), axis=-1, keepdims=True)

    for j in range(_SEQ_PER_STEP):
        b = s * _SEQ_PER_STEP + j
        mask = lane == b
        for h in range(N_HEADS):
            cs = slice(h * D_HEAD, (h + 1) * D_HEAD)
            c = c_ref[j, h]
            q_row = qm_s[pl.ds(b, 1), cs]
            k_row = km_s[pl.ds(b, 1), cs]
            v_col = take_col(vt_s[cs, :], mask)
            a = take_col(sct_s[4 + h:5 + h, :], mask)
            w = take_col(sct_s[12 + h:13 + h, :], mask)
            cq = jnp.sum(c * q_row, axis=-1, keepdims=True)
            cqt_s[cs, :] = jnp.where(mask, cq, cqt_s[cs, :])
            c_out_ref[j, h] = a * c + (w * v_col) * k_row
            st = s_ref[j, h]
            q_col = take_col(qrt_s[cs, :], mask)
            k_col = take_col(krt_s[cs, :], mask)
            v_row = vr_s[pl.ds(b, 1), cs]
            qk = jnp.sum(q_col * k_col, axis=0, keepdims=True)
            o_row = qk * v_row + gammas[h] * jnp.sum(st * q_col, axis=0, keepdims=True)
            oacc_s[pl.ds(b, 1), cs] = o_row
            s_out_ref[j, h] = gammas[h] * st + k_col * v_row

    @pl.when(s == ns - 1)
    def _():
        for h in range(N_HEADS):
            cs = slice(h * D_HEAD, (h + 1) * D_HEAD)
            ht = ((sct_s[h:h + 1, :] * vt_s[cs, :] + sct_s[4 + h:5 + h, :] * cqt_s[cs, :])
                  / sct_s[8 + h:9 + h, :])
            hn = _head_norm(ht.T, hngm_ref[:, cs])
            zsm = zsm_ref[:, cs].astype(f32)
            om_ref[:, cs] = ((hn + skip_ref[:, cs] * cm_s[:, cs]) * zsm).astype(bf16)
            zsr = zsr_ref[:, cs].astype(f32)
            or_ref[:, cs] = (_head_norm(oacc_s[:, cs], hngr_ref[:, cs]) * zsr).astype(bf16)


def _decode_step(P, KT, GI, OM, OR, prev, l, BT, conv_st, n_st, m_st, C_st, S_st,
                 conv_w, conv_b, wq, wk, brow, hngm, skip, hngr, dtabs):
    R = P.shape[0]
    NB = R - BT
    assert NB == LANES and BT % NB == 0 and NB % _SEQ_PER_STEP == 0
    rb = BT // NB
    depth = C_st.shape[0]
    bb = _SEQ_PER_STEP
    cosd, sind, cosdT, sindT = dtabs
    pcol = lambda k: (lambda s: (rb, k))
    rep2 = lambda s: (0, 0)
    st_spec = pl.BlockSpec((None, bb, N_HEADS, D_HEAD, D_HEAD), lambda s: (l, s, 0, 0, 0))
    conv_spec = _layer_spec((CONV_W - 1, NB, W_MIX), l)
    n_spec = _layer_spec((NB, W_MIX), l)
    m_spec = _layer_spec((N_HEADS, NB), l)
    in_specs = [
        pl.BlockSpec((NB, W_MIX), pcol(0)), pl.BlockSpec((NB, W_MIX), pcol(1)),
        pl.BlockSpec((NB, W_MIX), pcol(2)), pl.BlockSpec((NB, W_MIX), pcol(3)),
        pl.BlockSpec((NB, W_MIX), pcol(4)), pl.BlockSpec((NB, W_MIX), pcol(5)),
        pl.BlockSpec((W_MIX, NB), lambda s: (0, rb)),
        pl.BlockSpec((NB, LANES), lambda s: (rb, 0)),
        conv_spec, n_spec, m_spec, st_spec, st_spec,
        _layer_spec((CONV_W, W_MIX), l), _layer_spec((1, W_MIX), l),
        _layer_spec((N_HEADS, D_HEAD, D_HEAD), l), _layer_spec((N_HEADS, D_HEAD, D_HEAD), l),
        _layer_spec((1, LANES), l),
        _layer_spec((1, W_MIX), l), _layer_spec((1, W_MIX), l), _layer_spec((1, W_MIX), l),
        pl.BlockSpec((8, HALF), rep2), pl.BlockSpec((8, HALF), rep2),
        pl.BlockSpec((HALF, LANES), rep2), pl.BlockSpec((HALF, LANES), rep2),
    ]
    args = [P, P, P, P, P, P, KT, GI, conv_st, n_st, m_st, C_st, S_st,
            conv_w, conv_b, wq, wk, brow, hngm, skip, hngr, cosd, sind, cosdT, sindT]
    carried = [OM, OR] + (list(prev) if prev is not None else [])
    kern, xspecs, xargs, aliases = _chain(_decode_kernel, len(args), carried, 0)
    out_row = pl.BlockSpec((NB, W_MIX), lambda s: (rb, 0))
    return pl.pallas_call(
        kern,
        grid=(NB // bb,),
        in_specs=in_specs + xspecs,
        out_specs=[out_row, out_row, st_spec, st_spec, n_spec, m_spec, conv_spec],
        out_shape=[
            jax.ShapeDtypeStruct(OM.shape, bf16),
            jax.ShapeDtypeStruct(OR.shape, bf16),
            jax.ShapeDtypeStruct((depth, NB, N_HEADS, D_HEAD, D_HEAD), f32),
            jax.ShapeDtypeStruct((depth, NB, N_HEADS, D_HEAD, D_HEAD), f32),
            jax.ShapeDtypeStruct((depth, NB, W_MIX), f32),
            jax.ShapeDtypeStruct((depth, N_HEADS, NB), f32),
            jax.ShapeDtypeStruct((depth, CONV_W - 1, NB, W_MIX), f32),
        ],
        scratch_shapes=[
            pltpu.VMEM((W_MIX, NB), f32),
            pltpu.VMEM((W_MIX, NB), f32),
            pltpu.VMEM((W_MIX, NB), f32),
            pltpu.VMEM((NB, W_MIX), f32),
            pltpu.VMEM((NB, W_MIX), f32),
            pltpu.VMEM((NB, W_MIX), f32),
            pltpu.VMEM((LANES, NB), f32),
            pltpu.VMEM((W_MIX, NB), f32),
            pltpu.VMEM((NB, W_MIX), f32),
            pltpu.VMEM((NB, W_MIX), f32),
        ],
        input_output_aliases=aliases,
        compiler_params=_cparams(("arbitrary",)),
        name="decode_step",
    )(*args, *xargs)


def _merge_kernel(h_ref, om_ref, or_ref, wgm_ref, wgr_ref, bgm_ref, bgr_ref, wpm_ref, wpr_ref, o_ref):
    h = h_ref[...]
    gm = jax.nn.sigmoid(_dot(h, wgm_ref[...].astype(bf16)) + bgm_ref[...])
    gr = jax.nn.sigmoid(_dot(h, wgr_ref[...].astype(bf16)) + bgr_ref[...])
    pm = _dot(om_ref[...], wpm_ref[...].astype(bf16))
    pr = _dot(or_ref[...], wpr_ref[...].astype(bf16))
    o_ref[...] = (gm * pm + gr * pr).astype(bf16)


def _merge(h, OM, OR, w_gate, b_gate3, w_pm, w_pr, l):
    R = h.shape[0]
    tm = _pick(R, (1664, 1280, 640, 384, 128))
    tn = 256
    nj = D_MODEL // tn
    return pl.pallas_call(
        _merge_kernel,
        grid=(R // tm, nj),
        in_specs=[
            pl.BlockSpec((tm, D_MODEL), lambda i, j: (i, 0)),
            pl.BlockSpec((tm, W_MIX), lambda i, j: (i, 0)),
            pl.BlockSpec((tm, W_MIX), lambda i, j: (i, 0)),
            pl.BlockSpec((None, D_MODEL, tn), lambda i, j: (l, 0, j)),
            pl.BlockSpec((None, D_MODEL, tn), lambda i, j: (l, 0, j + nj)),
            pl.BlockSpec((None, 1, tn), lambda i, j: (l, 0, j)),
            pl.BlockSpec((None, 1, tn), lambda i, j: (l, 0, j + nj)),
            pl.BlockSpec((None, W_MIX, tn), lambda i, j: (l, 0, j)),
            pl.BlockSpec((None, W_MIX, tn), lambda i, j: (l, 0, j)),
        ],
        out_specs=pl.BlockSpec((tm, tn), lambda i, j: (i, j)),
        out_shape=jax.ShapeDtypeStruct((R, D_MODEL), bf16),
        compiler_params=_cparams(("arbitrary", "arbitrary")),
        name="merge",
    )(h, OM, OR, w_gate, w_gate, b_gate3, b_gate3, w_pm, w_pr)


def _row_halves(tm):
    return (slice(0, tm // 2), slice(tm // 2, tm))


def _outproj_norm_kernel(mg_ref, wo_ref, x_ref, g_ref, o_ref, h_ref):
    for r in _row_halves(x_ref.shape[0]):
        y = x_ref[r, :] + _dot(mg_ref[r, :], wo_ref[...])
        o_ref[r, :] = y
        ms = jnp.mean(y * y, axis=-1, keepdims=True)
        h_ref[r, :] = (y * lax.rsqrt(ms + EPS) * g_ref[...]).astype(bf16)


def _outproj_final_kernel(mg_ref, wo_ref, x_ref, g_ref, y_ref):
    for r in _row_halves(x_ref.shape[0]):
        y = x_ref[r, :] + _dot(mg_ref[r, :], wo_ref[...])
        ms = jnp.mean(y * y, axis=-1, keepdims=True)
        y_ref[r, :] = y * lax.rsqrt(ms + EPS) * g_ref[...]


def _wo_spec():
    return pl.BlockSpec((D_MODEL, D_MODEL), lambda i: (0, 0), pipeline_mode=pl.Buffered(1))


def _outproj(merged, w_o_bf, x, norm_g3, l_next, row0, nrows, x_local, prev):
    R = merged.shape[0]
    tm = _pick(nrows, (640, 512, 384, 128))
    assert row0 % tm == 0
    off = row0 // tm
    row_g = pl.BlockSpec((tm, D_MODEL), lambda i: (i + off, 0))
    row_x = pl.BlockSpec((tm, D_MODEL), lambda i: (i, 0)) if x_local else row_g
    kern, xspecs, xargs, aliases = _chain(_outproj_norm_kernel, 4, prev, 0)
    return pl.pallas_call(
        kern,
        grid=(nrows // tm,),
        in_specs=[row_g, _wo_spec(), row_x, _layer_spec((1, D_MODEL), l_next)] + xspecs,
        out_specs=[row_g, row_g],
        out_shape=[jax.ShapeDtypeStruct((R, D_MODEL), f32), jax.ShapeDtypeStruct((R, D_MODEL), bf16)],
        input_output_aliases=aliases,
        compiler_params=_cparams(("arbitrary",)),
        name="outproj_norm",
    )(merged, w_o_bf, x, norm_g3, *xargs)


def _outproj_final(merged, w_o_bf, x, g, row0, nrows):
    tm = _pick(nrows, (512, 256, 128))
    assert row0 % tm == 0
    off = row0 // tm
    row_in = pl.BlockSpec((tm, D_MODEL), lambda i: (i + off, 0))
    return pl.pallas_call(
        _outproj_final_kernel,
        grid=(nrows // tm,),
        in_specs=[row_in, _wo_spec(), row_in, pl.BlockSpec((1, D_MODEL), lambda i: (0, 0))],
        out_specs=pl.BlockSpec((tm, D_MODEL), lambda i: (i, 0)),
        out_shape=jax.ShapeDtypeStruct((nrows, D_MODEL), f32),
        compiler_params=_cparams(("arbitrary",)),
        name="outproj_final",
    )(merged, w_o_bf, x, g)


def kernel(x_prompt, x_sample, state_mlstm_C, state_mlstm_n, state_mlstm_m, state_mlstm_conv, state_ret_S,
           norm_g, w_in, conv_w, conv_b, w_qm, w_km, b_i, b_f, hn_m_g, skip_m, hn_r_g,
           w_pm, w_pr, w_gate, b_gate, w_o, final_g):
    B, T, D = x_prompt.shape
    NB = x_sample.shape[0]
    depth = w_in.shape[0]
    BT = B * T
    assert D == D_MODEL and x_sample.shape[1] == 1 and T % CHUNK == 0 and depth >= 2

    R = BT + NB
    xp = x_prompt.reshape(BT, D)
    xs = x_sample.reshape(NB, D)

    w_inT = jnp.swapaxes(w_in, 1, 2)
    wq = w_qm.astype(bf16)
    wk = w_km.astype(bf16)
    wkT = jnp.swapaxes(w_km, 2, 3).astype(bf16)
    bias = jnp.pad(jnp.concatenate([b_i, b_f], axis=1), ((0, 0), (0, LANES - 2 * N_HEADS)))
    brow = bias.reshape(depth, 1, LANES)
    bcol = bias[:, :8].reshape(depth, 8, 1)
    b_gate3 = b_gate.reshape(depth, 1, 2 * D)
    norm_g3 = norm_g.reshape(depth, 1, D)
    conv_b3 = conv_b.reshape(depth, 1, W_MIX)
    hngm = hn_m_g.reshape(depth, 1, W_MIX)
    skp = skip_m.reshape(depth, 1, W_MIX)
    hngr = hn_r_g.reshape(depth, 1, W_MIX)
    conv_st = jnp.transpose(state_mlstm_conv, (0, 2, 1, 3))
    m_st = jnp.swapaxes(state_mlstm_m, 1, 2)
    n_st = state_mlstm_n.reshape(depth, NB, W_MIX)

    cos, sin, cosT, sinT, cosd, sind, cosdT, sindT = _rope_tables(T)

    prev_m = prev_r = prev_d = None
    h = _rmsnorm_into(xp, norm_g3, 0, R, 0, None)
    h = _rmsnorm_into(xs, norm_g3, 0, R, BT, h)
    x = None
    for l in range(depth):
        P, KT, GI, w_o_bf = _inproj(h, w_inT, w_o, l)
        OM, *prev_m = _mlstm_prompt(P, GI, B, T, l, depth, prev_m, conv_w, conv_b3, wq, wkT,
                                    brow, bcol, hngm, skp)
        OR, *prev_r = _ret_prompt(P, KT, B, T, l, depth, prev_r, (cos, sin, cosT, sinT), hngr)
        OM, OR, *prev_d = _decode_step(
            P, KT, GI, OM, OR, prev_d, l, BT, conv_st, n_st, m_st, state_mlstm_C, state_ret_S,
            conv_w, conv_b3, wq, wk, brow, hngm, skp, hngr, (cosd, sind, cosdT, sindT))
        merged = _merge(h, OM, OR, w_gate, b_gate3, w_pm, w_pr, l)
        if l + 1 == depth:
            break
        if x is None:
            x, h = _outproj(merged, w_o_bf, xp, norm_g3, l + 1, 0, BT, True, None)
            x, h = _outproj(merged, w_o_bf, xs, norm_g3, l + 1, BT, NB, True, (x, h))
        else:
            x, h = _outproj(merged, w_o_bf, x, norm_g3, l + 1, 0, R, False, None)

    fg = final_g.reshape(1, D)
    y_prompt = _outproj_final(merged, w_o_bf, x, fg, 0, BT).reshape(B, T, D)
    y_sample = _outproj_final(merged, w_o_bf, x, fg, BT, NB).reshape(NB, 1, D)
    Cp, np_, mp, convp = prev_m
    (Sp,) = prev_r
    Cs, Ss, n_s, m_s, conv_s = prev_d
    return (y_prompt, y_sample,
            Cp, np_.reshape(depth, B, N_HEADS, D_HEAD), mp[:, :, :N_HEADS, 0], convp, Sp,
            Cs, n_s.reshape(depth, NB, N_HEADS, D_HEAD), jnp.swapaxes(m_s, 1, 2),
            jnp.transpose(conv_s, (0, 2, 1, 3)), Ss)
```

```python
import functools
import math

import jax
import jax.numpy as jnp
import numpy as np
from jax import lax
from jax.experimental import pallas as pl
from jax.experimental.pallas import tpu as pltpu

f32 = jnp.float32
bf16 = jnp.bfloat16

D_MODEL = 2048
N_HEADS = 4
D_HEAD = 256
W_MIX = N_HEADS * D_HEAD
CONV_W = 4
CHUNK = 128
ROPE_BASE = 10000.0
EPS = 1e-6
NEG = -1e30
PAST_LEN = 16384
LANES = 128
HALF = D_HEAD // 2
VMEM_LIMIT = 56 * 1024 * 1024

OFF_IF = 3 * W_MIX
OFF_RET = 3 * W_MIX + 2 * N_HEADS


def _pick(n, cands):
    for c in cands:
        if n % c == 0:
            return c
    return n


def _cparams(sem):
    return pltpu.CompilerParams(dimension_semantics=sem, vmem_limit_bytes=VMEM_LIMIT)


def _dot(a, b):
    return jnp.dot(a, b, preferred_element_type=f32)


def _dot_nt(a, b):
    return lax.dot_general(a, b, (((1,), (1,)), ((), ())), preferred_element_type=f32)


def _split3(x):
    hi = x.astype(bf16).astype(f32)
    r = x - hi
    mid = r.astype(bf16).astype(f32)
    lo = (r - mid).astype(bf16).astype(f32)
    return hi, mid, lo


def _silu(x):
    return x * (0.5 * jnp.tanh(0.5 * x) + 0.5)


def _log_sigmoid(x):
    return jnp.minimum(x, 0.0) - jnp.log1p(jnp.exp(-jnp.abs(x)))


def _head_norm(hc, g):
    mu = jnp.mean(hc, axis=-1, keepdims=True)
    d = hc - mu
    var = jnp.mean(d * d, axis=-1, keepdims=True)
    return d * lax.rsqrt(var + EPS) * g


def _rope_kernel(inv_ref, invT_ref, cos_ref, sin_ref, cosT_ref, sinT_ref,
                 cosd_ref, sind_ref, cosdT_ref, sindT_ref):
    T = cos_ref.shape[0]
    pos = lax.broadcasted_iota(jnp.int32, (T, HALF), 0).astype(f32)
    ang = pos * inv_ref[...]
    cos = jnp.cos(ang)
    sin = jnp.sin(ang)
    cos_ref[...] = cos
    sin_ref[...] = sin
    cosT_ref[...] = cos.T
    sinT_ref[...] = sin.T
    angd = jnp.full((8, HALF), float(PAST_LEN), f32) * inv_ref[...]
    cosd_ref[...] = jnp.cos(angd)
    sind_ref[...] = jnp.sin(angd)
    angdT = jnp.full((HALF, LANES), float(PAST_LEN), f32) * invT_ref[...]
    cosdT_ref[...] = jnp.cos(angdT)
    sindT_ref[...] = jnp.sin(angdT)


def _rope_tables(T):
    inv = ROPE_BASE ** (-jnp.arange(0, HALF, dtype=f32) / HALF)
    shapes = [(T, HALF), (T, HALF), (HALF, T), (HALF, T),
              (8, HALF), (8, HALF), (HALF, LANES), (HALF, LANES)]
    return pl.pallas_call(
        _rope_kernel,
        out_shape=[jax.ShapeDtypeStruct(s, f32) for s in shapes],
        name="rope_tables",
    )(inv.reshape(1, HALF), inv.reshape(HALF, 1))


def _rms_kernel(x_ref, g_ref, o_ref):
    x = x_ref[...]
    ms = jnp.mean(x * x, axis=-1, keepdims=True)
    o_ref[...] = (x * lax.rsqrt(ms + EPS) * g_ref[...]).astype(o_ref.dtype)


def _rmsnorm_into(x, norm_g3, l, R, row0, prev):
    n, D = x.shape
    tr = _pick(n, (512, 256, 128))
    assert row0 % tr == 0
    off = row0 // tr
    kern, xspecs, xargs, aliases = _chain(_rms_kernel, 2, None if prev is None else [prev], 0)
    return pl.pallas_call(
        kern,
        grid=(n // tr,),
        in_specs=[pl.BlockSpec((tr, D), lambda i: (i, 0)), _layer_spec((1, D), l)] + xspecs,
        out_specs=pl.BlockSpec((tr, D), lambda i: (i + off, 0)),
        out_shape=jax.ShapeDtypeStruct((R, D), bf16),
        input_output_aliases=aliases,
        compiler_params=_cparams(("arbitrary",)),
        name="rmsnorm",
    )(x, norm_g3, *xargs)


_TN_IN = 512
_TPG = W_MIX // _TN_IN
_NJ_P = 6 * _TPG
_NJ_K = _TPG
_GATE_PAD = 16


def _inproj_row_offset(j):
    jp = jnp.minimum(j, _NJ_P - 1)
    grp = jp // _TPG
    sub = jp - grp * _TPG
    src_grp = jnp.where(grp < 3, grp, jnp.where(grp == 3, 3, grp + 1))
    off_p = src_grp * W_MIX + jnp.where(grp < 3, 0, 2 * N_HEADS) + sub * _TN_IN
    off_k = OFF_RET + W_MIX + jnp.clip(j - _NJ_P, 0, _NJ_K - 1) * _TN_IN
    return jnp.where(j < _NJ_P, off_p, off_k)


def _inproj_kernel(h_ref, wt_ref, wif_ref, wo_ref, p_ref, kt_ref, gi_ref, wob_ref):
    j = pl.program_id(1)
    wob_ref[...] = wo_ref[...].astype(bf16)

    def weights():
        return wt_ref[0].astype(bf16)

    is_z = (j // _TPG == 2) | (j // _TPG == 5)
    last = _NJ_P + _NJ_K - 1

    @pl.when((j < _NJ_P) & jnp.logical_not(is_z))
    def _():
        p_ref[...] = _dot_nt(h_ref[...], weights()).astype(bf16)

    @pl.when((j < _NJ_P) & is_z)
    def _():
        p_ref[...] = _silu(_dot_nt(h_ref[...], weights())).astype(bf16)

    @pl.when((j >= _NJ_P) & (j < last))
    def _():
        kt_ref[...] = _dot_nt(weights(), h_ref[...]).astype(bf16)

    @pl.when(j == last)
    def _():
        pad = jnp.zeros((_GATE_PAD - 2 * N_HEADS, D_MODEL), f32)
        w_all = jnp.concatenate([wt_ref[0], wif_ref[...], pad], axis=0).astype(bf16)
        res = _dot_nt(w_all, h_ref[...])
        kt_ref[...] = res[0:_TN_IN, :].astype(bf16)
        gi_ref[...] = res[_TN_IN:_TN_IN + 2 * N_HEADS, :]


def _inproj(h, w_inT, w_o, l):
    R = h.shape[0]
    tm = _pick(R, (1664, 1280, 640, 384, 128))
    tn = _TN_IN
    nj = _NJ_P + _NJ_K
    nsteps = (R // tm) * nj
    rps = next(r for r in (32, 64, 128, 256, 512, 1024, 2048) if D_MODEL // r <= nsteps)
    wo_blk = lambda i, j: jnp.minimum(i * nj + j, D_MODEL // rps - 1)
    return pl.pallas_call(
        _inproj_kernel,
        grid=(R // tm, nj),
        in_specs=[
            pl.BlockSpec((tm, D_MODEL), lambda i, j: (i, 0)),
            pl.BlockSpec((pl.Element(1), pl.Element(tn), pl.Element(D_MODEL)),
                         lambda i, j: (l, pl.multiple_of(_inproj_row_offset(j), 8), 0)),
            pl.BlockSpec((None, 2 * N_HEADS, D_MODEL), lambda i, j: (l, OFF_IF // (2 * N_HEADS), 0)),
            pl.BlockSpec((None, rps, D_MODEL), lambda i, j: (l, wo_blk(i, j), 0)),
        ],
        out_specs=[
            pl.BlockSpec((tm, tn), lambda i, j: (i, jnp.minimum(j, _NJ_P - 1))),
            pl.BlockSpec((tn, tm), lambda i, j: (jnp.clip(j - _NJ_P, 0, _NJ_K - 1), i)),
            pl.BlockSpec((2 * N_HEADS, tm), lambda i, j: (0, i)),
            pl.BlockSpec((rps, D_MODEL), lambda i, j: (wo_blk(i, j), 0)),
        ],
        out_shape=[
            jax.ShapeDtypeStruct((R, 6 * W_MIX), bf16),
            jax.ShapeDtypeStruct((W_MIX, R), bf16),
            jax.ShapeDtypeStruct((2 * N_HEADS, R), f32),
            jax.ShapeDtypeStruct((D_MODEL, D_MODEL), bf16),
        ],
        compiler_params=_cparams(("arbitrary", "arbitrary")),
        name="inproj",
    )(h, w_inT, w_inT, w_o)


def _mlstm_kernel(um_ref, vm_ref, zs_ref, gi_ref, convw_ref, convb_ref, wq_ref, wkT_ref,
                  brow_ref, bcol_ref, hng_ref, skip_ref,
                  om_ref, c_out_ref, n_out_ref, m_out_ref, conv_out_ref,
                  tail_s, cta_s, m_s):
    g = pl.program_id(1)
    ng = pl.num_programs(1)
    Lg = um_ref.shape[0]
    L = CHUNK
    DA = D_HEAD + LANES
    scale = D_HEAD ** -0.5
    heads = range(N_HEADS)

    @pl.when(g == 0)
    def _():
        tail_s[...] = jnp.zeros(tail_s.shape, bf16)
        cta_s[...] = jnp.zeros(cta_s.shape, f32)
        m_s[...] = jnp.full(m_s.shape, NEG, f32)

    srow = lax.broadcasted_iota(jnp.int32, (L, 2 * L), 0)
    scol = lax.broadcasted_iota(jnp.int32, (L, 2 * L), 1)
    shift_mat = jnp.concatenate([(scol == srow + L - s).astype(bf16) for s in range(1, CONV_W)], axis=0)

    row = lax.broadcasted_iota(jnp.int32, (L, L), 0)
    col = lax.broadcasted_iota(jnp.int32, (L, L), 1)
    causal = row >= col
    tril = causal.astype(bf16)
    triu = (row <= col).astype(bf16)
    brow = brow_ref[...]
    bcol = bcol_ref[...]
    ones_b = jnp.ones((L, LANES), bf16)
    ones_f = jnp.ones((L, LANES), f32)

    for c in range(Lg // L):
        rows = slice(c * L, (c + 1) * L)
        g8 = gi_ref[:, rows]
        gtb = g8 + bcol
        gb = jnp.concatenate([g8, jnp.zeros((LANES - 8, L), f32)], axis=0).T + brow
        lf = _log_sigmoid(gb)
        lft = _log_sigmoid(gtb)
        bc3 = _dot(tril, jnp.concatenate(_split3(lf), axis=1).astype(bf16))
        bc_all = bc3[:, 0:LANES] + bc3[:, LANES:2 * LANES] + bc3[:, 2 * LANES:3 * LANES]
        br3 = _dot(jnp.concatenate(_split3(lft), axis=0).astype(bf16), triu)
        br_all = br3[0:8, :] + br3[8:16, :] + br3[16:24, :]

        u_cur = um_ref[rows, :]
        u_prev = tail_s[...] if c == 0 else um_ref[(c - 1) * L:c * L, :]
        delayed = _dot(shift_mat, jnp.concatenate([u_prev, u_cur], axis=0))
        acc = convb_ref[...] + u_cur.astype(f32) * convw_ref[CONV_W - 1:CONV_W, :]
        for s in range(1, CONV_W):
            acc = acc + delayed[(s - 1) * L:s * L, :] * convw_ref[CONV_W - 1 - s:CONV_W - s, :]
        cm = _silu(acc)

        cs = [slice(h * D_HEAD, (h + 1) * D_HEAD) for h in heads]
        ch = [cm[:, cs[h]] for h in heads]
        chb = [x.astype(bf16) for x in ch]
        qb = [(_dot(chb[h], wq_ref[h]) * scale).astype(bf16) for h in heads]
        kTb = [_dot_nt(wkT_ref[h], chb[h]).astype(bf16) for h in heads]
        v = [vm_ref[rows, cs[h]] for h in heads]
        bc = [bc_all[:, 4 + h:5 + h] for h in heads]
        ic = [gb[:, h:h + 1] for h in heads]
        m_prev = [m_s[h][0:1, 0:1] for h in heads]
        dlog = [jnp.where(causal, bc[h] - br_all[4 + h:5 + h, :] + gtb[h:h + 1, :], NEG) for h in heads]
        inter = [bc[h] + m_prev[h] for h in heads]
        m_t = [jnp.maximum(inter[h], jnp.max(dlog[h], axis=-1, keepdims=True)) for h in heads]
        a = [jnp.exp(inter[h] - m_t[h]) for h in heads]
        ws = [(jnp.exp(dlog[h] - m_t[h]) * _dot(qb[h], kTb[h])).astype(bf16) for h in heads]
        cta = [cta_s[h] for h in heads]
        nd = [_dot(ws[h], jnp.concatenate([v[h], ones_b], axis=1))
              + a[h] * _dot(qb[h], cta[h].astype(bf16)) for h in heads]
        m_new = [m_t[h][L - 1:L, :] for h in heads]
        b_last = [bc[h][L - 1:L, :] for h in heads]
        for h in heads:
            wk_c = jnp.exp(b_last[h] - bc[h] + ic[h] - m_new[h])
            dec = jnp.exp(b_last[h] + m_prev[h] - m_new[h])
            vaug = jnp.concatenate([v[h].astype(f32), ones_f], axis=1)
            cta_s[h] = dec * cta[h] + _dot(kTb[h], (wk_c * vaug).astype(bf16))
            m_s[h] = jnp.broadcast_to(m_new[h], (8, LANES))
        for h in heads:
            den = jnp.maximum(jnp.abs(nd[h][:, D_HEAD:DA]), jnp.exp(-m_t[h]))
            rden = 1.0 / den
            hc = jnp.concatenate([nd[h][:, 0:HALF] * rden, nd[h][:, HALF:D_HEAD] * rden], axis=1)
            hn = _head_norm(hc, hng_ref[:, cs[h]])
            zs = zs_ref[rows, cs[h]].astype(f32)
            om_ref[rows, cs[h]] = ((hn + skip_ref[:, cs[h]] * ch[h]) * zs).astype(bf16)

    tail_s[...] = um_ref[Lg - L:Lg, :]

    @pl.when(g == ng - 1)
    def _():
        for h in heads:
            st = cta_s[h]
            c_out_ref[h] = st[:, 0:D_HEAD].T
            n_out_ref[h] = st[:, D_HEAD:DA].T[0:1, :]
        m_out_ref[...] = jnp.concatenate([m_s[h][0:1, :] for h in heads]
                                         + [jnp.zeros((8 - N_HEADS, LANES), f32)], axis=0)
        conv_out_ref[...] = um_ref[Lg - 8:Lg, :].astype(f32)[8 - (CONV_W - 1):8, :]


def _drop_refs(kernel_fn, n_in, n_drop, *refs):
    return kernel_fn(*refs[:n_in], *refs[n_in + n_drop:])


def _chain(kernel_fn, n_in, prev, first_out):
    if prev is None:
        return kernel_fn, [], [], {}
    specs = [pl.BlockSpec(memory_space=pl.ANY)] * len(prev)
    aliases = {n_in + k: first_out + k for k in range(len(prev))}
    return functools.partial(_drop_refs, kernel_fn, n_in, len(prev)), specs, list(prev), aliases


def _layer_spec(shape, l):
    nd = len(shape)
    return pl.BlockSpec((None,) + tuple(shape), lambda *_: (l,) + (0,) * nd)


def _mlstm_prompt(P, GI, B, T, l, depth, prev, conv_w, conv_b, wq, wkT, brow, bcol, hng, skip):
    R = P.shape[0]
    Lg = _pick(T, (512, 256, 128))
    ng = T // Lg
    col = lambda k: (lambda b, g: (b * ng + g, k))
    in_specs = [
        pl.BlockSpec((Lg, W_MIX), col(0)),
        pl.BlockSpec((Lg, W_MIX), col(1)),
        pl.BlockSpec((Lg, W_MIX), col(2)),
        pl.BlockSpec((2 * N_HEADS, Lg), lambda b, g: (0, b * ng + g)),
        _layer_spec((CONV_W, W_MIX), l),
        _layer_spec((1, W_MIX), l),
        _layer_spec((N_HEADS, D_HEAD, D_HEAD), l),
        _layer_spec((N_HEADS, D_HEAD, D_HEAD), l),
        _layer_spec((1, LANES), l),
        _layer_spec((8, 1), l),
        _layer_spec((1, W_MIX), l),
        _layer_spec((1, W_MIX), l),
    ]
    args = [P, P, P, GI, conv_w, conv_b, wq, wkT, brow, bcol, hng, skip]
    kern, xspecs, xargs, aliases = _chain(_mlstm_kernel, len(args), prev, 1)
    return pl.pallas_call(
        kern,
        grid=(B, ng),
        in_specs=in_specs + xspecs,
        out_specs=[
            pl.BlockSpec((Lg, W_MIX), col(0)),
            pl.BlockSpec((None, None, N_HEADS, D_HEAD, D_HEAD), lambda b, g: (l, b, 0, 0, 0)),
            pl.BlockSpec((None, None, N_HEADS, 1, D_HEAD), lambda b, g: (l, b, 0, 0, 0)),
            pl.BlockSpec((None, None, 8, LANES), lambda b, g: (l, b, 0, 0)),
            pl.BlockSpec((None, None, CONV_W - 1, W_MIX), lambda b, g: (l, b, 0, 0)),
        ],
        out_shape=[
            jax.ShapeDtypeStruct((R, W_MIX), bf16),
            jax.ShapeDtypeStruct((depth, B, N_HEADS, D_HEAD, D_HEAD), f32),
            jax.ShapeDtypeStruct((depth, B, N_HEADS, 1, D_HEAD), f32),
            jax.ShapeDtypeStruct((depth, B, 8, LANES), f32),
            jax.ShapeDtypeStruct((depth, B, CONV_W - 1, W_MIX), f32),
        ],
        scratch_shapes=[
            pltpu.VMEM((CHUNK, W_MIX), bf16),
            pltpu.VMEM((N_HEADS, D_HEAD, D_HEAD + LANES), f32),
            pltpu.VMEM((N_HEADS, 8, LANES), f32),
        ],
        input_output_aliases=aliases,
        compiler_params=_cparams(("arbitrary", "arbitrary")),
        name="mlstm_prompt",
    )(*args, *xargs)


def _log_gammas():
    return [math.log(1.0 - 2.0 ** (-5.0 - h)) for h in range(N_HEADS)]


def _ret_kernel(q_ref, v_ref, zs_ref, kt_ref, cos_ref, sin_ref, cosT_ref, sinT_ref, hng_ref,
                o_ref, s_out_ref, s_s):
    g = pl.program_id(1)
    ng = pl.num_programs(1)
    Lg = q_ref.shape[0]
    L = CHUNK

    @pl.when(g == 0)
    def _():
        s_s[...] = jnp.zeros(s_s.shape, f32)

    row = lax.broadcasted_iota(jnp.int32, (L, L), 0)
    col = lax.broadcasted_iota(jnp.int32, (L, L), 1)
    causal = row >= col
    diff = (row - col).astype(f32)
    tcol = lax.broadcasted_iota(jnp.int32, (L, 1), 0).astype(f32)
    lgs = _log_gammas()
    intra = [jnp.where(causal, jnp.exp(diff * lg), 0.0) for lg in lgs]
    q_dec = [jnp.exp((tcol + 1.0) * lg) for lg in lgs]
    k_dec = [jnp.exp((L - 1.0 - tcol) * lg) for lg in lgs]
    s_dec = [math.exp(L * lg) for lg in lgs]
    kscale = D_HEAD ** -0.5

    for c in range(Lg // L):
        rows = slice(c * L, (c + 1) * L)
        cos = cos_ref[rows, :]
        sin = sin_ref[rows, :]
        cosT = cosT_ref[:, rows] * kscale
        sinT = sinT_ref[:, rows] * kscale
        heads = range(N_HEADS)
        cs = [slice(h * D_HEAD, (h + 1) * D_HEAD) for h in heads]
        lo = [slice(h * D_HEAD, h * D_HEAD + HALF) for h in heads]
        hi = [slice(h * D_HEAD + HALF, (h + 1) * D_HEAD) for h in heads]
        q1 = [q_ref[rows, lo[h]].astype(f32) for h in heads]
        q2 = [q_ref[rows, hi[h]].astype(f32) for h in heads]
        qb = [jnp.concatenate([q1[h] * cos - q2[h] * sin, q1[h] * sin + q2[h] * cos], axis=-1).astype(bf16)
              for h in heads]
        k1 = [kt_ref[lo[h], rows].astype(f32) for h in heads]
        k2 = [kt_ref[hi[h], rows].astype(f32) for h in heads]
        kTb = [jnp.concatenate([k1[h] * cosT - k2[h] * sinT, k1[h] * sinT + k2[h] * cosT],
                               axis=0).astype(bf16) for h in heads]
        v = [v_ref[rows, cs[h]] for h in heads]
        sc = [(_dot(qb[h], kTb[h]) * intra[h]).astype(bf16) for h in heads]
        s = [s_s[h] for h in heads]
        o = [_dot(sc[h], v[h]) + q_dec[h] * _dot(qb[h], s[h].astype(bf16)) for h in heads]
        for h in heads:
            s_s[h] = s_dec[h] * s[h] + _dot(kTb[h], (v[h].astype(f32) * k_dec[h]).astype(bf16))
        for h in heads:
            zs = zs_ref[rows, cs[h]].astype(f32)
            o_ref[rows, cs[h]] = (_head_norm(o[h], hng_ref[:, cs[h]]) * zs).astype(bf16)

    @pl.when(g == ng - 1)
    def _():
        s_out_ref[...] = s_s[...]


def _ret_prompt(P, KT, B, T, l, depth, prev, tabs, hng):
    R = P.shape[0]
    Lg = _pick(T, (1024, 512, 256, 128))
    ng = T // Lg
    col = lambda k: (lambda b, g: (b * ng + g, k))
    cos, sin, cosT, sinT = tabs
    in_specs = [
        pl.BlockSpec((Lg, W_MIX), col(3)),
        pl.BlockSpec((Lg, W_MIX), col(4)),
        pl.BlockSpec((Lg, W_MIX), col(5)),
        pl.BlockSpec((W_MIX, Lg), lambda b, g: (0, b * ng + g)),
        pl.BlockSpec((Lg, HALF), lambda b, g: (g, 0)),
        pl.BlockSpec((Lg, HALF), lambda b, g: (g, 0)),
        pl.BlockSpec((HALF, Lg), lambda b, g: (0, g)),
        pl.BlockSpec((HALF, Lg), lambda b, g: (0, g)),
        _layer_spec((1, W_MIX), l),
    ]
    args = [P, P, P, KT, cos, sin, cosT, sinT, hng]
    kern, xspecs, xargs, aliases = _chain(_ret_kernel, len(args), prev, 1)
    return pl.pallas_call(
        kern,
        grid=(B, ng),
        in_specs=in_specs + xspecs,
        out_specs=[
            pl.BlockSpec((Lg, W_MIX), col(0)),
            pl.BlockSpec((None, None, N_HEADS, D_HEAD, D_HEAD), lambda b, g: (l, b, 0, 0, 0)),
        ],
        out_shape=[
            jax.ShapeDtypeStruct((R, W_MIX), bf16),
            jax.ShapeDtypeStruct((depth, B, N_HEADS, D_HEAD, D_HEAD), f32),
        ],
        scratch_shapes=[pltpu.VMEM((N_HEADS, D_HEAD, D_HEAD), f32)],
        input_output_aliases=aliases,
        compiler_params=_cparams(("arbitrary", "arbitrary")),
        name="ret_prompt",
    )(*args, *xargs)


_SEQ_PER_STEP = 4


def _decode_kernel(um_ref, vm_ref, zsm_ref, qr_ref, vr_ref, zsr_ref, kt_ref, gi_ref,
                   conv_ref, n_ref, m_ref, c_ref, s_ref,
                   convw_ref, convb_ref, wq_ref, wk_ref, brow_ref, hngm_ref, skip_ref, hngr_ref,
                   cosd_ref, sind_ref, cosdT_ref, sindT_ref,
                   om_ref, or_ref, c_out_ref, s_out_ref, n_out_ref, m_out_ref, conv_out_ref,
                   vt_s, qrt_s, krt_s, qm_s, km_s, vr_s, sct_s, cqt_s, oacc_s, cm_s):
    s = pl.program_id(0)
    ns = pl.num_programs(0)
    NB = um_ref.shape[0]
    scale = D_HEAD ** -0.5
    lgs = _log_gammas()
    gammas = [math.exp(lg) for lg in lgs]
    lane = lax.broadcasted_iota(jnp.int32, (1, LANES), 1)

    @pl.when(s == 0)
    def _():
        u = um_ref[...].astype(f32)
        acc = convb_ref[...] + u * convw_ref[CONV_W - 1:CONV_W, :]
        for i in range(CONV_W - 1):
            acc = acc + conv_ref[i] * convw_ref[i:i + 1, :]
        cm = _silu(acc)
        cm_s[...] = cm
        for i in range(CONV_W - 2):
            conv_out_ref[i] = conv_ref[i + 1]
        conv_out_ref[CONV_W - 2] = u

        gb = jnp.concatenate([gi_ref[...], jnp.zeros((LANES - 8, NB), f32)], axis=0).T + brow_ref[...]
        lf = _log_sigmoid(gb)
        m_cols = jnp.concatenate([m_ref[...], jnp.zeros((8 - N_HEADS, NB), f32)], axis=0).T
        sc = jnp.zeros((NB, LANES), f32)
        mo = jnp.zeros((NB, LANES), f32)
        for h in range(N_HEADS):
            cs = slice(h * D_HEAD, (h + 1) * D_HEAD)
            chb = cm[:, cs].astype(bf16)
            q = _dot(chb, wq_ref[h]) * scale
            k = _dot(chb, wk_ref[h])
            qm_s[:, cs] = q
            km_s[:, cs] = k
            ig = gb[:, h:h + 1]
            lfh = lf[:, 4 + h:5 + h]
            mp = m_cols[:, h:h + 1]
            inter = lfh + mp
            m_t = jnp.maximum(inter, ig)
            w = jnp.exp(ig - m_t)
            a = jnp.exp(inter - m_t)
            nh = n_ref[:, cs]
            ws = w * jnp.sum(q * k, axis=-1, keepdims=True)
            den = ws + a * jnp.sum(nh * q, axis=-1, keepdims=True)
            den = jnp.maximum(jnp.abs(den), jnp.exp(-m_t))
            n_out_ref[:, cs] = a * nh + w * k
            for r, val in ((h, ws), (4 + h, a), (8 + h, den), (12 + h, w)):
                sc = sc + val * (lane == r).astype(f32)
            mo = mo + m_t * (lane == h).astype(f32)
        m_out_ref[...] = mo.T[0:N_HEADS, :]
        sct_s[...] = sc.T
        vt_s[...] = vm_ref[...].astype(f32).T
        vr_s[...] = vr_ref[...].astype(f32)
        cosd = cosd_ref[0:1, :]
        sind = sind_ref[0:1, :]
        cosdT = cosdT_ref[...]
        sindT = sindT_ref[...]
        for h in range(N_HEADS):
            lo = slice(h * D_HEAD, h * D_HEAD + HALF)
            hi = slice(h * D_HEAD + HALF, (h + 1) * D_HEAD)
            q1 = qr_ref[:, lo].astype(f32)
            q2 = qr_ref[:, hi].astype(f32)
            qrot = jnp.concatenate([q1 * cosd - q2 * sind, q1 * sind + q2 * cosd], axis=-1)
            qrt_s[h * D_HEAD:(h + 1) * D_HEAD, :] = qrot.T
            k1 = kt_ref[lo, :].astype(f32)
            k2 = kt_ref[hi, :].astype(f32)
            krt_s[lo, :] = (k1 * cosdT - k2 * sindT) * scale
            krt_s[hi, :] = (k1 * sindT + k2 * cosdT) * scale
        cqt_s[...] = jnp.zeros(cqt_s.shape, f32)

    def take_col(mat, mask):
        return jnp.sum(jnp.where(mask, mat, 0.0), axis=-1, keepdims=True)

    for j in range(_SEQ_PER_STEP):
        b = s * _SEQ_PER_STEP + j
        mask = lane == b
        for h in range(N_HEADS):
            cs = slice(h * D_HEAD, (h + 1) * D_HEAD)
            c = c_ref[j, h]
            q_row = qm_s[pl.ds(b, 1), cs]
            k_row = km_s[pl.ds(b, 1), cs]
            v_col = take_col(vt_s[cs, :], mask)
            a = take_col(sct_s[4 + h:5 + h, :], mask)
            w = take_col(sct_s[12 + h:13 + h, :], mask)
            cq = jnp.sum(c * q_row, axis=-1, keepdims=True)
            cqt_s[cs, :] = jnp.where(mask, cq, cqt_s[cs, :])
            c_out_ref[j, h] = a * c + (w * v_col) * k_row
            st = s_ref[j, h]
            q_col = take_col(qrt_s[cs, :], mask)
            k_col = take_col(krt_s[cs, :], mask)
            v_row = vr_s[pl.ds(b, 1), cs]
            qk = jnp.sum(q_col * k_col, axis=0, keepdims=True)
            o_row = qk * v_row + gammas[h] * jnp.sum(st * q_col, axis=0, keepdims=True)
            oacc_s[pl.ds(b, 1), cs] = o_row
            s_out_ref[j, h] = gammas[h] * st + k_col * v_row

    @pl.when(s == ns - 1)
    def _():
        for h in range(N_HEADS):
            cs = slice(h * D_HEAD, (h + 1) * D_HEAD)
            ht = ((sct_s[h:h + 1, :] * vt_s[cs, :] + sct_s[4 + h:5 + h, :] * cqt_s[cs, :])
                  / sct_s[8 + h:9 + h, :])
            hn = _head_norm(ht.T, hngm_ref[:, cs])
            zsm = zsm_ref[:, cs].astype(f32)
            om_ref[:, cs] = ((hn + skip_ref[:, cs] * cm_s[:, cs]) * zsm).astype(bf16)
            zsr = zsr_ref[:, cs].astype(f32)
            or_ref[:, cs] = (_head_norm(oacc_s[:, cs], hngr_ref[:, cs]) * zsr).astype(bf16)


def _decode_step(P, KT, GI, OM, OR, prev, l, BT, conv_st, n_st, m_st, C_st, S_st,
                 conv_w, conv_b, wq, wk, brow, hngm, skip, hngr, dtabs):
    R = P.shape[0]
    NB = R - BT
    assert NB == LANES and BT % NB == 0 and NB % _SEQ_PER_STEP == 0
    rb = BT // NB
    depth = C_st.shape[0]
    bb = _SEQ_PER_STEP
    cosd, sind, cosdT, sindT = dtabs
    pcol = lambda k: (lambda s: (rb, k))
    rep2 = lambda s: (0, 0)
    st_spec = pl.BlockSpec((None, bb, N_HEADS, D_HEAD, D_HEAD), lambda s: (l, s, 0, 0, 0))
    conv_spec = _layer_spec((CONV_W - 1, NB, W_MIX), l)
    n_spec = _layer_spec((NB, W_MIX), l)
    m_spec = _layer_spec((N_HEADS, NB), l)
    in_specs = [
        pl.BlockSpec((NB, W_MIX), pcol(0)), pl.BlockSpec((NB, W_MIX), pcol(1)),
        pl.BlockSpec((NB, W_MIX), pcol(2)), pl.BlockSpec((NB, W_MIX), pcol(3)),
        pl.BlockSpec((NB, W_MIX), pcol(4)), pl.BlockSpec((NB, W_MIX), pcol(5)),
        pl.BlockSpec((W_MIX, NB), lambda s: (0, rb)),
        pl.BlockSpec((2 * N_HEADS, NB), lambda s: (0, rb)),
        conv_spec, n_spec, m_spec, st_spec, st_spec,
        _layer_spec((CONV_W, W_MIX), l), _layer_spec((1, W_MIX), l),
        _layer_spec((N_HEADS, D_HEAD, D_HEAD), l), _layer_spec((N_HEADS, D_HEAD, D_HEAD), l),
        _layer_spec((1, LANES), l),
        _layer_spec((1, W_MIX), l), _layer_spec((1, W_MIX), l), _layer_spec((1, W_MIX), l),
        pl.BlockSpec((8, HALF), rep2), pl.BlockSpec((8, HALF), rep2),
        pl.BlockSpec((HALF, LANES), rep2), pl.BlockSpec((HALF, LANES), rep2),
    ]
    args = [P, P, P, P, P, P, KT, GI, conv_st, n_st, m_st, C_st, S_st,
            conv_w, conv_b, wq, wk, brow, hngm, skip, hngr, cosd, sind, cosdT, sindT]
    carried = [OM, OR] + (list(prev) if prev is not None else [])
    kern, xspecs, xargs, aliases = _chain(_decode_kernel, len(args), carried, 0)
    out_row = pl.BlockSpec((NB, W_MIX), lambda s: (rb, 0))
    return pl.pallas_call(
        kern,
        grid=(NB // bb,),
        in_specs=in_specs + xspecs,
        out_specs=[out_row, out_row, st_spec, st_spec, n_spec, m_spec, conv_spec],
        out_shape=[
            jax.ShapeDtypeStruct(OM.shape, bf16),
            jax.ShapeDtypeStruct(OR.shape, bf16),
            jax.ShapeDtypeStruct((depth, NB, N_HEADS, D_HEAD, D_HEAD), f32),
            jax.ShapeDtypeStruct((depth, NB, N_HEADS, D_HEAD, D_HEAD), f32),
            jax.ShapeDtypeStruct((depth, NB, W_MIX), f32),
            jax.ShapeDtypeStruct((depth, N_HEADS, NB), f32),
            jax.ShapeDtypeStruct((depth, CONV_W - 1, NB, W_MIX), f32),
        ],
        scratch_shapes=[
            pltpu.VMEM((W_MIX, NB), f32),
            pltpu.VMEM((W_MIX, NB), f32),
            pltpu.VMEM((W_MIX, NB), f32),
            pltpu.VMEM((NB, W_MIX), f32),
            pltpu.VMEM((NB, W_MIX), f32),
            pltpu.VMEM((NB, W_MIX), f32),
            pltpu.VMEM((LANES, NB), f32),
            pltpu.VMEM((W_MIX, NB), f32),
            pltpu.VMEM((NB, W_MIX), f32),
            pltpu.VMEM((NB, W_MIX), f32),
        ],
        input_output_aliases=aliases,
        compiler_params=_cparams(("arbitrary",)),
        name="decode_step",
    )(*args, *xargs)


def _merge_kernel(h_ref, om_ref, or_ref, wgm_ref, wgr_ref, bgm_ref, bgr_ref, wpm_ref, wpr_ref, o_ref):
    h = h_ref[...]
    gm = jax.nn.sigmoid(_dot(h, wgm_ref[...].astype(bf16)) + bgm_ref[...])
    gr = jax.nn.sigmoid(_dot(h, wgr_ref[...].astype(bf16)) + bgr_ref[...])
    pm = _dot(om_ref[...], wpm_ref[...].astype(bf16))
    pr = _dot(or_ref[...], wpr_ref[...].astype(bf16))
    o_ref[...] = (gm * pm + gr * pr).astype(bf16)


def _merge(h, OM, OR, w_gate, b_gate3, w_pm, w_pr, l):
    R = h.shape[0]
    tm = _pick(R, (1664, 1280, 640, 384, 128))
    tn = 256
    nj = D_MODEL // tn
    return pl.pallas_call(
        _merge_kernel,
        grid=(R // tm, nj),
        in_specs=[
            pl.BlockSpec((tm, D_MODEL), lambda i, j: (i, 0)),
            pl.BlockSpec((tm, W_MIX), lambda i, j: (i, 0)),
            pl.BlockSpec((tm, W_MIX), lambda i, j: (i, 0)),
            pl.BlockSpec((None, D_MODEL, tn), lambda i, j: (l, 0, j)),
            pl.BlockSpec((None, D_MODEL, tn), lambda i, j: (l, 0, j + nj)),
            pl.BlockSpec((None, 1, tn), lambda i, j: (l, 0, j)),
            pl.BlockSpec((None, 1, tn), lambda i, j: (l, 0, j + nj)),
            pl.BlockSpec((None, W_MIX, tn), lambda i, j: (l, 0, j)),
            pl.BlockSpec((None, W_MIX, tn), lambda i, j: (l, 0, j)),
        ],
        out_specs=pl.BlockSpec((tm, tn), lambda i, j: (i, j)),
        out_shape=jax.ShapeDtypeStruct((R, D_MODEL), bf16),
        compiler_params=_cparams(("arbitrary", "arbitrary")),
        name="merge",
    )(h, OM, OR, w_gate, w_gate, b_gate3, b_gate3, w_pm, w_pr)


def _row_halves(tm):
    return (slice(0, tm // 2), slice(tm // 2, tm))


def _outproj_norm_kernel(mg_ref, wo_ref, x_ref, g_ref, o_ref, h_ref):
    for r in _row_halves(x_ref.shape[0]):
        y = x_ref[r, :] + _dot(mg_ref[r, :], wo_ref[...])
        o_ref[r, :] = y
        ms = jnp.mean(y * y, axis=-1, keepdims=True)
        h_ref[r, :] = (y * lax.rsqrt(ms + EPS) * g_ref[...]).astype(bf16)


def _outproj_final_kernel(mg_ref, wo_ref, x_ref, g_ref, y_ref):
    for r in _row_halves(x_ref.shape[0]):
        y = x_ref[r, :] + _dot(mg_ref[r, :], wo_ref[...])
        ms = jnp.mean(y * y, axis=-1, keepdims=True)
        y_ref[r, :] = y * lax.rsqrt(ms + EPS) * g_ref[...]


def _wo_spec():
    return pl.BlockSpec((D_MODEL, D_MODEL), lambda i: (0, 0), pipeline_mode=pl.Buffered(1))


def _outproj(merged, w_o_bf, x, norm_g3, l_next, row0, nrows, x_local, prev):
    R = merged.shape[0]
    tm = _pick(nrows, (640, 512, 384, 128))
    assert row0 % tm == 0
    off = row0 // tm
    row_g = pl.BlockSpec((tm, D_MODEL), lambda i: (i + off, 0))
    row_x = pl.BlockSpec((tm, D_MODEL), lambda i: (i, 0)) if x_local else row_g
    kern, xspecs, xargs, aliases = _chain(_outproj_norm_kernel, 4, prev, 0)
    return pl.pallas_call(
        kern,
        grid=(nrows // tm,),
        in_specs=[row_g, _wo_spec(), row_x, _layer_spec((1, D_MODEL), l_next)] + xspecs,
        out_specs=[row_g, row_g],
        out_shape=[jax.ShapeDtypeStruct((R, D_MODEL), f32), jax.ShapeDtypeStruct((R, D_MODEL), bf16)],
        input_output_aliases=aliases,
        compiler_params=_cparams(("arbitrary",)),
        name="outproj_norm",
    )(merged, w_o_bf, x, norm_g3, *xargs)


def _outproj_final(merged, w_o_bf, x, g, row0, nrows):
    tm = _pick(nrows, (512, 256, 128))
    assert row0 % tm == 0
    off = row0 // tm
    row_in = pl.BlockSpec((tm, D_MODEL), lambda i: (i + off, 0))
    return pl.pallas_call(
        _outproj_final_kernel,
        grid=(nrows // tm,),
        in_specs=[row_in, _wo_spec(), row_in, pl.BlockSpec((1, D_MODEL), lambda i: (0, 0))],
        out_specs=pl.BlockSpec((tm, D_MODEL), lambda i: (i, 0)),
        out_shape=jax.ShapeDtypeStruct((nrows, D_MODEL), f32),
        compiler_params=_cparams(("arbitrary",)),
        name="outproj_final",
    )(merged, w_o_bf, x, g)


def kernel(x_prompt, x_sample, state_mlstm_C, state_mlstm_n, state_mlstm_m, state_mlstm_conv, state_ret_S,
           norm_g, w_in, conv_w, conv_b, w_qm, w_km, b_i, b_f, hn_m_g, skip_m, hn_r_g,
           w_pm, w_pr, w_gate, b_gate, w_o, final_g):
    B, T, D = x_prompt.shape
    NB = x_sample.shape[0]
    depth = w_in.shape[0]
    BT = B * T
    assert D == D_MODEL and x_sample.shape[1] == 1 and T % CHUNK == 0 and depth >= 2

    R = BT + NB
    xp = x_prompt.reshape(BT, D)
    xs = x_sample.reshape(NB, D)

    w_inT = jnp.swapaxes(w_in, 1, 2)
    wq = w_qm.astype(bf16)
    wk = w_km.astype(bf16)
    wkT = jnp.swapaxes(w_km, 2, 3).astype(bf16)
    bias = jnp.pad(jnp.concatenate([b_i, b_f], axis=1), ((0, 0), (0, LANES - 2 * N_HEADS)))
    brow = bias.reshape(depth, 1, LANES)
    bcol = bias[:, :8].reshape(depth, 8, 1)
    b_gate3 = b_gate.reshape(depth, 1, 2 * D)
    norm_g3 = norm_g.reshape(depth, 1, D)
    conv_b3 = conv_b.reshape(depth, 1, W_MIX)
    hngm = hn_m_g.reshape(depth, 1, W_MIX)
    skp = skip_m.reshape(depth, 1, W_MIX)
    hngr = hn_r_g.reshape(depth, 1, W_MIX)
    conv_st = jnp.transpose(state_mlstm_conv, (0, 2, 1, 3))
    m_st = jnp.swapaxes(state_mlstm_m, 1, 2)
    n_st = state_mlstm_n.reshape(depth, NB, W_MIX)

    cos, sin, cosT, sinT, cosd, sind, cosdT, sindT = _rope_tables(T)

    prev_m = prev_r = prev_d = None
    h = _rmsnorm_into(xp, norm_g3, 0, R, 0, None)
    h = _rmsnorm_into(xs, norm_g3, 0, R, BT, h)
    x = None
    for l in range(depth):
        P, KT, GI, w_o_bf = _inproj(h, w_inT, w_o, l)
        OM, *prev_m = _mlstm_prompt(P, GI, B, T, l, depth, prev_m, conv_w, conv_b3, wq, wkT,
                                    brow, bcol, hngm, skp)
        OR, *prev_r = _ret_prompt(P, KT, B, T, l, depth, prev_r, (cos, sin, cosT, sinT), hngr)
        OM, OR, *prev_d = _decode_step(
            P, KT, GI, OM, OR, prev_d, l, BT, conv_st, n_st, m_st, state_mlstm_C, state_ret_S,
            conv_w, conv_b3, wq, wk, brow, hngm, skp, hngr, (cosd, sind, cosdT, sindT))
        merged = _merge(h, OM, OR, w_gate, b_gate3, w_pm, w_pr, l)
        if l + 1 == depth:
            break
        if x is None:
            x, h = _outproj(merged, w_o_bf, xp, norm_g3, l + 1, 0, BT, True, None)
            x, h = _outproj(merged, w_o_bf, xs, norm_g3, l + 1, BT, NB, True, (x, h))
        else:
            x, h = _outproj(merged, w_o_bf, x, norm_g3, l + 1, 0, R, False, None)

    fg = final_g.reshape(1, D)
    y_prompt = _outproj_final(merged, w_o_bf, x, fg, 0, BT).reshape(B, T, D)
    y_sample = _outproj_final(merged, w_o_bf, x, fg, BT, NB).reshape(NB, 1, D)
    Cp, np_, mp, convp = prev_m
    (Sp,) = prev_r
    Cs, Ss, n_s, m_s, conv_s = prev_d
    return (y_prompt, y_sample,
            Cp, np_.reshape(depth, B, N_HEADS, D_HEAD), mp[:, :, :N_HEADS, 0], convp, Sp,
            Cs, n_s.reshape(depth, NB, N_HEADS, D_HEAD), jnp.swapaxes(m_s, 1, 2),
            jnp.transpose(conv_s, (0, 2, 1, 3)), Ss)
```

```python
import functools
import math

import jax
import jax.numpy as jnp
import numpy as np
from jax import lax
from jax.experimental import pallas as pl
from jax.experimental.pallas import tpu as pltpu

f32 = jnp.float32
bf16 = jnp.bfloat16

D_MODEL = 2048
N_HEADS = 4
D_HEAD = 256
W_MIX = N_HEADS * D_HEAD
CONV_W = 4
CHUNK = 128
ROPE_BASE = 10000.0
EPS = 1e-6
NEG = -1e30
PAST_LEN = 16384
LANES = 128
HALF = D_HEAD // 2
VMEM_LIMIT = 56 * 1024 * 1024

OFF_IF = 3 * W_MIX
OFF_RET = 3 * W_MIX + 2 * N_HEADS


def _pick(n, cands):
    for c in cands:
        if n % c == 0:
            return c
    return n


def _cparams(sem):
    return pltpu.CompilerParams(dimension_semantics=sem, vmem_limit_bytes=VMEM_LIMIT)


def _dot(a, b):
    return jnp.dot(a, b, preferred_element_type=f32)


def _dot_nt(a, b):
    return lax.dot_general(a, b, (((1,), (1,)), ((), ())), preferred_element_type=f32)


def _split3(x):
    hi = x.astype(bf16).astype(f32)
    r = x - hi
    mid = r.astype(bf16).astype(f32)
    lo = (r - mid).astype(bf16).astype(f32)
    return hi, mid, lo


def _silu(x):
    return x * (0.5 * jnp.tanh(0.5 * x) + 0.5)


def _log_sigmoid(x):
    return jnp.minimum(x, 0.0) - jnp.log1p(jnp.exp(-jnp.abs(x)))


def _head_norm(hc, g):
    mu = jnp.mean(hc, axis=-1, keepdims=True)
    d = hc - mu
    var = jnp.mean(d * d, axis=-1, keepdims=True)
    return d * lax.rsqrt(var + EPS) * g


def _rope_kernel(inv_ref, invT_ref, cos_ref, sin_ref, cosT_ref, sinT_ref,
                 cosd_ref, sind_ref, cosdT_ref, sindT_ref):
    T = cos_ref.shape[0]
    pos = lax.broadcasted_iota(jnp.int32, (T, HALF), 0).astype(f32)
    ang = pos * inv_ref[...]
    cos = jnp.cos(ang)
    sin = jnp.sin(ang)
    cos_ref[...] = cos
    sin_ref[...] = sin
    cosT_ref[...] = cos.T
    sinT_ref[...] = sin.T
    angd = jnp.full((8, HALF), float(PAST_LEN), f32) * inv_ref[...]
    cosd_ref[...] = jnp.cos(angd)
    sind_ref[...] = jnp.sin(angd)
    angdT = jnp.full((HALF, LANES), float(PAST_LEN), f32) * invT_ref[...]
    cosdT_ref[...] = jnp.cos(angdT)
    sindT_ref[...] = jnp.sin(angdT)


def _rope_tables(T):
    inv = ROPE_BASE ** (-jnp.arange(0, HALF, dtype=f32) / HALF)
    shapes = [(T, HALF), (T, HALF), (HALF, T), (HALF, T),
              (8, HALF), (8, HALF), (HALF, LANES), (HALF, LANES)]
    return pl.pallas_call(
        _rope_kernel,
        out_shape=[jax.ShapeDtypeStruct(s, f32) for s in shapes],
        name="rope_tables",
    )(inv.reshape(1, HALF), inv.reshape(HALF, 1))


def _rms_kernel(x_ref, g_ref, o_ref):
    x = x_ref[...]
    ms = jnp.mean(x * x, axis=-1, keepdims=True)
    o_ref[...] = (x * lax.rsqrt(ms + EPS) * g_ref[...]).astype(o_ref.dtype)


def _rmsnorm_into(x, norm_g3, l, R, row0, prev):
    n, D = x.shape
    tr = _pick(n, (512, 256, 128))
    assert row0 % tr == 0
    off = row0 // tr
    kern, xspecs, xargs, aliases = _chain(_rms_kernel, 2, None if prev is None else [prev], 0)
    return pl.pallas_call(
        kern,
        grid=(n // tr,),
        in_specs=[pl.BlockSpec((tr, D), lambda i: (i, 0)), _layer_spec((1, D), l)] + xspecs,
        out_specs=pl.BlockSpec((tr, D), lambda i: (i + off, 0)),
        out_shape=jax.ShapeDtypeStruct((R, D), bf16),
        input_output_aliases=aliases,
        compiler_params=_cparams(("arbitrary",)),
        name="rmsnorm",
    )(x, norm_g3, *xargs)


_TN_IN = 512
_TPG = W_MIX // _TN_IN
_NJ_P = 6 * _TPG
_NJ_K = _TPG
_GATE_PAD = 16


def _inproj_row_offset(j):
    jp = jnp.minimum(j, _NJ_P - 1)
    grp = jp // _TPG
    sub = jp - grp * _TPG
    src_grp = jnp.where(grp < 3, grp, jnp.where(grp == 3, 3, grp + 1))
    off_p = src_grp * W_MIX + jnp.where(grp < 3, 0, 2 * N_HEADS) + sub * _TN_IN
    off_k = OFF_RET + W_MIX + jnp.clip(j - _NJ_P, 0, _NJ_K - 1) * _TN_IN
    return jnp.where(j < _NJ_P, off_p, off_k)


def _inproj_kernel(h_ref, wt_ref, wif_ref, wo_ref, p_ref, kt_ref, gi_ref, wob_ref):
    j = pl.program_id(1)
    wob_ref[...] = wo_ref[...].astype(bf16)

    def weights():
        return wt_ref[0].astype(bf16)

    is_z = (j // _TPG == 2) | (j // _TPG == 5)
    last = _NJ_P + _NJ_K - 1

    @pl.when((j < _NJ_P) & jnp.logical_not(is_z))
    def _():
        p_ref[...] = _dot_nt(h_ref[...], weights()).astype(bf16)

    @pl.when((j < _NJ_P) & is_z)
    def _():
        p_ref[...] = _silu(_dot_nt(h_ref[...], weights())).astype(bf16)

    @pl.when((j >= _NJ_P) & (j < last))
    def _():
        kt_ref[...] = _dot_nt(weights(), h_ref[...]).astype(bf16)

    @pl.when(j == last)
    def _():
        pad = jnp.zeros((_GATE_PAD - 2 * N_HEADS, D_MODEL), f32)
        w_all = jnp.concatenate([wt_ref[0], wif_ref[...], pad], axis=0).astype(bf16)
        res = _dot_nt(w_all, h_ref[...])
        kt_ref[...] = res[0:_TN_IN, :].astype(bf16)
        gi_ref[...] = res[_TN_IN:_TN_IN + 2 * N_HEADS, :]


def _inproj(h, w_inT, w_o, l):
    R = h.shape[0]
    tm = _pick(R, (1664, 1280, 640, 384, 128))
    tn = _TN_IN
    nj = _NJ_P + _NJ_K
    nsteps = (R // tm) * nj
    rps = next(r for r in (32, 64, 128, 256, 512, 1024, 2048) if D_MODEL // r <= nsteps)
    wo_blk = lambda i, j: jnp.minimum(i * nj + j, D_MODEL // rps - 1)
    return pl.pallas_call(
        _inproj_kernel,
        grid=(R // tm, nj),
        in_specs=[
            pl.BlockSpec((tm, D_MODEL), lambda i, j: (i, 0)),
            pl.BlockSpec((pl.Element(1), pl.Element(tn), pl.Element(D_MODEL)),
                         lambda i, j: (l, pl.multiple_of(_inproj_row_offset(j), 8), 0)),
            pl.BlockSpec((None, 2 * N_HEADS, D_MODEL), lambda i, j: (l, OFF_IF // (2 * N_HEADS), 0)),
            pl.BlockSpec((None, rps, D_MODEL), lambda i, j: (l, wo_blk(i, j), 0)),
        ],
        out_specs=[
            pl.BlockSpec((tm, tn), lambda i, j: (i, jnp.minimum(j, _NJ_P - 1))),
            pl.BlockSpec((tn, tm), lambda i, j: (jnp.clip(j - _NJ_P, 0, _NJ_K - 1), i)),
            pl.BlockSpec((2 * N_HEADS, tm), lambda i, j: (0, i)),
            pl.BlockSpec((rps, D_MODEL), lambda i, j: (wo_blk(i, j), 0)),
        ],
        out_shape=[
            jax.ShapeDtypeStruct((R, 6 * W_MIX), bf16),
            jax.ShapeDtypeStruct((W_MIX, R), bf16),
            jax.ShapeDtypeStruct((2 * N_HEADS, R), f32),
            jax.ShapeDtypeStruct((D_MODEL, D_MODEL), bf16),
        ],
        compiler_params=_cparams(("arbitrary", "arbitrary")),
        name="inproj",
    )(h, w_inT, w_inT, w_o)


def _mlstm_kernel(um_ref, vm_ref, zs_ref, gi_ref, convw_ref, convb_ref, wq_ref, wkT_ref,
                  bcol_ref, hng_ref, skip_ref,
                  om_ref, c_out_ref, n_out_ref, m_out_ref, conv_out_ref,
                  tail_s, cta_s, m_s):
    g = pl.program_id(1)
    ng = pl.num_programs(1)
    Lg = um_ref.shape[0]
    L = CHUNK
    DA = D_HEAD + LANES
    scale = D_HEAD ** -0.5
    heads = range(N_HEADS)

    @pl.when(g == 0)
    def _():
        tail_s[...] = jnp.zeros(tail_s.shape, bf16)
        cta_s[...] = jnp.zeros(cta_s.shape, f32)
        m_s[...] = jnp.full(m_s.shape, NEG, f32)

    srow = lax.broadcasted_iota(jnp.int32, (L, 2 * L), 0)
    scol = lax.broadcasted_iota(jnp.int32, (L, 2 * L), 1)
    shift_mat = jnp.concatenate([(scol == srow + L - s).astype(bf16) for s in range(1, CONV_W)], axis=0)

    row = lax.broadcasted_iota(jnp.int32, (L, L), 0)
    col = lax.broadcasted_iota(jnp.int32, (L, L), 1)
    causal = row >= col
    triu = (row <= col).astype(bf16)
    bcol = bcol_ref[...]
    ones_b = jnp.ones((L, LANES), bf16)
    ones_f = jnp.ones((L, LANES), f32)

    for c in range(Lg // L):
        rows = slice(c * L, (c + 1) * L)
        gtb = gi_ref[:, rows] + bcol
        lft = _log_sigmoid(gtb)
        br3 = _dot(jnp.concatenate(_split3(lft), axis=0).astype(bf16), triu)
        br_all = br3[0:8, :] + br3[8:16, :] + br3[16:24, :]
        cols = jnp.concatenate([gtb, br_all, jnp.zeros((LANES - 16, L), f32)], axis=0).T

        u_cur = um_ref[rows, :]
        u_prev = tail_s[...] if c == 0 else um_ref[(c - 1) * L:c * L, :]
        delayed = _dot(shift_mat, jnp.concatenate([u_prev, u_cur], axis=0))
        acc = convb_ref[...] + u_cur.astype(f32) * convw_ref[CONV_W - 1:CONV_W, :]
        for s in range(1, CONV_W):
            acc = acc + delayed[(s - 1) * L:s * L, :] * convw_ref[CONV_W - 1 - s:CONV_W - s, :]
        cm = _silu(acc)

        cs = [slice(h * D_HEAD, (h + 1) * D_HEAD) for h in heads]
        ch = [cm[:, cs[h]] for h in heads]
        chb = [x.astype(bf16) for x in ch]
        qb = [(_dot(chb[h], wq_ref[h]) * scale).astype(bf16) for h in heads]
        kTb = [_dot_nt(wkT_ref[h], chb[h]).astype(bf16) for h in heads]
        v = [vm_ref[rows, cs[h]] for h in heads]
        bc = [cols[:, 12 + h:13 + h] for h in heads]
        ic = [cols[:, h:h + 1] for h in heads]
        m_prev = [m_s[h][0:1, 0:1] for h in heads]
        dlog = [jnp.where(causal, bc[h] - br_all[4 + h:5 + h, :] + gtb[h:h + 1, :], NEG) for h in heads]
        inter = [bc[h] + m_prev[h] for h in heads]
        m_t = [jnp.maximum(inter[h], jnp.max(dlog[h], axis=-1, keepdims=True)) for h in heads]
        a = [jnp.exp(inter[h] - m_t[h]) for h in heads]
        ws = [(jnp.exp(dlog[h] - m_t[h]) * _dot(qb[h], kTb[h])).astype(bf16) for h in heads]
        cta = [cta_s[h] for h in heads]
        nd = [_dot(ws[h], jnp.concatenate([v[h], ones_b], axis=1))
              + a[h] * _dot(qb[h], cta[h].astype(bf16)) for h in heads]
        m_new = [m_t[h][L - 1:L, :] for h in heads]
        b_last = [bc[h][L - 1:L, :] for h in heads]
        for h in heads:
            wk_c = jnp.exp(b_last[h] - bc[h] + ic[h] - m_new[h])
            dec = jnp.exp(b_last[h] + m_prev[h] - m_new[h])
            vaug = jnp.concatenate([v[h].astype(f32), ones_f], axis=1)
            cta_s[h] = dec * cta[h] + _dot(kTb[h], (wk_c * vaug).astype(bf16))
            m_s[h] = jnp.broadcast_to(m_new[h], (8, LANES))
        for h in heads:
            den = jnp.maximum(jnp.abs(nd[h][:, D_HEAD:DA]), jnp.exp(-m_t[h]))
            rden = 1.0 / den
            hc = jnp.concatenate([nd[h][:, 0:HALF] * rden, nd[h][:, HALF:D_HEAD] * rden], axis=1)
            hn = _head_norm(hc, hng_ref[:, cs[h]])
            zs = zs_ref[rows, cs[h]].astype(f32)
            om_ref[rows, cs[h]] = ((hn + skip_ref[:, cs[h]] * ch[h]) * zs).astype(bf16)

    tail_s[...] = um_ref[Lg - L:Lg, :]

    @pl.when(g == ng - 1)
    def _():
        for h in heads:
            st = cta_s[h]
            c_out_ref[h] = st[:, 0:D_HEAD].T
            n_out_ref[h] = st[:, D_HEAD:DA].T[0:1, :]
        m_out_ref[...] = jnp.concatenate([m_s[h][0:1, :] for h in heads]
                                         + [jnp.zeros((8 - N_HEADS, LANES), f32)], axis=0)
        conv_out_ref[...] = um_ref[Lg - 8:Lg, :].astype(f32)[8 - (CONV_W - 1):8, :]


def _drop_refs(kernel_fn, n_in, n_drop, *refs):
    return kernel_fn(*refs[:n_in], *refs[n_in + n_drop:])


def _chain(kernel_fn, n_in, prev, first_out):
    if prev is None:
        return kernel_fn, [], [], {}
    specs = [pl.BlockSpec(memory_space=pl.ANY)] * len(prev)
    aliases = {n_in + k: first_out + k for k in range(len(prev))}
    return functools.partial(_drop_refs, kernel_fn, n_in, len(prev)), specs, list(prev), aliases


def _layer_spec(shape, l):
    nd = len(shape)
    return pl.BlockSpec((None,) + tuple(shape), lambda *_: (l,) + (0,) * nd)


def _mlstm_prompt(P, GI, B, T, l, depth, prev, conv_w, conv_b, wq, wkT, bcol, hng, skip):
    R = P.shape[0]
    Lg = _pick(T, (512, 256, 128))
    ng = T // Lg
    col = lambda k: (lambda b, g: (b * ng + g, k))
    in_specs = [
        pl.BlockSpec((Lg, W_MIX), col(0)),
        pl.BlockSpec((Lg, W_MIX), col(1)),
        pl.BlockSpec((Lg, W_MIX), col(2)),
        pl.BlockSpec((2 * N_HEADS, Lg), lambda b, g: (0, b * ng + g)),
        _layer_spec((CONV_W, W_MIX), l),
        _layer_spec((1, W_MIX), l),
        _layer_spec((N_HEADS, D_HEAD, D_HEAD), l),
        _layer_spec((N_HEADS, D_HEAD, D_HEAD), l),
        _layer_spec((8, 1), l),
        _layer_spec((1, W_MIX), l),
        _layer_spec((1, W_MIX), l),
    ]
    args = [P, P, P, GI, conv_w, conv_b, wq, wkT, bcol, hng, skip]
    kern, xspecs, xargs, aliases = _chain(_mlstm_kernel, len(args), prev, 1)
    return pl.pallas_call(
        kern,
        grid=(B, ng),
        in_specs=in_specs + xspecs,
        out_specs=[
            pl.BlockSpec((Lg, W_MIX), col(0)),
            pl.BlockSpec((None, None, N_HEADS, D_HEAD, D_HEAD), lambda b, g: (l, b, 0, 0, 0)),
            pl.BlockSpec((None, None, N_HEADS, 1, D_HEAD), lambda b, g: (l, b, 0, 0, 0)),
            pl.BlockSpec((None, None, 8, LANES), lambda b, g: (l, b, 0, 0)),
            pl.BlockSpec((None, None, CONV_W - 1, W_MIX), lambda b, g: (l, b, 0, 0)),
        ],
        out_shape=[
            jax.ShapeDtypeStruct((R, W_MIX), bf16),
            jax.ShapeDtypeStruct((depth, B, N_HEADS, D_HEAD, D_HEAD), f32),
            jax.ShapeDtypeStruct((depth, B, N_HEADS, 1, D_HEAD), f32),
            jax.ShapeDtypeStruct((depth, B, 8, LANES), f32),
            jax.ShapeDtypeStruct((depth, B, CONV_W - 1, W_MIX), f32),
        ],
        scratch_shapes=[
            pltpu.VMEM((CHUNK, W_MIX), bf16),
            pltpu.VMEM((N_HEADS, D_HEAD, D_HEAD + LANES), f32),
            pltpu.VMEM((N_HEADS, 8, LANES), f32),
        ],
        input_output_aliases=aliases,
        compiler_params=_cparams(("arbitrary", "arbitrary")),
        name="mlstm_prompt",
    )(*args, *xargs)


def _log_gammas():
    return [math.log(1.0 - 2.0 ** (-5.0 - h)) for h in range(N_HEADS)]


def _ret_kernel(q_ref, v_ref, zs_ref, kt_ref, cos_ref, sin_ref, cosT_ref, sinT_ref, hng_ref,
                o_ref, s_out_ref, s_s):
    g = pl.program_id(1)
    ng = pl.num_programs(1)
    Lg = q_ref.shape[0]
    L = CHUNK

    @pl.when(g == 0)
    def _():
        s_s[...] = jnp.zeros(s_s.shape, f32)

    row = lax.broadcasted_iota(jnp.int32, (L, L), 0)
    col = lax.broadcasted_iota(jnp.int32, (L, L), 1)
    causal = row >= col
    diff = (row - col).astype(f32)
    tcol = lax.broadcasted_iota(jnp.int32, (L, 1), 0).astype(f32)
    lgs = _log_gammas()
    intra = [jnp.where(causal, jnp.exp(diff * lg), 0.0) for lg in lgs]
    q_dec = [jnp.exp((tcol + 1.0) * lg) for lg in lgs]
    k_dec = [jnp.exp((L - 1.0 - tcol) * lg) for lg in lgs]
    s_dec = [math.exp(L * lg) for lg in lgs]
    kscale = D_HEAD ** -0.5

    for c in range(Lg // L):
        rows = slice(c * L, (c + 1) * L)
        cos = cos_ref[rows, :]
        sin = sin_ref[rows, :]
        cosT = cosT_ref[:, rows] * kscale
        sinT = sinT_ref[:, rows] * kscale
        heads = range(N_HEADS)
        cs = [slice(h * D_HEAD, (h + 1) * D_HEAD) for h in heads]
        lo = [slice(h * D_HEAD, h * D_HEAD + HALF) for h in heads]
        hi = [slice(h * D_HEAD + HALF, (h + 1) * D_HEAD) for h in heads]
        q1 = [q_ref[rows, lo[h]].astype(f32) for h in heads]
        q2 = [q_ref[rows, hi[h]].astype(f32) for h in heads]
        qb = [jnp.concatenate([q1[h] * cos - q2[h] * sin, q1[h] * sin + q2[h] * cos], axis=-1).astype(bf16)
              for h in heads]
        k1 = [kt_ref[lo[h], rows].astype(f32) for h in heads]
        k2 = [kt_ref[hi[h], rows].astype(f32) for h in heads]
        kTb = [jnp.concatenate([k1[h] * cosT - k2[h] * sinT, k1[h] * sinT + k2[h] * cosT],
                               axis=0).astype(bf16) for h in heads]
        v = [v_ref[rows, cs[h]] for h in heads]
        sc = [(_dot(qb[h], kTb[h]) * intra[h]).astype(bf16) for h in heads]
        s = [s_s[h] for h in heads]
        o = [_dot(sc[h], v[h]) + q_dec[h] * _dot(qb[h], s[h].astype(bf16)) for h in heads]
        for h in heads:
            s_s[h] = s_dec[h] * s[h] + _dot(kTb[h], (v[h].astype(f32) * k_dec[h]).astype(bf16))
        for h in heads:
            zs = zs_ref[rows, cs[h]].astype(f32)
            o_ref[rows, cs[h]] = (_head_norm(o[h], hng_ref[:, cs[h]]) * zs).astype(bf16)

    @pl.when(g == ng - 1)
    def _():
        s_out_ref[...] = s_s[...]


def _ret_prompt(P, KT, B, T, l, depth, prev, tabs, hng):
    R = P.shape[0]
    Lg = _pick(T, (2048, 1024, 512, 256, 128))
    ng = T // Lg
    col = lambda k: (lambda b, g: (b * ng + g, k))
    cos, sin, cosT, sinT = tabs
    in_specs = [
        pl.BlockSpec((Lg, W_MIX), col(3)),
        pl.BlockSpec((Lg, W_MIX), col(4)),
        pl.BlockSpec((Lg, W_MIX), col(5)),
        pl.BlockSpec((W_MIX, Lg), lambda b, g: (0, b * ng + g)),
        pl.BlockSpec((Lg, HALF), lambda b, g: (g, 0)),
        pl.BlockSpec((Lg, HALF), lambda b, g: (g, 0)),
        pl.BlockSpec((HALF, Lg), lambda b, g: (0, g)),
        pl.BlockSpec((HALF, Lg), lambda b, g: (0, g)),
        _layer_spec((1, W_MIX), l),
    ]
    args = [P, P, P, KT, cos, sin, cosT, sinT, hng]
    kern, xspecs, xargs, aliases = _chain(_ret_kernel, len(args), prev, 1)
    return pl.pallas_call(
        kern,
        grid=(B, ng),
        in_specs=in_specs + xspecs,
        out_specs=[
            pl.BlockSpec((Lg, W_MIX), col(0)),
            pl.BlockSpec((None, None, N_HEADS, D_HEAD, D_HEAD), lambda b, g: (l, b, 0, 0, 0)),
        ],
        out_shape=[
            jax.ShapeDtypeStruct((R, W_MIX), bf16),
            jax.ShapeDtypeStruct((depth, B, N_HEADS, D_HEAD, D_HEAD), f32),
        ],
        scratch_shapes=[pltpu.VMEM((N_HEADS, D_HEAD, D_HEAD), f32)],
        input_output_aliases=aliases,
        compiler_params=_cparams(("arbitrary", "arbitrary")),
        name="ret_prompt",
    )(*args, *xargs)


_SEQ_PER_STEP = 4


def _decode_kernel(um_ref, vm_ref, zsm_ref, qr_ref, vr_ref, zsr_ref, kt_ref, gi_ref,
                   conv_ref, n_ref, m_ref, c_ref, s_ref,
                   convw_ref, convb_ref, wq_ref, wk_ref, brow_ref, hngm_ref, skip_ref, hngr_ref,
                   cosd_ref, sind_ref, cosdT_ref, sindT_ref,
                   om_ref, or_ref, c_out_ref, s_out_ref, n_out_ref, m_out_ref, conv_out_ref,
                   vt_s, qrt_s, krt_s, qm_s, km_s, vr_s, sct_s, cqt_s, oacc_s, cm_s):
    s = pl.program_id(0)
    ns = pl.num_programs(0)
    NB = um_ref.shape[0]
    scale = D_HEAD ** -0.5
    lgs = _log_gammas()
    gammas = [math.exp(lg) for lg in lgs]
    lane = lax.broadcasted_iota(jnp.int32, (1, LANES), 1)

    @pl.when(s == 0)
    def _():
        u = um_ref[...].astype(f32)
        acc = convb_ref[...] + u * convw_ref[CONV_W - 1:CONV_W, :]
        for i in range(CONV_W - 1):
            acc = acc + conv_ref[i] * convw_ref[i:i + 1, :]
        cm = _silu(acc)
        cm_s[...] = cm
        for i in range(CONV_W - 2):
            conv_out_ref[i] = conv_ref[i + 1]
        conv_out_ref[CONV_W - 2] = u

        gb = jnp.concatenate([gi_ref[...], jnp.zeros((LANES - 8, NB), f32)], axis=0).T + brow_ref[...]
        lf = _log_sigmoid(gb)
        m_cols = jnp.concatenate([m_ref[...], jnp.zeros((8 - N_HEADS, NB), f32)], axis=0).T
        sc = jnp.zeros((NB, LANES), f32)
        mo = jnp.zeros((NB, LANES), f32)
        for h in range(N_HEADS):
            cs = slice(h * D_HEAD, (h + 1) * D_HEAD)
            chb = cm[:, cs].astype(bf16)
            q = _dot(chb, wq_ref[h]) * scale
            k = _dot(chb, wk_ref[h])
            qm_s[:, cs] = q
            km_s[:, cs] = k
            ig = gb[:, h:h + 1]
            lfh = lf[:, 4 + h:5 + h]
            mp = m_cols[:, h:h + 1]
            inter = lfh + mp
            m_t = jnp.maximum(inter, ig)
            w = jnp.exp(ig - m_t)
            a = jnp.exp(inter - m_t)
            nh = n_ref[:, cs]
            ws = w * jnp.sum(q * k, axis=-1, keepdims=True)
            den = ws + a * jnp.sum(nh * q, axis=-1, keepdims=True)
            den = jnp.maximum(jnp.abs(den), jnp.exp(-m_t))
            n_out_ref[:, cs] = a * nh + w * k
            for r, val in ((h, ws), (4 + h, a), (8 + h, den), (12 + h, w)):
                sc = sc + val * (lane == r).astype(f32)
            mo = mo + m_t * (lane == h).astype(f32)
        m_out_ref[...] = mo.T[0:N_HEADS, :]
        sct_s[...] = sc.T
        vt_s[...] = vm_ref[...].astype(f32).T
        vr_s[...] = vr_ref[...].astype(f32)
        cosd = cosd_ref[0:1, :]
        sind = sind_ref[0:1, :]
        cosdT = cosdT_ref[...]
        sindT = sindT_ref[...]
        for h in range(N_HEADS):
            lo = slice(h * D_HEAD, h * D_HEAD + HALF)
            hi = slice(h * D_HEAD + HALF, (h + 1) * D_HEAD)
            q1 = qr_ref[:, lo].astype(f32)
            q2 = qr_ref[:, hi].astype(f32)
            qrot = jnp.concatenate([q1 * cosd - q2 * sind, q1 * sind + q2 * cosd], axis=-1)
            qrt_s[h * D_HEAD:(h + 1) * D_HEAD, :] = qrot.T
            k1 = kt_ref[lo, :].astype(f32)
            k2 = kt_ref[hi, :].astype(f32)
            krt_s[lo, :] = (k1 * cosdT - k2 * sindT) * scale
            krt_s[hi, :] = (k1 * sindT + k2 * cosdT) * scale
        cqt_s[...] = jnp.zeros(cqt_s.shape, f32)

    def take_col(mat, mask):
        return jnp.sum(jnp.where(mask, mat, 0.0), axis=-1, keepdims=True)

    for j in range(_SEQ_PER_STEP):
        b = s * _SEQ_PER_STEP + j
        mask = lane == b
        for h in range(N_HEADS):
            cs = slice(h * D_HEAD, (h + 1) * D_HEAD)
            c = c_ref[j, h]
            q_row = qm_s[pl.ds(b, 1), cs]
            k_row = km_s[pl.ds(b, 1), cs]
            v_col = take_col(vt_s[cs, :], mask)
            a = take_col(sct_s[4 + h:5 + h, :], mask)
            w = take_col(sct_s[12 + h:13 + h, :], mask)
            cq = jnp.sum(c * q_row, axis=-1, keepdims=True)
            cqt_s[cs, :] = jnp.where(mask, cq, cqt_s[cs, :])
            c_out_ref[j, h] = a * c + (w * v_col) * k_row
            st = s_ref[j, h]
            q_col = take_col(qrt_s[cs, :], mask)
            k_col = take_col(krt_s[cs, :], mask)
            v_row = vr_s[pl.ds(b, 1), cs]
            qk = jnp.sum(q_col * k_col, axis=0, keepdims=True)
            o_row = qk * v_row + gammas[h] * jnp.sum(st * q_col, axis=0, keepdims=True)
            oacc_s[pl.ds(b, 1), cs] = o_row
            s_out_ref[j, h] = gammas[h] * st + k_col * v_row

    @pl.when(s == ns - 1)
    def _():
        for h in range(N_HEADS):
            cs = slice(h * D_HEAD, (h + 1) * D_HEAD)
            ht = ((sct_s[h:h + 1, :] * vt_s[cs, :] + sct_s[4 + h:5 + h, :] * cqt_s[cs, :])
                  / sct_s[8 + h:9 + h, :])
            hn = _head_norm(ht.T, hngm_ref[:, cs])
            zsm = zsm_ref[:, cs].astype(f32)
            om_ref[:, cs] = ((hn + skip_ref[:, cs] * cm_s[:, cs]) * zsm).astype(bf16)
            zsr = zsr_ref[:, cs].astype(f32)
            or_ref[:, cs] = (_head_norm(oacc_s[:, cs], hngr_ref[:, cs]) * zsr).astype(bf16)


def _decode_step(P, KT, GI, OM, OR, prev, l, BT, conv_st, n_st, m_st, C_st, S_st,
                 conv_w, conv_b, wq, wk, brow, hngm, skip, hngr, dtabs):
    R = P.shape[0]
    NB = R - BT
    assert NB == LANES and BT % NB == 0 and NB % _SEQ_PER_STEP == 0
    rb = BT // NB
    depth = C_st.shape[0]
    bb = _SEQ_PER_STEP
    cosd, sind, cosdT, sindT = dtabs
    pcol = lambda k: (lambda s: (rb, k))
    rep2 = lambda s: (0, 0)
    st_spec = pl.BlockSpec((None, bb, N_HEADS, D_HEAD, D_HEAD), lambda s: (l, s, 0, 0, 0))
    conv_spec = _layer_spec((CONV_W - 1, NB, W_MIX), l)
    n_spec = _layer_spec((NB, W_MIX), l)
    m_spec = _layer_spec((N_HEADS, NB), l)
    in_specs = [
        pl.BlockSpec((NB, W_MIX), pcol(0)), pl.BlockSpec((NB, W_MIX), pcol(1)),
        pl.BlockSpec((NB, W_MIX), pcol(2)), pl.BlockSpec((NB, W_MIX), pcol(3)),
        pl.BlockSpec((NB, W_MIX), pcol(4)), pl.BlockSpec((NB, W_MIX), pcol(5)),
        pl.BlockSpec((W_MIX, NB), lambda s: (0, rb)),
        pl.BlockSpec((2 * N_HEADS, NB), lambda s: (0, rb)),
        conv_spec, n_spec, m_spec, st_spec, st_spec,
        _layer_spec((CONV_W, W_MIX), l), _layer_spec((1, W_MIX), l),
        _layer_spec((N_HEADS, D_HEAD, D_HEAD), l), _layer_spec((N_HEADS, D_HEAD, D_HEAD), l),
        _layer_spec((1, LANES), l),
        _layer_spec((1, W_MIX), l), _layer_spec((1, W_MIX), l), _layer_spec((1, W_MIX), l),
        pl.BlockSpec((8, HALF), rep2), pl.BlockSpec((8, HALF), rep2),
        pl.BlockSpec((HALF, LANES), rep2), pl.BlockSpec((HALF, LANES), rep2),
    ]
    args = [P, P, P, P, P, P, KT, GI, conv_st, n_st, m_st, C_st, S_st,
            conv_w, conv_b, wq, wk, brow, hngm, skip, hngr, cosd, sind, cosdT, sindT]
    carried = [OM, OR] + (list(prev) if prev is not None else [])
    kern, xspecs, xargs, aliases = _chain(_decode_kernel, len(args), carried, 0)
    out_row = pl.BlockSpec((NB, W_MIX), lambda s: (rb, 0))
    return pl.pallas_call(
        kern,
        grid=(NB // bb,),
        in_specs=in_specs + xspecs,
        out_specs=[out_row, out_row, st_spec, st_spec, n_spec, m_spec, conv_spec],
        out_shape=[
            jax.ShapeDtypeStruct(OM.shape, bf16),
            jax.ShapeDtypeStruct(OR.shape, bf16),
            jax.ShapeDtypeStruct((depth, NB, N_HEADS, D_HEAD, D_HEAD), f32),
            jax.ShapeDtypeStruct((depth, NB, N_HEADS, D_HEAD, D_HEAD), f32),
            jax.ShapeDtypeStruct((depth, NB, W_MIX), f32),
            jax.ShapeDtypeStruct((depth, N_HEADS, NB), f32),
            jax.ShapeDtypeStruct((depth, CONV_W - 1, NB, W_MIX), f32),
        ],
        scratch_shapes=[
            pltpu.VMEM((W_MIX, NB), f32),
            pltpu.VMEM((W_MIX, NB), f32),
            pltpu.VMEM((W_MIX, NB), f32),
            pltpu.VMEM((NB, W_MIX), f32),
            pltpu.VMEM((NB, W_MIX), f32),
            pltpu.VMEM((NB, W_MIX), f32),
            pltpu.VMEM((LANES, NB), f32),
            pltpu.VMEM((W_MIX, NB), f32),
            pltpu.VMEM((NB, W_MIX), f32),
            pltpu.VMEM((NB, W_MIX), f32),
        ],
        input_output_aliases=aliases,
        compiler_params=_cparams(("arbitrary",)),
        name="decode_step",
    )(*args, *xargs)


def _merge_kernel(h_ref, om_ref, or_ref, wgm_ref, wgr_ref, bgm_ref, bgr_ref, wpm_ref, wpr_ref, o_ref):
    h = h_ref[...]
    gm = jax.nn.sigmoid(_dot(h, wgm_ref[...].astype(bf16)) + bgm_ref[...])
    gr = jax.nn.sigmoid(_dot(h, wgr_ref[...].astype(bf16)) + bgr_ref[...])
    pm = _dot(om_ref[...], wpm_ref[...].astype(bf16))
    pr = _dot(or_ref[...], wpr_ref[...].astype(bf16))
    o_ref[...] = (gm * pm + gr * pr).astype(bf16)


def _merge(h, OM, OR, w_gate, b_gate3, w_pm, w_pr, l):
    R = h.shape[0]
    tm = _pick(R, (1664, 1280, 640, 384, 128))
    tn = 256
    nj = D_MODEL // tn
    return pl.pallas_call(
        _merge_kernel,
        grid=(R // tm, nj),
        in_specs=[
            pl.BlockSpec((tm, D_MODEL), lambda i, j: (i, 0)),
            pl.BlockSpec((tm, W_MIX), lambda i, j: (i, 0)),
            pl.BlockSpec((tm, W_MIX), lambda i, j: (i, 0)),
            pl.BlockSpec((None, D_MODEL, tn), lambda i, j: (l, 0, j)),
            pl.BlockSpec((None, D_MODEL, tn), lambda i, j: (l, 0, j + nj)),
            pl.BlockSpec((None, 1, tn), lambda i, j: (l, 0, j)),
            pl.BlockSpec((None, 1, tn), lambda i, j: (l, 0, j + nj)),
            pl.BlockSpec((None, W_MIX, tn), lambda i, j: (l, 0, j)),
            pl.BlockSpec((None, W_MIX, tn), lambda i, j: (l, 0, j)),
        ],
        out_specs=pl.BlockSpec((tm, tn), lambda i, j: (i, j)),
        out_shape=jax.ShapeDtypeStruct((R, D_MODEL), bf16),
        compiler_params=_cparams(("arbitrary", "arbitrary")),
        name="merge",
    )(h, OM, OR, w_gate, w_gate, b_gate3, b_gate3, w_pm, w_pr)


def _row_halves(tm):
    return (slice(0, tm // 2), slice(tm // 2, tm))


def _outproj_norm_kernel(mg_ref, wo_ref, x_ref, g_ref, o_ref, h_ref):
    for r in _row_halves(x_ref.shape[0]):
        y = x_ref[r, :] + _dot(mg_ref[r, :], wo_ref[...])
        o_ref[r, :] = y
        ms = jnp.mean(y * y, axis=-1, keepdims=True)
        h_ref[r, :] = (y * lax.rsqrt(ms + EPS) * g_ref[...]).astype(bf16)


def _outproj_final_kernel(mg_ref, wo_ref, x_ref, g_ref, y_ref):
    for r in _row_halves(x_ref.shape[0]):
        y = x_ref[r, :] + _dot(mg_ref[r, :], wo_ref[...])
        ms = jnp.mean(y * y, axis=-1, keepdims=True)
        y_ref[r, :] = y * lax.rsqrt(ms + EPS) * g_ref[...]


def _wo_spec():
    return pl.BlockSpec((D_MODEL, D_MODEL), lambda i: (0, 0), pipeline_mode=pl.Buffered(1))


def _outproj(merged, w_o_bf, x, norm_g3, l_next, row0, nrows, x_local, prev):
    R = merged.shape[0]
    tm = _pick(nrows, (640, 512, 384, 128))
    assert row0 % tm == 0
    off = row0 // tm
    row_g = pl.BlockSpec((tm, D_MODEL), lambda i: (i + off, 0))
    row_x = pl.BlockSpec((tm, D_MODEL), lambda i: (i, 0)) if x_local else row_g
    kern, xspecs, xargs, aliases = _chain(_outproj_norm_kernel, 4, prev, 0)
    return pl.pallas_call(
        kern,
        grid=(nrows // tm,),
        in_specs=[row_g, _wo_spec(), row_x, _layer_spec((1, D_MODEL), l_next)] + xspecs,
        out_specs=[row_g, row_g],
        out_shape=[jax.ShapeDtypeStruct((R, D_MODEL), f32), jax.ShapeDtypeStruct((R, D_MODEL), bf16)],
        input_output_aliases=aliases,
        compiler_params=_cparams(("arbitrary",)),
        name="outproj_norm",
    )(merged, w_o_bf, x, norm_g3, *xargs)


def _outproj_final(merged, w_o_bf, x, g, row0, nrows):
    tm = _pick(nrows, (512, 256, 128))
    assert row0 % tm == 0
    off = row0 // tm
    row_in = pl.BlockSpec((tm, D_MODEL), lambda i: (i + off, 0))
    return pl.pallas_call(
        _outproj_final_kernel,
        grid=(nrows // tm,),
        in_specs=[row_in, _wo_spec(), row_in, pl.BlockSpec((1, D_MODEL), lambda i: (0, 0))],
        out_specs=pl.BlockSpec((tm, D_MODEL), lambda i: (i, 0)),
        out_shape=jax.ShapeDtypeStruct((nrows, D_MODEL), f32),
        compiler_params=_cparams(("arbitrary",)),
        name="outproj_final",
    )(merged, w_o_bf, x, g)


def kernel(x_prompt, x_sample, state_mlstm_C, state_mlstm_n, state_mlstm_m, state_mlstm_conv, state_ret_S,
           norm_g, w_in, conv_w, conv_b, w_qm, w_km, b_i, b_f, hn_m_g, skip_m, hn_r_g,
           w_pm, w_pr, w_gate, b_gate, w_o, final_g):
    B, T, D = x_prompt.shape
    NB = x_sample.shape[0]
    depth = w_in.shape[0]
    BT = B * T
    assert D == D_MODEL and x_sample.shape[1] == 1 and T % CHUNK == 0 and depth >= 2

    R = BT + NB
    xp = x_prompt.reshape(BT, D)
    xs = x_sample.reshape(NB, D)

    w_inT = jnp.swapaxes(w_in, 1, 2)
    wq = w_qm.astype(bf16)
    wk = w_km.astype(bf16)
    wkT = jnp.swapaxes(w_km, 2, 3).astype(bf16)
    bias = jnp.pad(jnp.concatenate([b_i, b_f], axis=1), ((0, 0), (0, LANES - 2 * N_HEADS)))
    brow = bias.reshape(depth, 1, LANES)
    bcol = bias[:, :8].reshape(depth, 8, 1)
    b_gate3 = b_gate.reshape(depth, 1, 2 * D)
    norm_g3 = norm_g.reshape(depth, 1, D)
    conv_b3 = conv_b.reshape(depth, 1, W_MIX)
    hngm = hn_m_g.reshape(depth, 1, W_MIX)
    skp = skip_m.reshape(depth, 1, W_MIX)
    hngr = hn_r_g.reshape(depth, 1, W_MIX)
    conv_st = jnp.transpose(state_mlstm_conv, (0, 2, 1, 3))
    m_st = jnp.swapaxes(state_mlstm_m, 1, 2)
    n_st = state_mlstm_n.reshape(depth, NB, W_MIX)

    cos, sin, cosT, sinT, cosd, sind, cosdT, sindT = _rope_tables(T)

    prev_m = prev_r = prev_d = None
    h = _rmsnorm_into(xp, norm_g3, 0, R, 0, None)
    h = _rmsnorm_into(xs, norm_g3, 0, R, BT, h)
    x = None
    for l in range(depth):
        P, KT, GI, w_o_bf = _inproj(h, w_inT, w_o, l)
        OM, *prev_m = _mlstm_prompt(P, GI, B, T, l, depth, prev_m, conv_w, conv_b3, wq, wkT,
                                    bcol, hngm, skp)
        OR, *prev_r = _ret_prompt(P, KT, B, T, l, depth, prev_r, (cos, sin, cosT, sinT), hngr)
        OM, OR, *prev_d = _decode_step(
            P, KT, GI, OM, OR, prev_d, l, BT, conv_st, n_st, m_st, state_mlstm_C, state_ret_S,
            conv_w, conv_b3, wq, wk, brow, hngm, skp, hngr, (cosd, sind, cosdT, sindT))
        merged = _merge(h, OM, OR, w_gate, b_gate3, w_pm, w_pr, l)
        if l + 1 == depth:
            break
        if x is None:
            x, h = _outproj(merged, w_o_bf, xp, norm_g3, l + 1, 0, BT, True, None)
            x, h = _outproj(merged, w_o_bf, xs, norm_g3, l + 1, BT, NB, True, (x, h))
        else:
            x, h = _outproj(merged, w_o_bf, x, norm_g3, l + 1, 0, R, False, None)

    fg = final_g.reshape(1, D)
    y_prompt = _outproj_final(merged, w_o_bf, x, fg, 0, BT).reshape(B, T, D)
    y_sample = _outproj_final(merged, w_o_bf, x, fg, BT, NB).reshape(NB, 1, D)
    Cp, np_, mp, convp = prev_m
    (Sp,) = prev_r
    Cs, Ss, n_s, m_s, conv_s = prev_d
    return (y_prompt, y_sample,
            Cp, np_.reshape(depth, B, N_HEADS, D_HEAD), mp[:, :, :N_HEADS, 0], convp, Sp,
            Cs, n_s.reshape(depth, NB, N_HEADS, D_HEAD), jnp.swapaxes(m_s, 1, 2),
            jnp.transpose(conv_s, (0, 2, 1, 3)), Ss)
```

```python
import functools
import math

import jax
import jax.numpy as jnp
from jax import lax
from jax.experimental import pallas as pl
from jax.experimental.pallas import tpu as pltpu

f32 = jnp.float32
bf16 = jnp.bfloat16

D_MODEL = 2048
N_HEADS = 4
D_HEAD = 256
W_MIX = N_HEADS * D_HEAD
CONV_W = 4
CHUNK = 128
ROPE_BASE = 10000.0
EPS = 1e-6
NEG = -1e30
PAST_LEN = 16384
LANES = 128
HALF = D_HEAD // 2
VMEM_LIMIT = 56 * 1024 * 1024

OFF_IF = 3 * W_MIX
OFF_RET = 3 * W_MIX + 2 * N_HEADS


def _pick(n, cands):
    for c in cands:
        if n % c == 0:
            return c
    return n


def _cparams(sem):
    return pltpu.CompilerParams(dimension_semantics=sem, vmem_limit_bytes=VMEM_LIMIT)


def _dot(a, b):
    return jnp.dot(a, b, preferred_element_type=f32)


def _dot_nt(a, b):
    return lax.dot_general(a, b, (((1,), (1,)), ((), ())), preferred_element_type=f32)


def _split3(x):
    hi = x.astype(bf16).astype(f32)
    r = x - hi
    mid = r.astype(bf16).astype(f32)
    lo = (r - mid).astype(bf16).astype(f32)
    return hi, mid, lo


def _silu(x):
    return x * (0.5 * jnp.tanh(0.5 * x) + 0.5)


def _log_sigmoid(x):
    return jnp.minimum(x, 0.0) - jnp.log1p(jnp.exp(-jnp.abs(x)))


def _head_norm(hc, g):
    mu = jnp.mean(hc, axis=-1, keepdims=True)
    d = hc - mu
    var = jnp.mean(d * d, axis=-1, keepdims=True)
    return d * lax.rsqrt(var + EPS) * g


def _rope_kernel(inv_ref, invT_ref, cos_ref, sin_ref, cosT_ref, sinT_ref,
                 cosd_ref, sind_ref, cosdT_ref, sindT_ref):
    T = cos_ref.shape[0]
    pos = lax.broadcasted_iota(jnp.int32, (T, HALF), 0).astype(f32)
    ang = pos * inv_ref[...]
    cos = jnp.cos(ang)
    sin = jnp.sin(ang)
    cos_ref[...] = cos
    sin_ref[...] = sin
    cosT_ref[...] = cos.T
    sinT_ref[...] = sin.T
    angd = jnp.full((8, HALF), float(PAST_LEN), f32) * inv_ref[...]
    cosd_ref[...] = jnp.cos(angd)
    sind_ref[...] = jnp.sin(angd)
    angdT = jnp.full((HALF, LANES), float(PAST_LEN), f32) * invT_ref[...]
    cosdT_ref[...] = jnp.cos(angdT)
    sindT_ref[...] = jnp.sin(angdT)


def _rope_tables(T):
    inv = ROPE_BASE ** (-jnp.arange(0, HALF, dtype=f32) / HALF)
    shapes = [(T, HALF), (T, HALF), (HALF, T), (HALF, T),
              (8, HALF), (8, HALF), (HALF, LANES), (HALF, LANES)]
    return pl.pallas_call(
        _rope_kernel,
        out_shape=[jax.ShapeDtypeStruct(s, f32) for s in shapes],
        name="rope_tables",
    )(inv.reshape(1, HALF), inv.reshape(HALF, 1))


def _rms_kernel(x_ref, g_ref, o_ref):
    x = x_ref[...]
    ms = jnp.mean(x * x, axis=-1, keepdims=True)
    o_ref[...] = (x * lax.rsqrt(ms + EPS) * g_ref[...]).astype(o_ref.dtype)


def _rmsnorm_into(x, norm_g3, l, R, row0, prev):
    n, D = x.shape
    tr = _pick(n, (512, 256, 128))
    assert row0 % tr == 0
    off = row0 // tr
    kern, xspecs, xargs, aliases = _chain(_rms_kernel, 2, None if prev is None else [prev], 0)
    return pl.pallas_call(
        kern,
        grid=(n // tr,),
        in_specs=[pl.BlockSpec((tr, D), lambda i: (i, 0)), _layer_spec((1, D), l)] + xspecs,
        out_specs=pl.BlockSpec((tr, D), lambda i: (i + off, 0)),
        out_shape=jax.ShapeDtypeStruct((R, D), bf16),
        input_output_aliases=aliases,
        compiler_params=_cparams(("arbitrary",)),
        name="rmsnorm",
    )(x, norm_g3, *xargs)


_TN_IN = 512
_TPG = W_MIX // _TN_IN
_NJ_P = 6 * _TPG
_NJ_K = _TPG
_GATE_PAD = 16


def _inproj_row_offset(j):
    jp = jnp.minimum(j, _NJ_P - 1)
    grp = jp // _TPG
    sub = jp - grp * _TPG
    src_grp = jnp.where(grp < 3, grp, jnp.where(grp == 3, 3, grp + 1))
    off_p = src_grp * W_MIX + jnp.where(grp < 3, 0, 2 * N_HEADS) + sub * _TN_IN
    off_k = OFF_RET + W_MIX + jnp.clip(j - _NJ_P, 0, _NJ_K - 1) * _TN_IN
    return jnp.where(j < _NJ_P, off_p, off_k)


def _inproj_kernel(h_ref, wt_ref, wif_ref, wo_ref, p_ref, kt_ref, gi_ref, wob_ref):
    j = pl.program_id(1)
    wob_ref[...] = wo_ref[...].astype(bf16)

    def weights():
        return wt_ref[0].astype(bf16)

    is_z = (j // _TPG == 2) | (j // _TPG == 5)
    last = _NJ_P + _NJ_K - 1

    @pl.when((j < _NJ_P) & jnp.logical_not(is_z))
    def _():
        p_ref[...] = _dot_nt(h_ref[...], weights()).astype(bf16)

    @pl.when((j < _NJ_P) & is_z)
    def _():
        p_ref[...] = _silu(_dot_nt(h_ref[...], weights())).astype(bf16)

    @pl.when((j >= _NJ_P) & (j < last))
    def _():
        kt_ref[...] = _dot_nt(weights(), h_ref[...]).astype(bf16)

    @pl.when(j == last)
    def _():
        pad = jnp.zeros((_GATE_PAD - 2 * N_HEADS, D_MODEL), f32)
        w_all = jnp.concatenate([wt_ref[0], wif_ref[...], pad], axis=0).astype(bf16)
        res = _dot_nt(w_all, h_ref[...])
        kt_ref[...] = res[0:_TN_IN, :].astype(bf16)
        gi_ref[...] = res[_TN_IN:_TN_IN + 2 * N_HEADS, :]


def _inproj(h, w_inT, w_o, l):
    R = h.shape[0]
    tm = _pick(R, (1664, 1280, 640, 384, 128))
    tn = _TN_IN
    nj = _NJ_P + _NJ_K
    nsteps = (R // tm) * nj
    rps = next(r for r in (32, 64, 128, 256, 512, 1024, 2048) if D_MODEL // r <= nsteps)
    wo_blk = lambda i, j: jnp.minimum(i * nj + j, D_MODEL // rps - 1)
    return pl.pallas_call(
        _inproj_kernel,
        grid=(R // tm, nj),
        in_specs=[
            pl.BlockSpec((tm, D_MODEL), lambda i, j: (i, 0)),
            pl.BlockSpec((pl.Element(1), pl.Element(tn), pl.Element(D_MODEL)),
                         lambda i, j: (l, pl.multiple_of(_inproj_row_offset(j), 8), 0)),
            pl.BlockSpec((None, 2 * N_HEADS, D_MODEL), lambda i, j: (l, OFF_IF // (2 * N_HEADS), 0)),
            pl.BlockSpec((None, rps, D_MODEL), lambda i, j: (l, wo_blk(i, j), 0)),
        ],
        out_specs=[
            pl.BlockSpec((tm, tn), lambda i, j: (i, jnp.minimum(j, _NJ_P - 1))),
            pl.BlockSpec((tn, tm), lambda i, j: (jnp.clip(j - _NJ_P, 0, _NJ_K - 1), i)),
            pl.BlockSpec((2 * N_HEADS, tm), lambda i, j: (0, i)),
            pl.BlockSpec((rps, D_MODEL), lambda i, j: (wo_blk(i, j), 0)),
        ],
        out_shape=[
            jax.ShapeDtypeStruct((R, 6 * W_MIX), bf16),
            jax.ShapeDtypeStruct((W_MIX, R), bf16),
            jax.ShapeDtypeStruct((2 * N_HEADS, R), f32),
            jax.ShapeDtypeStruct((D_MODEL, D_MODEL), bf16),
        ],
        compiler_params=_cparams(("arbitrary", "arbitrary")),
        name="inproj",
    )(h, w_inT, w_inT, w_o)


def _mlstm_kernel(um_ref, vm_ref, zs_ref, gi_ref, convw_ref, convb_ref, wq_ref, wkT_ref,
                  bcol_ref, hng_ref, skip_ref,
                  om_ref, c_out_ref, n_out_ref, m_out_ref, conv_out_ref,
                  tail_s, cta_s, m_s):
    g = pl.program_id(1)
    ng = pl.num_programs(1)
    Lg = um_ref.shape[0]
    L = CHUNK
    DA = D_HEAD + LANES
    heads = range(N_HEADS)

    @pl.when(g == 0)
    def _():
        tail_s[...] = jnp.zeros(tail_s.shape, bf16)
        cta_s[...] = jnp.zeros(cta_s.shape, f32)
        m_s[...] = jnp.full(m_s.shape, NEG, f32)

    srow = lax.broadcasted_iota(jnp.int32, (L, 2 * L), 0)
    scol = lax.broadcasted_iota(jnp.int32, (L, 2 * L), 1)
    shift_mat = jnp.concatenate([(scol == srow + L - s).astype(bf16) for s in range(1, CONV_W)], axis=0)

    row = lax.broadcasted_iota(jnp.int32, (L, L), 0)
    col = lax.broadcasted_iota(jnp.int32, (L, L), 1)
    causal = row >= col
    triu = (row <= col).astype(bf16)
    lane = col
    bcol = bcol_ref[...]
    ones_b = jnp.ones((L, LANES), bf16)
    ones_f = jnp.ones((L, LANES), f32)

    for c in range(Lg // L):
        rows = slice(c * L, (c + 1) * L)
        gtb = gi_ref[:, rows] + bcol
        lft = _log_sigmoid(gtb)
        br3 = _dot(jnp.concatenate(_split3(lft), axis=0).astype(bf16), triu)
        br_all = br3[0:8, :] + br3[8:16, :] + br3[16:24, :]
        cols = jnp.concatenate([gtb, br_all, jnp.zeros((LANES - 16, L), f32)], axis=0).T

        u_cur = um_ref[rows, :]
        u_prev = tail_s[...] if c == 0 else um_ref[(c - 1) * L:c * L, :]
        delayed = _dot(shift_mat, jnp.concatenate([u_prev, u_cur], axis=0))
        acc = convb_ref[...] + u_cur.astype(f32) * convw_ref[CONV_W - 1:CONV_W, :]
        for s in range(1, CONV_W):
            acc = acc + delayed[(s - 1) * L:s * L, :] * convw_ref[CONV_W - 1 - s:CONV_W - s, :]
        cm = _silu(acc)

        cs = [slice(h * D_HEAD, (h + 1) * D_HEAD) for h in heads]
        ch = [cm[:, cs[h]] for h in heads]
        chb = [x.astype(bf16) for x in ch]
        qb = [_dot(chb[h], wq_ref[h]).astype(bf16) for h in heads]
        kTb = [_dot_nt(wkT_ref[h], chb[h]).astype(bf16) for h in heads]
        v = [vm_ref[rows, cs[h]] for h in heads]
        bc = [cols[:, 12 + h:13 + h] for h in heads]
        dlog = [jnp.where(causal, bc[h] - br_all[4 + h:5 + h, :] + gtb[h:h + 1, :], NEG) for h in heads]
        dmax = jnp.zeros((L, LANES), f32)
        for h in heads:
            dmax = jnp.where(lane == 12 + h, jnp.max(dlog[h], axis=-1, keepdims=True), dmax)
        m_row = m_s[0:1, :]
        inter = cols + m_row
        m_all = jnp.maximum(inter, dmax)
        a_all = jnp.exp(inter - m_all)
        em_all = jnp.exp(-m_all)
        m_new = m_all[L - 1:L, :]
        b_last = cols[L - 1:L, :]
        ic_al = pltpu.roll(cols, 12, axis=1)
        wk_all = jnp.exp(b_last - cols + ic_al - m_new)
        dec_row = jnp.exp(b_last + m_row - m_new)
        m_s[...] = jnp.broadcast_to(m_new, (8, LANES))
        hl = [slice(12 + h, 13 + h) for h in heads]
        ws = [(jnp.exp(dlog[h] - m_all[:, hl[h]]) * _dot(qb[h], kTb[h])).astype(bf16) for h in heads]
        cta = [cta_s[h] for h in heads]
        nd = [_dot(ws[h], jnp.concatenate([v[h], ones_b], axis=1))
              + a_all[:, hl[h]] * _dot(qb[h], cta[h].astype(bf16)) for h in heads]
        for h in heads:
            vaug = jnp.concatenate([v[h].astype(f32), ones_f], axis=1)
            cta_s[h] = dec_row[:, hl[h]] * cta[h] + _dot(kTb[h], (wk_all[:, hl[h]] * vaug).astype(bf16))
        for h in heads:
            den = jnp.maximum(jnp.abs(nd[h][:, D_HEAD:DA]), em_all[:, hl[h]])
            rden = 1.0 / den
            hc = jnp.concatenate([nd[h][:, 0:HALF] * rden, nd[h][:, HALF:D_HEAD] * rden], axis=1)
            hn = _head_norm(hc, hng_ref[:, cs[h]])
            zs = zs_ref[rows, cs[h]].astype(f32)
            om_ref[rows, cs[h]] = ((hn + skip_ref[:, cs[h]] * ch[h]) * zs).astype(bf16)

    tail_s[...] = um_ref[Lg - L:Lg, :]

    @pl.when(g == ng - 1)
    def _():
        for h in heads:
            st = cta_s[h]
            c_out_ref[h] = st[:, 0:D_HEAD].T
            n_out_ref[h] = st[:, D_HEAD:DA].T[0:1, :]
        m_out_ref[...] = m_s[...]
        conv_out_ref[...] = um_ref[Lg - 8:Lg, :].astype(f32)[8 - (CONV_W - 1):8, :]


def _drop_refs(kernel_fn, n_in, n_drop, *refs):
    return kernel_fn(*refs[:n_in], *refs[n_in + n_drop:])


def _chain(kernel_fn, n_in, prev, first_out):
    if prev is None:
        return kernel_fn, [], [], {}
    specs = [pl.BlockSpec(memory_space=pl.ANY)] * len(prev)
    aliases = {n_in + k: first_out + k for k in range(len(prev))}
    return functools.partial(_drop_refs, kernel_fn, n_in, len(prev)), specs, list(prev), aliases


def _layer_spec(shape, l):
    nd = len(shape)
    return pl.BlockSpec((None,) + tuple(shape), lambda *_: (l,) + (0,) * nd)


def _mlstm_prompt(P, GI, B, T, l, depth, prev, conv_w, conv_b, wq, wkT, bcol, hng, skip):
    R = P.shape[0]
    Lg = _pick(T, (512, 256, 128))
    ng = T // Lg
    col = lambda k: (lambda b, g: (b * ng + g, k))
    in_specs = [
        pl.BlockSpec((Lg, W_MIX), col(0)),
        pl.BlockSpec((Lg, W_MIX), col(1)),
        pl.BlockSpec((Lg, W_MIX), col(2)),
        pl.BlockSpec((2 * N_HEADS, Lg), lambda b, g: (0, b * ng + g)),
        _layer_spec((CONV_W, W_MIX), l),
        _layer_spec((1, W_MIX), l),
        _layer_spec((N_HEADS, D_HEAD, D_HEAD), l),
        _layer_spec((N_HEADS, D_HEAD, D_HEAD), l),
        _layer_spec((8, 1), l),
        _layer_spec((1, W_MIX), l),
        _layer_spec((1, W_MIX), l),
    ]
    args = [P, P, P, GI, conv_w, conv_b, wq, wkT, bcol, hng, skip]
    kern, xspecs, xargs, aliases = _chain(_mlstm_kernel, len(args), prev, 1)
    return pl.pallas_call(
        kern,
        grid=(B, ng),
        in_specs=in_specs + xspecs,
        out_specs=[
            pl.BlockSpec((Lg, W_MIX), col(0)),
            pl.BlockSpec((None, None, N_HEADS, D_HEAD, D_HEAD), lambda b, g: (l, b, 0, 0, 0)),
            pl.BlockSpec((None, None, N_HEADS, 1, D_HEAD), lambda b, g: (l, b, 0, 0, 0)),
            pl.BlockSpec((None, None, 8, LANES), lambda b, g: (l, b, 0, 0)),
            pl.BlockSpec((None, None, CONV_W - 1, W_MIX), lambda b, g: (l, b, 0, 0)),
        ],
        out_shape=[
            jax.ShapeDtypeStruct((R, W_MIX), bf16),
            jax.ShapeDtypeStruct((depth, B, N_HEADS, D_HEAD, D_HEAD), f32),
            jax.ShapeDtypeStruct((depth, B, N_HEADS, 1, D_HEAD), f32),
            jax.ShapeDtypeStruct((depth, B, 8, LANES), f32),
            jax.ShapeDtypeStruct((depth, B, CONV_W - 1, W_MIX), f32),
        ],
        scratch_shapes=[
            pltpu.VMEM((CHUNK, W_MIX), bf16),
            pltpu.VMEM((N_HEADS, D_HEAD, D_HEAD + LANES), f32),
            pltpu.VMEM((8, LANES), f32),
        ],
        input_output_aliases=aliases,
        compiler_params=_cparams(("arbitrary", "arbitrary")),
        name="mlstm_prompt",
    )(*args, *xargs)


def _log_gammas():
    return [math.log(1.0 - 2.0 ** (-5.0 - h)) for h in range(N_HEADS)]


def _ret_kernel(q_ref, v_ref, zs_ref, kt_ref, cos_ref, sin_ref, cosT_ref, sinT_ref, hng_ref,
                o_ref, s_out_ref, s_s):
    g = pl.program_id(1)
    ng = pl.num_programs(1)
    Lg = q_ref.shape[0]
    L = CHUNK

    @pl.when(g == 0)
    def _():
        s_s[...] = jnp.zeros(s_s.shape, f32)

    row = lax.broadcasted_iota(jnp.int32, (L, L), 0)
    col = lax.broadcasted_iota(jnp.int32, (L, L), 1)
    causal = row >= col
    diff = (row - col).astype(f32)
    tcol = lax.broadcasted_iota(jnp.int32, (L, 1), 0).astype(f32)
    lgs = _log_gammas()
    intra = [jnp.where(causal, jnp.exp(diff * lg), 0.0) for lg in lgs]
    q_dec = [jnp.exp((tcol + 1.0) * lg) for lg in lgs]
    k_dec = [jnp.exp((L - 1.0 - tcol) * lg) for lg in lgs]
    s_dec = [math.exp(L * lg) for lg in lgs]
    kscale = D_HEAD ** -0.5

    for c in range(Lg // L):
        rows = slice(c * L, (c + 1) * L)
        cos = cos_ref[rows, :]
        sin = sin_ref[rows, :]
        cosT = cosT_ref[:, rows] * kscale
        sinT = sinT_ref[:, rows] * kscale
        heads = range(N_HEADS)
        cs = [slice(h * D_HEAD, (h + 1) * D_HEAD) for h in heads]
        lo = [slice(h * D_HEAD, h * D_HEAD + HALF) for h in heads]
        hi = [slice(h * D_HEAD + HALF, (h + 1) * D_HEAD) for h in heads]
        q1 = [q_ref[rows, lo[h]].astype(f32) for h in heads]
        q2 = [q_ref[rows, hi[h]].astype(f32) for h in heads]
        qb = [jnp.concatenate([q1[h] * cos - q2[h] * sin, q1[h] * sin + q2[h] * cos], axis=-1).astype(bf16)
              for h in heads]
        k1 = [kt_ref[lo[h], rows].astype(f32) for h in heads]
        k2 = [kt_ref[hi[h], rows].astype(f32) for h in heads]
        kTb = [jnp.concatenate([k1[h] * cosT - k2[h] * sinT, k1[h] * sinT + k2[h] * cosT],
                               axis=0).astype(bf16) for h in heads]
        v = [v_ref[rows, cs[h]] for h in heads]
        sc = [(_dot(qb[h], kTb[h]) * intra[h]).astype(bf16) for h in heads]
        s = [s_s[h] for h in heads]
        o = [_dot(sc[h], v[h]) + q_dec[h] * _dot(qb[h], s[h].astype(bf16)) for h in heads]
        for h in heads:
            s_s[h] = s_dec[h] * s[h] + _dot(kTb[h], (v[h].astype(f32) * k_dec[h]).astype(bf16))
        for h in heads:
            zs = zs_ref[rows, cs[h]].astype(f32)
            o_ref[rows, cs[h]] = (_head_norm(o[h], hng_ref[:, cs[h]]) * zs).astype(bf16)

    @pl.when(g == ng - 1)
    def _():
        s_out_ref[...] = s_s[...]


def _ret_prompt(P, KT, B, T, l, depth, prev, tabs, hng):
    R = P.shape[0]
    Lg = _pick(T, (1024, 512, 256, 128))
    ng = T // Lg
    col = lambda k: (lambda b, g: (b * ng + g, k))
    cos, sin, cosT, sinT = tabs
    in_specs = [
        pl.BlockSpec((Lg, W_MIX), col(3)),
        pl.BlockSpec((Lg, W_MIX), col(4)),
        pl.BlockSpec((Lg, W_MIX), col(5)),
        pl.BlockSpec((W_MIX, Lg), lambda b, g: (0, b * ng + g)),
        pl.BlockSpec((Lg, HALF), lambda b, g: (g, 0)),
        pl.BlockSpec((Lg, HALF), lambda b, g: (g, 0)),
        pl.BlockSpec((HALF, Lg), lambda b, g: (0, g)),
        pl.BlockSpec((HALF, Lg), lambda b, g: (0, g)),
        _layer_spec((1, W_MIX), l),
    ]
    args = [P, P, P, KT, cos, sin, cosT, sinT, hng]
    kern, xspecs, xargs, aliases = _chain(_ret_kernel, len(args), prev, 1)
    return pl.pallas_call(
        kern,
        grid=(B, ng),
        in_specs=in_specs + xspecs,
        out_specs=[
            pl.BlockSpec((Lg, W_MIX), col(0)),
            pl.BlockSpec((None, None, N_HEADS, D_HEAD, D_HEAD), lambda b, g: (l, b, 0, 0, 0)),
        ],
        out_shape=[
            jax.ShapeDtypeStruct((R, W_MIX), bf16),
            jax.ShapeDtypeStruct((depth, B, N_HEADS, D_HEAD, D_HEAD), f32),
        ],
        scratch_shapes=[pltpu.VMEM((N_HEADS, D_HEAD, D_HEAD), f32)],
        input_output_aliases=aliases,
        compiler_params=_cparams(("arbitrary", "arbitrary")),
        name="ret_prompt",
    )(*args, *xargs)


_SEQ_PER_STEP = 4


def _decode_kernel(um_ref, vm_ref, zsm_ref, qr_ref, vr_ref, zsr_ref, kt_ref, gi_ref,
                   conv_ref, n_ref, m_ref, c_ref, s_ref,
                   convw_ref, convb_ref, wq_ref, wk_ref, brow_ref, hngm_ref, skip_ref, hngr_ref,
                   cosd_ref, sind_ref, cosdT_ref, sindT_ref,
                   om_ref, or_ref, c_out_ref, s_out_ref, n_out_ref, m_out_ref, conv_out_ref,
                   vt_s, qrt_s, krt_s, qm_s, km_s, vr_s, sct_s, cqt_s, oacc_s, cm_s):
    s = pl.program_id(0)
    ns = pl.num_programs(0)
    NB = um_ref.shape[0]
    scale = D_HEAD ** -0.5
    lgs = _log_gammas()
    gammas = [math.exp(lg) for lg in lgs]
    lane = lax.broadcasted_iota(jnp.int32, (1, LANES), 1)

    @pl.when(s == 0)
    def _():
        u = um_ref[...].astype(f32)
        acc = convb_ref[...] + u * convw_ref[CONV_W - 1:CONV_W, :]
        for i in range(CONV_W - 1):
            acc = acc + conv_ref[i] * convw_ref[i:i + 1, :]
        cm = _silu(acc)
        cm_s[...] = cm
        for i in range(CONV_W - 2):
            conv_out_ref[i] = conv_ref[i + 1]
        conv_out_ref[CONV_W - 2] = u

        gb = jnp.concatenate([gi_ref[...], jnp.zeros((LANES - 8, NB), f32)], axis=0).T + brow_ref[...]
        lf = _log_sigmoid(gb)
        m_cols = jnp.concatenate([m_ref[...], jnp.zeros((8 - N_HEADS, NB), f32)], axis=0).T
        sc = jnp.zeros((NB, LANES), f32)
        mo = jnp.zeros((NB, LANES), f32)
        for h in range(N_HEADS):
            cs = slice(h * D_HEAD, (h + 1) * D_HEAD)
            chb = cm[:, cs].astype(bf16)
            q = _dot(chb, wq_ref[h])
            k = _dot(chb, wk_ref[h])
            qm_s[:, cs] = q
            km_s[:, cs] = k
            ig = gb[:, h:h + 1]
            lfh = lf[:, 4 + h:5 + h]
            mp = m_cols[:, h:h + 1]
            inter = lfh + mp
            m_t = jnp.maximum(inter, ig)
            w = jnp.exp(ig - m_t)
            a = jnp.exp(inter - m_t)
            nh = n_ref[:, cs]
            ws = w * jnp.sum(q * k, axis=-1, keepdims=True)
            den = ws + a * jnp.sum(nh * q, axis=-1, keepdims=True)
            den = jnp.maximum(jnp.abs(den), jnp.exp(-m_t))
            n_out_ref[:, cs] = a * nh + w * k
            for r, val in ((h, ws), (4 + h, a), (8 + h, den), (12 + h, w)):
                sc = sc + val * (lane == r).astype(f32)
            mo = mo + m_t * (lane == h).astype(f32)
        m_out_ref[...] = mo.T[0:N_HEADS, :]
        sct_s[...] = sc.T
        vt_s[...] = vm_ref[...].astype(f32).T
        vr_s[...] = vr_ref[...].astype(f32)
        cosd = cosd_ref[0:1, :]
        sind = sind_ref[0:1, :]
        cosdT = cosdT_ref[...]
        sindT = sindT_ref[...]
        for h in range(N_HEADS):
            lo = slice(h * D_HEAD, h * D_HEAD + HALF)
            hi = slice(h * D_HEAD + HALF, (h + 1) * D_HEAD)
            q1 = qr_ref[:, lo].astype(f32)
            q2 = qr_ref[:, hi].astype(f32)
            qrot = jnp.concatenate([q1 * cosd - q2 * sind, q1 * sind + q2 * cosd], axis=-1)
            qrt_s[h * D_HEAD:(h + 1) * D_HEAD, :] = qrot.T
            k1 = kt_ref[lo, :].astype(f32)
            k2 = kt_ref[hi, :].astype(f32)
            krt_s[lo, :] = (k1 * cosdT - k2 * sindT) * scale
            krt_s[hi, :] = (k1 * sindT + k2 * cosdT) * scale
        cqt_s[...] = jnp.zeros(cqt_s.shape, f32)

    def take_col(mat, mask):
        return jnp.sum(jnp.where(mask, mat, 0.0), axis=-1, keepdims=True)

    for j in range(_SEQ_PER_STEP):
        b = s * _SEQ_PER_STEP + j
        mask = lane == b
        for h in range(N_HEADS):
            cs = slice(h * D_HEAD, (h + 1) * D_HEAD)
            c = c_ref[j, h]
            q_row = qm_s[pl.ds(b, 1), cs]
            k_row = km_s[pl.ds(b, 1), cs]
            v_col = take_col(vt_s[cs, :], mask)
            a = take_col(sct_s[4 + h:5 + h, :], mask)
            w = take_col(sct_s[12 + h:13 + h, :], mask)
            cq = jnp.sum(c * q_row, axis=-1, keepdims=True)
            cqt_s[cs, :] = jnp.where(mask, cq, cqt_s[cs, :])
            c_out_ref[j, h] = a * c + (w * v_col) * k_row
            st = s_ref[j, h]
            q_col = take_col(qrt_s[cs, :], mask)
            k_col = take_col(krt_s[cs, :], mask)
            v_row = vr_s[pl.ds(b, 1), cs]
            qk = jnp.sum(q_col * k_col, axis=0, keepdims=True)
            o_row = qk * v_row + gammas[h] * jnp.sum(st * q_col, axis=0, keepdims=True)
            oacc_s[pl.ds(b, 1), cs] = o_row
            s_out_ref[j, h] = gammas[h] * st + k_col * v_row

    @pl.when(s == ns - 1)
    def _():
        for h in range(N_HEADS):
            cs = slice(h * D_HEAD, (h + 1) * D_HEAD)
            ht = ((sct_s[h:h + 1, :] * vt_s[cs, :] + sct_s[4 + h:5 + h, :] * cqt_s[cs, :])
                  / sct_s[8 + h:9 + h, :])
            hn = _head_norm(ht.T, hngm_ref[:, cs])
            zsm = zsm_ref[:, cs].astype(f32)
            om_ref[:, cs] = ((hn + skip_ref[:, cs] * cm_s[:, cs]) * zsm).astype(bf16)
            zsr = zsr_ref[:, cs].astype(f32)
            or_ref[:, cs] = (_head_norm(oacc_s[:, cs], hngr_ref[:, cs]) * zsr).astype(bf16)


def _decode_step(P, KT, GI, OM, OR, prev, l, BT, conv_st, n_st, m_st, C_st, S_st,
                 conv_w, conv_b, wq, wk, brow, hngm, skip, hngr, dtabs):
    R = P.shape[0]
    NB = R - BT
    assert NB == LANES and BT % NB == 0 and NB % _SEQ_PER_STEP == 0
    rb = BT // NB
    depth = C_st.shape[0]
    bb = _SEQ_PER_STEP
    cosd, sind, cosdT, sindT = dtabs
    pcol = lambda k: (lambda s: (rb, k))
    rep2 = lambda s: (0, 0)
    st_spec = pl.BlockSpec((None, bb, N_HEADS, D_HEAD, D_HEAD), lambda s: (l, s, 0, 0, 0))
    conv_spec = _layer_spec((CONV_W - 1, NB, W_MIX), l)
    n_spec = _layer_spec((NB, W_MIX), l)
    m_spec = _layer_spec((N_HEADS, NB), l)
    in_specs = [
        pl.BlockSpec((NB, W_MIX), pcol(0)), pl.BlockSpec((NB, W_MIX), pcol(1)),
        pl.BlockSpec((NB, W_MIX), pcol(2)), pl.BlockSpec((NB, W_MIX), pcol(3)),
        pl.BlockSpec((NB, W_MIX), pcol(4)), pl.BlockSpec((NB, W_MIX), pcol(5)),
        pl.BlockSpec((W_MIX, NB), lambda s: (0, rb)),
        pl.BlockSpec((2 * N_HEADS, NB), lambda s: (0, rb)),
        conv_spec, n_spec, m_spec, st_spec, st_spec,
        _layer_spec((CONV_W, W_MIX), l), _layer_spec((1, W_MIX), l),
        _layer_spec((N_HEADS, D_HEAD, D_HEAD), l), _layer_spec((N_HEADS, D_HEAD, D_HEAD), l),
        _layer_spec((1, LANES), l),
        _layer_spec((1, W_MIX), l), _layer_spec((1, W_MIX), l), _layer_spec((1, W_MIX), l),
        pl.BlockSpec((8, HALF), rep2), pl.BlockSpec((8, HALF), rep2),
        pl.BlockSpec((HALF, LANES), rep2), pl.BlockSpec((HALF, LANES), rep2),
    ]
    args = [P, P, P, P, P, P, KT, GI, conv_st, n_st, m_st, C_st, S_st,
            conv_w, conv_b, wq, wk, brow, hngm, skip, hngr, cosd, sind, cosdT, sindT]
    carried = [OM, OR] + (list(prev) if prev is not None else [])
    kern, xspecs, xargs, aliases = _chain(_decode_kernel, len(args), carried, 0)
    out_row = pl.BlockSpec((NB, W_MIX), lambda s: (rb, 0))
    return pl.pallas_call(
        kern,
        grid=(NB // bb,),
        in_specs=in_specs + xspecs,
        out_specs=[out_row, out_row, st_spec, st_spec, n_spec, m_spec, conv_spec],
        out_shape=[
            jax.ShapeDtypeStruct(OM.shape, bf16),
            jax.ShapeDtypeStruct(OR.shape, bf16),
            jax.ShapeDtypeStruct((depth, NB, N_HEADS, D_HEAD, D_HEAD), f32),
            jax.ShapeDtypeStruct((depth, NB, N_HEADS, D_HEAD, D_HEAD), f32),
            jax.ShapeDtypeStruct((depth, NB, W_MIX), f32),
            jax.ShapeDtypeStruct((depth, N_HEADS, NB), f32),
            jax.ShapeDtypeStruct((depth, CONV_W - 1, NB, W_MIX), f32),
        ],
        scratch_shapes=[
            pltpu.VMEM((W_MIX, NB), f32),
            pltpu.VMEM((W_MIX, NB), f32),
            pltpu.VMEM((W_MIX, NB), f32),
            pltpu.VMEM((NB, W_MIX), f32),
            pltpu.VMEM((NB, W_MIX), f32),
            pltpu.VMEM((NB, W_MIX), f32),
            pltpu.VMEM((LANES, NB), f32),
            pltpu.VMEM((W_MIX, NB), f32),
            pltpu.VMEM((NB, W_MIX), f32),
            pltpu.VMEM((NB, W_MIX), f32),
        ],
        input_output_aliases=aliases,
        compiler_params=_cparams(("arbitrary",)),
        name="decode_step",
    )(*args, *xargs)


def _merge_kernel(h_ref, om_ref, or_ref, wgm_ref, wgr_ref, bgm_ref, bgr_ref, wpm_ref, wpr_ref, o_ref):
    h = h_ref[...]
    gm = jax.nn.sigmoid(_dot(h, wgm_ref[...].astype(bf16)) + bgm_ref[...])
    gr = jax.nn.sigmoid(_dot(h, wgr_ref[...].astype(bf16)) + bgr_ref[...])
    pm = _dot(om_ref[...], wpm_ref[...].astype(bf16))
    pr = _dot(or_ref[...], wpr_ref[...].astype(bf16))
    o_ref[...] = (gm * pm + gr * pr).astype(bf16)


def _merge(h, OM, OR, w_gate, b_gate3, w_pm, w_pr, l):
    R = h.shape[0]
    tm = _pick(R, (1664, 1280, 640, 384, 128))
    tn = 256
    nj = D_MODEL // tn
    return pl.pallas_call(
        _merge_kernel,
        grid=(R // tm, nj),
        in_specs=[
            pl.BlockSpec((tm, D_MODEL), lambda i, j: (i, 0)),
            pl.BlockSpec((tm, W_MIX), lambda i, j: (i, 0)),
            pl.BlockSpec((tm, W_MIX), lambda i, j: (i, 0)),
            pl.BlockSpec((None, D_MODEL, tn), lambda i, j: (l, 0, j)),
            pl.BlockSpec((None, D_MODEL, tn), lambda i, j: (l, 0, j + nj)),
            pl.BlockSpec((None, 1, tn), lambda i, j: (l, 0, j)),
            pl.BlockSpec((None, 1, tn), lambda i, j: (l, 0, j + nj)),
            pl.BlockSpec((None, W_MIX, tn), lambda i, j: (l, 0, j)),
            pl.BlockSpec((None, W_MIX, tn), lambda i, j: (l, 0, j)),
        ],
        out_specs=pl.BlockSpec((tm, tn), lambda i, j: (i, j)),
        out_shape=jax.ShapeDtypeStruct((R, D_MODEL), bf16),
        compiler_params=_cparams(("arbitrary", "arbitrary")),
        name="merge",
    )(h, OM, OR, w_gate, w_gate, b_gate3, b_gate3, w_pm, w_pr)


def _row_halves(tm):
    return (slice(0, tm // 2), slice(tm // 2, tm))


def _outproj_norm_kernel(mg_ref, wo_ref, x_ref, g_ref, o_ref, h_ref):
    for r in _row_halves(x_ref.shape[0]):
        y = x_ref[r, :] + _dot(mg_ref[r, :], wo_ref[...])
        o_ref[r, :] = y
        ms = jnp.mean(y * y, axis=-1, keepdims=True)
        h_ref[r, :] = (y * lax.rsqrt(ms + EPS) * g_ref[...]).astype(bf16)


def _outproj_final_kernel(mg_ref, wo_ref, x_ref, g_ref, y_ref):
    for r in _row_halves(x_ref.shape[0]):
        y = x_ref[r, :] + _dot(mg_ref[r, :], wo_ref[...])
        ms = jnp.mean(y * y, axis=-1, keepdims=True)
        y_ref[r, :] = y * lax.rsqrt(ms + EPS) * g_ref[...]


def _wo_spec():
    return pl.BlockSpec((D_MODEL, D_MODEL), lambda i: (0, 0), pipeline_mode=pl.Buffered(1))


def _outproj(merged, w_o_bf, x, norm_g3, l_next, row0, nrows, x_local, prev):
    R = merged.shape[0]
    tm = _pick(nrows, (640, 512, 384, 128))
    assert row0 % tm == 0
    off = row0 // tm
    row_g = pl.BlockSpec((tm, D_MODEL), lambda i: (i + off, 0))
    row_x = pl.BlockSpec((tm, D_MODEL), lambda i: (i, 0)) if x_local else row_g
    kern, xspecs, xargs, aliases = _chain(_outproj_norm_kernel, 4, prev, 0)
    return pl.pallas_call(
        kern,
        grid=(nrows // tm,),
        in_specs=[row_g, _wo_spec(), row_x, _layer_spec((1, D_MODEL), l_next)] + xspecs,
        out_specs=[row_g, row_g],
        out_shape=[jax.ShapeDtypeStruct((R, D_MODEL), f32), jax.ShapeDtypeStruct((R, D_MODEL), bf16)],
        input_output_aliases=aliases,
        compiler_params=_cparams(("arbitrary",)),
        name="outproj_norm",
    )(merged, w_o_bf, x, norm_g3, *xargs)


def _outproj_final(merged, w_o_bf, x, g, row0, nrows):
    tm = _pick(nrows, (512, 256, 128))
    assert row0 % tm == 0
    off = row0 // tm
    row_in = pl.BlockSpec((tm, D_MODEL), lambda i: (i + off, 0))
    return pl.pallas_call(
        _outproj_final_kernel,
        grid=(nrows // tm,),
        in_specs=[row_in, _wo_spec(), row_in, pl.BlockSpec((1, D_MODEL), lambda i: (0, 0))],
        out_specs=pl.BlockSpec((tm, D_MODEL), lambda i: (i, 0)),
        out_shape=jax.ShapeDtypeStruct((nrows, D_MODEL), f32),
        compiler_params=_cparams(("arbitrary",)),
        name="outproj_final",
    )(merged, w_o_bf, x, g)


def kernel(x_prompt, x_sample, state_mlstm_C, state_mlstm_n, state_mlstm_m, state_mlstm_conv, state_ret_S,
           norm_g, w_in, conv_w, conv_b, w_qm, w_km, b_i, b_f, hn_m_g, skip_m, hn_r_g,
           w_pm, w_pr, w_gate, b_gate, w_o, final_g):
    B, T, D = x_prompt.shape
    NB = x_sample.shape[0]
    depth = w_in.shape[0]
    BT = B * T
    assert D == D_MODEL and x_sample.shape[1] == 1 and T % CHUNK == 0 and depth >= 2

    R = BT + NB
    xp = x_prompt.reshape(BT, D)
    xs = x_sample.reshape(NB, D)

    w_inT = jnp.swapaxes(w_in, 1, 2)
    assert math.log2(D_HEAD) % 2 == 0
    wq = (w_qm * D_HEAD ** -0.5).astype(bf16)
    wk = w_km.astype(bf16)
    wkT = jnp.swapaxes(w_km, 2, 3).astype(bf16)
    bias = jnp.pad(jnp.concatenate([b_i, b_f], axis=1), ((0, 0), (0, LANES - 2 * N_HEADS)))
    brow = bias.reshape(depth, 1, LANES)
    bcol = bias[:, :8].reshape(depth, 8, 1)
    b_gate3 = b_gate.reshape(depth, 1, 2 * D)
    norm_g3 = norm_g.reshape(depth, 1, D)
    conv_b3 = conv_b.reshape(depth, 1, W_MIX)
    hngm = hn_m_g.reshape(depth, 1, W_MIX)
    skp = skip_m.reshape(depth, 1, W_MIX)
    hngr = hn_r_g.reshape(depth, 1, W_MIX)
    conv_st = jnp.transpose(state_mlstm_conv, (0, 2, 1, 3))
    m_st = jnp.swapaxes(state_mlstm_m, 1, 2)
    n_st = state_mlstm_n.reshape(depth, NB, W_MIX)

    cos, sin, cosT, sinT, cosd, sind, cosdT, sindT = _rope_tables(T)

    prev_m = prev_r = prev_d = None
    h = _rmsnorm_into(xp, norm_g3, 0, R, 0, None)
    h = _rmsnorm_into(xs, norm_g3, 0, R, BT, h)
    x = None
    for l in range(depth):
        P, KT, GI, w_o_bf = _inproj(h, w_inT, w_o, l)
        OM, *prev_m = _mlstm_prompt(P, GI, B, T, l, depth, prev_m, conv_w, conv_b3, wq, wkT,
                                    bcol, hngm, skp)
        OR, *prev_r = _ret_prompt(P, KT, B, T, l, depth, prev_r, (cos, sin, cosT, sinT), hngr)
        OM, OR, *prev_d = _decode_step(
            P, KT, GI, OM, OR, prev_d, l, BT, conv_st, n_st, m_st, state_mlstm_C, state_ret_S,
            conv_w, conv_b3, wq, wk, brow, hngm, skp, hngr, (cosd, sind, cosdT, sindT))
        merged = _merge(h, OM, OR, w_gate, b_gate3, w_pm, w_pr, l)
        if l + 1 == depth:
            break
        if x is None:
            x, h = _outproj(merged, w_o_bf, xp, norm_g3, l + 1, 0, BT, True, None)
            x, h = _outproj(merged, w_o_bf, xs, norm_g3, l + 1, BT, NB, True, (x, h))
        else:
            x, h = _outproj(merged, w_o_bf, x, norm_g3, l + 1, 0, R, False, None)

    fg = final_g.reshape(1, D)
    y_prompt = _outproj_final(merged, w_o_bf, x, fg, 0, BT).reshape(B, T, D)
    y_sample = _outproj_final(merged, w_o_bf, x, fg, BT, NB).reshape(NB, 1, D)
    Cp, np_, mp, convp = prev_m
    (Sp,) = prev_r
    Cs, Ss, n_s, m_s, conv_s = prev_d
    return (y_prompt, y_sample,
            Cp, np_.reshape(depth, B, N_HEADS, D_HEAD), mp[:, :, 0, 12:12 + N_HEADS], convp, Sp,
            Cs, n_s.reshape(depth, NB, N_HEADS, D_HEAD), jnp.swapaxes(m_s, 1, 2),
            jnp.transpose(conv_s, (0, 2, 1, 3)), Ss)
```

```python
import functools
import math

import jax
import jax.numpy as jnp
from jax import lax
from jax.experimental import pallas as pl
from jax.experimental.pallas import tpu as pltpu

f32 = jnp.float32
bf16 = jnp.bfloat16

D_MODEL = 2048
N_HEADS = 4
D_HEAD = 256
W_MIX = N_HEADS * D_HEAD
CONV_W = 4
CHUNK = 128
ROPE_BASE = 10000.0
EPS = 1e-6
NEG = -1e30
PAST_LEN = 16384
LANES = 128
HALF = D_HEAD // 2
VMEM_LIMIT = 62 * 1024 * 1024

OFF_IF = 3 * W_MIX
OFF_RET = 3 * W_MIX + 2 * N_HEADS


def _pick(n, cands):
    for c in cands:
        if n % c == 0:
            return c
    return n


def _cparams(sem):
    return pltpu.CompilerParams(dimension_semantics=sem, vmem_limit_bytes=VMEM_LIMIT)


def _dot(a, b):
    return jnp.dot(a, b, preferred_element_type=f32)


def _dot_nt(a, b):
    return lax.dot_general(a, b, (((1,), (1,)), ((), ())), preferred_element_type=f32)


def _split3(x):
    hi = x.astype(bf16).astype(f32)
    r = x - hi
    mid = r.astype(bf16).astype(f32)
    lo = (r - mid).astype(bf16).astype(f32)
    return hi, mid, lo


def _silu(x):
    return x * (0.5 * jnp.tanh(0.5 * x) + 0.5)


def _log_sigmoid(x):
    return jnp.minimum(x, 0.0) - jnp.log1p(jnp.exp(-jnp.abs(x)))


def _head_norm(hc, g):
    mu = jnp.mean(hc, axis=-1, keepdims=True)
    d = hc - mu
    var = jnp.mean(d * d, axis=-1, keepdims=True)
    return d * lax.rsqrt(var + EPS) * g


def _rope_kernel(inv_ref, invT_ref, cos_ref, sin_ref, cosT_ref, sinT_ref,
                 cosd_ref, sind_ref, cosdT_ref, sindT_ref):
    T = cos_ref.shape[0]
    pos = lax.broadcasted_iota(jnp.int32, (T, HALF), 0).astype(f32)
    ang = pos * inv_ref[...]
    cos = jnp.cos(ang)
    sin = jnp.sin(ang)
    cos_ref[...] = cos
    sin_ref[...] = sin
    cosT_ref[...] = cos.T
    sinT_ref[...] = sin.T
    angd = jnp.full((8, HALF), float(PAST_LEN), f32) * inv_ref[...]
    cosd_ref[...] = jnp.cos(angd)
    sind_ref[...] = jnp.sin(angd)
    angdT = jnp.full((HALF, LANES), float(PAST_LEN), f32) * invT_ref[...]
    cosdT_ref[...] = jnp.cos(angdT)
    sindT_ref[...] = jnp.sin(angdT)


def _rope_tables(T):
    inv = ROPE_BASE ** (-jnp.arange(0, HALF, dtype=f32) / HALF)
    shapes = [(T, HALF), (T, HALF), (HALF, T), (HALF, T),
              (8, HALF), (8, HALF), (HALF, LANES), (HALF, LANES)]
    return pl.pallas_call(
        _rope_kernel,
        out_shape=[jax.ShapeDtypeStruct(s, f32) for s in shapes],
        name="rope_tables",
    )(inv.reshape(1, HALF), inv.reshape(HALF, 1))


def _rms_kernel(x_ref, g_ref, o_ref):
    x = x_ref[...]
    ms = jnp.mean(x * x, axis=-1, keepdims=True)
    o_ref[...] = (x * lax.rsqrt(ms + EPS) * g_ref[...]).astype(o_ref.dtype)


def _rmsnorm_into(x, norm_g3, l, R, row0, prev):
    n, D = x.shape
    tr = _pick(n, (512, 256, 128))
    assert row0 % tr == 0
    off = row0 // tr
    kern, xspecs, xargs, aliases = _chain(_rms_kernel, 2, None if prev is None else [prev], 0)
    return pl.pallas_call(
        kern,
        grid=(n // tr,),
        in_specs=[pl.BlockSpec((tr, D), lambda i: (i, 0)), _layer_spec((1, D), l)] + xspecs,
        out_specs=pl.BlockSpec((tr, D), lambda i: (i + off, 0)),
        out_shape=jax.ShapeDtypeStruct((R, D), bf16),
        input_output_aliases=aliases,
        compiler_params=_cparams(("arbitrary",)),
        name="rmsnorm",
    )(x, norm_g3, *xargs)


_TN_IN = 512
_TPG = W_MIX // _TN_IN
_NJ_P = 6 * _TPG
_NJ_K = _TPG
_GATE_PAD = 16


def _inproj_row_offset(j):
    jp = jnp.minimum(j, _NJ_P - 1)
    grp = jp // _TPG
    sub = jp - grp * _TPG
    src_grp = jnp.where(grp < 3, grp, jnp.where(grp == 3, 3, grp + 1))
    off_p = src_grp * W_MIX + jnp.where(grp < 3, 0, 2 * N_HEADS) + sub * _TN_IN
    off_k = OFF_RET + W_MIX + jnp.clip(j - _NJ_P, 0, _NJ_K - 1) * _TN_IN
    return jnp.where(j < _NJ_P, off_p, off_k)


def _inproj_kernel(h_ref, wt_ref, wif_ref, wo_ref, p_ref, kt_ref, gi_ref, wob_ref):
    j = pl.program_id(1)
    wob_ref[...] = wo_ref[...].astype(bf16)

    def weights():
        return wt_ref[0].astype(bf16)

    is_z = (j // _TPG == 2) | (j // _TPG == 5)
    last = _NJ_P + _NJ_K - 1

    @pl.when((j < _NJ_P) & jnp.logical_not(is_z))
    def _():
        p_ref[...] = _dot_nt(h_ref[...], weights()).astype(bf16)

    @pl.when((j < _NJ_P) & is_z)
    def _():
        p_ref[...] = _silu(_dot_nt(h_ref[...], weights())).astype(bf16)

    @pl.when((j >= _NJ_P) & (j < last))
    def _():
        kt_ref[...] = _dot_nt(weights(), h_ref[...]).astype(bf16)

    @pl.when(j == last)
    def _():
        pad = jnp.zeros((_GATE_PAD - 2 * N_HEADS, D_MODEL), f32)
        w_all = jnp.concatenate([wt_ref[0], wif_ref[...], pad], axis=0).astype(bf16)
        res = _dot_nt(w_all, h_ref[...])
        kt_ref[...] = res[0:_TN_IN, :].astype(bf16)
        gi_ref[...] = res[_TN_IN:_TN_IN + 2 * N_HEADS, :]


def _inproj(h, w_inT, w_o, l):
    R = h.shape[0]
    tm = _pick(R, (1664, 1280, 640, 384, 128))
    tn = _TN_IN
    nj = _NJ_P + _NJ_K
    nsteps = (R // tm) * nj
    rps = next(r for r in (32, 64, 128, 256, 512, 1024, 2048) if D_MODEL // r <= nsteps)
    wo_blk = lambda i, j: jnp.minimum(i * nj + j, D_MODEL // rps - 1)
    return pl.pallas_call(
        _inproj_kernel,
        grid=(R // tm, nj),
        in_specs=[
            pl.BlockSpec((tm, D_MODEL), lambda i, j: (i, 0)),
            pl.BlockSpec((pl.Element(1), pl.Element(tn), pl.Element(D_MODEL)),
                         lambda i, j: (l, pl.multiple_of(_inproj_row_offset(j), 8), 0)),
            pl.BlockSpec((None, 2 * N_HEADS, D_MODEL), lambda i, j: (l, OFF_IF // (2 * N_HEADS), 0)),
            pl.BlockSpec((None, rps, D_MODEL), lambda i, j: (l, wo_blk(i, j), 0)),
        ],
        out_specs=[
            pl.BlockSpec((tm, tn), lambda i, j: (i, jnp.minimum(j, _NJ_P - 1))),
            pl.BlockSpec((tn, tm), lambda i, j: (jnp.clip(j - _NJ_P, 0, _NJ_K - 1), i)),
            pl.BlockSpec((2 * N_HEADS, tm), lambda i, j: (0, i)),
            pl.BlockSpec((rps, D_MODEL), lambda i, j: (wo_blk(i, j), 0)),
        ],
        out_shape=[
            jax.ShapeDtypeStruct((R, 6 * W_MIX), bf16),
            jax.ShapeDtypeStruct((W_MIX, R), bf16),
            jax.ShapeDtypeStruct((2 * N_HEADS, R), f32),
            jax.ShapeDtypeStruct((D_MODEL, D_MODEL), bf16),
        ],
        compiler_params=_cparams(("arbitrary", "arbitrary")),
        name="inproj",
    )(h, w_inT, w_inT, w_o)


def _mlstm_kernel(um_ref, vm_ref, zs_ref, gi_ref, convw_ref, convb_ref, wq_ref, wkT_ref,
                  bcol_ref, hng_ref, skip_ref,
                  om_ref, c_out_ref, n_out_ref, m_out_ref, conv_out_ref,
                  tail_s, cta_s, m_s):
    g = pl.program_id(1)
    ng = pl.num_programs(1)
    Lg = um_ref.shape[0]
    L = CHUNK
    DA = D_HEAD + LANES
    heads = range(N_HEADS)

    @pl.when(g == 0)
    def _():
        tail_s[...] = jnp.zeros(tail_s.shape, bf16)
        cta_s[...] = jnp.zeros(cta_s.shape, f32)
        m_s[...] = jnp.full(m_s.shape, NEG, f32)

    srow = lax.broadcasted_iota(jnp.int32, (L, 2 * L), 0)
    scol = lax.broadcasted_iota(jnp.int32, (L, 2 * L), 1)
    shift_mat = jnp.concatenate([(scol == srow + L - s).astype(bf16) for s in range(1, CONV_W)], axis=0)

    row = lax.broadcasted_iota(jnp.int32, (L, L), 0)
    col = lax.broadcasted_iota(jnp.int32, (L, L), 1)
    causal = row >= col
    triu = (row <= col).astype(bf16)
    lane = col
    bcol = bcol_ref[...]
    ones_b = jnp.ones((L, LANES), bf16)
    ones_f = jnp.ones((L, LANES), f32)

    for c in range(Lg // L):
        rows = slice(c * L, (c + 1) * L)
        gtb = gi_ref[:, rows] + bcol
        lft = _log_sigmoid(gtb)
        br3 = _dot(jnp.concatenate(_split3(lft), axis=0).astype(bf16), triu)
        br_all = br3[0:8, :] + br3[8:16, :] + br3[16:24, :]
        cols = jnp.concatenate([gtb, br_all, jnp.zeros((LANES - 16, L), f32)], axis=0).T

        u_cur = um_ref[rows, :]
        u_prev = tail_s[...] if c == 0 else um_ref[(c - 1) * L:c * L, :]
        delayed = _dot(shift_mat, jnp.concatenate([u_prev, u_cur], axis=0))
        acc = convb_ref[...] + u_cur.astype(f32) * convw_ref[CONV_W - 1:CONV_W, :]
        for s in range(1, CONV_W):
            acc = acc + delayed[(s - 1) * L:s * L, :] * convw_ref[CONV_W - 1 - s:CONV_W - s, :]
        cm = _silu(acc)

        cs = [slice(h * D_HEAD, (h + 1) * D_HEAD) for h in heads]
        ch = [cm[:, cs[h]] for h in heads]
        chb = [x.astype(bf16) for x in ch]
        qb = [_dot(chb[h], wq_ref[h]).astype(bf16) for h in heads]
        kTb = [_dot_nt(wkT_ref[h], chb[h]).astype(bf16) for h in heads]
        v = [vm_ref[rows, cs[h]] for h in heads]
        bc = [cols[:, 12 + h:13 + h] for h in heads]
        dlog = [jnp.where(causal, bc[h] - br_all[4 + h:5 + h, :] + gtb[h:h + 1, :], NEG) for h in heads]
        dmax = jnp.zeros((L, LANES), f32)
        for h in heads:
            dmax = jnp.where(lane == 12 + h, jnp.max(dlog[h], axis=-1, keepdims=True), dmax)
        m_row = m_s[0:1, :]
        inter = cols + m_row
        m_all = jnp.maximum(inter, dmax)
        a_all = jnp.exp(inter - m_all)
        em_all = jnp.exp(-m_all)
        m_new = m_all[L - 1:L, :]
        b_last = cols[L - 1:L, :]
        ic_al = pltpu.roll(cols, 12, axis=1)
        wk_all = jnp.exp(b_last - cols + ic_al - m_new)
        dec_row = jnp.exp(b_last + m_row - m_new)
        m_s[...] = jnp.broadcast_to(m_new, (8, LANES))
        hl = [slice(12 + h, 13 + h) for h in heads]
        ws = [(jnp.exp(dlog[h] - m_all[:, hl[h]]) * _dot(qb[h], kTb[h])).astype(bf16) for h in heads]
        cta = [cta_s[h] for h in heads]
        nd = [_dot(ws[h], jnp.concatenate([v[h], ones_b], axis=1))
              + a_all[:, hl[h]] * _dot(qb[h], cta[h].astype(bf16)) for h in heads]
        for h in heads:
            vaug = jnp.concatenate([v[h].astype(f32), ones_f], axis=1)
            cta_s[h] = dec_row[:, hl[h]] * cta[h] + _dot(kTb[h], (wk_all[:, hl[h]] * vaug).astype(bf16))
        for h in heads:
            den = jnp.maximum(jnp.abs(nd[h][:, D_HEAD:DA]), em_all[:, hl[h]])
            rden = 1.0 / den
            hc = jnp.concatenate([nd[h][:, 0:HALF] * rden, nd[h][:, HALF:D_HEAD] * rden], axis=1)
            hn = _head_norm(hc, hng_ref[:, cs[h]])
            zs = zs_ref[rows, cs[h]].astype(f32)
            om_ref[rows, cs[h]] = ((hn + skip_ref[:, cs[h]] * ch[h]) * zs).astype(bf16)

    tail_s[...] = um_ref[Lg - L:Lg, :]

    @pl.when(g == ng - 1)
    def _():
        for h in heads:
            st = cta_s[h]
            c_out_ref[h] = st[:, 0:D_HEAD].T
            n_out_ref[h] = st[:, D_HEAD:DA].T[0:1, :]
        m_out_ref[...] = m_s[...]
        conv_out_ref[...] = um_ref[Lg - 8:Lg, :].astype(f32)[8 - (CONV_W - 1):8, :]


def _drop_refs(kernel_fn, n_in, n_drop, *refs):
    return kernel_fn(*refs[:n_in], *refs[n_in + n_drop:])


def _chain(kernel_fn, n_in, prev, first_out):
    if prev is None:
        return kernel_fn, [], [], {}
    specs = [pl.BlockSpec(memory_space=pl.ANY)] * len(prev)
    aliases = {n_in + k: first_out + k for k in range(len(prev))}
    return functools.partial(_drop_refs, kernel_fn, n_in, len(prev)), specs, list(prev), aliases


def _layer_spec(shape, l):
    nd = len(shape)
    return pl.BlockSpec((None,) + tuple(shape), lambda *_: (l,) + (0,) * nd)


def _mlstm_prompt(P, GI, B, T, l, depth, prev, conv_w, conv_b, wq, wkT, bcol, hng, skip):
    R = P.shape[0]
    Lg = _pick(T, (512, 256, 128))
    ng = T // Lg
    col = lambda k: (lambda b, g: (b * ng + g, k))
    in_specs = [
        pl.BlockSpec((Lg, W_MIX), col(0)),
        pl.BlockSpec((Lg, W_MIX), col(1)),
        pl.BlockSpec((Lg, W_MIX), col(2)),
        pl.BlockSpec((2 * N_HEADS, Lg), lambda b, g: (0, b * ng + g)),
        _layer_spec((CONV_W, W_MIX), l),
        _layer_spec((1, W_MIX), l),
        _layer_spec((N_HEADS, D_HEAD, D_HEAD), l),
        _layer_spec((N_HEADS, D_HEAD, D_HEAD), l),
        _layer_spec((8, 1), l),
        _layer_spec((1, W_MIX), l),
        _layer_spec((1, W_MIX), l),
    ]
    args = [P, P, P, GI, conv_w, conv_b, wq, wkT, bcol, hng, skip]
    kern, xspecs, xargs, aliases = _chain(_mlstm_kernel, len(args), prev, 1)
    return pl.pallas_call(
        kern,
        grid=(B, ng),
        in_specs=in_specs + xspecs,
        out_specs=[
            pl.BlockSpec((Lg, W_MIX), col(0)),
            pl.BlockSpec((None, None, N_HEADS, D_HEAD, D_HEAD), lambda b, g: (l, b, 0, 0, 0)),
            pl.BlockSpec((None, None, N_HEADS, 1, D_HEAD), lambda b, g: (l, b, 0, 0, 0)),
            pl.BlockSpec((None, None, 8, LANES), lambda b, g: (l, b, 0, 0)),
            pl.BlockSpec((None, None, CONV_W - 1, W_MIX), lambda b, g: (l, b, 0, 0)),
        ],
        out_shape=[
            jax.ShapeDtypeStruct((R, W_MIX), bf16),
            jax.ShapeDtypeStruct((depth, B, N_HEADS, D_HEAD, D_HEAD), f32),
            jax.ShapeDtypeStruct((depth, B, N_HEADS, 1, D_HEAD), f32),
            jax.ShapeDtypeStruct((depth, B, 8, LANES), f32),
            jax.ShapeDtypeStruct((depth, B, CONV_W - 1, W_MIX), f32),
        ],
        scratch_shapes=[
            pltpu.VMEM((CHUNK, W_MIX), bf16),
            pltpu.VMEM((N_HEADS, D_HEAD, D_HEAD + LANES), f32),
            pltpu.VMEM((8, LANES), f32),
        ],
        input_output_aliases=aliases,
        compiler_params=_cparams(("arbitrary", "arbitrary")),
        name="mlstm_prompt",
    )(*args, *xargs)


def _log_gammas():
    return [math.log(1.0 - 2.0 ** (-5.0 - h)) for h in range(N_HEADS)]


def _ret_kernel(q_ref, v_ref, zs_ref, kt_ref, cos_ref, sin_ref, cosT_ref, sinT_ref, hng_ref,
                o_ref, s_out_ref, s_s):
    g = pl.program_id(1)
    ng = pl.num_programs(1)
    Lg = q_ref.shape[0]
    L = CHUNK

    @pl.when(g == 0)
    def _():
        s_s[...] = jnp.zeros(s_s.shape, f32)

    row = lax.broadcasted_iota(jnp.int32, (L, L), 0)
    col = lax.broadcasted_iota(jnp.int32, (L, L), 1)
    causal = row >= col
    diff = (row - col).astype(f32)
    tcol = lax.broadcasted_iota(jnp.int32, (L, 1), 0).astype(f32)
    lgs = _log_gammas()
    intra = [jnp.where(causal, jnp.exp(diff * lg), 0.0) for lg in lgs]
    q_dec = [jnp.exp((tcol + 1.0) * lg) for lg in lgs]
    k_dec = [jnp.exp((L - 1.0 - tcol) * lg) for lg in lgs]
    s_dec = [math.exp(L * lg) for lg in lgs]
    kscale = D_HEAD ** -0.5

    for c in range(Lg // L):
        rows = slice(c * L, (c + 1) * L)
        cos = cos_ref[rows, :]
        sin = sin_ref[rows, :]
        cosT = cosT_ref[:, rows] * kscale
        sinT = sinT_ref[:, rows] * kscale
        heads = range(N_HEADS)
        cs = [slice(h * D_HEAD, (h + 1) * D_HEAD) for h in heads]
        lo = [slice(h * D_HEAD, h * D_HEAD + HALF) for h in heads]
        hi = [slice(h * D_HEAD + HALF, (h + 1) * D_HEAD) for h in heads]
        q1 = [q_ref[rows, lo[h]].astype(f32) for h in heads]
        q2 = [q_ref[rows, hi[h]].astype(f32) for h in heads]
        qb = [jnp.concatenate([q1[h] * cos - q2[h] * sin, q1[h] * sin + q2[h] * cos], axis=-1).astype(bf16)
              for h in heads]
        k1 = [kt_ref[lo[h], rows].astype(f32) for h in heads]
        k2 = [kt_ref[hi[h], rows].astype(f32) for h in heads]
        kTb = [jnp.concatenate([k1[h] * cosT - k2[h] * sinT, k1[h] * sinT + k2[h] * cosT],
                               axis=0).astype(bf16) for h in heads]
        v = [v_ref[rows, cs[h]] for h in heads]
        sc = [(_dot(qb[h], kTb[h]) * intra[h]).astype(bf16) for h in heads]
        s = [s_s[h] for h in heads]
        o = [_dot(sc[h], v[h]) + q_dec[h] * _dot(qb[h], s[h].astype(bf16)) for h in heads]
        for h in heads:
            s_s[h] = s_dec[h] * s[h] + _dot(kTb[h], (v[h].astype(f32) * k_dec[h]).astype(bf16))
        for h in heads:
            zs = zs_ref[rows, cs[h]].astype(f32)
            o_ref[rows, cs[h]] = (_head_norm(o[h], hng_ref[:, cs[h]]) * zs).astype(bf16)

    @pl.when(g == ng - 1)
    def _():
        s_out_ref[...] = s_s[...]


def _ret_prompt(P, KT, B, T, l, depth, prev, tabs, hng):
    R = P.shape[0]
    Lg = _pick(T, (1024, 512, 256, 128))
    ng = T // Lg
    col = lambda k: (lambda b, g: (b * ng + g, k))
    cos, sin, cosT, sinT = tabs
    in_specs = [
        pl.BlockSpec((Lg, W_MIX), col(3)),
        pl.BlockSpec((Lg, W_MIX), col(4)),
        pl.BlockSpec((Lg, W_MIX), col(5)),
        pl.BlockSpec((W_MIX, Lg), lambda b, g: (0, b * ng + g)),
        pl.BlockSpec((Lg, HALF), lambda b, g: (g, 0)),
        pl.BlockSpec((Lg, HALF), lambda b, g: (g, 0)),
        pl.BlockSpec((HALF, Lg), lambda b, g: (0, g)),
        pl.BlockSpec((HALF, Lg), lambda b, g: (0, g)),
        _layer_spec((1, W_MIX), l),
    ]
    args = [P, P, P, KT, cos, sin, cosT, sinT, hng]
    kern, xspecs, xargs, aliases = _chain(_ret_kernel, len(args), prev, 1)
    return pl.pallas_call(
        kern,
        grid=(B, ng),
        in_specs=in_specs + xspecs,
        out_specs=[
            pl.BlockSpec((Lg, W_MIX), col(0)),
            pl.BlockSpec((None, None, N_HEADS, D_HEAD, D_HEAD), lambda b, g: (l, b, 0, 0, 0)),
        ],
        out_shape=[
            jax.ShapeDtypeStruct((R, W_MIX), bf16),
            jax.ShapeDtypeStruct((depth, B, N_HEADS, D_HEAD, D_HEAD), f32),
        ],
        scratch_shapes=[pltpu.VMEM((N_HEADS, D_HEAD, D_HEAD), f32)],
        input_output_aliases=aliases,
        compiler_params=_cparams(("arbitrary", "arbitrary")),
        name="ret_prompt",
    )(*args, *xargs)


_SEQ_PER_STEP = 4


def _decode_kernel(um_ref, vm_ref, zsm_ref, qr_ref, vr_ref, zsr_ref, kt_ref, gi_ref,
                   conv_ref, n_ref, m_ref, c_ref, s_ref,
                   convw_ref, convb_ref, wq_ref, wk_ref, brow_ref, hngm_ref, skip_ref, hngr_ref,
                   cosd_ref, sind_ref, cosdT_ref, sindT_ref,
                   om_ref, or_ref, c_out_ref, s_out_ref, n_out_ref, m_out_ref, conv_out_ref,
                   vt_s, qrt_s, krt_s, qm_s, km_s, vr_s, sct_s, cqt_s, oacc_s, cm_s):
    s = pl.program_id(0)
    ns = pl.num_programs(0)
    NB = um_ref.shape[0]
    scale = D_HEAD ** -0.5
    lgs = _log_gammas()
    gammas = [math.exp(lg) for lg in lgs]
    lane = lax.broadcasted_iota(jnp.int32, (1, LANES), 1)

    @pl.when(s == 0)
    def _():
        u = um_ref[...].astype(f32)
        acc = convb_ref[...] + u * convw_ref[CONV_W - 1:CONV_W, :]
        for i in range(CONV_W - 1):
            acc = acc + conv_ref[i] * convw_ref[i:i + 1, :]
        cm = _silu(acc)
        cm_s[...] = cm
        for i in range(CONV_W - 2):
            conv_out_ref[i] = conv_ref[i + 1]
        conv_out_ref[CONV_W - 2] = u

        gb = jnp.concatenate([gi_ref[...], jnp.zeros((LANES - 8, NB), f32)], axis=0).T + brow_ref[...]
        lf = _log_sigmoid(gb)
        m_cols = jnp.concatenate([m_ref[...], jnp.zeros((8 - N_HEADS, NB), f32)], axis=0).T
        sc = jnp.zeros((NB, LANES), f32)
        mo = jnp.zeros((NB, LANES), f32)
        for h in range(N_HEADS):
            cs = slice(h * D_HEAD, (h + 1) * D_HEAD)
            chb = cm[:, cs].astype(bf16)
            q = _dot(chb, wq_ref[h])
            k = _dot(chb, wk_ref[h])
            qm_s[:, cs] = q
            km_s[:, cs] = k
            ig = gb[:, h:h + 1]
            lfh = lf[:, 4 + h:5 + h]
            mp = m_cols[:, h:h + 1]
            inter = lfh + mp
            m_t = jnp.maximum(inter, ig)
            w = jnp.exp(ig - m_t)
            a = jnp.exp(inter - m_t)
            nh = n_ref[:, cs]
            ws = w * jnp.sum(q * k, axis=-1, keepdims=True)
            den = ws + a * jnp.sum(nh * q, axis=-1, keepdims=True)
            den = jnp.maximum(jnp.abs(den), jnp.exp(-m_t))
            n_out_ref[:, cs] = a * nh + w * k
            for r, val in ((h, ws), (4 + h, a), (8 + h, den), (12 + h, w)):
                sc = sc + val * (lane == r).astype(f32)
            mo = mo + m_t * (lane == h).astype(f32)
        m_out_ref[...] = mo.T[0:N_HEADS, :]
        sct_s[...] = sc.T
        vt_s[...] = vm_ref[...].astype(f32).T
        vr_s[...] = vr_ref[...].astype(f32)
        cosd = cosd_ref[0:1, :]
        sind = sind_ref[0:1, :]
        cosdT = cosdT_ref[...]
        sindT = sindT_ref[...]
        for h in range(N_HEADS):
            lo = slice(h * D_HEAD, h * D_HEAD + HALF)
            hi = slice(h * D_HEAD + HALF, (h + 1) * D_HEAD)
            q1 = qr_ref[:, lo].astype(f32)
            q2 = qr_ref[:, hi].astype(f32)
            qrot = jnp.concatenate([q1 * cosd - q2 * sind, q1 * sind + q2 * cosd], axis=-1)
            qrt_s[h * D_HEAD:(h + 1) * D_HEAD, :] = qrot.T
            k1 = kt_ref[lo, :].astype(f32)
            k2 = kt_ref[hi, :].astype(f32)
            krt_s[lo, :] = (k1 * cosdT - k2 * sindT) * scale
            krt_s[hi, :] = (k1 * sindT + k2 * cosdT) * scale
        cqt_s[...] = jnp.zeros(cqt_s.shape, f32)

    def take_col(mat, mask):
        return jnp.sum(jnp.where(mask, mat, 0.0), axis=-1, keepdims=True)

    for j in range(_SEQ_PER_STEP):
        b = s * _SEQ_PER_STEP + j
        mask = lane == b
        for h in range(N_HEADS):
            cs = slice(h * D_HEAD, (h + 1) * D_HEAD)
            c = c_ref[j, h]
            q_row = qm_s[pl.ds(b, 1), cs]
            k_row = km_s[pl.ds(b, 1), cs]
            v_col = take_col(vt_s[cs, :], mask)
            a = take_col(sct_s[4 + h:5 + h, :], mask)
            w = take_col(sct_s[12 + h:13 + h, :], mask)
            cq = jnp.sum(c * q_row, axis=-1, keepdims=True)
            cqt_s[cs, :] = jnp.where(mask, cq, cqt_s[cs, :])
            c_out_ref[j, h] = a * c + (w * v_col) * k_row
            st = s_ref[j, h]
            q_col = take_col(qrt_s[cs, :], mask)
            k_col = take_col(krt_s[cs, :], mask)
            v_row = vr_s[pl.ds(b, 1), cs]
            qk = jnp.sum(q_col * k_col, axis=0, keepdims=True)
            o_row = qk * v_row + gammas[h] * jnp.sum(st * q_col, axis=0, keepdims=True)
            oacc_s[pl.ds(b, 1), cs] = o_row
            s_out_ref[j, h] = gammas[h] * st + k_col * v_row

    @pl.when(s == ns - 1)
    def _():
        for h in range(N_HEADS):
            cs = slice(h * D_HEAD, (h + 1) * D_HEAD)
            ht = ((sct_s[h:h + 1, :] * vt_s[cs, :] + sct_s[4 + h:5 + h, :] * cqt_s[cs, :])
                  / sct_s[8 + h:9 + h, :])
            hn = _head_norm(ht.T, hngm_ref[:, cs])
            zsm = zsm_ref[:, cs].astype(f32)
            om_ref[:, cs] = ((hn + skip_ref[:, cs] * cm_s[:, cs]) * zsm).astype(bf16)
            zsr = zsr_ref[:, cs].astype(f32)
            or_ref[:, cs] = (_head_norm(oacc_s[:, cs], hngr_ref[:, cs]) * zsr).astype(bf16)


def _decode_step(P, KT, GI, OM, OR, prev, l, BT, conv_st, n_st, m_st, C_st, S_st,
                 conv_w, conv_b, wq, wk, brow, hngm, skip, hngr, dtabs):
    R = P.shape[0]
    NB = R - BT
    assert NB == LANES and BT % NB == 0 and NB % _SEQ_PER_STEP == 0
    rb = BT // NB
    depth = C_st.shape[0]
    bb = _SEQ_PER_STEP
    cosd, sind, cosdT, sindT = dtabs
    pcol = lambda k: (lambda s: (rb, k))
    rep2 = lambda s: (0, 0)
    st_spec = pl.BlockSpec((None, bb, N_HEADS, D_HEAD, D_HEAD), lambda s: (l, s, 0, 0, 0))
    conv_spec = _layer_spec((CONV_W - 1, NB, W_MIX), l)
    n_spec = _layer_spec((NB, W_MIX), l)
    m_spec = _layer_spec((N_HEADS, NB), l)
    in_specs = [
        pl.BlockSpec((NB, W_MIX), pcol(0)), pl.BlockSpec((NB, W_MIX), pcol(1)),
        pl.BlockSpec((NB, W_MIX), pcol(2)), pl.BlockSpec((NB, W_MIX), pcol(3)),
        pl.BlockSpec((NB, W_MIX), pcol(4)), pl.BlockSpec((NB, W_MIX), pcol(5)),
        pl.BlockSpec((W_MIX, NB), lambda s: (0, rb)),
        pl.BlockSpec((2 * N_HEADS, NB), lambda s: (0, rb)),
        conv_spec, n_spec, m_spec, st_spec, st_spec,
        _layer_spec((CONV_W, W_MIX), l), _layer_spec((1, W_MIX), l),
        _layer_spec((N_HEADS, D_HEAD, D_HEAD), l), _layer_spec((N_HEADS, D_HEAD, D_HEAD), l),
        _layer_spec((1, LANES), l),
        _layer_spec((1, W_MIX), l), _layer_spec((1, W_MIX), l), _layer_spec((1, W_MIX), l),
        pl.BlockSpec((8, HALF), rep2), pl.BlockSpec((8, HALF), rep2),
        pl.BlockSpec((HALF, LANES), rep2), pl.BlockSpec((HALF, LANES), rep2),
    ]
    args = [P, P, P, P, P, P, KT, GI, conv_st, n_st, m_st, C_st, S_st,
            conv_w, conv_b, wq, wk, brow, hngm, skip, hngr, cosd, sind, cosdT, sindT]
    carried = [OM, OR] + (list(prev) if prev is not None else [])
    kern, xspecs, xargs, aliases = _chain(_decode_kernel, len(args), carried, 0)
    out_row = pl.BlockSpec((NB, W_MIX), lambda s: (rb, 0))
    return pl.pallas_call(
        kern,
        grid=(NB // bb,),
        in_specs=in_specs + xspecs,
        out_specs=[out_row, out_row, st_spec, st_spec, n_spec, m_spec, conv_spec],
        out_shape=[
            jax.ShapeDtypeStruct(OM.shape, bf16),
            jax.ShapeDtypeStruct(OR.shape, bf16),
            jax.ShapeDtypeStruct((depth, NB, N_HEADS, D_HEAD, D_HEAD), f32),
            jax.ShapeDtypeStruct((depth, NB, N_HEADS, D_HEAD, D_HEAD), f32),
            jax.ShapeDtypeStruct((depth, NB, W_MIX), f32),
            jax.ShapeDtypeStruct((depth, N_HEADS, NB), f32),
            jax.ShapeDtypeStruct((depth, CONV_W - 1, NB, W_MIX), f32),
        ],
        scratch_shapes=[
            pltpu.VMEM((W_MIX, NB), f32),
            pltpu.VMEM((W_MIX, NB), f32),
            pltpu.VMEM((W_MIX, NB), f32),
            pltpu.VMEM((NB, W_MIX), f32),
            pltpu.VMEM((NB, W_MIX), f32),
            pltpu.VMEM((NB, W_MIX), f32),
            pltpu.VMEM((LANES, NB), f32),
            pltpu.VMEM((W_MIX, NB), f32),
            pltpu.VMEM((NB, W_MIX), f32),
            pltpu.VMEM((NB, W_MIX), f32),
        ],
        input_output_aliases=aliases,
        compiler_params=_cparams(("arbitrary",)),
        name="decode_step",
    )(*args, *xargs)


_MERGE_SLAB = 256


def _merge_kernel(h_ref, om_ref, or_ref, wgm_ref, wgr_ref, bgm_ref, bgr_ref, wpm_ref, wpr_ref, o_ref):
    for c0 in range(0, o_ref.shape[1], _MERGE_SLAB):
        cols = slice(c0, c0 + _MERGE_SLAB)
        gm = jax.nn.sigmoid(_dot(h_ref[...], wgm_ref[:, cols].astype(bf16)) + bgm_ref[:, cols])
        gr = jax.nn.sigmoid(_dot(h_ref[...], wgr_ref[:, cols].astype(bf16)) + bgr_ref[:, cols])
        pm = _dot(om_ref[...], wpm_ref[:, cols].astype(bf16))
        pr = _dot(or_ref[...], wpr_ref[:, cols].astype(bf16))
        o_ref[:, cols] = (gm * pm + gr * pr).astype(bf16)


def _merge(h, OM, OR, w_gate, b_gate3, w_pm, w_pr, l):
    R = h.shape[0]
    tm = _pick(R, (1664, 1280, 640, 384, 128))
    tn = 512
    nj = D_MODEL // tn
    return pl.pallas_call(
        _merge_kernel,
        grid=(R // tm, nj),
        in_specs=[
            pl.BlockSpec((tm, D_MODEL), lambda i, j: (i, 0)),
            pl.BlockSpec((tm, W_MIX), lambda i, j: (i, 0)),
            pl.BlockSpec((tm, W_MIX), lambda i, j: (i, 0)),
            pl.BlockSpec((None, D_MODEL, tn), lambda i, j: (l, 0, j)),
            pl.BlockSpec((None, D_MODEL, tn), lambda i, j: (l, 0, j + nj)),
            pl.BlockSpec((None, 1, tn), lambda i, j: (l, 0, j)),
            pl.BlockSpec((None, 1, tn), lambda i, j: (l, 0, j + nj)),
            pl.BlockSpec((None, W_MIX, tn), lambda i, j: (l, 0, j)),
            pl.BlockSpec((None, W_MIX, tn), lambda i, j: (l, 0, j)),
        ],
        out_specs=pl.BlockSpec((tm, tn), lambda i, j: (i, j)),
        out_shape=jax.ShapeDtypeStruct((R, D_MODEL), bf16),
        compiler_params=_cparams(("arbitrary", "arbitrary")),
        name="merge",
    )(h, OM, OR, w_gate, w_gate, b_gate3, b_gate3, w_pm, w_pr)


def _outproj_norm_kernel(mg_ref, wo_ref, x_ref, g_ref, o_ref, h_ref):
    y = x_ref[...] + _dot(mg_ref[...], wo_ref[...])
    o_ref[...] = y
    ms = jnp.mean(y * y, axis=-1, keepdims=True)
    h_ref[...] = (y * lax.rsqrt(ms + EPS) * g_ref[...]).astype(bf16)


def _outproj_final_kernel(mg_ref, wo_ref, x_ref, g_ref, y_ref):
    y = x_ref[...] + _dot(mg_ref[...], wo_ref[...])
    ms = jnp.mean(y * y, axis=-1, keepdims=True)
    y_ref[...] = y * lax.rsqrt(ms + EPS) * g_ref[...]


def _wo_spec():
    return pl.BlockSpec((D_MODEL, D_MODEL), lambda i: (0, 0), pipeline_mode=pl.Buffered(1))


def _outproj(merged, w_o_bf, x, norm_g3, l_next, row0, nrows, x_local, prev):
    R = merged.shape[0]
    tm = _pick(nrows, (640, 512, 384, 128))
    assert row0 % tm == 0
    off = row0 // tm
    row_g = pl.BlockSpec((tm, D_MODEL), lambda i: (i + off, 0))
    row_x = pl.BlockSpec((tm, D_MODEL), lambda i: (i, 0)) if x_local else row_g
    kern, xspecs, xargs, aliases = _chain(_outproj_norm_kernel, 4, prev, 0)
    return pl.pallas_call(
        kern,
        grid=(nrows // tm,),
        in_specs=[row_g, _wo_spec(), row_x, _layer_spec((1, D_MODEL), l_next)] + xspecs,
        out_specs=[row_g, row_g],
        out_shape=[jax.ShapeDtypeStruct((R, D_MODEL), f32), jax.ShapeDtypeStruct((R, D_MODEL), bf16)],
        input_output_aliases=aliases,
        compiler_params=_cparams(("arbitrary",)),
        name="outproj_norm",
    )(merged, w_o_bf, x, norm_g3, *xargs)


def _outproj_final(merged, w_o_bf, x, g, row0, nrows):
    tm = _pick(nrows, (512, 256, 128))
    assert row0 % tm == 0
    off = row0 // tm
    row_in = pl.BlockSpec((tm, D_MODEL), lambda i: (i + off, 0))
    return pl.pallas_call(
        _outproj_final_kernel,
        grid=(nrows // tm,),
        in_specs=[row_in, _wo_spec(), row_in, pl.BlockSpec((1, D_MODEL), lambda i: (0, 0))],
        out_specs=pl.BlockSpec((tm, D_MODEL), lambda i: (i, 0)),
        out_shape=jax.ShapeDtypeStruct((nrows, D_MODEL), f32),
        compiler_params=_cparams(("arbitrary",)),
        name="outproj_final",
    )(merged, w_o_bf, x, g)


def kernel(x_prompt, x_sample, state_mlstm_C, state_mlstm_n, state_mlstm_m, state_mlstm_conv, state_ret_S,
           norm_g, w_in, conv_w, conv_b, w_qm, w_km, b_i, b_f, hn_m_g, skip_m, hn_r_g,
           w_pm, w_pr, w_gate, b_gate, w_o, final_g):
    B, T, D = x_prompt.shape
    NB = x_sample.shape[0]
    depth = w_in.shape[0]
    BT = B * T
    assert D == D_MODEL and x_sample.shape[1] == 1 and T % CHUNK == 0 and depth >= 2

    R = BT + NB
    xp = x_prompt.reshape(BT, D)
    xs = x_sample.reshape(NB, D)

    w_inT = jnp.swapaxes(w_in, 1, 2)
    assert math.log2(D_HEAD) % 2 == 0
    wq = (w_qm * D_HEAD ** -0.5).astype(bf16)
    wk = w_km.astype(bf16)
    wkT = jnp.swapaxes(w_km, 2, 3).astype(bf16)
    bias = jnp.pad(jnp.concatenate([b_i, b_f], axis=1), ((0, 0), (0, LANES - 2 * N_HEADS)))
    brow = bias.reshape(depth, 1, LANES)
    bcol = bias[:, :8].reshape(depth, 8, 1)
    b_gate3 = b_gate.reshape(depth, 1, 2 * D)
    norm_g3 = norm_g.reshape(depth, 1, D)
    conv_b3 = conv_b.reshape(depth, 1, W_MIX)
    hngm = hn_m_g.reshape(depth, 1, W_MIX)
    skp = skip_m.reshape(depth, 1, W_MIX)
    hngr = hn_r_g.reshape(depth, 1, W_MIX)
    conv_st = jnp.transpose(state_mlstm_conv, (0, 2, 1, 3))
    m_st = jnp.swapaxes(state_mlstm_m, 1, 2)
    n_st = state_mlstm_n.reshape(depth, NB, W_MIX)

    cos, sin, cosT, sinT, cosd, sind, cosdT, sindT = _rope_tables(T)

    prev_m = prev_r = prev_d = None
    h = _rmsnorm_into(xp, norm_g3, 0, R, 0, None)
    h = _rmsnorm_into(xs, norm_g3, 0, R, BT, h)
    x = None
    for l in range(depth):
        P, KT, GI, w_o_bf = _inproj(h, w_inT, w_o, l)
        OM, *prev_m = _mlstm_prompt(P, GI, B, T, l, depth, prev_m, conv_w, conv_b3, wq, wkT,
                                    bcol, hngm, skp)
        OR, *prev_r = _ret_prompt(P, KT, B, T, l, depth, prev_r, (cos, sin, cosT, sinT), hngr)
        OM, OR, *prev_d = _decode_step(
            P, KT, GI, OM, OR, prev_d, l, BT, conv_st, n_st, m_st, state_mlstm_C, state_ret_S,
            conv_w, conv_b3, wq, wk, brow, hngm, skp, hngr, (cosd, sind, cosdT, sindT))
        merged = _merge(h, OM, OR, w_gate, b_gate3, w_pm, w_pr, l)
        if l + 1 == depth:
            break
        if x is None:
            x, h = _outproj(merged, w_o_bf, xp, norm_g3, l + 1, 0, BT, True, None)
            x, h = _outproj(merged, w_o_bf, xs, norm_g3, l + 1, BT, NB, True, (x, h))
        else:
            x, h = _outproj(merged, w_o_bf, x, norm_g3, l + 1, 0, R, False, None)

    fg = final_g.reshape(1, D)
    y_prompt = _outproj_final(merged, w_o_bf, x, fg, 0, BT).reshape(B, T, D)
    y_sample = _outproj_final(merged, w_o_bf, x, fg, BT, NB).reshape(NB, 1, D)
    Cp, np_, mp, convp = prev_m
    (Sp,) = prev_r
    Cs, Ss, n_s, m_s, conv_s = prev_d
    return (y_prompt, y_sample,
            Cp, np_.reshape(depth, B, N_HEADS, D_HEAD), mp[:, :, 0, 12:12 + N_HEADS], convp, Sp,
            Cs, n_s.reshape(depth, NB, N_HEADS, D_HEAD), jnp.swapaxes(m_s, 1, 2),
            jnp.transpose(conv_s, (0, 2, 1, 3)), Ss)
```

```python
import functools
import math

import jax
import jax.numpy as jnp
from jax import lax
from jax.experimental import pallas as pl
from jax.experimental.pallas import tpu as pltpu

f32 = jnp.float32
bf16 = jnp.bfloat16

D_MODEL = 2048
N_HEADS = 4
D_HEAD = 256
W_MIX = N_HEADS * D_HEAD
CONV_W = 4
CHUNK = 128
ROPE_BASE = 10000.0
EPS = 1e-6
NEG = -1e30
PAST_LEN = 16384
LANES = 128
HALF = D_HEAD // 2
VMEM_LIMIT = 62 * 1024 * 1024

OFF_IF = 3 * W_MIX
OFF_RET = 3 * W_MIX + 2 * N_HEADS


def _pick(n, cands):
    for c in cands:
        if n % c == 0:
            return c
    return n


def _cparams(sem):
    return pltpu.CompilerParams(dimension_semantics=sem, vmem_limit_bytes=VMEM_LIMIT)


def _dot(a, b):
    return jnp.dot(a, b, preferred_element_type=f32)


def _dot_nt(a, b):
    return lax.dot_general(a, b, (((1,), (1,)), ((), ())), preferred_element_type=f32)


def _split3(x):
    hi = x.astype(bf16).astype(f32)
    r = x - hi
    mid = r.astype(bf16).astype(f32)
    lo = (r - mid).astype(bf16).astype(f32)
    return hi, mid, lo


def _silu(x):
    return x * (0.5 * jnp.tanh(0.5 * x) + 0.5)


def _log_sigmoid(x):
    return jnp.minimum(x, 0.0) - jnp.log1p(jnp.exp(-jnp.abs(x)))


def _head_norm(hc, g):
    mu = jnp.mean(hc, axis=-1, keepdims=True)
    d = hc - mu
    var = jnp.mean(d * d, axis=-1, keepdims=True)
    return d * lax.rsqrt(var + EPS) * g


def _rope_kernel(inv_ref, invT_ref, cos_ref, sin_ref, cosT_ref, sinT_ref,
                 cosd_ref, sind_ref, cosdT_ref, sindT_ref):
    T = cos_ref.shape[0]
    pos = lax.broadcasted_iota(jnp.int32, (T, HALF), 0).astype(f32)
    ang = pos * inv_ref[...]
    cos = jnp.cos(ang)
    sin = jnp.sin(ang)
    cos_ref[...] = cos
    sin_ref[...] = sin
    cosT_ref[...] = cos.T
    sinT_ref[...] = sin.T
    angd = jnp.full((8, HALF), float(PAST_LEN), f32) * inv_ref[...]
    cosd_ref[...] = jnp.cos(angd)
    sind_ref[...] = jnp.sin(angd)
    angdT = jnp.full((HALF, LANES), float(PAST_LEN), f32) * invT_ref[...]
    cosdT_ref[...] = jnp.cos(angdT)
    sindT_ref[...] = jnp.sin(angdT)


def _rope_tables(T):
    inv = ROPE_BASE ** (-jnp.arange(0, HALF, dtype=f32) / HALF)
    shapes = [(T, HALF), (T, HALF), (HALF, T), (HALF, T),
              (8, HALF), (8, HALF), (HALF, LANES), (HALF, LANES)]
    return pl.pallas_call(
        _rope_kernel,
        out_shape=[jax.ShapeDtypeStruct(s, f32) for s in shapes],
        name="rope_tables",
    )(inv.reshape(1, HALF), inv.reshape(HALF, 1))


def _rms_kernel(x_ref, g_ref, o_ref):
    x = x_ref[...]
    ms = jnp.mean(x * x, axis=-1, keepdims=True)
    o_ref[...] = (x * lax.rsqrt(ms + EPS) * g_ref[...]).astype(o_ref.dtype)


def _rmsnorm_into(x, norm_g3, l, R, row0, prev):
    n, D = x.shape
    tr = _pick(n, (512, 256, 128))
    assert row0 % tr == 0
    off = row0 // tr
    kern, xspecs, xargs, aliases = _chain(_rms_kernel, 2, None if prev is None else [prev], 0)
    return pl.pallas_call(
        kern,
        grid=(n // tr,),
        in_specs=[pl.BlockSpec((tr, D), lambda i: (i, 0)), _layer_spec((1, D), l)] + xspecs,
        out_specs=pl.BlockSpec((tr, D), lambda i: (i + off, 0)),
        out_shape=jax.ShapeDtypeStruct((R, D), bf16),
        input_output_aliases=aliases,
        compiler_params=_cparams(("arbitrary",)),
        name="rmsnorm",
    )(x, norm_g3, *xargs)


_TN_IN = 1024
_TPG = W_MIX // _TN_IN
_NJ_P = 6 * _TPG
_NJ_K = _TPG
_GATE_PAD = 16


def _inproj_row_offset(j):
    jp = jnp.minimum(j, _NJ_P - 1)
    grp = jp // _TPG
    sub = jp - grp * _TPG
    src_grp = jnp.where(grp < 3, grp, jnp.where(grp == 3, 3, grp + 1))
    off_p = src_grp * W_MIX + jnp.where(grp < 3, 0, 2 * N_HEADS) + sub * _TN_IN
    off_k = OFF_RET + W_MIX + jnp.clip(j - _NJ_P, 0, _NJ_K - 1) * _TN_IN
    return jnp.where(j < _NJ_P, off_p, off_k)


def _inproj_kernel(h_ref, wt_ref, wif_ref, wo_ref, p_ref, kt_ref, gi_ref, wob_ref):
    j = pl.program_id(1)
    wob_ref[...] = wo_ref[...].astype(bf16)

    def weights():
        return wt_ref[0].astype(bf16)

    is_z = (j // _TPG == 2) | (j // _TPG == 5)
    last = _NJ_P + _NJ_K - 1

    @pl.when((j < _NJ_P) & jnp.logical_not(is_z))
    def _():
        p_ref[...] = _dot_nt(h_ref[...], weights()).astype(bf16)

    @pl.when((j < _NJ_P) & is_z)
    def _():
        p_ref[...] = _silu(_dot_nt(h_ref[...], weights())).astype(bf16)

    if _NJ_K > 1:
        @pl.when((j >= _NJ_P) & (j < last))
        def _():
            kt_ref[...] = _dot_nt(weights(), h_ref[...]).astype(bf16)

    @pl.when(j == last)
    def _():
        pad = jnp.zeros((_GATE_PAD - 2 * N_HEADS, D_MODEL), f32)
        w_all = jnp.concatenate([wt_ref[0], wif_ref[...], pad], axis=0).astype(bf16)
        res = _dot_nt(w_all, h_ref[...])
        kt_ref[...] = res[0:_TN_IN, :].astype(bf16)
        gi_ref[...] = res[_TN_IN:_TN_IN + 2 * N_HEADS, :]


def _inproj(h, w_inT, w_o, l):
    R = h.shape[0]
    tm = _pick(R, (1664, 1280, 640, 384, 128))
    tn = _TN_IN
    nj = _NJ_P + _NJ_K
    nsteps = (R // tm) * nj
    rps = next(r for r in (32, 64, 128, 256, 512, 1024, 2048) if D_MODEL // r <= nsteps)
    wo_blk = lambda i, j: jnp.minimum(i * nj + j, D_MODEL // rps - 1)
    return pl.pallas_call(
        _inproj_kernel,
        grid=(R // tm, nj),
        in_specs=[
            pl.BlockSpec((tm, D_MODEL), lambda i, j: (i, 0)),
            pl.BlockSpec((pl.Element(1), pl.Element(tn), pl.Element(D_MODEL)),
                         lambda i, j: (l, pl.multiple_of(_inproj_row_offset(j), 8), 0)),
            pl.BlockSpec((None, 2 * N_HEADS, D_MODEL), lambda i, j: (l, OFF_IF // (2 * N_HEADS), 0)),
            pl.BlockSpec((None, rps, D_MODEL), lambda i, j: (l, wo_blk(i, j), 0)),
        ],
        out_specs=[
            pl.BlockSpec((tm, tn), lambda i, j: (i, jnp.minimum(j, _NJ_P - 1))),
            pl.BlockSpec((tn, tm), lambda i, j: (jnp.clip(j - _NJ_P, 0, _NJ_K - 1), i)),
            pl.BlockSpec((2 * N_HEADS, tm), lambda i, j: (0, i)),
            pl.BlockSpec((rps, D_MODEL), lambda i, j: (wo_blk(i, j), 0)),
        ],
        out_shape=[
            jax.ShapeDtypeStruct((R, 6 * W_MIX), bf16),
            jax.ShapeDtypeStruct((W_MIX, R), bf16),
            jax.ShapeDtypeStruct((2 * N_HEADS, R), f32),
            jax.ShapeDtypeStruct((D_MODEL, D_MODEL), bf16),
        ],
        compiler_params=_cparams(("arbitrary", "arbitrary")),
        name="inproj",
    )(h, w_inT, w_inT, w_o)


def _mlstm_kernel(um_ref, vm_ref, zs_ref, gi_ref, convw_ref, convb_ref, wq_ref, wkT_ref,
                  bcol_ref, hng_ref, skip_ref,
                  om_ref, c_out_ref, n_out_ref, m_out_ref, conv_out_ref,
                  tail_s, cta_s, m_s):
    g = pl.program_id(1)
    ng = pl.num_programs(1)
    Lg = um_ref.shape[0]
    L = CHUNK
    DA = D_HEAD + LANES
    heads = range(N_HEADS)

    @pl.when(g == 0)
    def _():
        tail_s[...] = jnp.zeros(tail_s.shape, bf16)
        cta_s[...] = jnp.zeros(cta_s.shape, f32)
        m_s[...] = jnp.full(m_s.shape, NEG, f32)

    srow = lax.broadcasted_iota(jnp.int32, (L, 2 * L), 0)
    scol = lax.broadcasted_iota(jnp.int32, (L, 2 * L), 1)
    shift_mat = jnp.concatenate([(scol == srow + L - s).astype(bf16) for s in range(1, CONV_W)], axis=0)

    row = lax.broadcasted_iota(jnp.int32, (L, L), 0)
    col = lax.broadcasted_iota(jnp.int32, (L, L), 1)
    causal = row >= col
    triu = (row <= col).astype(bf16)
    lane = col
    bcol = bcol_ref[...]
    ones_b = jnp.ones((L, LANES), bf16)
    ones_f = jnp.ones((L, LANES), f32)

    for c in range(Lg // L):
        rows = slice(c * L, (c + 1) * L)
        gtb = gi_ref[:, rows] + bcol
        lft = _log_sigmoid(gtb)
        br3 = _dot(jnp.concatenate(_split3(lft), axis=0).astype(bf16), triu)
        br_all = br3[0:8, :] + br3[8:16, :] + br3[16:24, :]
        cols = jnp.concatenate([gtb, br_all, jnp.zeros((LANES - 16, L), f32)], axis=0).T

        u_cur = um_ref[rows, :]
        u_prev = tail_s[...] if c == 0 else um_ref[(c - 1) * L:c * L, :]
        delayed = _dot(shift_mat, jnp.concatenate([u_prev, u_cur], axis=0))
        acc = convb_ref[...] + u_cur.astype(f32) * convw_ref[CONV_W - 1:CONV_W, :]
        for s in range(1, CONV_W):
            acc = acc + delayed[(s - 1) * L:s * L, :] * convw_ref[CONV_W - 1 - s:CONV_W - s, :]
        cm = _silu(acc)

        cs = [slice(h * D_HEAD, (h + 1) * D_HEAD) for h in heads]
        ch = [cm[:, cs[h]] for h in heads]
        chb = [x.astype(bf16) for x in ch]
        qb = [_dot(chb[h], wq_ref[h]).astype(bf16) for h in heads]
        kTb = [_dot_nt(wkT_ref[h], chb[h]).astype(bf16) for h in heads]
        v = [vm_ref[rows, cs[h]] for h in heads]
        bc = [cols[:, 12 + h:13 + h] for h in heads]
        dlog = [jnp.where(causal, bc[h] - br_all[4 + h:5 + h, :] + gtb[h:h + 1, :], NEG) for h in heads]
        dmax = jnp.zeros((L, LANES), f32)
        for h in heads:
            dmax = jnp.where(lane == 12 + h, jnp.max(dlog[h], axis=-1, keepdims=True), dmax)
        m_row = m_s[0:1, :]
        inter = cols + m_row
        m_all = jnp.maximum(inter, dmax)
        a_all = jnp.exp(inter - m_all)
        em_all = jnp.exp(-m_all)
        m_new = m_all[L - 1:L, :]
        b_last = cols[L - 1:L, :]
        ic_al = pltpu.roll(cols, 12, axis=1)
        wk_all = jnp.exp(b_last - cols + ic_al - m_new)
        dec_row = jnp.exp(b_last + m_row - m_new)
        m_s[...] = jnp.broadcast_to(m_new, (8, LANES))
        hl = [slice(12 + h, 13 + h) for h in heads]
        ws = [(jnp.exp(dlog[h] - m_all[:, hl[h]]) * _dot(qb[h], kTb[h])).astype(bf16) for h in heads]
        cta = [cta_s[h] for h in heads]
        nd = [_dot(ws[h], jnp.concatenate([v[h], ones_b], axis=1))
              + a_all[:, hl[h]] * _dot(qb[h], cta[h].astype(bf16)) for h in heads]
        for h in heads:
            vaug = jnp.concatenate([v[h].astype(f32), ones_f], axis=1)
            cta_s[h] = dec_row[:, hl[h]] * cta[h] + _dot(kTb[h], (wk_all[:, hl[h]] * vaug).astype(bf16))
        for h in heads:
            den = jnp.maximum(jnp.abs(nd[h][:, D_HEAD:DA]), em_all[:, hl[h]])
            rden = 1.0 / den
            hc = jnp.concatenate([nd[h][:, 0:HALF] * rden, nd[h][:, HALF:D_HEAD] * rden], axis=1)
            hn = _head_norm(hc, hng_ref[:, cs[h]])
            zs = zs_ref[rows, cs[h]].astype(f32)
            om_ref[rows, cs[h]] = ((hn + skip_ref[:, cs[h]] * ch[h]) * zs).astype(bf16)

    tail_s[...] = um_ref[Lg - L:Lg, :]

    @pl.when(g == ng - 1)
    def _():
        for h in heads:
            st = cta_s[h]
            c_out_ref[h] = st[:, 0:D_HEAD].T
            n_out_ref[h] = st[:, D_HEAD:DA].T[0:1, :]
        m_out_ref[...] = m_s[...]
        conv_out_ref[...] = um_ref[Lg - 8:Lg, :].astype(f32)[8 - (CONV_W - 1):8, :]


def _drop_refs(kernel_fn, n_in, n_drop, *refs):
    return kernel_fn(*refs[:n_in], *refs[n_in + n_drop:])


def _chain(kernel_fn, n_in, prev, first_out):
    if prev is None:
        return kernel_fn, [], [], {}
    specs = [pl.BlockSpec(memory_space=pl.ANY)] * len(prev)
    aliases = {n_in + k: first_out + k for k in range(len(prev))}
    return functools.partial(_drop_refs, kernel_fn, n_in, len(prev)), specs, list(prev), aliases


def _layer_spec(shape, l):
    nd = len(shape)
    return pl.BlockSpec((None,) + tuple(shape), lambda *_: (l,) + (0,) * nd)


def _mlstm_prompt(P, GI, B, T, l, depth, prev, conv_w, conv_b, wq, wkT, bcol, hng, skip):
    R = P.shape[0]
    Lg = _pick(T, (512, 256, 128))
    ng = T // Lg
    col = lambda k: (lambda b, g: (b * ng + g, k))
    in_specs = [
        pl.BlockSpec((Lg, W_MIX), col(0)),
        pl.BlockSpec((Lg, W_MIX), col(1)),
        pl.BlockSpec((Lg, W_MIX), col(2)),
        pl.BlockSpec((2 * N_HEADS, Lg), lambda b, g: (0, b * ng + g)),
        _layer_spec((CONV_W, W_MIX), l),
        _layer_spec((1, W_MIX), l),
        _layer_spec((N_HEADS, D_HEAD, D_HEAD), l),
        _layer_spec((N_HEADS, D_HEAD, D_HEAD), l),
        _layer_spec((8, 1), l),
        _layer_spec((1, W_MIX), l),
        _layer_spec((1, W_MIX), l),
    ]
    args = [P, P, P, GI, conv_w, conv_b, wq, wkT, bcol, hng, skip]
    kern, xspecs, xargs, aliases = _chain(_mlstm_kernel, len(args), prev, 1)
    return pl.pallas_call(
        kern,
        grid=(B, ng),
        in_specs=in_specs + xspecs,
        out_specs=[
            pl.BlockSpec((Lg, W_MIX), col(0)),
            pl.BlockSpec((None, None, N_HEADS, D_HEAD, D_HEAD), lambda b, g: (l, b, 0, 0, 0)),
            pl.BlockSpec((None, None, N_HEADS, 1, D_HEAD), lambda b, g: (l, b, 0, 0, 0)),
            pl.BlockSpec((None, None, 8, LANES), lambda b, g: (l, b, 0, 0)),
            pl.BlockSpec((None, None, CONV_W - 1, W_MIX), lambda b, g: (l, b, 0, 0)),
        ],
        out_shape=[
            jax.ShapeDtypeStruct((R, W_MIX), bf16),
            jax.ShapeDtypeStruct((depth, B, N_HEADS, D_HEAD, D_HEAD), f32),
            jax.ShapeDtypeStruct((depth, B, N_HEADS, 1, D_HEAD), f32),
            jax.ShapeDtypeStruct((depth, B, 8, LANES), f32),
            jax.ShapeDtypeStruct((depth, B, CONV_W - 1, W_MIX), f32),
        ],
        scratch_shapes=[
            pltpu.VMEM((CHUNK, W_MIX), bf16),
            pltpu.VMEM((N_HEADS, D_HEAD, D_HEAD + LANES), f32),
            pltpu.VMEM((8, LANES), f32),
        ],
        input_output_aliases=aliases,
        compiler_params=_cparams(("arbitrary", "arbitrary")),
        name="mlstm_prompt",
    )(*args, *xargs)


def _log_gammas():
    return [math.log(1.0 - 2.0 ** (-5.0 - h)) for h in range(N_HEADS)]


def _ret_kernel(q_ref, v_ref, zs_ref, kt_ref, cos_ref, sin_ref, cosT_ref, sinT_ref, hng_ref,
                o_ref, s_out_ref, s_s):
    g = pl.program_id(1)
    ng = pl.num_programs(1)
    Lg = q_ref.shape[0]
    L = CHUNK

    @pl.when(g == 0)
    def _():
        s_s[...] = jnp.zeros(s_s.shape, f32)

    row = lax.broadcasted_iota(jnp.int32, (L, L), 0)
    col = lax.broadcasted_iota(jnp.int32, (L, L), 1)
    causal = row >= col
    diff = (row - col).astype(f32)
    tcol = lax.broadcasted_iota(jnp.int32, (L, 1), 0).astype(f32)
    lgs = _log_gammas()
    intra = [jnp.where(causal, jnp.exp(diff * lg), 0.0) for lg in lgs]
    q_dec = [jnp.exp((tcol + 1.0) * lg) for lg in lgs]
    k_dec = [jnp.exp((L - 1.0 - tcol) * lg) for lg in lgs]
    s_dec = [math.exp(L * lg) for lg in lgs]
    kscale = D_HEAD ** -0.5

    for c in range(Lg // L):
        rows = slice(c * L, (c + 1) * L)
        cos = cos_ref[rows, :]
        sin = sin_ref[rows, :]
        cosT = cosT_ref[:, rows] * kscale
        sinT = sinT_ref[:, rows] * kscale
        heads = range(N_HEADS)
        cs = [slice(h * D_HEAD, (h + 1) * D_HEAD) for h in heads]
        lo = [slice(h * D_HEAD, h * D_HEAD + HALF) for h in heads]
        hi = [slice(h * D_HEAD + HALF, (h + 1) * D_HEAD) for h in heads]
        q1 = [q_ref[rows, lo[h]].astype(f32) for h in heads]
        q2 = [q_ref[rows, hi[h]].astype(f32) for h in heads]
        qb = [jnp.concatenate([q1[h] * cos - q2[h] * sin, q1[h] * sin + q2[h] * cos], axis=-1).astype(bf16)
              for h in heads]
        k1 = [kt_ref[lo[h], rows].astype(f32) for h in heads]
        k2 = [kt_ref[hi[h], rows].astype(f32) for h in heads]
        kTb = [jnp.concatenate([k1[h] * cosT - k2[h] * sinT, k1[h] * sinT + k2[h] * cosT],
                               axis=0).astype(bf16) for h in heads]
        v = [v_ref[rows, cs[h]] for h in heads]
        sc = [(_dot(qb[h], kTb[h]) * intra[h]).astype(bf16) for h in heads]
        s = [s_s[h] for h in heads]
        o = [_dot(sc[h], v[h]) + q_dec[h] * _dot(qb[h], s[h].astype(bf16)) for h in heads]
        for h in heads:
            s_s[h] = s_dec[h] * s[h] + _dot(kTb[h], (v[h].astype(f32) * k_dec[h]).astype(bf16))
        for h in heads:
            zs = zs_ref[rows, cs[h]].astype(f32)
            o_ref[rows, cs[h]] = (_head_norm(o[h], hng_ref[:, cs[h]]) * zs).astype(bf16)

    @pl.when(g == ng - 1)
    def _():
        s_out_ref[...] = s_s[...]


def _ret_prompt(P, KT, B, T, l, depth, prev, tabs, hng):
    R = P.shape[0]
    Lg = _pick(T, (1024, 512, 256, 128))
    ng = T // Lg
    col = lambda k: (lambda b, g: (b * ng + g, k))
    cos, sin, cosT, sinT = tabs
    in_specs = [
        pl.BlockSpec((Lg, W_MIX), col(3)),
        pl.BlockSpec((Lg, W_MIX), col(4)),
        pl.BlockSpec((Lg, W_MIX), col(5)),
        pl.BlockSpec((W_MIX, Lg), lambda b, g: (0, b * ng + g)),
        pl.BlockSpec((Lg, HALF), lambda b, g: (g, 0)),
        pl.BlockSpec((Lg, HALF), lambda b, g: (g, 0)),
        pl.BlockSpec((HALF, Lg), lambda b, g: (0, g)),
        pl.BlockSpec((HALF, Lg), lambda b, g: (0, g)),
        _layer_spec((1, W_MIX), l),
    ]
    args = [P, P, P, KT, cos, sin, cosT, sinT, hng]
    kern, xspecs, xargs, aliases = _chain(_ret_kernel, len(args), prev, 1)
    return pl.pallas_call(
        kern,
        grid=(B, ng),
        in_specs=in_specs + xspecs,
        out_specs=[
            pl.BlockSpec((Lg, W_MIX), col(0)),
            pl.BlockSpec((None, None, N_HEADS, D_HEAD, D_HEAD), lambda b, g: (l, b, 0, 0, 0)),
        ],
        out_shape=[
            jax.ShapeDtypeStruct((R, W_MIX), bf16),
            jax.ShapeDtypeStruct((depth, B, N_HEADS, D_HEAD, D_HEAD), f32),
        ],
        scratch_shapes=[pltpu.VMEM((N_HEADS, D_HEAD, D_HEAD), f32)],
        input_output_aliases=aliases,
        compiler_params=_cparams(("arbitrary", "arbitrary")),
        name="ret_prompt",
    )(*args, *xargs)


_SEQ_PER_STEP = 4


def _decode_kernel(um_ref, vm_ref, zsm_ref, qr_ref, vr_ref, zsr_ref, kt_ref, gi_ref,
                   conv_ref, n_ref, m_ref, c_ref, s_ref,
                   convw_ref, convb_ref, wq_ref, wk_ref, brow_ref, hngm_ref, skip_ref, hngr_ref,
                   cosd_ref, sind_ref, cosdT_ref, sindT_ref,
                   om_ref, or_ref, c_out_ref, s_out_ref, n_out_ref, m_out_ref, conv_out_ref,
                   vt_s, qrt_s, krt_s, qm_s, km_s, vr_s, sct_s, cqt_s, oacc_s, cm_s):
    s = pl.program_id(0)
    ns = pl.num_programs(0)
    NB = um_ref.shape[0]
    scale = D_HEAD ** -0.5
    lgs = _log_gammas()
    gammas = [math.exp(lg) for lg in lgs]
    lane = lax.broadcasted_iota(jnp.int32, (1, LANES), 1)

    @pl.when(s == 0)
    def _():
        u = um_ref[...].astype(f32)
        acc = convb_ref[...] + u * convw_ref[CONV_W - 1:CONV_W, :]
        for i in range(CONV_W - 1):
            acc = acc + conv_ref[i] * convw_ref[i:i + 1, :]
        cm = _silu(acc)
        cm_s[...] = cm
        for i in range(CONV_W - 2):
            conv_out_ref[i] = conv_ref[i + 1]
        conv_out_ref[CONV_W - 2] = u

        gb = jnp.concatenate([gi_ref[...], jnp.zeros((LANES - 8, NB), f32)], axis=0).T + brow_ref[...]
        lf = _log_sigmoid(gb)
        m_cols = jnp.concatenate([m_ref[...], jnp.zeros((8 - N_HEADS, NB), f32)], axis=0).T
        sc = jnp.zeros((NB, LANES), f32)
        mo = jnp.zeros((NB, LANES), f32)
        for h in range(N_HEADS):
            cs = slice(h * D_HEAD, (h + 1) * D_HEAD)
            chb = cm[:, cs].astype(bf16)
            q = _dot(chb, wq_ref[h])
            k = _dot(chb, wk_ref[h])
            qm_s[:, cs] = q
            km_s[:, cs] = k
            ig = gb[:, h:h + 1]
            lfh = lf[:, 4 + h:5 + h]
            mp = m_cols[:, h:h + 1]
            inter = lfh + mp
            m_t = jnp.maximum(inter, ig)
            w = jnp.exp(ig - m_t)
            a = jnp.exp(inter - m_t)
            nh = n_ref[:, cs]
            ws = w * jnp.sum(q * k, axis=-1, keepdims=True)
            den = ws + a * jnp.sum(nh * q, axis=-1, keepdims=True)
            den = jnp.maximum(jnp.abs(den), jnp.exp(-m_t))
            n_out_ref[:, cs] = a * nh + w * k
            for r, val in ((h, ws), (4 + h, a), (8 + h, den), (12 + h, w)):
                sc = sc + val * (lane == r).astype(f32)
            mo = mo + m_t * (lane == h).astype(f32)
        m_out_ref[...] = mo.T[0:N_HEADS, :]
        sct_s[...] = sc.T
        vt_s[...] = vm_ref[...].astype(f32).T
        vr_s[...] = vr_ref[...].astype(f32)
        cosd = cosd_ref[0:1, :]
        sind = sind_ref[0:1, :]
        cosdT = cosdT_ref[...]
        sindT = sindT_ref[...]
        for h in range(N_HEADS):
            lo = slice(h * D_HEAD, h * D_HEAD + HALF)
            hi = slice(h * D_HEAD + HALF, (h + 1) * D_HEAD)
            q1 = qr_ref[:, lo].astype(f32)
            q2 = qr_ref[:, hi].astype(f32)
            qrot = jnp.concatenate([q1 * cosd - q2 * sind, q1 * sind + q2 * cosd], axis=-1)
            qrt_s[h * D_HEAD:(h + 1) * D_HEAD, :] = qrot.T
            k1 = kt_ref[lo, :].astype(f32)
            k2 = kt_ref[hi, :].astype(f32)
            krt_s[lo, :] = (k1 * cosdT - k2 * sindT) * scale
            krt_s[hi, :] = (k1 * sindT + k2 * cosdT) * scale
        cqt_s[...] = jnp.zeros(cqt_s.shape, f32)

    def take_col(mat, mask):
        return jnp.sum(jnp.where(mask, mat, 0.0), axis=-1, keepdims=True)

    for j in range(_SEQ_PER_STEP):
        b = s * _SEQ_PER_STEP + j
        mask = lane == b
        for h in range(N_HEADS):
            cs = slice(h * D_HEAD, (h + 1) * D_HEAD)
            c = c_ref[j, h]
            q_row = qm_s[pl.ds(b, 1), cs]
            k_row = km_s[pl.ds(b, 1), cs]
            v_col = take_col(vt_s[cs, :], mask)
            a = take_col(sct_s[4 + h:5 + h, :], mask)
            w = take_col(sct_s[12 + h:13 + h, :], mask)
            cq = jnp.sum(c * q_row, axis=-1, keepdims=True)
            cqt_s[cs, :] = jnp.where(mask, cq, cqt_s[cs, :])
            c_out_ref[j, h] = a * c + (w * v_col) * k_row
            st = s_ref[j, h]
            q_col = take_col(qrt_s[cs, :], mask)
            k_col = take_col(krt_s[cs, :], mask)
            v_row = vr_s[pl.ds(b, 1), cs]
            qk = jnp.sum(q_col * k_col, axis=0, keepdims=True)
            o_row = qk * v_row + gammas[h] * jnp.sum(st * q_col, axis=0, keepdims=True)
            oacc_s[pl.ds(b, 1), cs] = o_row
            s_out_ref[j, h] = gammas[h] * st + k_col * v_row

    @pl.when(s == ns - 1)
    def _():
        for h in range(N_HEADS):
            cs = slice(h * D_HEAD, (h + 1) * D_HEAD)
            ht = ((sct_s[h:h + 1, :] * vt_s[cs, :] + sct_s[4 + h:5 + h, :] * cqt_s[cs, :])
                  / sct_s[8 + h:9 + h, :])
            hn = _head_norm(ht.T, hngm_ref[:, cs])
            zsm = zsm_ref[:, cs].astype(f32)
            om_ref[:, cs] = ((hn + skip_ref[:, cs] * cm_s[:, cs]) * zsm).astype(bf16)
            zsr = zsr_ref[:, cs].astype(f32)
            or_ref[:, cs] = (_head_norm(oacc_s[:, cs], hngr_ref[:, cs]) * zsr).astype(bf16)


def _decode_step(P, KT, GI, OM, OR, prev, l, BT, conv_st, n_st, m_st, C_st, S_st,
                 conv_w, conv_b, wq, wk, brow, hngm, skip, hngr, dtabs):
    R = P.shape[0]
    NB = R - BT
    assert NB == LANES and BT % NB == 0 and NB % _SEQ_PER_STEP == 0
    rb = BT // NB
    depth = C_st.shape[0]
    bb = _SEQ_PER_STEP
    cosd, sind, cosdT, sindT = dtabs
    pcol = lambda k: (lambda s: (rb, k))
    rep2 = lambda s: (0, 0)
    st_spec = pl.BlockSpec((None, bb, N_HEADS, D_HEAD, D_HEAD), lambda s: (l, s, 0, 0, 0))
    conv_spec = _layer_spec((CONV_W - 1, NB, W_MIX), l)
    n_spec = _layer_spec((NB, W_MIX), l)
    m_spec = _layer_spec((N_HEADS, NB), l)
    in_specs = [
        pl.BlockSpec((NB, W_MIX), pcol(0)), pl.BlockSpec((NB, W_MIX), pcol(1)),
        pl.BlockSpec((NB, W_MIX), pcol(2)), pl.BlockSpec((NB, W_MIX), pcol(3)),
        pl.BlockSpec((NB, W_MIX), pcol(4)), pl.BlockSpec((NB, W_MIX), pcol(5)),
        pl.BlockSpec((W_MIX, NB), lambda s: (0, rb)),
        pl.BlockSpec((2 * N_HEADS, NB), lambda s: (0, rb)),
        conv_spec, n_spec, m_spec, st_spec, st_spec,
        _layer_spec((CONV_W, W_MIX), l), _layer_spec((1, W_MIX), l),
        _layer_spec((N_HEADS, D_HEAD, D_HEAD), l), _layer_spec((N_HEADS, D_HEAD, D_HEAD), l),
        _layer_spec((1, LANES), l),
        _layer_spec((1, W_MIX), l), _layer_spec((1, W_MIX), l), _layer_spec((1, W_MIX), l),
        pl.BlockSpec((8, HALF), rep2), pl.BlockSpec((8, HALF), rep2),
        pl.BlockSpec((HALF, LANES), rep2), pl.BlockSpec((HALF, LANES), rep2),
    ]
    args = [P, P, P, P, P, P, KT, GI, conv_st, n_st, m_st, C_st, S_st,
            conv_w, conv_b, wq, wk, brow, hngm, skip, hngr, cosd, sind, cosdT, sindT]
    carried = [OM, OR] + (list(prev) if prev is not None else [])
    kern, xspecs, xargs, aliases = _chain(_decode_kernel, len(args), carried, 0)
    out_row = pl.BlockSpec((NB, W_MIX), lambda s: (rb, 0))
    return pl.pallas_call(
        kern,
        grid=(NB // bb,),
        in_specs=in_specs + xspecs,
        out_specs=[out_row, out_row, st_spec, st_spec, n_spec, m_spec, conv_spec],
        out_shape=[
            jax.ShapeDtypeStruct(OM.shape, bf16),
            jax.ShapeDtypeStruct(OR.shape, bf16),
            jax.ShapeDtypeStruct((depth, NB, N_HEADS, D_HEAD, D_HEAD), f32),
            jax.ShapeDtypeStruct((depth, NB, N_HEADS, D_HEAD, D_HEAD), f32),
            jax.ShapeDtypeStruct((depth, NB, W_MIX), f32),
            jax.ShapeDtypeStruct((depth, N_HEADS, NB), f32),
            jax.ShapeDtypeStruct((depth, CONV_W - 1, NB, W_MIX), f32),
        ],
        scratch_shapes=[
            pltpu.VMEM((W_MIX, NB), f32),
            pltpu.VMEM((W_MIX, NB), f32),
            pltpu.VMEM((W_MIX, NB), f32),
            pltpu.VMEM((NB, W_MIX), f32),
            pltpu.VMEM((NB, W_MIX), f32),
            pltpu.VMEM((NB, W_MIX), f32),
            pltpu.VMEM((LANES, NB), f32),
            pltpu.VMEM((W_MIX, NB), f32),
            pltpu.VMEM((NB, W_MIX), f32),
            pltpu.VMEM((NB, W_MIX), f32),
        ],
        input_output_aliases=aliases,
        compiler_params=_cparams(("arbitrary",)),
        name="decode_step",
    )(*args, *xargs)


_MERGE_SLAB = 256


def _merge_kernel(h_ref, om_ref, or_ref, wgm_ref, wgr_ref, bgm_ref, bgr_ref, wpm_ref, wpr_ref, o_ref):
    for c0 in range(0, o_ref.shape[1], _MERGE_SLAB):
        cols = slice(c0, c0 + _MERGE_SLAB)
        gm = jax.nn.sigmoid(_dot(h_ref[...], wgm_ref[:, cols].astype(bf16)) + bgm_ref[:, cols])
        gr = jax.nn.sigmoid(_dot(h_ref[...], wgr_ref[:, cols].astype(bf16)) + bgr_ref[:, cols])
        pm = _dot(om_ref[...], wpm_ref[:, cols].astype(bf16))
        pr = _dot(or_ref[...], wpr_ref[:, cols].astype(bf16))
        o_ref[:, cols] = (gm * pm + gr * pr).astype(bf16)


def _merge(h, OM, OR, w_gate, b_gate3, w_pm, w_pr, l):
    R = h.shape[0]
    tm = _pick(R, (1664, 1280, 640, 384, 128))
    tn = 512
    nj = D_MODEL // tn
    return pl.pallas_call(
        _merge_kernel,
        grid=(R // tm, nj),
        in_specs=[
            pl.BlockSpec((tm, D_MODEL), lambda i, j: (i, 0)),
            pl.BlockSpec((tm, W_MIX), lambda i, j: (i, 0)),
            pl.BlockSpec((tm, W_MIX), lambda i, j: (i, 0)),
            pl.BlockSpec((None, D_MODEL, tn), lambda i, j: (l, 0, j)),
            pl.BlockSpec((None, D_MODEL, tn), lambda i, j: (l, 0, j + nj)),
            pl.BlockSpec((None, 1, tn), lambda i, j: (l, 0, j)),
            pl.BlockSpec((None, 1, tn), lambda i, j: (l, 0, j + nj)),
            pl.BlockSpec((None, W_MIX, tn), lambda i, j: (l, 0, j)),
            pl.BlockSpec((None, W_MIX, tn), lambda i, j: (l, 0, j)),
        ],
        out_specs=pl.BlockSpec((tm, tn), lambda i, j: (i, j)),
        out_shape=jax.ShapeDtypeStruct((R, D_MODEL), bf16),
        compiler_params=_cparams(("arbitrary", "arbitrary")),
        name="merge",
    )(h, OM, OR, w_gate, w_gate, b_gate3, b_gate3, w_pm, w_pr)


def _outproj_norm_kernel(mg_ref, wo_ref, x_ref, g_ref, o_ref, h_ref):
    y = x_ref[...] + _dot(mg_ref[...], wo_ref[...])
    o_ref[...] = y
    ms = jnp.mean(y * y, axis=-1, keepdims=True)
    h_ref[...] = (y * lax.rsqrt(ms + EPS) * g_ref[...]).astype(bf16)


def _outproj_final_kernel(mg_ref, wo_ref, x_ref, g_ref, y_ref):
    y = x_ref[...] + _dot(mg_ref[...], wo_ref[...])
    ms = jnp.mean(y * y, axis=-1, keepdims=True)
    y_ref[...] = y * lax.rsqrt(ms + EPS) * g_ref[...]


def _wo_spec():
    return pl.BlockSpec((D_MODEL, D_MODEL), lambda i: (0, 0), pipeline_mode=pl.Buffered(1))


def _outproj(merged, w_o_bf, x, norm_g3, l_next, row0, nrows, x_local, prev):
    R = merged.shape[0]
    tm = _pick(nrows, (640, 512, 384, 128))
    assert row0 % tm == 0
    off = row0 // tm
    row_g = pl.BlockSpec((tm, D_MODEL), lambda i: (i + off, 0))
    row_x = pl.BlockSpec((tm, D_MODEL), lambda i: (i, 0)) if x_local else row_g
    kern, xspecs, xargs, aliases = _chain(_outproj_norm_kernel, 4, prev, 0)
    return pl.pallas_call(
        kern,
        grid=(nrows // tm,),
        in_specs=[row_g, _wo_spec(), row_x, _layer_spec((1, D_MODEL), l_next)] + xspecs,
        out_specs=[row_g, row_g],
        out_shape=[jax.ShapeDtypeStruct((R, D_MODEL), f32), jax.ShapeDtypeStruct((R, D_MODEL), bf16)],
        input_output_aliases=aliases,
        compiler_params=_cparams(("arbitrary",)),
        name="outproj_norm",
    )(merged, w_o_bf, x, norm_g3, *xargs)


def _outproj_final(merged, w_o_bf, x, g, row0, nrows):
    tm = _pick(nrows, (512, 256, 128))
    assert row0 % tm == 0
    off = row0 // tm
    row_in = pl.BlockSpec((tm, D_MODEL), lambda i: (i + off, 0))
    return pl.pallas_call(
        _outproj_final_kernel,
        grid=(nrows // tm,),
        in_specs=[row_in, _wo_spec(), row_in, pl.BlockSpec((1, D_MODEL), lambda i: (0, 0))],
        out_specs=pl.BlockSpec((tm, D_MODEL), lambda i: (i, 0)),
        out_shape=jax.ShapeDtypeStruct((nrows, D_MODEL), f32),
        compiler_params=_cparams(("arbitrary",)),
        name="outproj_final",
    )(merged, w_o_bf, x, g)


def kernel(x_prompt, x_sample, state_mlstm_C, state_mlstm_n, state_mlstm_m, state_mlstm_conv, state_ret_S,
           norm_g, w_in, conv_w, conv_b, w_qm, w_km, b_i, b_f, hn_m_g, skip_m, hn_r_g,
           w_pm, w_pr, w_gate, b_gate, w_o, final_g):
    B, T, D = x_prompt.shape
    NB = x_sample.shape[0]
    depth = w_in.shape[0]
    BT = B * T
    assert D == D_MODEL and x_sample.shape[1] == 1 and T % CHUNK == 0 and depth >= 2

    R = BT + NB
    xp = x_prompt.reshape(BT, D)
    xs = x_sample.reshape(NB, D)

    w_inT = jnp.swapaxes(w_in, 1, 2)
    assert math.log2(D_HEAD) % 2 == 0
    wq = (w_qm * D_HEAD ** -0.5).astype(bf16)
    wk = w_km.astype(bf16)
    wkT = jnp.swapaxes(w_km, 2, 3).astype(bf16)
    bias = jnp.pad(jnp.concatenate([b_i, b_f], axis=1), ((0, 0), (0, LANES - 2 * N_HEADS)))
    brow = bias.reshape(depth, 1, LANES)
    bcol = bias[:, :8].reshape(depth, 8, 1)
    b_gate3 = b_gate.reshape(depth, 1, 2 * D)
    norm_g3 = norm_g.reshape(depth, 1, D)
    conv_b3 = conv_b.reshape(depth, 1, W_MIX)
    hngm = hn_m_g.reshape(depth, 1, W_MIX)
    skp = skip_m.reshape(depth, 1, W_MIX)
    hngr = hn_r_g.reshape(depth, 1, W_MIX)
    conv_st = jnp.transpose(state_mlstm_conv, (0, 2, 1, 3))
    m_st = jnp.swapaxes(state_mlstm_m, 1, 2)
    n_st = state_mlstm_n.reshape(depth, NB, W_MIX)

    cos, sin, cosT, sinT, cosd, sind, cosdT, sindT = _rope_tables(T)

    prev_m = prev_r = prev_d = None
    h = _rmsnorm_into(xp, norm_g3, 0, R, 0, None)
    h = _rmsnorm_into(xs, norm_g3, 0, R, BT, h)
    x = None
    for l in range(depth):
        P, KT, GI, w_o_bf = _inproj(h, w_inT, w_o, l)
        OM, *prev_m = _mlstm_prompt(P, GI, B, T, l, depth, prev_m, conv_w, conv_b3, wq, wkT,
                                    bcol, hngm, skp)
        OR, *prev_r = _ret_prompt(P, KT, B, T, l, depth, prev_r, (cos, sin, cosT, sinT), hngr)
        OM, OR, *prev_d = _decode_step(
            P, KT, GI, OM, OR, prev_d, l, BT, conv_st, n_st, m_st, state_mlstm_C, state_ret_S,
            conv_w, conv_b3, wq, wk, brow, hngm, skp, hngr, (cosd, sind, cosdT, sindT))
        merged = _merge(h, OM, OR, w_gate, b_gate3, w_pm, w_pr, l)
        if l + 1 == depth:
            break
        if x is None:
            x, h = _outproj(merged, w_o_bf, xp, norm_g3, l + 1, 0, BT, True, None)
            x, h = _outproj(merged, w_o_bf, xs, norm_g3, l + 1, BT, NB, True, (x, h))
        else:
            x, h = _outproj(merged, w_o_bf, x, norm_g3, l + 1, 0, R, False, None)

    fg = final_g.reshape(1, D)
    y_prompt = _outproj_final(merged, w_o_bf, x, fg, 0, BT).reshape(B, T, D)
    y_sample = _outproj_final(merged, w_o_bf, x, fg, BT, NB).reshape(NB, 1, D)
    Cp, np_, mp, convp = prev_m
    (Sp,) = prev_r
    Cs, Ss, n_s, m_s, conv_s = prev_d
    return (y_prompt, y_sample,
            Cp, np_.reshape(depth, B, N_HEADS, D_HEAD), mp[:, :, 0, 12:12 + N_HEADS], convp, Sp,
            Cs, n_s.reshape(depth, NB, N_HEADS, D_HEAD), jnp.swapaxes(m_s, 1, 2),
            jnp.transpose(conv_s, (0, 2, 1, 3)), Ss)
```

```python
import functools
import math

import jax
import jax.numpy as jnp
from jax import lax
from jax.experimental import pallas as pl
from jax.experimental.pallas import tpu as pltpu

f32 = jnp.float32
bf16 = jnp.bfloat16

D_MODEL = 2048
N_HEADS = 4
D_HEAD = 256
W_MIX = N_HEADS * D_HEAD
CONV_W = 4
CHUNK = 128
ROPE_BASE = 10000.0
EPS = 1e-6
NEG = -1e30
PAST_LEN = 16384
LANES = 128
HALF = D_HEAD // 2
VMEM_LIMIT = 62 * 1024 * 1024

OFF_IF = 3 * W_MIX
OFF_RET = 3 * W_MIX + 2 * N_HEADS


def _pick(n, cands):
    for c in cands:
        if n % c == 0:
            return c
    return n


def _cparams(sem):
    return pltpu.CompilerParams(dimension_semantics=sem, vmem_limit_bytes=VMEM_LIMIT)


def _dot(a, b):
    return jnp.dot(a, b, preferred_element_type=f32)


def _dot_nt(a, b):
    return lax.dot_general(a, b, (((1,), (1,)), ((), ())), preferred_element_type=f32)


def _split3(x):
    hi = x.astype(bf16).astype(f32)
    r = x - hi
    mid = r.astype(bf16).astype(f32)
    lo = (r - mid).astype(bf16).astype(f32)
    return hi, mid, lo


def _silu(x):
    return x * (0.5 * jnp.tanh(0.5 * x) + 0.5)


def _log_sigmoid(x):
    return jnp.minimum(x, 0.0) - jnp.log1p(jnp.exp(-jnp.abs(x)))


def _head_norm(hc, g):
    mu = jnp.mean(hc, axis=-1, keepdims=True)
    d = hc - mu
    var = jnp.mean(d * d, axis=-1, keepdims=True)
    return d * lax.rsqrt(var + EPS) * g


def _rope_kernel(inv_ref, invT_ref, cos_ref, sin_ref, cosT_ref, sinT_ref,
                 cosd_ref, sind_ref, cosdT_ref, sindT_ref):
    T = cos_ref.shape[0]
    pos = lax.broadcasted_iota(jnp.int32, (T, HALF), 0).astype(f32)
    ang = pos * inv_ref[...]
    cos = jnp.cos(ang)
    sin = jnp.sin(ang)
    cos_ref[...] = cos
    sin_ref[...] = sin
    cosT_ref[...] = cos.T
    sinT_ref[...] = sin.T
    angd = jnp.full((8, HALF), float(PAST_LEN), f32) * inv_ref[...]
    cosd_ref[...] = jnp.cos(angd)
    sind_ref[...] = jnp.sin(angd)
    angdT = jnp.full((HALF, LANES), float(PAST_LEN), f32) * invT_ref[...]
    cosdT_ref[...] = jnp.cos(angdT)
    sindT_ref[...] = jnp.sin(angdT)


def _rope_tables(T):
    inv = ROPE_BASE ** (-jnp.arange(0, HALF, dtype=f32) / HALF)
    shapes = [(T, HALF), (T, HALF), (HALF, T), (HALF, T),
              (8, HALF), (8, HALF), (HALF, LANES), (HALF, LANES)]
    return pl.pallas_call(
        _rope_kernel,
        out_shape=[jax.ShapeDtypeStruct(s, f32) for s in shapes],
        name="rope_tables",
    )(inv.reshape(1, HALF), inv.reshape(HALF, 1))


def _rms_kernel(x_ref, g_ref, o_ref):
    x = x_ref[...]
    ms = jnp.mean(x * x, axis=-1, keepdims=True)
    o_ref[...] = (x * lax.rsqrt(ms + EPS) * g_ref[...]).astype(o_ref.dtype)


def _rmsnorm_into(x, norm_g3, l, R, row0, prev):
    n, D = x.shape
    tr = _pick(n, (512, 256, 128))
    assert row0 % tr == 0
    off = row0 // tr
    kern, xspecs, xargs, aliases = _chain(_rms_kernel, 2, None if prev is None else [prev], 0)
    return pl.pallas_call(
        kern,
        grid=(n // tr,),
        in_specs=[pl.BlockSpec((tr, D), lambda i: (i, 0)), _layer_spec((1, D), l)] + xspecs,
        out_specs=pl.BlockSpec((tr, D), lambda i: (i + off, 0)),
        out_shape=jax.ShapeDtypeStruct((R, D), bf16),
        input_output_aliases=aliases,
        compiler_params=_cparams(("arbitrary",)),
        name="rmsnorm",
    )(x, norm_g3, *xargs)


_TN_IN = 1024
_TPG = W_MIX // _TN_IN
_NJ_P = 6 * _TPG
_NJ_K = _TPG
_GATE_PAD = 16


def _inproj_row_offset(j):
    jp = jnp.minimum(j, _NJ_P - 1)
    grp = jp // _TPG
    sub = jp - grp * _TPG
    src_grp = jnp.where(grp < 3, grp, jnp.where(grp == 3, 3, grp + 1))
    off_p = src_grp * W_MIX + jnp.where(grp < 3, 0, 2 * N_HEADS) + sub * _TN_IN
    off_k = OFF_RET + W_MIX + jnp.clip(j - _NJ_P, 0, _NJ_K - 1) * _TN_IN
    return jnp.where(j < _NJ_P, off_p, off_k)


def _inproj_kernel(h_ref, wt_ref, wif_ref, wo_ref, p_ref, kt_ref, gi_ref, wob_ref):
    j = pl.program_id(1)
    wob_ref[...] = wo_ref[...].astype(bf16)

    def weights():
        return wt_ref[0].astype(bf16)

    is_z = (j // _TPG == 2) | (j // _TPG == 5)
    last = _NJ_P + _NJ_K - 1

    @pl.when((j < _NJ_P) & jnp.logical_not(is_z))
    def _():
        p_ref[...] = _dot_nt(h_ref[...], weights()).astype(bf16)

    @pl.when((j < _NJ_P) & is_z)
    def _():
        p_ref[...] = _silu(_dot_nt(h_ref[...], weights())).astype(bf16)

    if _NJ_K > 1:
        @pl.when((j >= _NJ_P) & (j < last))
        def _():
            kt_ref[...] = _dot_nt(weights(), h_ref[...]).astype(bf16)

    @pl.when(j == last)
    def _():
        pad = jnp.zeros((_GATE_PAD - 2 * N_HEADS, D_MODEL), f32)
        w_all = jnp.concatenate([wt_ref[0], wif_ref[...], pad], axis=0).astype(bf16)
        res = _dot_nt(w_all, h_ref[...])
        kt_ref[...] = res[0:_TN_IN, :].astype(bf16)
        gi_ref[...] = res[_TN_IN:_TN_IN + 2 * N_HEADS, :]


def _inproj(h, w_inT, w_o, l):
    R = h.shape[0]
    tm = _pick(R, (1664, 1280, 640, 384, 128))
    tn = _TN_IN
    nj = _NJ_P + _NJ_K
    nsteps = (R // tm) * nj
    rps = next(r for r in (32, 64, 128, 256, 512, 1024, 2048) if D_MODEL // r <= nsteps)
    wo_blk = lambda i, j: jnp.minimum(i * nj + j, D_MODEL // rps - 1)
    return pl.pallas_call(
        _inproj_kernel,
        grid=(R // tm, nj),
        in_specs=[
            pl.BlockSpec((tm, D_MODEL), lambda i, j: (i, 0)),
            pl.BlockSpec((pl.Element(1), pl.Element(tn), pl.Element(D_MODEL)),
                         lambda i, j: (l, pl.multiple_of(_inproj_row_offset(j), 8), 0)),
            pl.BlockSpec((None, 2 * N_HEADS, D_MODEL), lambda i, j: (l, OFF_IF // (2 * N_HEADS), 0)),
            pl.BlockSpec((None, rps, D_MODEL), lambda i, j: (l, wo_blk(i, j), 0)),
        ],
        out_specs=[
            pl.BlockSpec((tm, tn), lambda i, j: (i, jnp.minimum(j, _NJ_P - 1))),
            pl.BlockSpec((tn, tm), lambda i, j: (jnp.clip(j - _NJ_P, 0, _NJ_K - 1), i)),
            pl.BlockSpec((2 * N_HEADS, tm), lambda i, j: (0, i)),
            pl.BlockSpec((rps, D_MODEL), lambda i, j: (wo_blk(i, j), 0)),
        ],
        out_shape=[
            jax.ShapeDtypeStruct((R, 6 * W_MIX), bf16),
            jax.ShapeDtypeStruct((W_MIX, R), bf16),
            jax.ShapeDtypeStruct((2 * N_HEADS, R), f32),
            jax.ShapeDtypeStruct((D_MODEL, D_MODEL), bf16),
        ],
        compiler_params=_cparams(("arbitrary", "arbitrary")),
        name="inproj",
    )(h, w_inT, w_inT, w_o)


def _mlstm_kernel(um_ref, vm_ref, zs_ref, gi_ref, convw_ref, convb_ref, wq_ref, wkT_ref,
                  bcol_ref, hng_ref, skip_ref,
                  om_ref, c_out_ref, n_out_ref, m_out_ref, conv_out_ref,
                  tail_s, cta_s, m_s):
    g = pl.program_id(1)
    ng = pl.num_programs(1)
    Lg = um_ref.shape[0]
    L = CHUNK
    DA = D_HEAD + LANES
    heads = range(N_HEADS)

    @pl.when(g == 0)
    def _():
        tail_s[...] = jnp.zeros(tail_s.shape, bf16)
        cta_s[...] = jnp.zeros(cta_s.shape, f32)
        m_s[...] = jnp.full(m_s.shape, NEG, f32)

    srow = lax.broadcasted_iota(jnp.int32, (L, 2 * L), 0)
    scol = lax.broadcasted_iota(jnp.int32, (L, 2 * L), 1)
    shift_mat = jnp.concatenate([(scol == srow + L - s).astype(bf16) for s in range(1, CONV_W)], axis=0)

    row = lax.broadcasted_iota(jnp.int32, (L, L), 0)
    col = lax.broadcasted_iota(jnp.int32, (L, L), 1)
    causal = row >= col
    triu = (row <= col).astype(bf16)
    lane = col
    bcol = bcol_ref[...]
    ones_b = jnp.ones((L, LANES), bf16)
    ones_f = jnp.ones((L, LANES), f32)

    for c in range(Lg // L):
        rows = slice(c * L, (c + 1) * L)
        gtb = gi_ref[:, rows] + bcol
        lft = _log_sigmoid(gtb)
        br3 = _dot(jnp.concatenate(_split3(lft), axis=0).astype(bf16), triu)
        br_all = br3[0:8, :] + br3[8:16, :] + br3[16:24, :]
        cols = jnp.concatenate([gtb, br_all, jnp.zeros((LANES - 16, L), f32)], axis=0).T

        u_cur = um_ref[rows, :]
        u_prev = tail_s[...] if c == 0 else um_ref[(c - 1) * L:c * L, :]
        delayed = _dot(shift_mat, jnp.concatenate([u_prev, u_cur], axis=0))
        acc = convb_ref[...] + u_cur.astype(f32) * convw_ref[CONV_W - 1:CONV_W, :]
        for s in range(1, CONV_W):
            acc = acc + delayed[(s - 1) * L:s * L, :] * convw_ref[CONV_W - 1 - s:CONV_W - s, :]
        cm = _silu(acc)

        cs = [slice(h * D_HEAD, (h + 1) * D_HEAD) for h in heads]
        ch = [cm[:, cs[h]] for h in heads]
        chb = [x.astype(bf16) for x in ch]
        qb = [_dot(chb[h], wq_ref[h]).astype(bf16) for h in heads]
        kTb = [_dot_nt(wkT_ref[h], chb[h]).astype(bf16) for h in heads]
        v = [vm_ref[rows, cs[h]] for h in heads]
        bc = [cols[:, 12 + h:13 + h] for h in heads]
        dlog = [jnp.where(causal, bc[h] - br_all[4 + h:5 + h, :] + gtb[h:h + 1, :], NEG) for h in heads]
        dmax = jnp.zeros((L, LANES), f32)
        for h in heads:
            dmax = jnp.where(lane == 12 + h, jnp.max(dlog[h], axis=-1, keepdims=True), dmax)
        m_row = m_s[0:1, :]
        inter = cols + m_row
        m_all = jnp.maximum(inter, dmax)
        a_all = jnp.exp(inter - m_all)
        em_all = jnp.exp(-m_all)
        m_new = m_all[L - 1:L, :]
        b_last = cols[L - 1:L, :]
        ic_al = pltpu.roll(cols, 12, axis=1)
        wk_all = jnp.exp(b_last - cols + ic_al - m_new)
        dec_row = jnp.exp(b_last + m_row - m_new)
        m_s[...] = jnp.broadcast_to(m_new, (8, LANES))
        hl = [slice(12 + h, 13 + h) for h in heads]
        ws = [(jnp.exp(dlog[h] - m_all[:, hl[h]]) * _dot(qb[h], kTb[h])).astype(bf16) for h in heads]
        cta = [cta_s[h] for h in heads]
        nd = [_dot(ws[h], jnp.concatenate([v[h], ones_b], axis=1))
              + a_all[:, hl[h]] * _dot(qb[h], cta[h].astype(bf16)) for h in heads]
        for h in heads:
            vaug = jnp.concatenate([v[h].astype(f32), ones_f], axis=1)
            cta_s[h] = dec_row[:, hl[h]] * cta[h] + _dot(kTb[h], (wk_all[:, hl[h]] * vaug).astype(bf16))
        for h in heads:
            den = jnp.maximum(jnp.abs(nd[h][:, D_HEAD:DA]), em_all[:, hl[h]])
            rden = 1.0 / den
            hc = jnp.concatenate([nd[h][:, 0:HALF] * rden, nd[h][:, HALF:D_HEAD] * rden], axis=1)
            hn = _head_norm(hc, hng_ref[:, cs[h]])
            zs = zs_ref[rows, cs[h]].astype(f32)
            om_ref[rows, cs[h]] = ((hn + skip_ref[:, cs[h]] * ch[h]) * zs).astype(bf16)

    tail_s[...] = um_ref[Lg - L:Lg, :]

    @pl.when(g == ng - 1)
    def _():
        for h in heads:
            st = cta_s[h]
            c_out_ref[h] = st[:, 0:D_HEAD].T
            n_out_ref[h] = st[:, D_HEAD:DA].T[0:1, :]
        m_out_ref[...] = m_s[...]
        conv_out_ref[...] = um_ref[Lg - 8:Lg, :].astype(f32)[8 - (CONV_W - 1):8, :]


def _drop_refs(kernel_fn, n_in, n_drop, *refs):
    return kernel_fn(*refs[:n_in], *refs[n_in + n_drop:])


def _chain(kernel_fn, n_in, prev, first_out):
    if prev is None:
        return kernel_fn, [], [], {}
    specs = [pl.BlockSpec(memory_space=pl.ANY)] * len(prev)
    aliases = {n_in + k: first_out + k for k in range(len(prev))}
    return functools.partial(_drop_refs, kernel_fn, n_in, len(prev)), specs, list(prev), aliases


def _layer_spec(shape, l):
    nd = len(shape)
    return pl.BlockSpec((None,) + tuple(shape), lambda *_: (l,) + (0,) * nd)


def _mlstm_prompt(P, GI, B, T, l, depth, prev, conv_w, conv_b, wq, wkT, bcol, hng, skip):
    R = P.shape[0]
    Lg = _pick(T, (512, 256, 128))
    ng = T // Lg
    col = lambda k: (lambda b, g: (b * ng + g, k))
    in_specs = [
        pl.BlockSpec((Lg, W_MIX), col(0)),
        pl.BlockSpec((Lg, W_MIX), col(1)),
        pl.BlockSpec((Lg, W_MIX), col(2)),
        pl.BlockSpec((2 * N_HEADS, Lg), lambda b, g: (0, b * ng + g)),
        _layer_spec((CONV_W, W_MIX), l),
        _layer_spec((1, W_MIX), l),
        _layer_spec((N_HEADS, D_HEAD, D_HEAD), l),
        _layer_spec((N_HEADS, D_HEAD, D_HEAD), l),
        _layer_spec((8, 1), l),
        _layer_spec((1, W_MIX), l),
        _layer_spec((1, W_MIX), l),
    ]
    args = [P, P, P, GI, conv_w, conv_b, wq, wkT, bcol, hng, skip]
    kern, xspecs, xargs, aliases = _chain(_mlstm_kernel, len(args), prev, 1)
    return pl.pallas_call(
        kern,
        grid=(B, ng),
        in_specs=in_specs + xspecs,
        out_specs=[
            pl.BlockSpec((Lg, W_MIX), col(0)),
            pl.BlockSpec((None, None, N_HEADS, D_HEAD, D_HEAD), lambda b, g: (l, b, 0, 0, 0)),
            pl.BlockSpec((None, None, N_HEADS, 1, D_HEAD), lambda b, g: (l, b, 0, 0, 0)),
            pl.BlockSpec((None, None, 8, LANES), lambda b, g: (l, b, 0, 0)),
            pl.BlockSpec((None, None, CONV_W - 1, W_MIX), lambda b, g: (l, b, 0, 0)),
        ],
        out_shape=[
            jax.ShapeDtypeStruct((R, W_MIX), bf16),
            jax.ShapeDtypeStruct((depth, B, N_HEADS, D_HEAD, D_HEAD), f32),
            jax.ShapeDtypeStruct((depth, B, N_HEADS, 1, D_HEAD), f32),
            jax.ShapeDtypeStruct((depth, B, 8, LANES), f32),
            jax.ShapeDtypeStruct((depth, B, CONV_W - 1, W_MIX), f32),
        ],
        scratch_shapes=[
            pltpu.VMEM((CHUNK, W_MIX), bf16),
            pltpu.VMEM((N_HEADS, D_HEAD, D_HEAD + LANES), f32),
            pltpu.VMEM((8, LANES), f32),
        ],
        input_output_aliases=aliases,
        compiler_params=_cparams(("arbitrary", "arbitrary")),
        name="mlstm_prompt",
    )(*args, *xargs)


def _log_gammas():
    return [math.log(1.0 - 2.0 ** (-5.0 - h)) for h in range(N_HEADS)]


def _ret_kernel(q_ref, v_ref, zs_ref, kt_ref, cos_ref, sin_ref, cosT_ref, sinT_ref, hng_ref,
                o_ref, s_out_ref, s_s):
    g = pl.program_id(1)
    ng = pl.num_programs(1)
    Lg = q_ref.shape[0]
    L = CHUNK

    @pl.when(g == 0)
    def _():
        s_s[...] = jnp.zeros(s_s.shape, f32)

    row = lax.broadcasted_iota(jnp.int32, (L, L), 0)
    col = lax.broadcasted_iota(jnp.int32, (L, L), 1)
    causal = row >= col
    diff = (row - col).astype(f32)
    tcol = lax.broadcasted_iota(jnp.int32, (L, 1), 0).astype(f32)
    lgs = _log_gammas()
    intra = [jnp.where(causal, jnp.exp(diff * lg), 0.0) for lg in lgs]
    q_dec = [jnp.exp((tcol + 1.0) * lg) for lg in lgs]
    k_dec = [jnp.exp((L - 1.0 - tcol) * lg) for lg in lgs]
    s_dec = [math.exp(L * lg) for lg in lgs]
    kscale = D_HEAD ** -0.5

    for c in range(Lg // L):
        rows = slice(c * L, (c + 1) * L)
        cos = cos_ref[rows, :]
        sin = sin_ref[rows, :]
        cosT = cosT_ref[:, rows] * kscale
        sinT = sinT_ref[:, rows] * kscale
        heads = range(N_HEADS)
        cs = [slice(h * D_HEAD, (h + 1) * D_HEAD) for h in heads]
        lo = [slice(h * D_HEAD, h * D_HEAD + HALF) for h in heads]
        hi = [slice(h * D_HEAD + HALF, (h + 1) * D_HEAD) for h in heads]
        q1 = [q_ref[rows, lo[h]].astype(f32) for h in heads]
        q2 = [q_ref[rows, hi[h]].astype(f32) for h in heads]
        qb = [jnp.concatenate([q1[h] * cos - q2[h] * sin, q1[h] * sin + q2[h] * cos], axis=-1).astype(bf16)
              for h in heads]
        k1 = [kt_ref[lo[h], rows].astype(f32) for h in heads]
        k2 = [kt_ref[hi[h], rows].astype(f32) for h in heads]
        kTb = [jnp.concatenate([k1[h] * cosT - k2[h] * sinT, k1[h] * sinT + k2[h] * cosT],
                               axis=0).astype(bf16) for h in heads]
        v = [v_ref[rows, cs[h]] for h in heads]
        sc = [(_dot(qb[h], kTb[h]) * intra[h]).astype(bf16) for h in heads]
        s = [s_s[h] for h in heads]
        o = [_dot(sc[h], v[h]) + q_dec[h] * _dot(qb[h], s[h].astype(bf16)) for h in heads]
        for h in heads:
            s_s[h] = s_dec[h] * s[h] + _dot(kTb[h], (v[h].astype(f32) * k_dec[h]).astype(bf16))
        for h in heads:
            zs = zs_ref[rows, cs[h]].astype(f32)
            o_ref[rows, cs[h]] = (_head_norm(o[h], hng_ref[:, cs[h]]) * zs).astype(bf16)

    @pl.when(g == ng - 1)
    def _():
        s_out_ref[...] = s_s[...]


def _ret_prompt(P, KT, B, T, l, depth, prev, tabs, hng):
    R = P.shape[0]
    Lg = _pick(T, (1024, 512, 256, 128))
    ng = T // Lg
    col = lambda k: (lambda b, g: (b * ng + g, k))
    cos, sin, cosT, sinT = tabs
    in_specs = [
        pl.BlockSpec((Lg, W_MIX), col(3)),
        pl.BlockSpec((Lg, W_MIX), col(4)),
        pl.BlockSpec((Lg, W_MIX), col(5)),
        pl.BlockSpec((W_MIX, Lg), lambda b, g: (0, b * ng + g)),
        pl.BlockSpec((Lg, HALF), lambda b, g: (g, 0)),
        pl.BlockSpec((Lg, HALF), lambda b, g: (g, 0)),
        pl.BlockSpec((HALF, Lg), lambda b, g: (0, g)),
        pl.BlockSpec((HALF, Lg), lambda b, g: (0, g)),
        _layer_spec((1, W_MIX), l),
    ]
    args = [P, P, P, KT, cos, sin, cosT, sinT, hng]
    kern, xspecs, xargs, aliases = _chain(_ret_kernel, len(args), prev, 1)
    return pl.pallas_call(
        kern,
        grid=(B, ng),
        in_specs=in_specs + xspecs,
        out_specs=[
            pl.BlockSpec((Lg, W_MIX), col(0)),
            pl.BlockSpec((None, None, N_HEADS, D_HEAD, D_HEAD), lambda b, g: (l, b, 0, 0, 0)),
        ],
        out_shape=[
            jax.ShapeDtypeStruct((R, W_MIX), bf16),
            jax.ShapeDtypeStruct((depth, B, N_HEADS, D_HEAD, D_HEAD), f32),
        ],
        scratch_shapes=[pltpu.VMEM((N_HEADS, D_HEAD, D_HEAD), f32)],
        input_output_aliases=aliases,
        compiler_params=_cparams(("arbitrary", "arbitrary")),
        name="ret_prompt",
    )(*args, *xargs)


_SEQ_PER_STEP = 4


def _decode_kernel(um_ref, vm_ref, zsm_ref, qr_ref, vr_ref, zsr_ref, kt_ref, gi_ref,
                   conv_ref, n_ref, m_ref, c_ref, s_ref,
                   convw_ref, convb_ref, wq_ref, wk_ref, brow_ref, hngm_ref, skip_ref, hngr_ref,
                   cosd_ref, sind_ref, cosdT_ref, sindT_ref,
                   om_ref, or_ref, c_out_ref, s_out_ref, n_out_ref, m_out_ref, conv_out_ref,
                   vt_s, qrt_s, krt_s, qm_s, km_s, vr_s, sct_s, cqt_s, oacc_s, cm_s):
    s = pl.program_id(0)
    ns = pl.num_programs(0)
    NB = um_ref.shape[0]
    scale = D_HEAD ** -0.5
    lgs = _log_gammas()
    gammas = [math.exp(lg) for lg in lgs]
    lane = lax.broadcasted_iota(jnp.int32, (1, LANES), 1)

    @pl.when(s == 0)
    def _():
        u = um_ref[...].astype(f32)
        acc = convb_ref[...] + u * convw_ref[CONV_W - 1:CONV_W, :]
        for i in range(CONV_W - 1):
            acc = acc + conv_ref[i] * convw_ref[i:i + 1, :]
        cm = _silu(acc)
        cm_s[...] = cm
        for i in range(CONV_W - 2):
            conv_out_ref[i] = conv_ref[i + 1]
        conv_out_ref[CONV_W - 2] = u

        gb = jnp.concatenate([gi_ref[...], jnp.zeros((LANES - 8, NB), f32)], axis=0).T + brow_ref[...]
        lf = _log_sigmoid(gb)
        m_cols = jnp.concatenate([m_ref[...], jnp.zeros((8 - N_HEADS, NB), f32)], axis=0).T
        sc = jnp.zeros((NB, LANES), f32)
        mo = jnp.zeros((NB, LANES), f32)
        for h in range(N_HEADS):
            cs = slice(h * D_HEAD, (h + 1) * D_HEAD)
            chb = cm[:, cs].astype(bf16)
            q = _dot(chb, wq_ref[h])
            k = _dot(chb, wk_ref[h])
            qm_s[:, cs] = q
            km_s[:, cs] = k
            ig = gb[:, h:h + 1]
            lfh = lf[:, 4 + h:5 + h]
            mp = m_cols[:, h:h + 1]
            inter = lfh + mp
            m_t = jnp.maximum(inter, ig)
            w = jnp.exp(ig - m_t)
            a = jnp.exp(inter - m_t)
            nh = n_ref[:, cs]
            ws = w * jnp.sum(q * k, axis=-1, keepdims=True)
            den = ws + a * jnp.sum(nh * q, axis=-1, keepdims=True)
            den = jnp.maximum(jnp.abs(den), jnp.exp(-m_t))
            n_out_ref[:, cs] = a * nh + w * k
            for r, val in ((h, ws), (4 + h, a), (8 + h, den), (12 + h, w)):
                sc = sc + val * (lane == r).astype(f32)
            mo = mo + m_t * (lane == h).astype(f32)
        m_out_ref[...] = mo.T[0:N_HEADS, :]
        sct_s[...] = sc.T
        vt_s[...] = vm_ref[...].astype(f32).T
        vr_s[...] = vr_ref[...].astype(f32)
        cosd = cosd_ref[0:1, :]
        sind = sind_ref[0:1, :]
        cosdT = cosdT_ref[...]
        sindT = sindT_ref[...]
        for h in range(N_HEADS):
            lo = slice(h * D_HEAD, h * D_HEAD + HALF)
            hi = slice(h * D_HEAD + HALF, (h + 1) * D_HEAD)
            q1 = qr_ref[:, lo].astype(f32)
            q2 = qr_ref[:, hi].astype(f32)
            qrot = jnp.concatenate([q1 * cosd - q2 * sind, q1 * sind + q2 * cosd], axis=-1)
            qrt_s[h * D_HEAD:(h + 1) * D_HEAD, :] = qrot.T
            k1 = kt_ref[lo, :].astype(f32)
            k2 = kt_ref[hi, :].astype(f32)
            krt_s[lo, :] = (k1 * cosdT - k2 * sindT) * scale
            krt_s[hi, :] = (k1 * sindT + k2 * cosdT) * scale
        cqt_s[...] = jnp.zeros(cqt_s.shape, f32)

    def take_col(mat, mask):
        return jnp.sum(jnp.where(mask, mat, 0.0), axis=-1, keepdims=True)

    for j in range(_SEQ_PER_STEP):
        b = s * _SEQ_PER_STEP + j
        mask = lane == b
        for h in range(N_HEADS):
            cs = slice(h * D_HEAD, (h + 1) * D_HEAD)
            c = c_ref[j, h]
            q_row = qm_s[pl.ds(b, 1), cs]
            k_row = km_s[pl.ds(b, 1), cs]
            v_col = take_col(vt_s[cs, :], mask)
            a = take_col(sct_s[4 + h:5 + h, :], mask)
            w = take_col(sct_s[12 + h:13 + h, :], mask)
            cq = jnp.sum(c * q_row, axis=-1, keepdims=True)
            cqt_s[cs, :] = jnp.where(mask, cq, cqt_s[cs, :])
            c_out_ref[j, h] = a * c + (w * v_col) * k_row
            st = s_ref[j, h]
            q_col = take_col(qrt_s[cs, :], mask)
            k_col = take_col(krt_s[cs, :], mask)
            v_row = vr_s[pl.ds(b, 1), cs]
            qk = jnp.sum(q_col * k_col, axis=0, keepdims=True)
            o_row = qk * v_row + gammas[h] * jnp.sum(st * q_col, axis=0, keepdims=True)
            oacc_s[pl.ds(b, 1), cs] = o_row
            s_out_ref[j, h] = gammas[h] * st + k_col * v_row

    @pl.when(s == ns - 1)
    def _():
        for h in range(N_HEADS):
            cs = slice(h * D_HEAD, (h + 1) * D_HEAD)
            ht = ((sct_s[h:h + 1, :] * vt_s[cs, :] + sct_s[4 + h:5 + h, :] * cqt_s[cs, :])
                  / sct_s[8 + h:9 + h, :])
            hn = _head_norm(ht.T, hngm_ref[:, cs])
            zsm = zsm_ref[:, cs].astype(f32)
            om_ref[:, cs] = ((hn + skip_ref[:, cs] * cm_s[:, cs]) * zsm).astype(bf16)
            zsr = zsr_ref[:, cs].astype(f32)
            or_ref[:, cs] = (_head_norm(oacc_s[:, cs], hngr_ref[:, cs]) * zsr).astype(bf16)


def _decode_step(P, KT, GI, OM, OR, prev, l, BT, conv_st, n_st, m_st, C_st, S_st,
                 conv_w, conv_b, wq, wk, brow, hngm, skip, hngr, dtabs):
    R = P.shape[0]
    NB = R - BT
    assert NB == LANES and BT % NB == 0 and NB % _SEQ_PER_STEP == 0
    rb = BT // NB
    depth = C_st.shape[0]
    bb = _SEQ_PER_STEP
    cosd, sind, cosdT, sindT = dtabs
    pcol = lambda k: (lambda s: (rb, k))
    rep2 = lambda s: (0, 0)
    st_spec = pl.BlockSpec((None, bb, N_HEADS, D_HEAD, D_HEAD), lambda s: (l, s, 0, 0, 0))
    conv_spec = _layer_spec((CONV_W - 1, NB, W_MIX), l)
    n_spec = _layer_spec((NB, W_MIX), l)
    m_spec = _layer_spec((N_HEADS, NB), l)
    in_specs = [
        pl.BlockSpec((NB, W_MIX), pcol(0)), pl.BlockSpec((NB, W_MIX), pcol(1)),
        pl.BlockSpec((NB, W_MIX), pcol(2)), pl.BlockSpec((NB, W_MIX), pcol(3)),
        pl.BlockSpec((NB, W_MIX), pcol(4)), pl.BlockSpec((NB, W_MIX), pcol(5)),
        pl.BlockSpec((W_MIX, NB), lambda s: (0, rb)),
        pl.BlockSpec((2 * N_HEADS, NB), lambda s: (0, rb)),
        conv_spec, n_spec, m_spec, st_spec, st_spec,
        _layer_spec((CONV_W, W_MIX), l), _layer_spec((1, W_MIX), l),
        _layer_spec((N_HEADS, D_HEAD, D_HEAD), l), _layer_spec((N_HEADS, D_HEAD, D_HEAD), l),
        _layer_spec((1, LANES), l),
        _layer_spec((1, W_MIX), l), _layer_spec((1, W_MIX), l), _layer_spec((1, W_MIX), l),
        pl.BlockSpec((8, HALF), rep2), pl.BlockSpec((8, HALF), rep2),
        pl.BlockSpec((HALF, LANES), rep2), pl.BlockSpec((HALF, LANES), rep2),
    ]
    args = [P, P, P, P, P, P, KT, GI, conv_st, n_st, m_st, C_st, S_st,
            conv_w, conv_b, wq, wk, brow, hngm, skip, hngr, cosd, sind, cosdT, sindT]
    carried = [OM, OR] + (list(prev) if prev is not None else [])
    kern, xspecs, xargs, aliases = _chain(_decode_kernel, len(args), carried, 0)
    out_row = pl.BlockSpec((NB, W_MIX), lambda s: (rb, 0))
    return pl.pallas_call(
        kern,
        grid=(NB // bb,),
        in_specs=in_specs + xspecs,
        out_specs=[out_row, out_row, st_spec, st_spec, n_spec, m_spec, conv_spec],
        out_shape=[
            jax.ShapeDtypeStruct(OM.shape, bf16),
            jax.ShapeDtypeStruct(OR.shape, bf16),
            jax.ShapeDtypeStruct((depth, NB, N_HEADS, D_HEAD, D_HEAD), f32),
            jax.ShapeDtypeStruct((depth, NB, N_HEADS, D_HEAD, D_HEAD), f32),
            jax.ShapeDtypeStruct((depth, NB, W_MIX), f32),
            jax.ShapeDtypeStruct((depth, N_HEADS, NB), f32),
            jax.ShapeDtypeStruct((depth, CONV_W - 1, NB, W_MIX), f32),
        ],
        scratch_shapes=[
            pltpu.VMEM((W_MIX, NB), f32),
            pltpu.VMEM((W_MIX, NB), f32),
            pltpu.VMEM((W_MIX, NB), f32),
            pltpu.VMEM((NB, W_MIX), f32),
            pltpu.VMEM((NB, W_MIX), f32),
            pltpu.VMEM((NB, W_MIX), f32),
            pltpu.VMEM((LANES, NB), f32),
            pltpu.VMEM((W_MIX, NB), f32),
            pltpu.VMEM((NB, W_MIX), f32),
            pltpu.VMEM((NB, W_MIX), f32),
        ],
        input_output_aliases=aliases,
        compiler_params=_cparams(("arbitrary",)),
        name="decode_step",
    )(*args, *xargs)


_MERGE_SLAB = 256


def _merge_kernel(h_ref, om_ref, or_ref, wgm_ref, wgr_ref, bgm_ref, bgr_ref, wpm_ref, wpr_ref, o_ref):
    for c0 in range(0, o_ref.shape[1], _MERGE_SLAB):
        cols = slice(c0, c0 + _MERGE_SLAB)
        gm = jax.nn.sigmoid(_dot(h_ref[...], wgm_ref[:, cols].astype(bf16)) + bgm_ref[:, cols])
        gr = jax.nn.sigmoid(_dot(h_ref[...], wgr_ref[:, cols].astype(bf16)) + bgr_ref[:, cols])
        pm = _dot(om_ref[...], wpm_ref[:, cols].astype(bf16))
        pr = _dot(or_ref[...], wpr_ref[:, cols].astype(bf16))
        o_ref[:, cols] = (gm * pm + gr * pr).astype(bf16)


def _merge(h, OM, OR, w_gate, b_gate3, w_pm, w_pr, l):
    R = h.shape[0]
    tm = _pick(R, (1664, 1280, 640, 384, 128))
    tn = 512
    nj = D_MODEL // tn
    return pl.pallas_call(
        _merge_kernel,
        grid=(R // tm, nj),
        in_specs=[
            pl.BlockSpec((tm, D_MODEL), lambda i, j: (i, 0)),
            pl.BlockSpec((tm, W_MIX), lambda i, j: (i, 0)),
            pl.BlockSpec((tm, W_MIX), lambda i, j: (i, 0)),
            pl.BlockSpec((None, D_MODEL, tn), lambda i, j: (l, 0, j)),
            pl.BlockSpec((None, D_MODEL, tn), lambda i, j: (l, 0, j + nj)),
            pl.BlockSpec((None, 1, tn), lambda i, j: (l, 0, j)),
            pl.BlockSpec((None, 1, tn), lambda i, j: (l, 0, j + nj)),
            pl.BlockSpec((None, W_MIX, tn), lambda i, j: (l, 0, j)),
            pl.BlockSpec((None, W_MIX, tn), lambda i, j: (l, 0, j)),
        ],
        out_specs=pl.BlockSpec((tm, tn), lambda i, j: (i, j)),
        out_shape=jax.ShapeDtypeStruct((R, D_MODEL), bf16),
        compiler_params=_cparams(("arbitrary", "arbitrary")),
        name="merge",
    )(h, OM, OR, w_gate, w_gate, b_gate3, b_gate3, w_pm, w_pr)


def _outproj_norm_kernel(mg_ref, wo_ref, x_ref, g_ref, o_ref, h_ref):
    y = x_ref[...] + _dot(mg_ref[...], wo_ref[...])
    o_ref[...] = y
    ms = jnp.mean(y * y, axis=-1, keepdims=True)
    h_ref[...] = (y * lax.rsqrt(ms + EPS) * g_ref[...]).astype(bf16)


def _outproj_final_kernel(mg_ref, wo_ref, x_ref, g_ref, y_ref):
    y = x_ref[...] + _dot(mg_ref[...], wo_ref[...])
    ms = jnp.mean(y * y, axis=-1, keepdims=True)
    y_ref[...] = y * lax.rsqrt(ms + EPS) * g_ref[...]


def _wo_spec():
    return pl.BlockSpec((D_MODEL, D_MODEL), lambda i: (0, 0), pipeline_mode=pl.Buffered(1))


def _outproj(merged, w_o_bf, x, norm_g3, l_next, row0, nrows, x_local, prev):
    R = merged.shape[0]
    tm = _pick(nrows, (832, 512, 384, 128))
    assert row0 % tm == 0
    off = row0 // tm
    row_g = pl.BlockSpec((tm, D_MODEL), lambda i: (i + off, 0))
    row_x = pl.BlockSpec((tm, D_MODEL), lambda i: (i, 0)) if x_local else row_g
    kern, xspecs, xargs, aliases = _chain(_outproj_norm_kernel, 4, prev, 0)
    return pl.pallas_call(
        kern,
        grid=(nrows // tm,),
        in_specs=[row_g, _wo_spec(), row_x, _layer_spec((1, D_MODEL), l_next)] + xspecs,
        out_specs=[row_g, row_g],
        out_shape=[jax.ShapeDtypeStruct((R, D_MODEL), f32), jax.ShapeDtypeStruct((R, D_MODEL), bf16)],
        input_output_aliases=aliases,
        compiler_params=_cparams(("arbitrary",)),
        name="outproj_norm",
    )(merged, w_o_bf, x, norm_g3, *xargs)


def _outproj_final(merged, w_o_bf, x, g, row0, nrows):
    tm = _pick(nrows, (512, 256, 128))
    assert row0 % tm == 0
    off = row0 // tm
    row_in = pl.BlockSpec((tm, D_MODEL), lambda i: (i + off, 0))
    return pl.pallas_call(
        _outproj_final_kernel,
        grid=(nrows // tm,),
        in_specs=[row_in, _wo_spec(), row_in, pl.BlockSpec((1, D_MODEL), lambda i: (0, 0))],
        out_specs=pl.BlockSpec((tm, D_MODEL), lambda i: (i, 0)),
        out_shape=jax.ShapeDtypeStruct((nrows, D_MODEL), f32),
        compiler_params=_cparams(("arbitrary",)),
        name="outproj_final",
    )(merged, w_o_bf, x, g)


def kernel(x_prompt, x_sample, state_mlstm_C, state_mlstm_n, state_mlstm_m, state_mlstm_conv, state_ret_S,
           norm_g, w_in, conv_w, conv_b, w_qm, w_km, b_i, b_f, hn_m_g, skip_m, hn_r_g,
           w_pm, w_pr, w_gate, b_gate, w_o, final_g):
    B, T, D = x_prompt.shape
    NB = x_sample.shape[0]
    depth = w_in.shape[0]
    BT = B * T
    assert D == D_MODEL and x_sample.shape[1] == 1 and T % CHUNK == 0 and depth >= 2

    R = BT + NB
    xp = x_prompt.reshape(BT, D)
    xs = x_sample.reshape(NB, D)

    w_inT = jnp.swapaxes(w_in, 1, 2)
    assert math.log2(D_HEAD) % 2 == 0
    wq = (w_qm * D_HEAD ** -0.5).astype(bf16)
    wk = w_km.astype(bf16)
    wkT = jnp.swapaxes(w_km, 2, 3).astype(bf16)
    bias = jnp.pad(jnp.concatenate([b_i, b_f], axis=1), ((0, 0), (0, LANES - 2 * N_HEADS)))
    brow = bias.reshape(depth, 1, LANES)
    bcol = bias[:, :8].reshape(depth, 8, 1)
    b_gate3 = b_gate.reshape(depth, 1, 2 * D)
    norm_g3 = norm_g.reshape(depth, 1, D)
    conv_b3 = conv_b.reshape(depth, 1, W_MIX)
    hngm = hn_m_g.reshape(depth, 1, W_MIX)
    skp = skip_m.reshape(depth, 1, W_MIX)
    hngr = hn_r_g.reshape(depth, 1, W_MIX)
    conv_st = jnp.transpose(state_mlstm_conv, (0, 2, 1, 3))
    m_st = jnp.swapaxes(state_mlstm_m, 1, 2)
    n_st = state_mlstm_n.reshape(depth, NB, W_MIX)

    cos, sin, cosT, sinT, cosd, sind, cosdT, sindT = _rope_tables(T)

    prev_m = prev_r = prev_d = None
    h = _rmsnorm_into(xp, norm_g3, 0, R, 0, None)
    h = _rmsnorm_into(xs, norm_g3, 0, R, BT, h)
    x = None
    for l in range(depth):
        P, KT, GI, w_o_bf = _inproj(h, w_inT, w_o, l)
        OM, *prev_m = _mlstm_prompt(P, GI, B, T, l, depth, prev_m, conv_w, conv_b3, wq, wkT,
                                    bcol, hngm, skp)
        OR, *prev_r = _ret_prompt(P, KT, B, T, l, depth, prev_r, (cos, sin, cosT, sinT), hngr)
        OM, OR, *prev_d = _decode_step(
            P, KT, GI, OM, OR, prev_d, l, BT, conv_st, n_st, m_st, state_mlstm_C, state_ret_S,
            conv_w, conv_b3, wq, wk, brow, hngm, skp, hngr, (cosd, sind, cosdT, sindT))
        merged = _merge(h, OM, OR, w_gate, b_gate3, w_pm, w_pr, l)
        if l + 1 == depth:
            break
        if x is None:
            x, h = _outproj(merged, w_o_bf, xp, norm_g3, l + 1, 0, BT, True, None)
            x, h = _outproj(merged, w_o_bf, xs, norm_g3, l + 1, BT, NB, True, (x, h))
        else:
            x, h = _outproj(merged, w_o_bf, x, norm_g3, l + 1, 0, R, False, None)

    fg = final_g.reshape(1, D)
    y_prompt = _outproj_final(merged, w_o_bf, x, fg, 0, BT).reshape(B, T, D)
    y_sample = _outproj_final(merged, w_o_bf, x, fg, BT, NB).reshape(NB, 1, D)
    Cp, np_, mp, convp = prev_m
    (Sp,) = prev_r
    Cs, Ss, n_s, m_s, conv_s = prev_d
    return (y_prompt, y_sample,
            Cp, np_.reshape(depth, B, N_HEADS, D_HEAD), mp[:, :, 0, 12:12 + N_HEADS], convp, Sp,
            Cs, n_s.reshape(depth, NB, N_HEADS, D_HEAD), jnp.swapaxes(m_s, 1, 2),
            jnp.transpose(conv_s, (0, 2, 1, 3)), Ss)
```

```python
import functools
import math

import jax
import jax.numpy as jnp
from jax import lax
from jax.experimental import pallas as pl
from jax.experimental.pallas import tpu as pltpu

f32 = jnp.float32
bf16 = jnp.bfloat16

D_MODEL = 2048
N_HEADS = 4
D_HEAD = 256
W_MIX = N_HEADS * D_HEAD
CONV_W = 4
CHUNK = 128
ROPE_BASE = 10000.0
EPS = 1e-6
NEG = -1e30
PAST_LEN = 16384
LANES = 128
HALF = D_HEAD // 2
VMEM_LIMIT = 62 * 1024 * 1024

OFF_IF = 3 * W_MIX
OFF_RET = 3 * W_MIX + 2 * N_HEADS


def _pick(n, cands):
    for c in cands:
        if n % c == 0:
            return c
    return n


def _cparams(sem):
    return pltpu.CompilerParams(dimension_semantics=sem, vmem_limit_bytes=VMEM_LIMIT)


def _dot(a, b):
    return jnp.dot(a, b, preferred_element_type=f32)


def _dot_nt(a, b):
    return lax.dot_general(a, b, (((1,), (1,)), ((), ())), preferred_element_type=f32)


def _split3(x):
    hi = x.astype(bf16).astype(f32)
    r = x - hi
    mid = r.astype(bf16).astype(f32)
    lo = (r - mid).astype(bf16).astype(f32)
    return hi, mid, lo


def _silu(x):
    return x * (0.5 * jnp.tanh(0.5 * x) + 0.5)


def _log_sigmoid(x):
    return jnp.minimum(x, 0.0) - jnp.log1p(jnp.exp(-jnp.abs(x)))


def _head_norm(hc, g):
    mu = jnp.mean(hc, axis=-1, keepdims=True)
    d = hc - mu
    var = jnp.mean(d * d, axis=-1, keepdims=True)
    return d * lax.rsqrt(var + EPS) * g


def _rope_kernel(inv_ref, invT_ref, cos_ref, sin_ref, cosT_ref, sinT_ref,
                 cosd_ref, sind_ref, cosdT_ref, sindT_ref):
    T = cos_ref.shape[0]
    pos = lax.broadcasted_iota(jnp.int32, (T, HALF), 0).astype(f32)
    ang = pos * inv_ref[...]
    cos = jnp.cos(ang)
    sin = jnp.sin(ang)
    cos_ref[...] = cos
    sin_ref[...] = sin
    cosT_ref[...] = cos.T
    sinT_ref[...] = sin.T
    angd = jnp.full((8, HALF), float(PAST_LEN), f32) * inv_ref[...]
    cosd_ref[...] = jnp.cos(angd)
    sind_ref[...] = jnp.sin(angd)
    angdT = jnp.full((HALF, LANES), float(PAST_LEN), f32) * invT_ref[...]
    cosdT_ref[...] = jnp.cos(angdT)
    sindT_ref[...] = jnp.sin(angdT)


def _rope_tables(T):
    inv = ROPE_BASE ** (-jnp.arange(0, HALF, dtype=f32) / HALF)
    shapes = [(T, HALF), (T, HALF), (HALF, T), (HALF, T),
              (8, HALF), (8, HALF), (HALF, LANES), (HALF, LANES)]
    return pl.pallas_call(
        _rope_kernel,
        out_shape=[jax.ShapeDtypeStruct(s, f32) for s in shapes],
        name="rope_tables",
    )(inv.reshape(1, HALF), inv.reshape(HALF, 1))


def _rms_kernel(x_ref, g_ref, o_ref):
    x = x_ref[...]
    ms = jnp.mean(x * x, axis=-1, keepdims=True)
    o_ref[...] = (x * lax.rsqrt(ms + EPS) * g_ref[...]).astype(o_ref.dtype)


def _rmsnorm_into(x, norm_g3, l, R, row0, prev):
    n, D = x.shape
    tr = _pick(n, (512, 256, 128))
    assert row0 % tr == 0
    off = row0 // tr
    kern, xspecs, xargs, aliases = _chain(_rms_kernel, 2, None if prev is None else [prev], 0)
    return pl.pallas_call(
        kern,
        grid=(n // tr,),
        in_specs=[pl.BlockSpec((tr, D), lambda i: (i, 0)), _layer_spec((1, D), l)] + xspecs,
        out_specs=pl.BlockSpec((tr, D), lambda i: (i + off, 0)),
        out_shape=jax.ShapeDtypeStruct((R, D), bf16),
        input_output_aliases=aliases,
        compiler_params=_cparams(("arbitrary",)),
        name="rmsnorm",
    )(x, norm_g3, *xargs)


_TN_IN = 1024
_TPG = W_MIX // _TN_IN
_NJ_P = 6 * _TPG
_NJ_K = _TPG
_GATE_PAD = 16


def _inproj_row_offset(j):
    jp = jnp.minimum(j, _NJ_P - 1)
    grp = jp // _TPG
    sub = jp - grp * _TPG
    src_grp = jnp.where(grp < 3, grp, jnp.where(grp == 3, 3, grp + 1))
    off_p = src_grp * W_MIX + jnp.where(grp < 3, 0, 2 * N_HEADS) + sub * _TN_IN
    off_k = OFF_RET + W_MIX + jnp.clip(j - _NJ_P, 0, _NJ_K - 1) * _TN_IN
    return jnp.where(j < _NJ_P, off_p, off_k)


def _inproj_kernel(h_ref, wt_ref, wif_ref, wo_ref, p_ref, kt_ref, gi_ref, wob_ref):
    j = pl.program_id(1)
    wob_ref[...] = wo_ref[...].astype(bf16)

    def weights():
        return wt_ref[0].astype(bf16)

    is_z = (j // _TPG == 2) | (j // _TPG == 5)
    last = _NJ_P + _NJ_K - 1

    @pl.when((j < _NJ_P) & jnp.logical_not(is_z))
    def _():
        p_ref[...] = _dot_nt(h_ref[...], weights()).astype(bf16)

    @pl.when((j < _NJ_P) & is_z)
    def _():
        p_ref[...] = _silu(_dot_nt(h_ref[...], weights())).astype(bf16)

    if _NJ_K > 1:
        @pl.when((j >= _NJ_P) & (j < last))
        def _():
            kt_ref[...] = _dot_nt(weights(), h_ref[...]).astype(bf16)

    @pl.when(j == last)
    def _():
        pad = jnp.zeros((_GATE_PAD - 2 * N_HEADS, D_MODEL), f32)
        w_all = jnp.concatenate([wt_ref[0], wif_ref[...], pad], axis=0).astype(bf16)
        res = _dot_nt(w_all, h_ref[...])
        kt_ref[...] = res[0:_TN_IN, :].astype(bf16)
        gi_ref[...] = res[_TN_IN:_TN_IN + 2 * N_HEADS, :]


def _inproj(h, w_inT, w_o, l):
    R = h.shape[0]
    tm = _pick(R, (1664, 1280, 640, 384, 128))
    tn = _TN_IN
    nj = _NJ_P + _NJ_K
    nsteps = (R // tm) * nj
    rps = next(r for r in (32, 64, 128, 256, 512, 1024, 2048) if D_MODEL // r <= nsteps)
    wo_blk = lambda i, j: jnp.minimum(i * nj + j, D_MODEL // rps - 1)
    return pl.pallas_call(
        _inproj_kernel,
        grid=(R // tm, nj),
        in_specs=[
            pl.BlockSpec((tm, D_MODEL), lambda i, j: (i, 0)),
            pl.BlockSpec((pl.Element(1), pl.Element(tn), pl.Element(D_MODEL)),
                         lambda i, j: (l, pl.multiple_of(_inproj_row_offset(j), 8), 0)),
            pl.BlockSpec((None, 2 * N_HEADS, D_MODEL), lambda i, j: (l, OFF_IF // (2 * N_HEADS), 0)),
            pl.BlockSpec((None, rps, D_MODEL), lambda i, j: (l, wo_blk(i, j), 0)),
        ],
        out_specs=[
            pl.BlockSpec((tm, tn), lambda i, j: (i, jnp.minimum(j, _NJ_P - 1))),
            pl.BlockSpec((tn, tm), lambda i, j: (jnp.clip(j - _NJ_P, 0, _NJ_K - 1), i)),
            pl.BlockSpec((2 * N_HEADS, tm), lambda i, j: (0, i)),
            pl.BlockSpec((rps, D_MODEL), lambda i, j: (wo_blk(i, j), 0)),
        ],
        out_shape=[
            jax.ShapeDtypeStruct((R, 6 * W_MIX), bf16),
            jax.ShapeDtypeStruct((W_MIX, R), bf16),
            jax.ShapeDtypeStruct((2 * N_HEADS, R), f32),
            jax.ShapeDtypeStruct((D_MODEL, D_MODEL), bf16),
        ],
        compiler_params=_cparams(("arbitrary", "arbitrary")),
        name="inproj",
    )(h, w_inT, w_inT, w_o)


def _mlstm_kernel(um_ref, vm_ref, zs_ref, gi_ref, convw_ref, convb_ref, wq_ref, wkT_ref,
                  bcol_ref, hng_ref, skip_ref,
                  om_ref, c_out_ref, n_out_ref, m_out_ref, conv_out_ref,
                  tail_s, cta_s, m_s):
    g = pl.program_id(1)
    ng = pl.num_programs(1)
    Lg = um_ref.shape[0]
    L = CHUNK
    DA = D_HEAD + LANES
    heads = range(N_HEADS)

    @pl.when(g == 0)
    def _():
        tail_s[...] = jnp.zeros(tail_s.shape, bf16)
        cta_s[...] = jnp.zeros(cta_s.shape, f32)
        m_s[...] = jnp.full(m_s.shape, NEG, f32)

    srow = lax.broadcasted_iota(jnp.int32, (L, 2 * L), 0)
    scol = lax.broadcasted_iota(jnp.int32, (L, 2 * L), 1)
    shift_mat = jnp.concatenate([(scol == srow + L - s).astype(bf16) for s in range(1, CONV_W)], axis=0)

    row = lax.broadcasted_iota(jnp.int32, (L, L), 0)
    col = lax.broadcasted_iota(jnp.int32, (L, L), 1)
    causal = row >= col
    triu = (row <= col).astype(bf16)
    lane = col
    bcol = bcol_ref[...]
    ones_b = jnp.ones((L, LANES), bf16)
    ones_f = jnp.ones((L, LANES), f32)

    for c in range(Lg // L):
        rows = slice(c * L, (c + 1) * L)
        gtb = gi_ref[:, rows] + bcol
        lft = _log_sigmoid(gtb)
        br3 = _dot(jnp.concatenate(_split3(lft), axis=0).astype(bf16), triu)
        br_all = br3[0:8, :] + br3[8:16, :] + br3[16:24, :]
        cols = jnp.concatenate([gtb, br_all, jnp.zeros((LANES - 16, L), f32)], axis=0).T

        u_cur = um_ref[rows, :]
        u_prev = tail_s[...] if c == 0 else um_ref[(c - 1) * L:c * L, :]
        delayed = _dot(shift_mat, jnp.concatenate([u_prev, u_cur], axis=0))
        acc = convb_ref[...] + u_cur.astype(f32) * convw_ref[CONV_W - 1:CONV_W, :]
        for s in range(1, CONV_W):
            acc = acc + delayed[(s - 1) * L:s * L, :] * convw_ref[CONV_W - 1 - s:CONV_W - s, :]
        cm = _silu(acc)

        cs = [slice(h * D_HEAD, (h + 1) * D_HEAD) for h in heads]
        ch = [cm[:, cs[h]] for h in heads]
        chb = [x.astype(bf16) for x in ch]
        qb = [_dot(chb[h], wq_ref[h]).astype(bf16) for h in heads]
        kTb = [_dot_nt(wkT_ref[h], chb[h]).astype(bf16) for h in heads]
        v = [vm_ref[rows, cs[h]] for h in heads]
        bc = [cols[:, 12 + h:13 + h] for h in heads]
        dlog = [jnp.where(causal, bc[h] - br_all[4 + h:5 + h, :] + gtb[h:h + 1, :], NEG) for h in heads]
        dmax = jnp.zeros((L, LANES), f32)
        for h in heads:
            dmax = jnp.where(lane == 12 + h, jnp.max(dlog[h], axis=-1, keepdims=True), dmax)
        m_row = m_s[0:1, :]
        inter = cols + m_row
        m_all = jnp.maximum(inter, dmax)
        a_all = jnp.exp(inter - m_all)
        em_all = jnp.exp(-m_all)
        m_new = m_all[L - 1:L, :]
        b_last = cols[L - 1:L, :]
        ic_al = pltpu.roll(cols, 12, axis=1)
        wk_all = jnp.exp(b_last - cols + ic_al - m_new)
        dec_row = jnp.exp(b_last + m_row - m_new)
        m_s[...] = jnp.broadcast_to(m_new, (8, LANES))
        hl = [slice(12 + h, 13 + h) for h in heads]
        ws = [(jnp.exp(dlog[h] - m_all[:, hl[h]]) * _dot(qb[h], kTb[h])).astype(bf16) for h in heads]
        cta = [cta_s[h] for h in heads]
        nd = [_dot(ws[h], jnp.concatenate([v[h], ones_b], axis=1))
              + a_all[:, hl[h]] * _dot(qb[h], cta[h].astype(bf16)) for h in heads]
        for h in heads:
            vaug = jnp.concatenate([v[h].astype(f32), ones_f], axis=1)
            cta_s[h] = dec_row[:, hl[h]] * cta[h] + _dot(kTb[h], (wk_all[:, hl[h]] * vaug).astype(bf16))
        for h in heads:
            den = jnp.maximum(jnp.abs(nd[h][:, D_HEAD:DA]), em_all[:, hl[h]])
            rden = 1.0 / den
            hc = jnp.concatenate([nd[h][:, 0:HALF] * rden, nd[h][:, HALF:D_HEAD] * rden], axis=1)
            hn = _head_norm(hc, hng_ref[:, cs[h]])
            zs = zs_ref[rows, cs[h]].astype(f32)
            om_ref[rows, cs[h]] = ((hn + skip_ref[:, cs[h]] * ch[h]) * zs).astype(bf16)

    tail_s[...] = um_ref[Lg - L:Lg, :]

    @pl.when(g == ng - 1)
    def _():
        for h in heads:
            st = cta_s[h]
            c_out_ref[h] = st[:, 0:D_HEAD].T
            n_out_ref[h] = st[:, D_HEAD:DA].T[0:1, :]
        m_out_ref[...] = m_s[...]
        conv_out_ref[...] = um_ref[Lg - 8:Lg, :].astype(f32)[8 - (CONV_W - 1):8, :]


def _drop_refs(kernel_fn, n_in, n_drop, *refs):
    return kernel_fn(*refs[:n_in], *refs[n_in + n_drop:])


def _chain(kernel_fn, n_in, prev, first_out):
    if prev is None:
        return kernel_fn, [], [], {}
    specs = [pl.BlockSpec(memory_space=pl.ANY)] * len(prev)
    aliases = {n_in + k: first_out + k for k in range(len(prev))}
    return functools.partial(_drop_refs, kernel_fn, n_in, len(prev)), specs, list(prev), aliases


def _layer_spec(shape, l):
    nd = len(shape)
    return pl.BlockSpec((None,) + tuple(shape), lambda *_: (l,) + (0,) * nd)


def _mlstm_prompt(P, GI, B, T, l, depth, prev, conv_w, conv_b, wq, wkT, bcol, hng, skip):
    R = P.shape[0]
    Lg = _pick(T, (512, 256, 128))
    ng = T // Lg
    col = lambda k: (lambda b, g: (b * ng + g, k))
    in_specs = [
        pl.BlockSpec((Lg, W_MIX), col(0)),
        pl.BlockSpec((Lg, W_MIX), col(1)),
        pl.BlockSpec((Lg, W_MIX), col(2)),
        pl.BlockSpec((2 * N_HEADS, Lg), lambda b, g: (0, b * ng + g)),
        _layer_spec((CONV_W, W_MIX), l),
        _layer_spec((1, W_MIX), l),
        _layer_spec((N_HEADS, D_HEAD, D_HEAD), l),
        _layer_spec((N_HEADS, D_HEAD, D_HEAD), l),
        _layer_spec((8, 1), l),
        _layer_spec((1, W_MIX), l),
        _layer_spec((1, W_MIX), l),
    ]
    args = [P, P, P, GI, conv_w, conv_b, wq, wkT, bcol, hng, skip]
    kern, xspecs, xargs, aliases = _chain(_mlstm_kernel, len(args), prev, 1)
    return pl.pallas_call(
        kern,
        grid=(B, ng),
        in_specs=in_specs + xspecs,
        out_specs=[
            pl.BlockSpec((Lg, W_MIX), col(0)),
            pl.BlockSpec((None, None, N_HEADS, D_HEAD, D_HEAD), lambda b, g: (l, b, 0, 0, 0)),
            pl.BlockSpec((None, None, N_HEADS, 1, D_HEAD), lambda b, g: (l, b, 0, 0, 0)),
            pl.BlockSpec((None, None, 8, LANES), lambda b, g: (l, b, 0, 0)),
            pl.BlockSpec((None, None, CONV_W - 1, W_MIX), lambda b, g: (l, b, 0, 0)),
        ],
        out_shape=[
            jax.ShapeDtypeStruct((R, W_MIX), bf16),
            jax.ShapeDtypeStruct((depth, B, N_HEADS, D_HEAD, D_HEAD), f32),
            jax.ShapeDtypeStruct((depth, B, N_HEADS, 1, D_HEAD), f32),
            jax.ShapeDtypeStruct((depth, B, 8, LANES), f32),
            jax.ShapeDtypeStruct((depth, B, CONV_W - 1, W_MIX), f32),
        ],
        scratch_shapes=[
            pltpu.VMEM((CHUNK, W_MIX), bf16),
            pltpu.VMEM((N_HEADS, D_HEAD, D_HEAD + LANES), f32),
            pltpu.VMEM((8, LANES), f32),
        ],
        input_output_aliases=aliases,
        compiler_params=_cparams(("arbitrary", "arbitrary")),
        name="mlstm_prompt",
    )(*args, *xargs)


def _log_gammas():
    return [math.log(1.0 - 2.0 ** (-5.0 - h)) for h in range(N_HEADS)]


def _ret_kernel(q_ref, v_ref, zs_ref, kt_ref, cos_ref, sin_ref, cosT_ref, sinT_ref, hng_ref,
                o_ref, s_out_ref, s_s):
    g = pl.program_id(1)
    ng = pl.num_programs(1)
    Lg = q_ref.shape[0]
    L = CHUNK

    @pl.when(g == 0)
    def _():
        s_s[...] = jnp.zeros(s_s.shape, f32)

    row = lax.broadcasted_iota(jnp.int32, (L, L), 0)
    col = lax.broadcasted_iota(jnp.int32, (L, L), 1)
    causal = row >= col
    diff = (row - col).astype(f32)
    tcol = lax.broadcasted_iota(jnp.int32, (L, 1), 0).astype(f32)
    lgs = _log_gammas()
    intra = [jnp.where(causal, jnp.exp(diff * lg), 0.0) for lg in lgs]
    q_dec = [jnp.exp((tcol + 1.0) * lg) for lg in lgs]
    k_dec = [jnp.exp((L - 1.0 - tcol) * lg) for lg in lgs]
    s_dec = [math.exp(L * lg) for lg in lgs]
    kscale = D_HEAD ** -0.5

    for c in range(Lg // L):
        rows = slice(c * L, (c + 1) * L)
        cos = cos_ref[rows, :]
        sin = sin_ref[rows, :]
        cosT = cosT_ref[:, rows] * kscale
        sinT = sinT_ref[:, rows] * kscale
        heads = range(N_HEADS)
        cs = [slice(h * D_HEAD, (h + 1) * D_HEAD) for h in heads]
        lo = [slice(h * D_HEAD, h * D_HEAD + HALF) for h in heads]
        hi = [slice(h * D_HEAD + HALF, (h + 1) * D_HEAD) for h in heads]
        q1 = [q_ref[rows, lo[h]].astype(f32) for h in heads]
        q2 = [q_ref[rows, hi[h]].astype(f32) for h in heads]
        qb = [jnp.concatenate([q1[h] * cos - q2[h] * sin, q1[h] * sin + q2[h] * cos], axis=-1).astype(bf16)
              for h in heads]
        k1 = [kt_ref[lo[h], rows].astype(f32) for h in heads]
        k2 = [kt_ref[hi[h], rows].astype(f32) for h in heads]
        kTb = [jnp.concatenate([k1[h] * cosT - k2[h] * sinT, k1[h] * sinT + k2[h] * cosT],
                               axis=0).astype(bf16) for h in heads]
        v = [v_ref[rows, cs[h]] for h in heads]
        sc = [(_dot(qb[h], kTb[h]) * intra[h]).astype(bf16) for h in heads]
        s = [s_s[h] for h in heads]
        o = [_dot(sc[h], v[h]) + q_dec[h] * _dot(qb[h], s[h].astype(bf16)) for h in heads]
        for h in heads:
            s_s[h] = s_dec[h] * s[h] + _dot(kTb[h], (v[h].astype(f32) * k_dec[h]).astype(bf16))
        for h in heads:
            zs = zs_ref[rows, cs[h]].astype(f32)
            o_ref[rows, cs[h]] = (_head_norm(o[h], hng_ref[:, cs[h]]) * zs).astype(bf16)

    @pl.when(g == ng - 1)
    def _():
        s_out_ref[...] = s_s[...]


def _ret_prompt(P, KT, B, T, l, depth, prev, tabs, hng):
    R = P.shape[0]
    Lg = _pick(T, (1024, 512, 256, 128))
    ng = T // Lg
    col = lambda k: (lambda b, g: (b * ng + g, k))
    cos, sin, cosT, sinT = tabs
    in_specs = [
        pl.BlockSpec((Lg, W_MIX), col(3)),
        pl.BlockSpec((Lg, W_MIX), col(4)),
        pl.BlockSpec((Lg, W_MIX), col(5)),
        pl.BlockSpec((W_MIX, Lg), lambda b, g: (0, b * ng + g)),
        pl.BlockSpec((Lg, HALF), lambda b, g: (g, 0)),
        pl.BlockSpec((Lg, HALF), lambda b, g: (g, 0)),
        pl.BlockSpec((HALF, Lg), lambda b, g: (0, g)),
        pl.BlockSpec((HALF, Lg), lambda b, g: (0, g)),
        _layer_spec((1, W_MIX), l),
    ]
    args = [P, P, P, KT, cos, sin, cosT, sinT, hng]
    kern, xspecs, xargs, aliases = _chain(_ret_kernel, len(args), prev, 1)
    return pl.pallas_call(
        kern,
        grid=(B, ng),
        in_specs=in_specs + xspecs,
        out_specs=[
            pl.BlockSpec((Lg, W_MIX), col(0)),
            pl.BlockSpec((None, None, N_HEADS, D_HEAD, D_HEAD), lambda b, g: (l, b, 0, 0, 0)),
        ],
        out_shape=[
            jax.ShapeDtypeStruct((R, W_MIX), bf16),
            jax.ShapeDtypeStruct((depth, B, N_HEADS, D_HEAD, D_HEAD), f32),
        ],
        scratch_shapes=[pltpu.VMEM((N_HEADS, D_HEAD, D_HEAD), f32)],
        input_output_aliases=aliases,
        compiler_params=_cparams(("arbitrary", "arbitrary")),
        name="ret_prompt",
    )(*args, *xargs)


_SEQ_PER_STEP = 4


_STATE_SLOTS = 3


def _decode_kernel(l, um_ref, vm_ref, zsm_ref, qr_ref, vr_ref, zsr_ref, kt_ref, gi_ref,
                   conv_ref, n_ref, m_ref, c_hbm, s_hbm,
                   convw_ref, convb_ref, wq_ref, wk_ref, brow_ref, hngm_ref, skip_ref, hngr_ref,
                   cosd_ref, sind_ref, cosdT_ref, sindT_ref,
                   om_ref, or_ref, c_out_ref, s_out_ref, n_out_ref, m_out_ref, conv_out_ref,
                   vt_s, qrt_s, krt_s, qm_s, km_s, vr_s, sct_s, cqt_s, oacc_s, cm_s,
                   cbuf, sbuf, sems):
    s = pl.program_id(0)
    ns = pl.num_programs(0)
    NB = um_ref.shape[0]

    def state_copies(step):
        slot = step % _STATE_SLOTS
        seqs = pl.ds(step * _SEQ_PER_STEP, _SEQ_PER_STEP)
        return (pltpu.make_async_copy(c_hbm.at[l, seqs], cbuf.at[slot], sems.at[0, slot]),
                pltpu.make_async_copy(s_hbm.at[l, seqs], sbuf.at[slot], sems.at[1, slot]))

    def start(step):
        for cp in state_copies(step):
            cp.start()

    @pl.when(s == 0)
    def _():
        start(0)

    @pl.when((s == 0) & (ns > 1))
    def _():
        start(1)

    @pl.when(s + 2 < ns)
    def _():
        start(s + 2)
    scale = D_HEAD ** -0.5
    lgs = _log_gammas()
    gammas = [math.exp(lg) for lg in lgs]
    lane = lax.broadcasted_iota(jnp.int32, (1, LANES), 1)

    @pl.when(s == 0)
    def _():
        u = um_ref[...].astype(f32)
        acc = convb_ref[...] + u * convw_ref[CONV_W - 1:CONV_W, :]
        for i in range(CONV_W - 1):
            acc = acc + conv_ref[i] * convw_ref[i:i + 1, :]
        cm = _silu(acc)
        cm_s[...] = cm
        for i in range(CONV_W - 2):
            conv_out_ref[i] = conv_ref[i + 1]
        conv_out_ref[CONV_W - 2] = u

        gb = jnp.concatenate([gi_ref[...], jnp.zeros((LANES - 8, NB), f32)], axis=0).T + brow_ref[...]
        lf = _log_sigmoid(gb)
        m_cols = jnp.concatenate([m_ref[...], jnp.zeros((8 - N_HEADS, NB), f32)], axis=0).T
        sc = jnp.zeros((NB, LANES), f32)
        mo = jnp.zeros((NB, LANES), f32)
        for h in range(N_HEADS):
            cs = slice(h * D_HEAD, (h + 1) * D_HEAD)
            chb = cm[:, cs].astype(bf16)
            q = _dot(chb, wq_ref[h])
            k = _dot(chb, wk_ref[h])
            qm_s[:, cs] = q
            km_s[:, cs] = k
            ig = gb[:, h:h + 1]
            lfh = lf[:, 4 + h:5 + h]
            mp = m_cols[:, h:h + 1]
            inter = lfh + mp
            m_t = jnp.maximum(inter, ig)
            w = jnp.exp(ig - m_t)
            a = jnp.exp(inter - m_t)
            nh = n_ref[:, cs]
            ws = w * jnp.sum(q * k, axis=-1, keepdims=True)
            den = ws + a * jnp.sum(nh * q, axis=-1, keepdims=True)
            den = jnp.maximum(jnp.abs(den), jnp.exp(-m_t))
            n_out_ref[:, cs] = a * nh + w * k
            for r, val in ((h, ws), (4 + h, a), (8 + h, den), (12 + h, w)):
                sc = sc + val * (lane == r).astype(f32)
            mo = mo + m_t * (lane == h).astype(f32)
        m_out_ref[...] = mo.T[0:N_HEADS, :]
        sct_s[...] = sc.T
        vt_s[...] = vm_ref[...].astype(f32).T
        vr_s[...] = vr_ref[...].astype(f32)
        cosd = cosd_ref[0:1, :]
        sind = sind_ref[0:1, :]
        cosdT = cosdT_ref[...]
        sindT = sindT_ref[...]
        for h in range(N_HEADS):
            lo = slice(h * D_HEAD, h * D_HEAD + HALF)
            hi = slice(h * D_HEAD + HALF, (h + 1) * D_HEAD)
            q1 = qr_ref[:, lo].astype(f32)
            q2 = qr_ref[:, hi].astype(f32)
            qrot = jnp.concatenate([q1 * cosd - q2 * sind, q1 * sind + q2 * cosd], axis=-1)
            qrt_s[h * D_HEAD:(h + 1) * D_HEAD, :] = qrot.T
            k1 = kt_ref[lo, :].astype(f32)
            k2 = kt_ref[hi, :].astype(f32)
            krt_s[lo, :] = (k1 * cosdT - k2 * sindT) * scale
            krt_s[hi, :] = (k1 * sindT + k2 * cosdT) * scale
        cqt_s[...] = jnp.zeros(cqt_s.shape, f32)

    def take_col(mat, mask):
        return jnp.sum(jnp.where(mask, mat, 0.0), axis=-1, keepdims=True)

    for cp in state_copies(s):
        cp.wait()
    slot = s % _STATE_SLOTS

    for j in range(_SEQ_PER_STEP):
        b = s * _SEQ_PER_STEP + j
        mask = lane == b
        for h in range(N_HEADS):
            cs = slice(h * D_HEAD, (h + 1) * D_HEAD)
            c = cbuf[slot, j, h]
            q_row = qm_s[pl.ds(b, 1), cs]
            k_row = km_s[pl.ds(b, 1), cs]
            v_col = take_col(vt_s[cs, :], mask)
            a = take_col(sct_s[4 + h:5 + h, :], mask)
            w = take_col(sct_s[12 + h:13 + h, :], mask)
            cq = jnp.sum(c * q_row, axis=-1, keepdims=True)
            cqt_s[cs, :] = jnp.where(mask, cq, cqt_s[cs, :])
            c_out_ref[j, h] = a * c + (w * v_col) * k_row
            st = sbuf[slot, j, h]
            q_col = take_col(qrt_s[cs, :], mask)
            k_col = take_col(krt_s[cs, :], mask)
            v_row = vr_s[pl.ds(b, 1), cs]
            qk = jnp.sum(q_col * k_col, axis=0, keepdims=True)
            o_row = qk * v_row + gammas[h] * jnp.sum(st * q_col, axis=0, keepdims=True)
            oacc_s[pl.ds(b, 1), cs] = o_row
            s_out_ref[j, h] = gammas[h] * st + k_col * v_row

    @pl.when(s == ns - 1)
    def _():
        for h in range(N_HEADS):
            cs = slice(h * D_HEAD, (h + 1) * D_HEAD)
            ht = ((sct_s[h:h + 1, :] * vt_s[cs, :] + sct_s[4 + h:5 + h, :] * cqt_s[cs, :])
                  / sct_s[8 + h:9 + h, :])
            hn = _head_norm(ht.T, hngm_ref[:, cs])
            zsm = zsm_ref[:, cs].astype(f32)
            om_ref[:, cs] = ((hn + skip_ref[:, cs] * cm_s[:, cs]) * zsm).astype(bf16)
            zsr = zsr_ref[:, cs].astype(f32)
            or_ref[:, cs] = (_head_norm(oacc_s[:, cs], hngr_ref[:, cs]) * zsr).astype(bf16)


def _decode_step(P, KT, GI, OM, OR, prev, l, BT, conv_st, n_st, m_st, C_st, S_st,
                 conv_w, conv_b, wq, wk, brow, hngm, skip, hngr, dtabs):
    R = P.shape[0]
    NB = R - BT
    assert NB == LANES and BT % NB == 0 and NB % _SEQ_PER_STEP == 0
    rb = BT // NB
    depth = C_st.shape[0]
    bb = _SEQ_PER_STEP
    cosd, sind, cosdT, sindT = dtabs
    pcol = lambda k: (lambda s: (rb, k))
    rep2 = lambda s: (0, 0)
    st_spec = pl.BlockSpec((None, bb, N_HEADS, D_HEAD, D_HEAD), lambda s: (l, s, 0, 0, 0))
    conv_spec = _layer_spec((CONV_W - 1, NB, W_MIX), l)
    n_spec = _layer_spec((NB, W_MIX), l)
    m_spec = _layer_spec((N_HEADS, NB), l)
    in_specs = [
        pl.BlockSpec((NB, W_MIX), pcol(0)), pl.BlockSpec((NB, W_MIX), pcol(1)),
        pl.BlockSpec((NB, W_MIX), pcol(2)), pl.BlockSpec((NB, W_MIX), pcol(3)),
        pl.BlockSpec((NB, W_MIX), pcol(4)), pl.BlockSpec((NB, W_MIX), pcol(5)),
        pl.BlockSpec((W_MIX, NB), lambda s: (0, rb)),
        pl.BlockSpec((2 * N_HEADS, NB), lambda s: (0, rb)),
        conv_spec, n_spec, m_spec,
        pl.BlockSpec(memory_space=pl.ANY), pl.BlockSpec(memory_space=pl.ANY),
        _layer_spec((CONV_W, W_MIX), l), _layer_spec((1, W_MIX), l),
        _layer_spec((N_HEADS, D_HEAD, D_HEAD), l), _layer_spec((N_HEADS, D_HEAD, D_HEAD), l),
        _layer_spec((1, LANES), l),
        _layer_spec((1, W_MIX), l), _layer_spec((1, W_MIX), l), _layer_spec((1, W_MIX), l),
        pl.BlockSpec((8, HALF), rep2), pl.BlockSpec((8, HALF), rep2),
        pl.BlockSpec((HALF, LANES), rep2), pl.BlockSpec((HALF, LANES), rep2),
    ]
    args = [P, P, P, P, P, P, KT, GI, conv_st, n_st, m_st, C_st, S_st,
            conv_w, conv_b, wq, wk, brow, hngm, skip, hngr, cosd, sind, cosdT, sindT]
    carried = [OM, OR] + (list(prev) if prev is not None else [])
    kern, xspecs, xargs, aliases = _chain(functools.partial(_decode_kernel, l), len(args), carried, 0)
    out_row = pl.BlockSpec((NB, W_MIX), lambda s: (rb, 0))
    return pl.pallas_call(
        kern,
        grid=(NB // bb,),
        in_specs=in_specs + xspecs,
        out_specs=[out_row, out_row, st_spec, st_spec, n_spec, m_spec, conv_spec],
        out_shape=[
            jax.ShapeDtypeStruct(OM.shape, bf16),
            jax.ShapeDtypeStruct(OR.shape, bf16),
            jax.ShapeDtypeStruct((depth, NB, N_HEADS, D_HEAD, D_HEAD), f32),
            jax.ShapeDtypeStruct((depth, NB, N_HEADS, D_HEAD, D_HEAD), f32),
            jax.ShapeDtypeStruct((depth, NB, W_MIX), f32),
            jax.ShapeDtypeStruct((depth, N_HEADS, NB), f32),
            jax.ShapeDtypeStruct((depth, CONV_W - 1, NB, W_MIX), f32),
        ],
        scratch_shapes=[
            pltpu.VMEM((W_MIX, NB), f32),
            pltpu.VMEM((W_MIX, NB), f32),
            pltpu.VMEM((W_MIX, NB), f32),
            pltpu.VMEM((NB, W_MIX), f32),
            pltpu.VMEM((NB, W_MIX), f32),
            pltpu.VMEM((NB, W_MIX), f32),
            pltpu.VMEM((LANES, NB), f32),
            pltpu.VMEM((W_MIX, NB), f32),
            pltpu.VMEM((NB, W_MIX), f32),
            pltpu.VMEM((NB, W_MIX), f32),
            pltpu.VMEM((_STATE_SLOTS, bb, N_HEADS, D_HEAD, D_HEAD), f32),
            pltpu.VMEM((_STATE_SLOTS, bb, N_HEADS, D_HEAD, D_HEAD), f32),
            pltpu.SemaphoreType.DMA((2, _STATE_SLOTS)),
        ],
        input_output_aliases=aliases,
        compiler_params=_cparams(("arbitrary",)),
        name="decode_step",
    )(*args, *xargs)


_MERGE_SLAB = 256


def _merge_kernel(h_ref, om_ref, or_ref, wgm_ref, wgr_ref, bgm_ref, bgr_ref, wpm_ref, wpr_ref, o_ref):
    for c0 in range(0, o_ref.shape[1], _MERGE_SLAB):
        cols = slice(c0, c0 + _MERGE_SLAB)
        gm = jax.nn.sigmoid(_dot(h_ref[...], wgm_ref[:, cols].astype(bf16)) + bgm_ref[:, cols])
        gr = jax.nn.sigmoid(_dot(h_ref[...], wgr_ref[:, cols].astype(bf16)) + bgr_ref[:, cols])
        pm = _dot(om_ref[...], wpm_ref[:, cols].astype(bf16))
        pr = _dot(or_ref[...], wpr_ref[:, cols].astype(bf16))
        o_ref[:, cols] = (gm * pm + gr * pr).astype(bf16)


def _merge(h, OM, OR, w_gate, b_gate3, w_pm, w_pr, l):
    R = h.shape[0]
    tm = _pick(R, (1664, 1280, 640, 384, 128))
    tn = 512
    nj = D_MODEL // tn
    return pl.pallas_call(
        _merge_kernel,
        grid=(R // tm, nj),
        in_specs=[
            pl.BlockSpec((tm, D_MODEL), lambda i, j: (i, 0)),
            pl.BlockSpec((tm, W_MIX), lambda i, j: (i, 0)),
            pl.BlockSpec((tm, W_MIX), lambda i, j: (i, 0)),
            pl.BlockSpec((None, D_MODEL, tn), lambda i, j: (l, 0, j)),
            pl.BlockSpec((None, D_MODEL, tn), lambda i, j: (l, 0, j + nj)),
            pl.BlockSpec((None, 1, tn), lambda i, j: (l, 0, j)),
            pl.BlockSpec((None, 1, tn), lambda i, j: (l, 0, j + nj)),
            pl.BlockSpec((None, W_MIX, tn), lambda i, j: (l, 0, j)),
            pl.BlockSpec((None, W_MIX, tn), lambda i, j: (l, 0, j)),
        ],
        out_specs=pl.BlockSpec((tm, tn), lambda i, j: (i, j)),
        out_shape=jax.ShapeDtypeStruct((R, D_MODEL), bf16),
        compiler_params=_cparams(("arbitrary", "arbitrary")),
        name="merge",
    )(h, OM, OR, w_gate, w_gate, b_gate3, b_gate3, w_pm, w_pr)


def _outproj_norm_kernel(mg_ref, wo_ref, x_ref, g_ref, o_ref, h_ref):
    y = x_ref[...] + _dot(mg_ref[...], wo_ref[...])
    o_ref[...] = y
    ms = jnp.mean(y * y, axis=-1, keepdims=True)
    h_ref[...] = (y * lax.rsqrt(ms + EPS) * g_ref[...]).astype(bf16)


def _outproj_final_kernel(mg_ref, wo_ref, x_ref, g_ref, y_ref):
    y = x_ref[...] + _dot(mg_ref[...], wo_ref[...])
    ms = jnp.mean(y * y, axis=-1, keepdims=True)
    y_ref[...] = y * lax.rsqrt(ms + EPS) * g_ref[...]


def _wo_spec():
    return pl.BlockSpec((D_MODEL, D_MODEL), lambda i: (0, 0), pipeline_mode=pl.Buffered(1))


def _outproj(merged, w_o_bf, x, norm_g3, l_next, row0, nrows, x_local, prev):
    R = merged.shape[0]
    tm = _pick(nrows, (832, 512, 384, 128))
    assert row0 % tm == 0
    off = row0 // tm
    row_g = pl.BlockSpec((tm, D_MODEL), lambda i: (i + off, 0))
    row_x = pl.BlockSpec((tm, D_MODEL), lambda i: (i, 0)) if x_local else row_g
    kern, xspecs, xargs, aliases = _chain(_outproj_norm_kernel, 4, prev, 0)
    return pl.pallas_call(
        kern,
        grid=(nrows // tm,),
        in_specs=[row_g, _wo_spec(), row_x, _layer_spec((1, D_MODEL), l_next)] + xspecs,
        out_specs=[row_g, row_g],
        out_shape=[jax.ShapeDtypeStruct((R, D_MODEL), f32), jax.ShapeDtypeStruct((R, D_MODEL), bf16)],
        input_output_aliases=aliases,
        compiler_params=_cparams(("arbitrary",)),
        name="outproj_norm",
    )(merged, w_o_bf, x, norm_g3, *xargs)


def _outproj_final(merged, w_o_bf, x, g, row0, nrows):
    tm = _pick(nrows, (512, 256, 128))
    assert row0 % tm == 0
    off = row0 // tm
    row_in = pl.BlockSpec((tm, D_MODEL), lambda i: (i + off, 0))
    return pl.pallas_call(
        _outproj_final_kernel,
        grid=(nrows // tm,),
        in_specs=[row_in, _wo_spec(), row_in, pl.BlockSpec((1, D_MODEL), lambda i: (0, 0))],
        out_specs=pl.BlockSpec((tm, D_MODEL), lambda i: (i, 0)),
        out_shape=jax.ShapeDtypeStruct((nrows, D_MODEL), f32),
        compiler_params=_cparams(("arbitrary",)),
        name="outproj_final",
    )(merged, w_o_bf, x, g)


def kernel(x_prompt, x_sample, state_mlstm_C, state_mlstm_n, state_mlstm_m, state_mlstm_conv, state_ret_S,
           norm_g, w_in, conv_w, conv_b, w_qm, w_km, b_i, b_f, hn_m_g, skip_m, hn_r_g,
           w_pm, w_pr, w_gate, b_gate, w_o, final_g):
    B, T, D = x_prompt.shape
    NB = x_sample.shape[0]
    depth = w_in.shape[0]
    BT = B * T
    assert D == D_MODEL and x_sample.shape[1] == 1 and T % CHUNK == 0 and depth >= 2

    R = BT + NB
    xp = x_prompt.reshape(BT, D)
    xs = x_sample.reshape(NB, D)

    w_inT = jnp.swapaxes(w_in, 1, 2)
    assert math.log2(D_HEAD) % 2 == 0
    wq = (w_qm * D_HEAD ** -0.5).astype(bf16)
    wk = w_km.astype(bf16)
    wkT = jnp.swapaxes(w_km, 2, 3).astype(bf16)
    bias = jnp.pad(jnp.concatenate([b_i, b_f], axis=1), ((0, 0), (0, LANES - 2 * N_HEADS)))
    brow = bias.reshape(depth, 1, LANES)
    bcol = bias[:, :8].reshape(depth, 8, 1)
    b_gate3 = b_gate.reshape(depth, 1, 2 * D)
    norm_g3 = norm_g.reshape(depth, 1, D)
    conv_b3 = conv_b.reshape(depth, 1, W_MIX)
    hngm = hn_m_g.reshape(depth, 1, W_MIX)
    skp = skip_m.reshape(depth, 1, W_MIX)
    hngr = hn_r_g.reshape(depth, 1, W_MIX)
    conv_st = jnp.transpose(state_mlstm_conv, (0, 2, 1, 3))
    m_st = jnp.swapaxes(state_mlstm_m, 1, 2)
    n_st = state_mlstm_n.reshape(depth, NB, W_MIX)

    cos, sin, cosT, sinT, cosd, sind, cosdT, sindT = _rope_tables(T)

    prev_m = prev_r = prev_d = None
    h = _rmsnorm_into(xp, norm_g3, 0, R, 0, None)
    h = _rmsnorm_into(xs, norm_g3, 0, R, BT, h)
    x = None
    for l in range(depth):
        P, KT, GI, w_o_bf = _inproj(h, w_inT, w_o, l)
        OM, *prev_m = _mlstm_prompt(P, GI, B, T, l, depth, prev_m, conv_w, conv_b3, wq, wkT,
                                    bcol, hngm, skp)
        OR, *prev_r = _ret_prompt(P, KT, B, T, l, depth, prev_r, (cos, sin, cosT, sinT), hngr)
        OM, OR, *prev_d = _decode_step(
            P, KT, GI, OM, OR, prev_d, l, BT, conv_st, n_st, m_st, state_mlstm_C, state_ret_S,
            conv_w, conv_b3, wq, wk, brow, hngm, skp, hngr, (cosd, sind, cosdT, sindT))
        merged = _merge(h, OM, OR, w_gate, b_gate3, w_pm, w_pr, l)
        if l + 1 == depth:
            break
        if x is None:
            x, h = _outproj(merged, w_o_bf, xp, norm_g3, l + 1, 0, BT, True, None)
            x, h = _outproj(merged, w_o_bf, xs, norm_g3, l + 1, BT, NB, True, (x, h))
        else:
            x, h = _outproj(merged, w_o_bf, x, norm_g3, l + 1, 0, R, False, None)

    fg = final_g.reshape(1, D)
    y_prompt = _outproj_final(merged, w_o_bf, x, fg, 0, BT).reshape(B, T, D)
    y_sample = _outproj_final(merged, w_o_bf, x, fg, BT, NB).reshape(NB, 1, D)
    Cp, np_, mp, convp = prev_m
    (Sp,) = prev_r
    Cs, Ss, n_s, m_s, conv_s = prev_d
    return (y_prompt, y_sample,
            Cp, np_.reshape(depth, B, N_HEADS, D_HEAD), mp[:, :, 0, 12:12 + N_HEADS], convp, Sp,
            Cs, n_s.reshape(depth, NB, N_HEADS, D_HEAD), jnp.swapaxes(m_s, 1, 2),
            jnp.transpose(conv_s, (0, 2, 1, 3)), Ss)
```

```python
import functools
import math

import jax
import jax.numpy as jnp
from jax import lax
from jax.experimental import pallas as pl
from jax.experimental.pallas import tpu as pltpu

f32 = jnp.float32
bf16 = jnp.bfloat16

D_MODEL = 2048
N_HEADS = 4
D_HEAD = 256
W_MIX = N_HEADS * D_HEAD
CONV_W = 4
CHUNK = 128
ROPE_BASE = 10000.0
EPS = 1e-6
NEG = -1e30
PAST_LEN = 16384
LANES = 128
HALF = D_HEAD // 2
VMEM_LIMIT = 62 * 1024 * 1024

OFF_IF = 3 * W_MIX
OFF_RET = 3 * W_MIX + 2 * N_HEADS


def _pick(n, cands):
    for c in cands:
        if n % c == 0:
            return c
    return n


def _cparams(sem):
    return pltpu.CompilerParams(dimension_semantics=sem, vmem_limit_bytes=VMEM_LIMIT)


def _dot(a, b):
    return jnp.dot(a, b, preferred_element_type=f32)


def _dot_nt(a, b):
    return lax.dot_general(a, b, (((1,), (1,)), ((), ())), preferred_element_type=f32)


def _split3(x):
    hi = x.astype(bf16).astype(f32)
    r = x - hi
    mid = r.astype(bf16).astype(f32)
    lo = (r - mid).astype(bf16).astype(f32)
    return hi, mid, lo


def _silu(x):
    return x * (0.5 * jnp.tanh(0.5 * x) + 0.5)


def _log_sigmoid(x):
    return jnp.minimum(x, 0.0) - jnp.log1p(jnp.exp(-jnp.abs(x)))


def _head_norm(hc, g):
    mu = jnp.mean(hc, axis=-1, keepdims=True)
    d = hc - mu
    var = jnp.mean(d * d, axis=-1, keepdims=True)
    return d * lax.rsqrt(var + EPS) * g


def _rope_kernel(inv_ref, invT_ref, cos_ref, sin_ref, cosT_ref, sinT_ref,
                 cosd_ref, sind_ref, cosdT_ref, sindT_ref):
    T = cos_ref.shape[0]
    pos = lax.broadcasted_iota(jnp.int32, (T, HALF), 0).astype(f32)
    ang = pos * inv_ref[...]
    cos = jnp.cos(ang)
    sin = jnp.sin(ang)
    cos_ref[...] = cos
    sin_ref[...] = sin
    cosT_ref[...] = cos.T
    sinT_ref[...] = sin.T
    angd = jnp.full((8, HALF), float(PAST_LEN), f32) * inv_ref[...]
    cosd_ref[...] = jnp.cos(angd)
    sind_ref[...] = jnp.sin(angd)
    angdT = jnp.full((HALF, LANES), float(PAST_LEN), f32) * invT_ref[...]
    cosdT_ref[...] = jnp.cos(angdT)
    sindT_ref[...] = jnp.sin(angdT)


def _rope_tables(T):
    inv = ROPE_BASE ** (-jnp.arange(0, HALF, dtype=f32) / HALF)
    shapes = [(T, HALF), (T, HALF), (HALF, T), (HALF, T),
              (8, HALF), (8, HALF), (HALF, LANES), (HALF, LANES)]
    return pl.pallas_call(
        _rope_kernel,
        out_shape=[jax.ShapeDtypeStruct(s, f32) for s in shapes],
        name="rope_tables",
    )(inv.reshape(1, HALF), inv.reshape(HALF, 1))


def _rms_kernel(x_ref, g_ref, o_ref):
    x = x_ref[...]
    ms = jnp.mean(x * x, axis=-1, keepdims=True)
    o_ref[...] = (x * lax.rsqrt(ms + EPS) * g_ref[...]).astype(o_ref.dtype)


def _rmsnorm_into(x, norm_g3, l, R, row0, prev):
    n, D = x.shape
    tr = _pick(n, (512, 256, 128))
    assert row0 % tr == 0
    off = row0 // tr
    kern, xspecs, xargs, aliases = _chain(_rms_kernel, 2, None if prev is None else [prev], 0)
    return pl.pallas_call(
        kern,
        grid=(n // tr,),
        in_specs=[pl.BlockSpec((tr, D), lambda i: (i, 0)), _layer_spec((1, D), l)] + xspecs,
        out_specs=pl.BlockSpec((tr, D), lambda i: (i + off, 0)),
        out_shape=jax.ShapeDtypeStruct((R, D), bf16),
        input_output_aliases=aliases,
        compiler_params=_cparams(("arbitrary",)),
        name="rmsnorm",
    )(x, norm_g3, *xargs)


_TN_IN = 1024
_TPG = W_MIX // _TN_IN
_NJ_P = 6 * _TPG
_NJ_K = _TPG
_GATE_PAD = 16


def _inproj_row_offset(j):
    jp = jnp.minimum(j, _NJ_P - 1)
    grp = jp // _TPG
    sub = jp - grp * _TPG
    src_grp = jnp.where(grp < 3, grp, jnp.where(grp == 3, 3, grp + 1))
    off_p = src_grp * W_MIX + jnp.where(grp < 3, 0, 2 * N_HEADS) + sub * _TN_IN
    off_k = OFF_RET + W_MIX + jnp.clip(j - _NJ_P, 0, _NJ_K - 1) * _TN_IN
    return jnp.where(j < _NJ_P, off_p, off_k)


def _inproj_kernel(h_ref, wt_ref, wif_ref, wo_ref, p_ref, kt_ref, gi_ref, wob_ref):
    j = pl.program_id(1)
    wob_ref[...] = wo_ref[...].astype(bf16)

    def weights():
        return wt_ref[0].astype(bf16)

    is_z = (j // _TPG == 2) | (j // _TPG == 5)
    last = _NJ_P + _NJ_K - 1

    @pl.when((j < _NJ_P) & jnp.logical_not(is_z))
    def _():
        p_ref[...] = _dot_nt(h_ref[...], weights()).astype(bf16)

    @pl.when((j < _NJ_P) & is_z)
    def _():
        p_ref[...] = _silu(_dot_nt(h_ref[...], weights())).astype(bf16)

    if _NJ_K > 1:
        @pl.when((j >= _NJ_P) & (j < last))
        def _():
            kt_ref[...] = _dot_nt(weights(), h_ref[...]).astype(bf16)

    @pl.when(j == last)
    def _():
        pad = jnp.zeros((_GATE_PAD - 2 * N_HEADS, D_MODEL), f32)
        w_all = jnp.concatenate([wt_ref[0], wif_ref[...], pad], axis=0).astype(bf16)
        res = _dot_nt(w_all, h_ref[...])
        kt_ref[...] = res[0:_TN_IN, :].astype(bf16)
        gi_ref[...] = res[_TN_IN:_TN_IN + 2 * N_HEADS, :]


def _inproj(h, w_inT, w_o, l):
    R = h.shape[0]
    tm = _pick(R, (1664, 1280, 640, 384, 128))
    tn = _TN_IN
    nj = _NJ_P + _NJ_K
    nsteps = (R // tm) * nj
    rps = next(r for r in (32, 64, 128, 256, 512, 1024, 2048) if D_MODEL // r <= nsteps)
    wo_blk = lambda i, j: jnp.minimum(i * nj + j, D_MODEL // rps - 1)
    return pl.pallas_call(
        _inproj_kernel,
        grid=(R // tm, nj),
        in_specs=[
            pl.BlockSpec((tm, D_MODEL), lambda i, j: (i, 0)),
            pl.BlockSpec((pl.Element(1), pl.Element(tn), pl.Element(D_MODEL)),
                         lambda i, j: (l, pl.multiple_of(_inproj_row_offset(j), 8), 0)),
            pl.BlockSpec((None, 2 * N_HEADS, D_MODEL), lambda i, j: (l, OFF_IF // (2 * N_HEADS), 0)),
            pl.BlockSpec((None, rps, D_MODEL), lambda i, j: (l, wo_blk(i, j), 0)),
        ],
        out_specs=[
            pl.BlockSpec((tm, tn), lambda i, j: (i, jnp.minimum(j, _NJ_P - 1))),
            pl.BlockSpec((tn, tm), lambda i, j: (jnp.clip(j - _NJ_P, 0, _NJ_K - 1), i)),
            pl.BlockSpec((2 * N_HEADS, tm), lambda i, j: (0, i)),
            pl.BlockSpec((rps, D_MODEL), lambda i, j: (wo_blk(i, j), 0)),
        ],
        out_shape=[
            jax.ShapeDtypeStruct((R, 6 * W_MIX), bf16),
            jax.ShapeDtypeStruct((W_MIX, R), bf16),
            jax.ShapeDtypeStruct((2 * N_HEADS, R), f32),
            jax.ShapeDtypeStruct((D_MODEL, D_MODEL), bf16),
        ],
        compiler_params=_cparams(("arbitrary", "arbitrary")),
        name="inproj",
    )(h, w_inT, w_inT, w_o)


def _mlstm_kernel(um_ref, vm_ref, zs_ref, gi_ref, convw_ref, convb_ref, wq_ref, wkT_ref,
                  bcol_ref, hng_ref, skip_ref,
                  om_ref, c_out_ref, n_out_ref, m_out_ref, conv_out_ref,
                  tail_s, cta_s, m_s):
    g = pl.program_id(1)
    ng = pl.num_programs(1)
    Lg = um_ref.shape[0]
    L = CHUNK
    DA = D_HEAD + LANES
    heads = range(N_HEADS)

    @pl.when(g == 0)
    def _():
        tail_s[...] = jnp.zeros(tail_s.shape, bf16)
        cta_s[...] = jnp.zeros(cta_s.shape, f32)
        m_s[...] = jnp.full(m_s.shape, NEG, f32)

    srow = lax.broadcasted_iota(jnp.int32, (L, 2 * L), 0)
    scol = lax.broadcasted_iota(jnp.int32, (L, 2 * L), 1)
    shift_mat = jnp.concatenate([(scol == srow + L - s).astype(bf16) for s in range(1, CONV_W)], axis=0)

    row = lax.broadcasted_iota(jnp.int32, (L, L), 0)
    col = lax.broadcasted_iota(jnp.int32, (L, L), 1)
    causal = row >= col
    triu = (row <= col).astype(bf16)
    lane = col
    bcol = bcol_ref[...]
    ones_b = jnp.ones((L, LANES), bf16)
    ones_f = jnp.ones((L, LANES), f32)

    for c in range(Lg // L):
        rows = slice(c * L, (c + 1) * L)
        gtb = gi_ref[:, rows] + bcol
        lft = _log_sigmoid(gtb)
        br3 = _dot(jnp.concatenate(_split3(lft), axis=0).astype(bf16), triu)
        br_all = br3[0:8, :] + br3[8:16, :] + br3[16:24, :]
        cols = jnp.concatenate([gtb, br_all, jnp.zeros((LANES - 16, L), f32)], axis=0).T

        u_cur = um_ref[rows, :]
        u_prev = tail_s[...] if c == 0 else um_ref[(c - 1) * L:c * L, :]
        delayed = _dot(shift_mat, jnp.concatenate([u_prev, u_cur], axis=0))
        acc = convb_ref[...] + u_cur.astype(f32) * convw_ref[CONV_W - 1:CONV_W, :]
        for s in range(1, CONV_W):
            acc = acc + delayed[(s - 1) * L:s * L, :] * convw_ref[CONV_W - 1 - s:CONV_W - s, :]
        cm = _silu(acc)

        cs = [slice(h * D_HEAD, (h + 1) * D_HEAD) for h in heads]
        ch = [cm[:, cs[h]] for h in heads]
        chb = [x.astype(bf16) for x in ch]
        qb = [_dot(chb[h], wq_ref[h]).astype(bf16) for h in heads]
        kTb = [_dot_nt(wkT_ref[h], chb[h]).astype(bf16) for h in heads]
        v = [vm_ref[rows, cs[h]] for h in heads]
        bc = [cols[:, 12 + h:13 + h] for h in heads]
        dlog = [jnp.where(causal, bc[h] - br_all[4 + h:5 + h, :] + gtb[h:h + 1, :], NEG) for h in heads]
        dmax = jnp.zeros((L, LANES), f32)
        for h in heads:
            dmax = jnp.where(lane == 12 + h, jnp.max(dlog[h], axis=-1, keepdims=True), dmax)
        m_row = m_s[0:1, :]
        inter = cols + m_row
        m_all = jnp.maximum(inter, dmax)
        a_all = jnp.exp(inter - m_all)
        em_all = jnp.exp(-m_all)
        m_new = m_all[L - 1:L, :]
        b_last = cols[L - 1:L, :]
        ic_al = pltpu.roll(cols, 12, axis=1)
        wk_all = jnp.exp(b_last - cols + ic_al - m_new)
        dec_row = jnp.exp(b_last + m_row - m_new)
        m_s[...] = jnp.broadcast_to(m_new, (8, LANES))
        hl = [slice(12 + h, 13 + h) for h in heads]
        ws = [(jnp.exp(dlog[h] - m_all[:, hl[h]]) * _dot(qb[h], kTb[h])).astype(bf16) for h in heads]
        cta = [cta_s[h] for h in heads]
        nd = [_dot(ws[h], jnp.concatenate([v[h], ones_b], axis=1))
              + a_all[:, hl[h]] * _dot(qb[h], cta[h].astype(bf16)) for h in heads]
        for h in heads:
            vaug = jnp.concatenate([v[h].astype(f32), ones_f], axis=1)
            cta_s[h] = dec_row[:, hl[h]] * cta[h] + _dot(kTb[h], (wk_all[:, hl[h]] * vaug).astype(bf16))
        for h in heads:
            den = jnp.maximum(jnp.abs(nd[h][:, D_HEAD:DA]), em_all[:, hl[h]])
            rden = 1.0 / den
            hc = jnp.concatenate([nd[h][:, 0:HALF] * rden, nd[h][:, HALF:D_HEAD] * rden], axis=1)
            hn = _head_norm(hc, hng_ref[:, cs[h]])
            zs = zs_ref[rows, cs[h]].astype(f32)
            om_ref[rows, cs[h]] = ((hn + skip_ref[:, cs[h]] * ch[h]) * zs).astype(bf16)

    tail_s[...] = um_ref[Lg - L:Lg, :]

    @pl.when(g == ng - 1)
    def _():
        for h in heads:
            st = cta_s[h]
            c_out_ref[h] = st[:, 0:D_HEAD].T
            n_out_ref[h] = st[:, D_HEAD:DA].T[0:1, :]
        m_out_ref[...] = m_s[...]
        conv_out_ref[...] = um_ref[Lg - 8:Lg, :].astype(f32)[8 - (CONV_W - 1):8, :]


def _drop_refs(kernel_fn, n_in, n_drop, *refs):
    return kernel_fn(*refs[:n_in], *refs[n_in + n_drop:])


def _chain(kernel_fn, n_in, prev, first_out):
    if prev is None:
        return kernel_fn, [], [], {}
    specs = [pl.BlockSpec(memory_space=pl.ANY)] * len(prev)
    aliases = {n_in + k: first_out + k for k in range(len(prev))}
    return functools.partial(_drop_refs, kernel_fn, n_in, len(prev)), specs, list(prev), aliases


def _layer_spec(shape, l):
    nd = len(shape)
    return pl.BlockSpec((None,) + tuple(shape), lambda *_: (l,) + (0,) * nd)


def _mlstm_prompt(P, GI, B, T, l, depth, prev, conv_w, conv_b, wq, wkT, bcol, hng, skip):
    R = P.shape[0]
    Lg = _pick(T, (512, 256, 128))
    ng = T // Lg
    col = lambda k: (lambda b, g: (b * ng + g, k))
    in_specs = [
        pl.BlockSpec((Lg, W_MIX), col(0)),
        pl.BlockSpec((Lg, W_MIX), col(1)),
        pl.BlockSpec((Lg, W_MIX), col(2)),
        pl.BlockSpec((2 * N_HEADS, Lg), lambda b, g: (0, b * ng + g)),
        _layer_spec((CONV_W, W_MIX), l),
        _layer_spec((1, W_MIX), l),
        _layer_spec((N_HEADS, D_HEAD, D_HEAD), l),
        _layer_spec((N_HEADS, D_HEAD, D_HEAD), l),
        _layer_spec((8, 1), l),
        _layer_spec((1, W_MIX), l),
        _layer_spec((1, W_MIX), l),
    ]
    args = [P, P, P, GI, conv_w, conv_b, wq, wkT, bcol, hng, skip]
    kern, xspecs, xargs, aliases = _chain(_mlstm_kernel, len(args), prev, 1)
    return pl.pallas_call(
        kern,
        grid=(B, ng),
        in_specs=in_specs + xspecs,
        out_specs=[
            pl.BlockSpec((Lg, W_MIX), col(0)),
            pl.BlockSpec((None, None, N_HEADS, D_HEAD, D_HEAD), lambda b, g: (l, b, 0, 0, 0)),
            pl.BlockSpec((None, None, N_HEADS, 1, D_HEAD), lambda b, g: (l, b, 0, 0, 0)),
            pl.BlockSpec((None, None, 8, LANES), lambda b, g: (l, b, 0, 0)),
            pl.BlockSpec((None, None, CONV_W - 1, W_MIX), lambda b, g: (l, b, 0, 0)),
        ],
        out_shape=[
            jax.ShapeDtypeStruct((R, W_MIX), bf16),
            jax.ShapeDtypeStruct((depth, B, N_HEADS, D_HEAD, D_HEAD), f32),
            jax.ShapeDtypeStruct((depth, B, N_HEADS, 1, D_HEAD), f32),
            jax.ShapeDtypeStruct((depth, B, 8, LANES), f32),
            jax.ShapeDtypeStruct((depth, B, CONV_W - 1, W_MIX), f32),
        ],
        scratch_shapes=[
            pltpu.VMEM((CHUNK, W_MIX), bf16),
            pltpu.VMEM((N_HEADS, D_HEAD, D_HEAD + LANES), f32),
            pltpu.VMEM((8, LANES), f32),
        ],
        input_output_aliases=aliases,
        compiler_params=_cparams(("arbitrary", "arbitrary")),
        name="mlstm_prompt",
    )(*args, *xargs)


def _log_gammas():
    return [math.log(1.0 - 2.0 ** (-5.0 - h)) for h in range(N_HEADS)]


def _ret_kernel(q_ref, v_ref, zs_ref, kt_ref, cos_ref, sin_ref, cosT_ref, sinT_ref, hng_ref,
                o_ref, s_out_ref, s_s):
    g = pl.program_id(1)
    ng = pl.num_programs(1)
    Lg = q_ref.shape[0]
    L = CHUNK

    @pl.when(g == 0)
    def _():
        s_s[...] = jnp.zeros(s_s.shape, f32)

    row = lax.broadcasted_iota(jnp.int32, (L, L), 0)
    col = lax.broadcasted_iota(jnp.int32, (L, L), 1)
    causal = row >= col
    diff = (row - col).astype(f32)
    tcol = lax.broadcasted_iota(jnp.int32, (L, 1), 0).astype(f32)
    lgs = _log_gammas()
    intra = [jnp.where(causal, jnp.exp(diff * lg), 0.0) for lg in lgs]
    q_dec = [jnp.exp((tcol + 1.0) * lg) for lg in lgs]
    k_dec = [jnp.exp((L - 1.0 - tcol) * lg) for lg in lgs]
    s_dec = [math.exp(L * lg) for lg in lgs]
    kscale = D_HEAD ** -0.5

    for c in range(Lg // L):
        rows = slice(c * L, (c + 1) * L)
        cos = cos_ref[rows, :]
        sin = sin_ref[rows, :]
        cosT = cosT_ref[:, rows] * kscale
        sinT = sinT_ref[:, rows] * kscale
        heads = range(N_HEADS)
        cs = [slice(h * D_HEAD, (h + 1) * D_HEAD) for h in heads]
        lo = [slice(h * D_HEAD, h * D_HEAD + HALF) for h in heads]
        hi = [slice(h * D_HEAD + HALF, (h + 1) * D_HEAD) for h in heads]
        q1 = [q_ref[rows, lo[h]].astype(f32) for h in heads]
        q2 = [q_ref[rows, hi[h]].astype(f32) for h in heads]
        qb = [jnp.concatenate([q1[h] * cos - q2[h] * sin, q1[h] * sin + q2[h] * cos], axis=-1).astype(bf16)
              for h in heads]
        k1 = [kt_ref[lo[h], rows].astype(f32) for h in heads]
        k2 = [kt_ref[hi[h], rows].astype(f32) for h in heads]
        kTb = [jnp.concatenate([k1[h] * cosT - k2[h] * sinT, k1[h] * sinT + k2[h] * cosT],
                               axis=0).astype(bf16) for h in heads]
        v = [v_ref[rows, cs[h]] for h in heads]
        sc = [(_dot(qb[h], kTb[h]) * intra[h]).astype(bf16) for h in heads]
        s = [s_s[h] for h in heads]
        o = [_dot(sc[h], v[h]) + q_dec[h] * _dot(qb[h], s[h].astype(bf16)) for h in heads]
        for h in heads:
            s_s[h] = s_dec[h] * s[h] + _dot(kTb[h], (v[h].astype(f32) * k_dec[h]).astype(bf16))
        for h in heads:
            zs = zs_ref[rows, cs[h]].astype(f32)
            o_ref[rows, cs[h]] = (_head_norm(o[h], hng_ref[:, cs[h]]) * zs).astype(bf16)

    @pl.when(g == ng - 1)
    def _():
        s_out_ref[...] = s_s[...]


def _ret_prompt(P, KT, B, T, l, depth, prev, tabs, hng):
    R = P.shape[0]
    Lg = _pick(T, (1024, 512, 256, 128))
    ng = T // Lg
    col = lambda k: (lambda b, g: (b * ng + g, k))
    cos, sin, cosT, sinT = tabs
    in_specs = [
        pl.BlockSpec((Lg, W_MIX), col(3)),
        pl.BlockSpec((Lg, W_MIX), col(4)),
        pl.BlockSpec((Lg, W_MIX), col(5)),
        pl.BlockSpec((W_MIX, Lg), lambda b, g: (0, b * ng + g)),
        pl.BlockSpec((Lg, HALF), lambda b, g: (g, 0)),
        pl.BlockSpec((Lg, HALF), lambda b, g: (g, 0)),
        pl.BlockSpec((HALF, Lg), lambda b, g: (0, g)),
        pl.BlockSpec((HALF, Lg), lambda b, g: (0, g)),
        _layer_spec((1, W_MIX), l),
    ]
    args = [P, P, P, KT, cos, sin, cosT, sinT, hng]
    kern, xspecs, xargs, aliases = _chain(_ret_kernel, len(args), prev, 1)
    return pl.pallas_call(
        kern,
        grid=(B, ng),
        in_specs=in_specs + xspecs,
        out_specs=[
            pl.BlockSpec((Lg, W_MIX), col(0)),
            pl.BlockSpec((None, None, N_HEADS, D_HEAD, D_HEAD), lambda b, g: (l, b, 0, 0, 0)),
        ],
        out_shape=[
            jax.ShapeDtypeStruct((R, W_MIX), bf16),
            jax.ShapeDtypeStruct((depth, B, N_HEADS, D_HEAD, D_HEAD), f32),
        ],
        scratch_shapes=[pltpu.VMEM((N_HEADS, D_HEAD, D_HEAD), f32)],
        input_output_aliases=aliases,
        compiler_params=_cparams(("arbitrary", "arbitrary")),
        name="ret_prompt",
    )(*args, *xargs)


_SEQ_PER_STEP = 4


_STATE_SLOTS = 3


def _decode_kernel(l, um_ref, vm_ref, zsm_ref, qr_ref, vr_ref, zsr_ref, kt_ref, gi_ref,
                   conv_ref, n_ref, m_ref, c_hbm, s_hbm,
                   convw_ref, convb_ref, wq_ref, wk_ref, brow_ref, hngm_ref, skip_ref, hngr_ref,
                   cosd_ref, sind_ref, cosdT_ref, sindT_ref,
                   om_ref, or_ref, c_out_ref, s_out_ref, n_out_ref, m_out_ref, conv_out_ref,
                   vt_s, qrt_s, krt_s, qm_s, km_s, vr_s, sct_s, cqt_s, oacc_s, cm_s,
                   cbuf, sbuf, sems):
    s = pl.program_id(0)
    ns = pl.num_programs(0)
    NB = um_ref.shape[0]

    def state_copies(step):
        slot = step % _STATE_SLOTS
        seqs = pl.ds(step * _SEQ_PER_STEP, _SEQ_PER_STEP)
        return (pltpu.make_async_copy(c_hbm.at[l, seqs], cbuf.at[slot], sems.at[0, slot]),
                pltpu.make_async_copy(s_hbm.at[l, seqs], sbuf.at[slot], sems.at[1, slot]))

    def start(step):
        for cp in state_copies(step):
            cp.start()

    @pl.when(s == 0)
    def _():
        start(0)

    @pl.when((s == 0) & (ns > 1))
    def _():
        start(1)

    @pl.when(s + 2 < ns)
    def _():
        start(s + 2)
    scale = D_HEAD ** -0.5
    lgs = _log_gammas()
    gammas = [math.exp(lg) for lg in lgs]
    lane = lax.broadcasted_iota(jnp.int32, (1, LANES), 1)

    @pl.when(s == 0)
    def _():
        u = um_ref[...].astype(f32)
        acc = convb_ref[...] + u * convw_ref[CONV_W - 1:CONV_W, :]
        for i in range(CONV_W - 1):
            acc = acc + conv_ref[i] * convw_ref[i:i + 1, :]
        cm = _silu(acc)
        cm_s[...] = cm
        for i in range(CONV_W - 2):
            conv_out_ref[i] = conv_ref[i + 1]
        conv_out_ref[CONV_W - 2] = u

        gb = jnp.concatenate([gi_ref[...], jnp.zeros((LANES - 8, NB), f32)], axis=0).T + brow_ref[...]
        lf = _log_sigmoid(gb)
        m_cols = jnp.concatenate([m_ref[...], jnp.zeros((8 - N_HEADS, NB), f32)], axis=0).T
        sc = jnp.zeros((NB, LANES), f32)
        mo = jnp.zeros((NB, LANES), f32)
        for h in range(N_HEADS):
            cs = slice(h * D_HEAD, (h + 1) * D_HEAD)
            chb = cm[:, cs].astype(bf16)
            q = _dot(chb, wq_ref[h])
            k = _dot(chb, wk_ref[h])
            qm_s[:, cs] = q
            km_s[:, cs] = k
            ig = gb[:, h:h + 1]
            lfh = lf[:, 4 + h:5 + h]
            mp = m_cols[:, h:h + 1]
            inter = lfh + mp
            m_t = jnp.maximum(inter, ig)
            w = jnp.exp(ig - m_t)
            a = jnp.exp(inter - m_t)
            nh = n_ref[:, cs]
            ws = w * jnp.sum(q * k, axis=-1, keepdims=True)
            den = ws + a * jnp.sum(nh * q, axis=-1, keepdims=True)
            den = jnp.maximum(jnp.abs(den), jnp.exp(-m_t))
            n_out_ref[:, cs] = a * nh + w * k
            for r, val in ((h, ws), (4 + h, a), (8 + h, den), (12 + h, w)):
                sc = sc + val * (lane == r).astype(f32)
            mo = mo + m_t * (lane == h).astype(f32)
        m_out_ref[...] = mo.T[0:N_HEADS, :]
        sct_s[...] = sc.T
        vt_s[...] = vm_ref[...].astype(f32).T
        vr_s[...] = vr_ref[...].astype(f32)
        cosd = cosd_ref[0:1, :]
        sind = sind_ref[0:1, :]
        cosdT = cosdT_ref[...]
        sindT = sindT_ref[...]
        for h in range(N_HEADS):
            lo = slice(h * D_HEAD, h * D_HEAD + HALF)
            hi = slice(h * D_HEAD + HALF, (h + 1) * D_HEAD)
            q1 = qr_ref[:, lo].astype(f32)
            q2 = qr_ref[:, hi].astype(f32)
            qrot = jnp.concatenate([q1 * cosd - q2 * sind, q1 * sind + q2 * cosd], axis=-1)
            qrt_s[h * D_HEAD:(h + 1) * D_HEAD, :] = qrot.T
            k1 = kt_ref[lo, :].astype(f32)
            k2 = kt_ref[hi, :].astype(f32)
            krt_s[lo, :] = (k1 * cosdT - k2 * sindT) * scale
            krt_s[hi, :] = (k1 * sindT + k2 * cosdT) * scale
        cqt_s[...] = jnp.zeros(cqt_s.shape, f32)

    def take_col(mat, mask):
        return jnp.sum(jnp.where(mask, mat, 0.0), axis=-1, keepdims=True)

    for cp in state_copies(s):
        cp.wait()
    slot = s % _STATE_SLOTS

    for j in range(_SEQ_PER_STEP):
        b = s * _SEQ_PER_STEP + j
        mask = lane == b
        for h in range(N_HEADS):
            cs = slice(h * D_HEAD, (h + 1) * D_HEAD)
            c = cbuf[slot, j, h]
            q_row = qm_s[pl.ds(b, 1), cs]
            k_row = km_s[pl.ds(b, 1), cs]
            v_col = take_col(vt_s[cs, :], mask)
            a = take_col(sct_s[4 + h:5 + h, :], mask)
            w = take_col(sct_s[12 + h:13 + h, :], mask)
            cq = jnp.sum(c * q_row, axis=-1, keepdims=True)
            cqt_s[cs, :] = jnp.where(mask, cq, cqt_s[cs, :])
            c_out_ref[j, h] = a * c + (w * v_col) * k_row
            st = sbuf[slot, j, h]
            q_col = take_col(qrt_s[cs, :], mask)
            k_col = take_col(krt_s[cs, :], mask)
            v_row = vr_s[pl.ds(b, 1), cs]
            qk = jnp.sum(q_col * k_col, axis=0, keepdims=True)
            o_row = qk * v_row + gammas[h] * jnp.sum(st * q_col, axis=0, keepdims=True)
            oacc_s[pl.ds(b, 1), cs] = o_row
            s_out_ref[j, h] = gammas[h] * st + k_col * v_row

    @pl.when(s == ns - 1)
    def _():
        for h in range(N_HEADS):
            cs = slice(h * D_HEAD, (h + 1) * D_HEAD)
            ht = ((sct_s[h:h + 1, :] * vt_s[cs, :] + sct_s[4 + h:5 + h, :] * cqt_s[cs, :])
                  / sct_s[8 + h:9 + h, :])
            hn = _head_norm(ht.T, hngm_ref[:, cs])
            zsm = zsm_ref[:, cs].astype(f32)
            om_ref[:, cs] = ((hn + skip_ref[:, cs] * cm_s[:, cs]) * zsm).astype(bf16)
            zsr = zsr_ref[:, cs].astype(f32)
            or_ref[:, cs] = (_head_norm(oacc_s[:, cs], hngr_ref[:, cs]) * zsr).astype(bf16)


def _decode_step(P, KT, GI, OM, OR, prev, l, BT, conv_st, n_st, m_st, C_st, S_st,
                 conv_w, conv_b, wq, wk, brow, hngm, skip, hngr, dtabs):
    R = P.shape[0]
    NB = R - BT
    assert NB == LANES and BT % NB == 0 and NB % _SEQ_PER_STEP == 0
    rb = BT // NB
    depth = C_st.shape[0]
    bb = _SEQ_PER_STEP
    cosd, sind, cosdT, sindT = dtabs
    pcol = lambda k: (lambda s: (rb, k))
    rep2 = lambda s: (0, 0)
    st_spec = pl.BlockSpec((None, bb, N_HEADS, D_HEAD, D_HEAD), lambda s: (l, s, 0, 0, 0))
    conv_spec = _layer_spec((CONV_W - 1, NB, W_MIX), l)
    n_spec = _layer_spec((NB, W_MIX), l)
    m_spec = _layer_spec((N_HEADS, NB), l)
    in_specs = [
        pl.BlockSpec((NB, W_MIX), pcol(0)), pl.BlockSpec((NB, W_MIX), pcol(1)),
        pl.BlockSpec((NB, W_MIX), pcol(2)), pl.BlockSpec((NB, W_MIX), pcol(3)),
        pl.BlockSpec((NB, W_MIX), pcol(4)), pl.BlockSpec((NB, W_MIX), pcol(5)),
        pl.BlockSpec((W_MIX, NB), lambda s: (0, rb)),
        pl.BlockSpec((2 * N_HEADS, NB), lambda s: (0, rb)),
        conv_spec, n_spec, m_spec,
        pl.BlockSpec(memory_space=pl.ANY), pl.BlockSpec(memory_space=pl.ANY),
        _layer_spec((CONV_W, W_MIX), l), _layer_spec((1, W_MIX), l),
        _layer_spec((N_HEADS, D_HEAD, D_HEAD), l), _layer_spec((N_HEADS, D_HEAD, D_HEAD), l),
        _layer_spec((1, LANES), l),
        _layer_spec((1, W_MIX), l), _layer_spec((1, W_MIX), l), _layer_spec((1, W_MIX), l),
        pl.BlockSpec((8, HALF), rep2), pl.BlockSpec((8, HALF), rep2),
        pl.BlockSpec((HALF, LANES), rep2), pl.BlockSpec((HALF, LANES), rep2),
    ]
    args = [P, P, P, P, P, P, KT, GI, conv_st, n_st, m_st, C_st, S_st,
            conv_w, conv_b, wq, wk, brow, hngm, skip, hngr, cosd, sind, cosdT, sindT]
    carried = [OM, OR] + (list(prev) if prev is not None else [])
    kern, xspecs, xargs, aliases = _chain(functools.partial(_decode_kernel, l), len(args), carried, 0)
    out_row = pl.BlockSpec((NB, W_MIX), lambda s: (rb, 0))
    return pl.pallas_call(
        kern,
        grid=(NB // bb,),
        in_specs=in_specs + xspecs,
        out_specs=[out_row, out_row, st_spec, st_spec, n_spec, m_spec, conv_spec],
        out_shape=[
            jax.ShapeDtypeStruct(OM.shape, bf16),
            jax.ShapeDtypeStruct(OR.shape, bf16),
            jax.ShapeDtypeStruct((depth, NB, N_HEADS, D_HEAD, D_HEAD), f32),
            jax.ShapeDtypeStruct((depth, NB, N_HEADS, D_HEAD, D_HEAD), f32),
            jax.ShapeDtypeStruct((depth, NB, W_MIX), f32),
            jax.ShapeDtypeStruct((depth, N_HEADS, NB), f32),
            jax.ShapeDtypeStruct((depth, CONV_W - 1, NB, W_MIX), f32),
        ],
        scratch_shapes=[
            pltpu.VMEM((W_MIX, NB), f32),
            pltpu.VMEM((W_MIX, NB), f32),
            pltpu.VMEM((W_MIX, NB), f32),
            pltpu.VMEM((NB, W_MIX), f32),
            pltpu.VMEM((NB, W_MIX), f32),
            pltpu.VMEM((NB, W_MIX), f32),
            pltpu.VMEM((LANES, NB), f32),
            pltpu.VMEM((W_MIX, NB), f32),
            pltpu.VMEM((NB, W_MIX), f32),
            pltpu.VMEM((NB, W_MIX), f32),
            pltpu.VMEM((_STATE_SLOTS, bb, N_HEADS, D_HEAD, D_HEAD), f32),
            pltpu.VMEM((_STATE_SLOTS, bb, N_HEADS, D_HEAD, D_HEAD), f32),
            pltpu.SemaphoreType.DMA((2, _STATE_SLOTS)),
        ],
        input_output_aliases=aliases,
        compiler_params=_cparams(("arbitrary",)),
        name="decode_step",
    )(*args, *xargs)


_MERGE_SLAB = 256


def _merge_kernel(h_ref, om_ref, or_ref, wgm_ref, wgr_ref, bgm_ref, bgr_ref, wpm_ref, wpr_ref, o_ref):
    for c0 in range(0, o_ref.shape[1], _MERGE_SLAB):
        cols = slice(c0, c0 + _MERGE_SLAB)
        gm = jax.nn.sigmoid(_dot(h_ref[...], wgm_ref[:, cols].astype(bf16)) + bgm_ref[:, cols])
        gr = jax.nn.sigmoid(_dot(h_ref[...], wgr_ref[:, cols].astype(bf16)) + bgr_ref[:, cols])
        pm = _dot(om_ref[...], wpm_ref[:, cols].astype(bf16))
        pr = _dot(or_ref[...], wpr_ref[:, cols].astype(bf16))
        o_ref[:, cols] = (gm * pm + gr * pr).astype(bf16)


def _merge(h, OM, OR, w_gate, b_gate3, w_pm, w_pr, l):
    R = h.shape[0]
    tm = _pick(R, (1664, 1280, 640, 384, 128))
    tn = 512
    nj = D_MODEL // tn
    return pl.pallas_call(
        _merge_kernel,
        grid=(R // tm, nj),
        in_specs=[
            pl.BlockSpec((tm, D_MODEL), lambda i, j: (i, 0)),
            pl.BlockSpec((tm, W_MIX), lambda i, j: (i, 0)),
            pl.BlockSpec((tm, W_MIX), lambda i, j: (i, 0)),
            pl.BlockSpec((None, D_MODEL, tn), lambda i, j: (l, 0, j)),
            pl.BlockSpec((None, D_MODEL, tn), lambda i, j: (l, 0, j + nj)),
            pl.BlockSpec((None, 1, tn), lambda i, j: (l, 0, j)),
            pl.BlockSpec((None, 1, tn), lambda i, j: (l, 0, j + nj)),
            pl.BlockSpec((None, W_MIX, tn), lambda i, j: (l, 0, j)),
            pl.BlockSpec((None, W_MIX, tn), lambda i, j: (l, 0, j)),
        ],
        out_specs=pl.BlockSpec((tm, tn), lambda i, j: (i, j)),
        out_shape=jax.ShapeDtypeStruct((R, D_MODEL), bf16),
        compiler_params=_cparams(("arbitrary", "arbitrary")),
        name="merge",
    )(h, OM, OR, w_gate, w_gate, b_gate3, b_gate3, w_pm, w_pr)


def _outproj_norm_kernel(nsteps, mg_off, x_off, mg_hbm, wo_ref, x_hbm, g_ref, o_ref, h_ref, mbuf, xbuf, sems):
    i = pl.program_id(0)
    n = pl.num_programs(0)
    tm = o_ref.shape[0]

    def tile_copies(step):
        slot = step % _STATE_SLOTS
        return (pltpu.make_async_copy(mg_hbm.at[pl.ds((step + mg_off) * tm, tm)], mbuf.at[slot], sems.at[0, slot]),
                pltpu.make_async_copy(x_hbm.at[pl.ds((step + x_off) * tm, tm)], xbuf.at[slot], sems.at[1, slot]))

    def start(step):
        for cp in tile_copies(step):
            cp.start()

    @pl.when(i == 0)
    def _():
        start(0)

    if nsteps > 1:
        @pl.when(i == 0)
        def _():
            start(1)

    @pl.when(i + 2 < n)
    def _():
        start(i + 2)

    for cp in tile_copies(i):
        cp.wait()
    slot = i % _STATE_SLOTS
    y = xbuf[slot] + _dot(mbuf[slot], wo_ref[...])
    o_ref[...] = y
    ms = jnp.mean(y * y, axis=-1, keepdims=True)
    h_ref[...] = (y * lax.rsqrt(ms + EPS) * g_ref[...]).astype(bf16)


def _outproj_final_kernel(mg_ref, wo_ref, x_ref, g_ref, y_ref):
    y = x_ref[...] + _dot(mg_ref[...], wo_ref[...])
    ms = jnp.mean(y * y, axis=-1, keepdims=True)
    y_ref[...] = y * lax.rsqrt(ms + EPS) * g_ref[...]


def _wo_spec():
    return pl.BlockSpec((D_MODEL, D_MODEL), lambda i: (0, 0), pipeline_mode=pl.Buffered(1))


def _outproj(merged, w_o_bf, x, norm_g3, l_next, row0, nrows, x_local, prev):
    R = merged.shape[0]
    tm = _pick(nrows, (640, 512, 384, 128))
    assert row0 % tm == 0
    off = row0 // tm
    row_g = pl.BlockSpec((tm, D_MODEL), lambda i: (i + off, 0))
    any_spec = pl.BlockSpec(memory_space=pl.ANY)
    body = functools.partial(_outproj_norm_kernel, nrows // tm, off, 0 if x_local else off)
    kern, xspecs, xargs, aliases = _chain(body, 4, prev, 0)
    return pl.pallas_call(
        kern,
        grid=(nrows // tm,),
        in_specs=[any_spec, _wo_spec(), any_spec, _layer_spec((1, D_MODEL), l_next)] + xspecs,
        out_specs=[row_g, row_g],
        out_shape=[jax.ShapeDtypeStruct((R, D_MODEL), f32), jax.ShapeDtypeStruct((R, D_MODEL), bf16)],
        scratch_shapes=[
            pltpu.VMEM((_STATE_SLOTS, tm, D_MODEL), bf16),
            pltpu.VMEM((_STATE_SLOTS, tm, D_MODEL), f32),
            pltpu.SemaphoreType.DMA((2, _STATE_SLOTS)),
        ],
        input_output_aliases=aliases,
        compiler_params=_cparams(("arbitrary",)),
        name="outproj_norm",
    )(merged, w_o_bf, x, norm_g3, *xargs)


def _outproj_final(merged, w_o_bf, x, g, row0, nrows):
    tm = _pick(nrows, (512, 256, 128))
    assert row0 % tm == 0
    off = row0 // tm
    row_in = pl.BlockSpec((tm, D_MODEL), lambda i: (i + off, 0))
    return pl.pallas_call(
        _outproj_final_kernel,
        grid=(nrows // tm,),
        in_specs=[row_in, _wo_spec(), row_in, pl.BlockSpec((1, D_MODEL), lambda i: (0, 0))],
        out_specs=pl.BlockSpec((tm, D_MODEL), lambda i: (i, 0)),
        out_shape=jax.ShapeDtypeStruct((nrows, D_MODEL), f32),
        compiler_params=_cparams(("arbitrary",)),
        name="outproj_final",
    )(merged, w_o_bf, x, g)


def kernel(x_prompt, x_sample, state_mlstm_C, state_mlstm_n, state_mlstm_m, state_mlstm_conv, state_ret_S,
           norm_g, w_in, conv_w, conv_b, w_qm, w_km, b_i, b_f, hn_m_g, skip_m, hn_r_g,
           w_pm, w_pr, w_gate, b_gate, w_o, final_g):
    B, T, D = x_prompt.shape
    NB = x_sample.shape[0]
    depth = w_in.shape[0]
    BT = B * T
    assert D == D_MODEL and x_sample.shape[1] == 1 and T % CHUNK == 0 and depth >= 2

    R = BT + NB
    xp = x_prompt.reshape(BT, D)
    xs = x_sample.reshape(NB, D)

    w_inT = jnp.swapaxes(w_in, 1, 2)
    assert math.log2(D_HEAD) % 2 == 0
    wq = (w_qm * D_HEAD ** -0.5).astype(bf16)
    wk = w_km.astype(bf16)
    wkT = jnp.swapaxes(w_km, 2, 3).astype(bf16)
    bias = jnp.pad(jnp.concatenate([b_i, b_f], axis=1), ((0, 0), (0, LANES - 2 * N_HEADS)))
    brow = bias.reshape(depth, 1, LANES)
    bcol = bias[:, :8].reshape(depth, 8, 1)
    b_gate3 = b_gate.reshape(depth, 1, 2 * D)
    norm_g3 = norm_g.reshape(depth, 1, D)
    conv_b3 = conv_b.reshape(depth, 1, W_MIX)
    hngm = hn_m_g.reshape(depth, 1, W_MIX)
    skp = skip_m.reshape(depth, 1, W_MIX)
    hngr = hn_r_g.reshape(depth, 1, W_MIX)
    conv_st = jnp.transpose(state_mlstm_conv, (0, 2, 1, 3))
    m_st = jnp.swapaxes(state_mlstm_m, 1, 2)
    n_st = state_mlstm_n.reshape(depth, NB, W_MIX)

    cos, sin, cosT, sinT, cosd, sind, cosdT, sindT = _rope_tables(T)

    prev_m = prev_r = prev_d = None
    h = _rmsnorm_into(xp, norm_g3, 0, R, 0, None)
    h = _rmsnorm_into(xs, norm_g3, 0, R, BT, h)
    x = None
    for l in range(depth):
        P, KT, GI, w_o_bf = _inproj(h, w_inT, w_o, l)
        OM, *prev_m = _mlstm_prompt(P, GI, B, T, l, depth, prev_m, conv_w, conv_b3, wq, wkT,
                                    bcol, hngm, skp)
        OR, *prev_r = _ret_prompt(P, KT, B, T, l, depth, prev_r, (cos, sin, cosT, sinT), hngr)
        OM, OR, *prev_d = _decode_step(
            P, KT, GI, OM, OR, prev_d, l, BT, conv_st, n_st, m_st, state_mlstm_C, state_ret_S,
            conv_w, conv_b3, wq, wk, brow, hngm, skp, hngr, (cosd, sind, cosdT, sindT))
        merged = _merge(h, OM, OR, w_gate, b_gate3, w_pm, w_pr, l)
        if l + 1 == depth:
            break
        if x is None:
            x, h = _outproj(merged, w_o_bf, xp, norm_g3, l + 1, 0, BT, True, None)
            x, h = _outproj(merged, w_o_bf, xs, norm_g3, l + 1, BT, NB, True, (x, h))
        else:
            x, h = _outproj(merged, w_o_bf, x, norm_g3, l + 1, 0, R, False, None)

    fg = final_g.reshape(1, D)
    y_prompt = _outproj_final(merged, w_o_bf, x, fg, 0, BT).reshape(B, T, D)
    y_sample = _outproj_final(merged, w_o_bf, x, fg, BT, NB).reshape(NB, 1, D)
    Cp, np_, mp, convp = prev_m
    (Sp,) = prev_r
    Cs, Ss, n_s, m_s, conv_s = prev_d
    return (y_prompt, y_sample,
            Cp, np_.reshape(depth, B, N_HEADS, D_HEAD), mp[:, :, 0, 12:12 + N_HEADS], convp, Sp,
            Cs, n_s.reshape(depth, NB, N_HEADS, D_HEAD), jnp.swapaxes(m_s, 1, 2),
            jnp.transpose(conv_s, (0, 2, 1, 3)), Ss)
```
